```python
import math
import jax, jax.numpy as jnp
from jax import lax
import numpy as np

D_MODEL = 1024
BATCH = 4
SEQ = 4096
DEPTH = 2

GRID_W = 64
CTX_LEN = 256
EPS = 1e-6

A_HEADS = 4
A_HEAD_DIM = 128
A_WIDTH = A_HEADS * A_HEAD_DIM
SHORT_CONV_W = 5
DELTA_CHUNK = 64
B_Q_HEADS = 8
B_KV_HEADS = 2
B_HEAD_DIM = 64
B_GROUP = B_Q_HEADS // B_KV_HEADS
B_WIDTH = B_Q_HEADS * B_HEAD_DIM
B_KV_WIDTH = B_KV_HEADS * B_HEAD_DIM
ROPE_AXIS_PAIRS = B_HEAD_DIM // 4
ROPE_THETA = 10000.0
Q_BLOCK = 128
OFF_Z_A = 3 * A_WIDTH
OFF_BA_A = OFF_Z_A + A_WIDTH
OFF_Q_B = OFF_BA_A + 4 * A_HEADS
OFF_K_B = OFF_Q_B + B_WIDTH
OFF_V_B = OFF_K_B + B_KV_WIDTH
HYB_IN_WIDTH = OFF_V_B + B_KV_WIDTH
MIX_WIDTH = A_WIDTH + B_WIDTH
CONF_WIDTH = D_MODEL
CONF_KERNEL = 31
FFN_HIDDEN = ((8 * D_MODEL + 3 * 256 - 1) // (3 * 256)) * 256
N_EVEN = (DEPTH + 1) // 2
N_ODD = DEPTH // 2

kernel_name = "hybrid_deltanet_gqa_conformer_diffusion_trunk"

F32 = jnp.float32


def rms_norm(x, gain):
    xf = x.astype(F32)
    y = xf * lax.rsqrt(jnp.mean(jnp.square(xf), axis=-1, keepdims=True) + EPS)
    return (y * gain.astype(F32)).astype(x.dtype)


def layer_norm(x, gain, bias):
    xf = x.astype(F32)
    mu = jnp.mean(xf, axis=-1, keepdims=True)
    var = jnp.mean(jnp.square(xf - mu), axis=-1, keepdims=True)
    return ((xf - mu) * lax.rsqrt(var + EPS) * gain.astype(F32) + bias.astype(F32)).astype(x.dtype)


def l2_normalize(x):
    return x * lax.rsqrt(jnp.sum(jnp.square(x), axis=-1, keepdims=True) + EPS)


def depthwise_conv_centred(x, w):
    pad = w.shape[0] // 2
    return lax.conv_general_dilated(
        x, w[:, None, :].astype(x.dtype), window_strides=(1,), padding=[(pad, pad)],
        dimension_numbers=("NWC", "WIO", "NWC"), feature_group_count=x.shape[-1])


def ada_terms(cond, w_mod, b_mod):
    m = jax.nn.silu(cond) @ w_mod + b_mod
    return jnp.split(m, 6, axis=-1)


def modulate(h, shift, scale):
    return h * (1.0 + scale) + shift


def axial_rope_tables(n_tokens):
    rows = n_tokens // GRID_W
    row = jnp.broadcast_to(jnp.arange(rows, dtype=F32)[:, None], (rows, GRID_W)).reshape(n_tokens)
    col = jnp.broadcast_to(jnp.arange(GRID_W, dtype=F32)[None, :], (rows, GRID_W)).reshape(n_tokens)
    inv_freq = ROPE_THETA ** (-jnp.arange(ROPE_AXIS_PAIRS, dtype=F32) / ROPE_AXIS_PAIRS)
    ang_r = row[:, None] * inv_freq
    ang_c = col[:, None] * inv_freq
    ang = jnp.concatenate([ang_r, ang_r, ang_c, ang_c], axis=-1)
    return jnp.cos(ang), jnp.sin(ang)


def apply_axial_rope(x, cos, sin):
    xf = x.astype(F32)
    xr = xf.reshape(*x.shape[:-1], 2, 2, ROPE_AXIS_PAIRS)
    rot = jnp.stack([-xr[..., 1, :], xr[..., 0, :]], axis=-2).reshape(x.shape)
    return (xf * cos[None, :, None, :] + rot * sin[None, :, None, :]).astype(x.dtype)


def gated_delta_chunked(q, k, v, g, beta, s0, need_out):
    bsz, n_tok, n_h, _ = q.shape
    dv = v.shape[-1]
    n_ch = n_tok // DELTA_CHUNK

    def chunks(t):
        return t.reshape(bsz, n_ch, DELTA_CHUNK, n_h, -1).transpose(1, 0, 3, 2, 4)

    qc, kc, vc = chunks(q), chunks(k), chunks(v)
    gcum = jnp.cumsum(chunks(g[..., None])[..., 0], axis=-1)
    bc = chunks(beta[..., None])
    idx = jnp.arange(DELTA_CHUNK)
    incl = idx[:, None] >= idx[None, :]
    strict = idx[:, None] > idx[None, :]
    decay = jnp.exp(jnp.where(incl, gcum[..., :, None] - gcum[..., None, :], -jnp.inf))
    kb = kc * bc
    lower = jnp.where(strict, jnp.einsum("nbhcd,nbhsd->nbhcs", kb, kc) * decay, 0.0)
    eye = jnp.eye(DELTA_CHUNK, dtype=F32)
    rhs = jnp.concatenate([vc * bc, kb * jnp.exp(gcum)[..., None]], axis=-1)
    sol = lax.linalg.triangular_solve(lower + eye, rhs, left_side=True, lower=True, unit_diagonal=True)
    u, w = sol[..., :dv], sol[..., dv:]
    g_last = gcum[..., -1]
    k_tail = kc * jnp.exp(g_last[..., None] - gcum)[..., None]
    xs = (u, w, k_tail, g_last)
    if need_out:
        intra = jnp.einsum("nbhcd,nbhsd->nbhcs", qc, kc) * decay
        q_dec = qc * jnp.exp(gcum)[..., None]
        xs = xs + (q_dec, intra)

    def step(state, xs_i):
        u_i, w_i, kt_i, gl_i = xs_i[:4]
        v_new = u_i - jnp.einsum("bhcd,bhde->bhce", w_i, state)
        new_state = state * jnp.exp(gl_i)[..., None, None] + jnp.einsum("bhcd,bhce->bhde", kt_i, v_new)
        if need_out:
            qd_i, a_i = xs_i[4:]
            o_i = jnp.einsum("bhcd,bhde->bhce", qd_i, state) + jnp.einsum("bhcs,bhse->bhce", a_i, v_new)
            return new_state, o_i
        return new_state, None

    s_fin, o = lax.scan(step, s0, xs)
    if need_out:
        o = o.transpose(1, 0, 3, 2, 4).reshape(bsz, n_tok, n_h, dv)
    return o, s_fin


def bidirectional_delta(q, k, v, g, beta, s0_fwd, s0_bwd, need_out):
    o_f, s_f = gated_delta_chunked(q, k, v, g[:, :, 0], beta[:, :, 0], s0_fwd, need_out)
    rev = lambda t: jnp.flip(t, axis=1)
    o_b, s_b = gated_delta_chunked(rev(q), rev(k), rev(v), rev(g[:, :, 1]), rev(beta[:, :, 1]), s0_bwd, need_out)
    o = o_f + rev(o_b) if need_out else None
    return o, s_f, s_b


def split_hybrid_projection(p, conv_w, a_log, dt_bias, q_norm, k_norm):
    bsz, n_tok, _ = p.shape
    qkv = jax.nn.silu(depthwise_conv_centred(p[..., :OFF_Z_A], conv_w))
    qkv = qkv.astype(F32).reshape(bsz, n_tok, 3, A_HEADS, A_HEAD_DIM)
    qa = l2_normalize(qkv[:, :, 0]) * (A_HEAD_DIM ** -0.5)
    ka = l2_normalize(qkv[:, :, 1])
    va = qkv[:, :, 2]
    za = p[..., OFF_Z_A:OFF_BA_A].reshape(bsz, n_tok, A_HEADS, A_HEAD_DIM)
    ba = p[..., OFF_BA_A:OFF_Q_B].astype(F32).reshape(bsz, n_tok, 2, 2, A_HEADS)
    beta = jax.nn.sigmoid(ba[:, :, 0])
    g = -jnp.exp(a_log.astype(F32)) * jax.nn.softplus(ba[:, :, 1] + dt_bias.astype(F32))
    qb = rms_norm(p[..., OFF_Q_B:OFF_K_B].reshape(bsz, n_tok, B_Q_HEADS, B_HEAD_DIM), q_norm)
    kb = rms_norm(p[..., OFF_K_B:OFF_V_B].reshape(bsz, n_tok, B_KV_HEADS, B_HEAD_DIM), k_norm)
    vb = p[..., OFF_V_B:].reshape(bsz, n_tok, B_KV_HEADS, B_HEAD_DIM)
    return qa, ka, va, za, g, beta, qb, kb, vb


def attend(q, k, v):
    s = jnp.einsum("bqkgd,bskd->bkgqs", q, k, preferred_element_type=F32) * (B_HEAD_DIM ** -0.5)
    p = jax.nn.softmax(s, axis=-1).astype(v.dtype)
    return jnp.einsum("bkgqs,bskd->bqkgd", p, v)


def latent_attention_blocks(q, k, v):
    bsz, n_tok = q.shape[:2]
    nb = n_tok // Q_BLOCK
    qb = q.reshape(bsz, nb, Q_BLOCK, *q.shape[2:]).swapaxes(0, 1)
    o = lax.map(lambda blk: attend(blk, k, v), qb)
    return o.swapaxes(0, 1).reshape(bsz, n_tok, B_WIDTH)


def hybrid_mixer(a_lat, a_ctx, w_in, conv_w, a_log, dt_bias, out_norm, q_norm, k_norm, w_out, cos, sin, ctx_out):
    bsz, n_tok, _ = a_lat.shape
    n_ctx = a_ctx.shape[1]
    qa_l, ka_l, va_l, za_l, g_l, be_l, qb_l, kb_l, vb_l = split_hybrid_projection(
        a_lat @ w_in, conv_w, a_log, dt_bias, q_norm, k_norm)
    qa_c, ka_c, va_c, za_c, g_c, be_c, qb_c, kb_c, vb_c = split_hybrid_projection(
        a_ctx @ w_in, conv_w, a_log, dt_bias, q_norm, k_norm)

    zeros = jnp.zeros((bsz, A_HEADS, A_HEAD_DIM, A_HEAD_DIM), F32)
    o_ctx_a, s_fwd, s_bwd = bidirectional_delta(qa_c, ka_c, va_c, g_c, be_c, zeros, zeros, ctx_out)
    o_lat_a, _, _ = bidirectional_delta(qa_l, ka_l, va_l, g_l, be_l, s_fwd, s_bwd, True)

    def delta_readout(o, z, n):
        y = rms_norm(o, out_norm) * jax.nn.silu(z.astype(F32))
        return y.reshape(bsz, n, A_WIDTH).astype(a_lat.dtype)

    qb_l = apply_axial_rope(qb_l, cos, sin)
    kb_l = apply_axial_rope(kb_l, cos, sin)
    k_all = jnp.concatenate([kb_c, kb_l], axis=1)
    v_all = jnp.concatenate([vb_c, vb_l], axis=1)
    group = lambda t: t.reshape(*t.shape[:2], B_KV_HEADS, B_GROUP, B_HEAD_DIM)
    o_lat_b = latent_attention_blocks(group(qb_l), k_all, v_all)

    y_lat = jnp.concatenate([delta_readout(o_lat_a, za_l, n_tok), o_lat_b.astype(a_lat.dtype)], axis=-1) @ w_out
    y_ctx = None
    if ctx_out:
        o_ctx_b = attend(group(qb_c), kb_c, vb_c).reshape(bsz, n_ctx, B_WIDTH)
        y_ctx = jnp.concatenate([delta_readout(o_ctx_a, za_c, n_ctx), o_ctx_b.astype(a_ctx.dtype)], axis=-1) @ w_out
    return y_lat, y_ctx


def conformer_conv(h, w_in, b_in, dw_w, dw_b, ln_g, ln_b, w_out, b_out):
    val, gate = jnp.split(h @ w_in + b_in, 2, axis=-1)
    u = val * jax.nn.sigmoid(gate)
    u = depthwise_conv_centred(u, dw_w) + dw_b
    u = jax.nn.silu(layer_norm(u, ln_g, ln_b))
    return u @ w_out + b_out


def swiglu_ffn(h, w_in, w_out):
    gate, up = jnp.split(h @ w_in, 2, axis=-1)
    return (jax.nn.silu(gate) * up) @ w_out


def setup_inputs(seed: int = 0) -> dict:
    key = jax.random.key(seed)
    ks = iter(jax.random.split(key, 40))
    D = D_MODEL

    def nrm(shape, scale):
        return jax.random.normal(next(ks), shape, F32) * scale

    def gain(shape):
        return 1.0 + nrm(shape, 0.05)

    a_log = jnp.log(jax.random.uniform(next(ks), (N_EVEN, 2, A_HEADS), F32, 1.0, 16.0))
    dt = jnp.exp(jax.random.uniform(next(ks), (N_EVEN, 2, A_HEADS), F32, math.log(1e-3), math.log(1e-1)))
    dt_bias = dt + jnp.log(-jnp.expm1(-dt))
    return {
        "x": nrm((BATCH, SEQ, D), 1.0),
        "c": nrm((BATCH, D), 1.0),
        "ctx": nrm((BATCH, CTX_LEN, D), 1.0),
        "c_ctx": nrm((D,), 1.0),
        "w_mod": nrm((DEPTH, D, 6 * D), 0.5 * D ** -0.5),
        "b_mod": nrm((DEPTH, 6 * D), 0.02),
        "g_mix_pre": gain((DEPTH, D)),
        "g_mix_post": gain((DEPTH, D)),
        "g_ffn_pre": gain((DEPTH, D)),
        "g_ffn_post": gain((DEPTH, D)),
        "w_ffn_in": nrm((DEPTH, D, 2 * FFN_HIDDEN), D ** -0.5),
        "w_ffn_out": nrm((DEPTH, FFN_HIDDEN, D), FFN_HIDDEN ** -0.5),
        "hyb_w_in": nrm((N_EVEN, D, HYB_IN_WIDTH), D ** -0.5),
        "hyb_conv_w": nrm((N_EVEN, SHORT_CONV_W, 3 * A_WIDTH), SHORT_CONV_W ** -0.5),
        "hyb_a_log": a_log,
        "hyb_dt_bias": dt_bias,
        "hyb_out_norm": gain((N_EVEN, A_HEAD_DIM)),
        "hyb_q_norm": gain((N_EVEN, B_HEAD_DIM)),
        "hyb_k_norm": gain((N_EVEN, B_HEAD_DIM)),
        "hyb_w_out": nrm((N_EVEN, MIX_WIDTH, D), MIX_WIDTH ** -0.5),
        "conf_w_in": nrm((N_ODD, D, 2 * CONF_WIDTH), D ** -0.5),
        "conf_b_in": nrm((N_ODD, 2 * CONF_WIDTH), 0.02),
        "conf_dw_w": nrm((N_ODD, CONF_KERNEL, CONF_WIDTH), CONF_KERNEL ** -0.5),
        "conf_dw_b": nrm((N_ODD, CONF_WIDTH), 0.02),
        "conf_ln_g": gain((N_ODD, CONF_WIDTH)),
        "conf_ln_b": nrm((N_ODD, CONF_WIDTH), 0.02),
        "conf_w_out": nrm((N_ODD, CONF_WIDTH, D), CONF_WIDTH ** -0.5),
        "conf_b_out": nrm((N_ODD, D), 0.02),
    }


def reference(x, c, ctx, c_ctx, w_mod, b_mod, g_mix_pre, g_mix_post, g_ffn_pre, g_ffn_post, w_ffn_in, w_ffn_out,
              hyb_w_in, hyb_conv_w, hyb_a_log, hyb_dt_bias, hyb_out_norm, hyb_q_norm, hyb_k_norm, hyb_w_out,
              conf_w_in, conf_b_in, conf_dw_w, conf_dw_b, conf_ln_g, conf_ln_b, conf_w_out, conf_b_out):
    cos, sin = axial_rope_tables(x.shape[1])
    h, hc = x, ctx
    for layer in range(DEPTH):
        is_hybrid = layer % 2 == 0
        idx = layer // 2
        advance_ctx = any(j % 2 == 0 for j in range(layer + 1, DEPTH))
        sh1, sc1, gt1, sh2, sc2, gt2 = ada_terms(c[:, None, :], w_mod[layer], b_mod[layer])
        a = modulate(rms_norm(h, g_mix_pre[layer]), sh1, sc1)
        if is_hybrid or advance_ctx:
            csh1, csc1, cgt1, csh2, csc2, cgt2 = ada_terms(c_ctx, w_mod[layer], b_mod[layer])
            ac = modulate(rms_norm(hc, g_mix_pre[layer]), csh1, csc1)
        if is_hybrid:
            y, yc = hybrid_mixer(a, ac, hyb_w_in[idx], hyb_conv_w[idx], hyb_a_log[idx], hyb_dt_bias[idx],
                                 hyb_out_norm[idx], hyb_q_norm[idx], hyb_k_norm[idx], hyb_w_out[idx],
                                 cos, sin, advance_ctx)
        else:
            conf = (conf_w_in[idx], conf_b_in[idx], conf_dw_w[idx], conf_dw_b[idx],
                    conf_ln_g[idx], conf_ln_b[idx], conf_w_out[idx], conf_b_out[idx])
            y = conformer_conv(a, *conf)
            yc = conformer_conv(ac, *conf) if advance_ctx else None
        h = h + gt1 * rms_norm(y, g_mix_post[layer])
        f = swiglu_ffn(modulate(rms_norm(h, g_ffn_pre[layer]), sh2, sc2), w_ffn_in[layer], w_ffn_out[layer])
        h = h + gt2 * rms_norm(f, g_ffn_post[layer])
        if advance_ctx:
            hc = hc + cgt1 * rms_norm(yc, g_mix_post[layer])
            fc = swiglu_ffn(modulate(rms_norm(hc, g_ffn_pre[layer]), csh2, csc2), w_ffn_in[layer], w_ffn_out[layer])
            hc = hc + cgt2 * rms_norm(fc, g_ffn_post[layer])
    return h
```

```python
import functools
import math

import jax
import jax.numpy as jnp
import numpy as np
from jax import lax
from jax.experimental import pallas as pl
from jax.experimental.pallas import tpu as pltpu

F32 = jnp.float32
BF16 = jnp.bfloat16
HIGHEST = lax.Precision.HIGHEST

EPS = 1e-6
GRID_W = 64
ROPE_THETA = 10000.0
A_HEADS = 4
A_HEAD_DIM = 128
A_WIDTH = A_HEADS * A_HEAD_DIM
SHORT_CONV_W = 5
CHUNK = 64
B_Q_HEADS = 8
B_KV_HEADS = 2
B_HEAD_DIM = 64
B_GROUP = B_Q_HEADS // B_KV_HEADS
B_WIDTH = B_Q_HEADS * B_HEAD_DIM
B_KV_WIDTH = B_KV_HEADS * B_HEAD_DIM
ROPE_AXIS_PAIRS = B_HEAD_DIM // 4
CONF_KERNEL = 31
CONF_HALO = 16
LANES = 128
VMEM_LIMIT = 56 * 1024 * 1024


def _cparams(sem):
    return pltpu.CompilerParams(dimension_semantics=sem, vmem_limit_bytes=VMEM_LIMIT)


def _silu(x):
    return x * jax.nn.sigmoid(x)


def _dot(a, b):
    return jnp.dot(a, b, preferred_element_type=F32)


def _dot_hi(a, b):
    return jnp.dot(a, b, preferred_element_type=F32, precision=HIGHEST)


def _dot_nt(a, b):
    return lax.dot_general(a, b, (((1,), (1,)), ((), ())), preferred_element_type=F32)


def _dot_tn(a, b):
    return lax.dot_general(a, b, (((0,), (0,)), ((), ())), preferred_element_type=F32)


def _rms(x, gain):
    return x * lax.rsqrt(jnp.mean(x * x, axis=-1, keepdims=True) + EPS) * gain


def _ada_kernel(c_ref, w_ref, b_ref, o_ref):
    o_ref[0] = _dot_hi(_silu(c_ref[...]), w_ref[0]) + b_ref[0]


def _ada_terms(cond, w_mod, b_mod):
    depth, d, n6 = w_mod.shape
    tn = 1536
    return pl.pallas_call(
        _ada_kernel,
        grid=(depth, n6 // tn),
        in_specs=[pl.BlockSpec((8, d), lambda l, j: (0, 0)),
                  pl.BlockSpec((1, d, tn), lambda l, j: (l, 0, j)),
                  pl.BlockSpec((1, 1, tn), lambda l, j: (l, 0, j))],
        out_specs=pl.BlockSpec((1, 8, tn), lambda l, j: (l, 0, j)),
        out_shape=jax.ShapeDtypeStruct((depth, 8, n6), F32),
        compiler_params=_cparams(("arbitrary", "arbitrary")),
        name="ada",
    )(cond, w_mod, b_mod.reshape(depth, 1, n6))


def _modulated(x, mod_ref, gain_ref, shift_row):
    y = _rms(x, gain_ref[...])
    return y * (1.0 + mod_ref[0, shift_row + 1:shift_row + 2, :]) + mod_ref[0, shift_row:shift_row + 1, :]


def _inproj_kernel(x_ref, mod_ref, g_ref, w_ref, *o_refs, widths):
    a = _modulated(x_ref[...], mod_ref, g_ref, 0).astype(BF16)
    off = 0
    for o_ref, wd in zip(o_refs, widths):
        o_ref[...] = _dot(a, w_ref[:, off:off + wd])
        off += wd


def _inproj(x2, mod, gain, w, widths, rows_per_mod, tm):
    n, d = x2.shape
    blocks_per_mod = rows_per_mod // tm
    return pl.pallas_call(
        functools.partial(_inproj_kernel, widths=widths),
        grid=(n // tm,),
        in_specs=[pl.BlockSpec((tm, d), lambda i: (i, 0)),
                  pl.BlockSpec((1, 6, d), lambda i: (i // blocks_per_mod, 0, 0)),
                  pl.BlockSpec((1, d), lambda i: (0, 0)),
                  pl.BlockSpec(w.shape, lambda i: (0, 0))],
        out_specs=[pl.BlockSpec((tm, wd), lambda i: (i, 0)) for wd in widths],
        out_shape=[jax.ShapeDtypeStruct((n, wd), F32) for wd in widths],
        compiler_params=_cparams(("arbitrary",)),
        name="inproj",
    )(x2, mod, gain, w)


def _tri_masks():
    i = lax.broadcasted_iota(jnp.int32, (CHUNK, CHUNK), 0)
    j = lax.broadcasted_iota(jnp.int32, (CHUNK, CHUNK), 1)
    return i, j


def _unit_tri_inverse(l_mat, i, j):
    eye = (i == j).astype(F32)
    same16 = jnp.right_shift(i, 4) == jnp.right_shift(j, 4)
    same32 = jnp.right_shift(i, 5) == jnp.right_shift(j, 5)
    d1 = jnp.where(same16, l_mat, 0.0)
    c1 = jnp.where(jnp.logical_and(same32, jnp.logical_not(same16)), l_mat, 0.0)
    c2 = jnp.where(same32, 0.0, l_mat)
    p = eye - d1
    d2 = _dot_hi(d1, d1)
    p = p + _dot_hi(p, d2)
    d4 = _dot_hi(d2, d2)
    p = p + _dot_hi(p, d4)
    d8 = _dot_hi(d4, d4)
    p = p + _dot_hi(p, d8)
    t1 = p - _dot_hi(p, _dot_hi(c1, p))
    return t1 - _dot_hi(t1, _dot_hi(c2, t1))


def _delta_kernel(q_ref, k_ref, v_ref, qc_ref, kc_ref, vc_ref, wq_ref, wk_ref, wv_ref,
                  ba_ref, bac_ref, coef_ref, z_ref, on_ref, o_ref,
                  xp, qs, ks, vs, bs, of_s, ob_s, *, n_lat, n_ctx):
    n_all = n_ctx + n_lat
    nch_ctx = n_ctx // CHUNK
    nch_all = n_all // CHUNK
    rb = 256

    def conv_into(src_ref, w_ref, dst, dst_off, n_rows, kind):
        xp[0:8, :] = jnp.zeros((8, A_HEAD_DIM), F32)
        xp[pl.ds(8, n_rows), :] = src_ref[...]
        xp[pl.ds(8 + n_rows, 8), :] = jnp.zeros((8, A_HEAD_DIM), F32)
        w = w_ref[...]

        def body(t, carry):
            r0 = pl.multiple_of(t * rb, rb)
            blk = xp[pl.ds(r0, rb + 16), :]
            acc = jnp.zeros((rb, A_HEAD_DIM), F32)
            for tap in range(SHORT_CONV_W):
                shift = (SHORT_CONV_W // 2 - tap) % (rb + 16)
                rolled = pltpu.roll(blk, shift, axis=0) if shift else blk
                acc = acc + rolled[8:8 + rb, :] * w[tap:tap + 1, :]
            y = _silu(acc)
            if kind < 2:
                y = y * lax.rsqrt(jnp.sum(y * y, axis=-1, keepdims=True) + EPS)
            if kind == 0:
                y = y * (A_HEAD_DIM ** -0.5)
            dst[pl.ds(pl.multiple_of(dst_off + r0, rb), rb), :] = y
            return carry

        lax.fori_loop(0, n_rows // rb, body, 0)

    for kind, (lat_ref, ctx_ref, w_ref, dst) in enumerate(
            ((q_ref, qc_ref, wq_ref, qs), (k_ref, kc_ref, wk_ref, ks), (v_ref, vc_ref, wv_ref, vs))):
        conv_into(ctx_ref, w_ref, dst, 0, n_ctx, kind)
        conv_into(lat_ref, w_ref, dst, n_ctx, n_lat, kind)

    lane = lax.broadcasted_iota(jnp.int32, (1, LANES), 1)
    coef = coef_ref[0]
    neg_a = -jnp.exp(jnp.where(lane == 2, coef[0:1, :], coef[1:2, :]))
    dtb = jnp.where(lane == 2, coef[2:3, :], coef[3:4, :])

    def beta_g(raw):
        xg = raw + dtb
        softplus = jnp.maximum(xg, 0.0) + jnp.log(1.0 + jnp.exp(-jnp.abs(xg)))
        return jnp.where(lane < 2, jax.nn.sigmoid(raw), neg_a * softplus)

    bs[pl.ds(0, n_ctx), :] = beta_g(bac_ref[...])
    bs[pl.ds(n_ctx, n_lat), :] = beta_g(ba_ref[...])

    i_idx, j_idx = _tri_masks()
    ones8 = jnp.ones((8, CHUNK), F32)

    def precompute(chunk, d):
        r = pl.multiple_of(chunk * CHUNK, CHUNK)
        q = qs[pl.ds(r, CHUNK), :]
        k = ks[pl.ds(r, CHUNK), :]
        v = vs[pl.ds(r, CHUNK), :]
        bb = bs[pl.ds(r, CHUNK), :]
        beta = bb[:, d:d + 1]
        g = bb[:, 2 + d:3 + d]
        incl = (i_idx >= j_idx) if d == 0 else (i_idx <= j_idx)
        strict = (i_idx > j_idx) if d == 0 else (i_idx < j_idx)
        g_b = jnp.broadcast_to(g, (CHUNK, CHUNK))
        gc_mat = _dot_hi(incl.astype(F32), g_b)
        seen = (i_idx <= j_idx) if d == 0 else (i_idx >= j_idx)
        gr8 = _dot_hi(ones8, jnp.where(seen, g_b, 0.0))
        gr_mat = jnp.broadcast_to(gr8[0:1, :], (CHUNK, CHUNK))
        decay = jnp.where(incl, jnp.exp(jnp.minimum(gc_mat - gr_mat, 0.0)), 0.0)
        gc = gc_mat[:, 0:1]
        g_last = gc[CHUNK - 1:CHUNK, :] if d == 0 else gc[0:1, :]
        egc = jnp.exp(gc)
        kb = k * beta
        kbf = k.astype(BF16)
        l_mat = jnp.where(strict, _dot_nt(kb.astype(BF16), kbf) * decay, 0.0)
        t_inv = _unit_tri_inverse(l_mat, i_idx, j_idx)
        rhs = jnp.concatenate([v * beta, kb * egc], axis=1)
        sol = _dot_hi(t_inv, rhs)
        u = sol[:, :A_HEAD_DIM]
        w = sol[:, A_HEAD_DIM:]
        k_tail = k * jnp.exp(g_last - gc)
        q_dec = q * egc
        intra = _dot_nt(q.astype(BF16), kbf) * decay
        return u, w, k_tail, q_dec, intra, jnp.exp(g_last)

    def scan_step(state, pre):
        u, w, k_tail, q_dec, intra, e_last = pre
        ws = _dot(jnp.concatenate([w, q_dec], axis=0).astype(BF16), state.astype(BF16))
        v_new = u - ws[:CHUNK]
        v_new_b = v_new.astype(BF16)
        o = ws[CHUNK:] + _dot(intra.astype(BF16), v_new_b)
        new_state = state * e_last + _dot_tn(k_tail.astype(BF16), v_new_b)
        return new_state, o

    def bwd_chunk(step):
        return jnp.where(step < nch_ctx, nch_ctx - 1 - step, nch_all + nch_ctx - 1 - step)

    def body(step, carry):
        s_f, s_b = carry
        cf = step
        cb = bwd_chunk(step)
        s_f, o_f = scan_step(s_f, precompute(cf, 0))
        s_b, o_b = scan_step(s_b, precompute(cb, 1))

        @pl.when(step >= nch_ctx)
        def _():
            of_s[pl.ds(pl.multiple_of((cf - nch_ctx) * CHUNK, CHUNK), CHUNK), :] = o_f
            ob_s[pl.ds(pl.multiple_of((cb - nch_ctx) * CHUNK, CHUNK), CHUNK), :] = o_b

        return s_f, s_b

    zero_state = jnp.zeros((A_HEAD_DIM, A_HEAD_DIM), F32)
    lax.fori_loop(0, nch_all, body, (zero_state, zero_state))

    o = of_s[...] + ob_s[...]
    o_ref[...] = _rms(o, on_ref[...]) * _silu(z_ref[...])


def _delta_mixer(qkv, qkv_c, conv_w, ba, ba_c, coef, z, out_norm, bsz, n_lat, n_ctx):
    h = A_HEADS
    hd = A_HEAD_DIM
    n_all = n_lat + n_ctx

    def col(kind):
        return lambda b, hh: (b, kind * h + hh)

    in_specs = (
        [pl.BlockSpec((n_lat, hd), col(kd)) for kd in range(3)]
        + [pl.BlockSpec((n_ctx, hd), col(kd)) for kd in range(3)]
        + [pl.BlockSpec((SHORT_CONV_W, hd), (lambda b, hh, kd=kd: (0, kd * h + hh))) for kd in range(3)]
        + [pl.BlockSpec((n_lat, LANES), lambda b, hh: (b, hh)),
           pl.BlockSpec((n_ctx, LANES), lambda b, hh: (b, hh)),
           pl.BlockSpec((1, 8, LANES), lambda b, hh: (hh, 0, 0)),
           pl.BlockSpec((n_lat, hd), lambda b, hh: (b, hh)),
           pl.BlockSpec((1, hd), lambda b, hh: (0, 0))])
    return pl.pallas_call(
        functools.partial(_delta_kernel, n_lat=n_lat, n_ctx=n_ctx),
        grid=(bsz, h),
        in_specs=in_specs,
        out_specs=pl.BlockSpec((n_lat, hd), lambda b, hh: (b, hh)),
        out_shape=jax.ShapeDtypeStruct((bsz * n_lat, h * hd), F32),
        scratch_shapes=[pltpu.VMEM((n_lat + 16, hd), F32),
                        pltpu.VMEM((n_all, hd), F32), pltpu.VMEM((n_all, hd), F32), pltpu.VMEM((n_all, hd), F32),
                        pltpu.VMEM((n_all, LANES), F32),
                        pltpu.VMEM((n_lat, hd), F32), pltpu.VMEM((n_lat, hd), F32)],
        compiler_params=_cparams(("arbitrary", "arbitrary")),
        name="delta",
    )(qkv, qkv, qkv, qkv_c, qkv_c, qkv_c, conv_w, conv_w, conv_w, ba, ba_c, coef, z, out_norm)


def _head_rms(x, gain_row, head_dim):
    outs = []
    for s in range(x.shape[1] // LANES):
        xs = x[:, s * LANES:(s + 1) * LANES]
        lane = lax.broadcasted_iota(jnp.int32, xs.shape, 1)
        sq = xs * xs
        scale = jnp.zeros_like(xs)
        for part in range(LANES // head_dim):
            m = jnp.logical_and(lane >= part * head_dim, lane < (part + 1) * head_dim)
            ms = jnp.sum(jnp.where(m, sq, 0.0), axis=-1, keepdims=True) * (1.0 / head_dim)
            scale = jnp.where(m, lax.rsqrt(ms + EPS), scale)
        outs.append(xs * scale)
    y = outs[0] if len(outs) == 1 else jnp.concatenate(outs, axis=1)
    return y * gain_row


def _rope(x, cos, sin_lo, sin_hi):
    width = x.shape[1]
    fwd = pltpu.roll(x, width - ROPE_AXIS_PAIRS, axis=1)
    back = pltpu.roll(x, ROPE_AXIS_PAIRS, axis=1)
    return x * cos + fwd * sin_lo + back * sin_hi


def _kvprep_kernel(k_ref, v_ref, g_ref, cos_ref, slo_ref, shi_ref, ko_ref, vo_ref):
    k = _head_rms(k_ref[0], g_ref[...], B_HEAD_DIM)
    k = _rope(k, cos_ref[...], slo_ref[...], shi_ref[...]).astype(BF16)
    v = v_ref[0].astype(BF16)
    for hh in range(B_KV_HEADS):
        ko_ref[0, hh] = k[:, hh * B_HEAD_DIM:(hh + 1) * B_HEAD_DIM]
        vo_ref[0, hh] = v[:, hh * B_HEAD_DIM:(hh + 1) * B_HEAD_DIM]


def _kvprep(k_all, v_all, gain, cos, slo, shi, ts):
    bsz, s_len, w = k_all.shape
    row = pl.BlockSpec((1, ts, w), lambda b, i: (b, i, 0))
    tab = pl.BlockSpec((ts, w), lambda b, i: (i, 0))
    out = pl.BlockSpec((1, B_KV_HEADS, ts, B_HEAD_DIM), lambda b, i: (b, 0, i, 0))
    shape = jax.ShapeDtypeStruct((bsz, B_KV_HEADS, s_len, B_HEAD_DIM), BF16)
    return pl.pallas_call(
        _kvprep_kernel,
        grid=(bsz, s_len // ts),
        in_specs=[row, row, pl.BlockSpec((1, w), lambda b, i: (0, 0)), tab, tab, tab],
        out_specs=[out, out],
        out_shape=[shape, shape],
        compiler_params=_cparams(("arbitrary", "arbitrary")),
        name="kvprep",
    )(k_all, v_all, gain, cos, slo, shi)


def _attn_kernel(q_ref, g_ref, cos_ref, slo_ref, shi_ref, k_ref, v_ref, o_ref):
    q = _head_rms(q_ref[...], g_ref[...], B_HEAD_DIM)
    q = (_rope(q, cos_ref[...], slo_ref[...], shi_ref[...]) * (B_HEAD_DIM ** -0.5)).astype(BF16)
    k = k_ref[0, 0]
    v = v_ref[0, 0]
    outs = []
    for g in range(B_GROUP):
        s = _dot_nt(q[:, g * B_HEAD_DIM:(g + 1) * B_HEAD_DIM], k)
        p = jnp.exp(s - jnp.max(s, axis=-1, keepdims=True))
        denom = jnp.sum(p, axis=-1, keepdims=True)
        outs.append(_dot(p.astype(BF16), v) / denom)
    o_ref[...] = jnp.concatenate(outs, axis=1)


def _attention(qb, gain, cos, slo, shi, k_hm, v_hm, n_lat, tq):
    n, _ = qb.shape
    bsz, _, s_len, _ = k_hm.shape
    gw = B_GROUP * B_HEAD_DIM
    nq = n_lat // tq
    kv = pl.BlockSpec((1, 1, s_len, B_HEAD_DIM), lambda b, kh, i: (b, kh, 0, 0))
    tab = pl.BlockSpec((tq, gw), lambda b, kh, i: (i, 0))
    return pl.pallas_call(
        _attn_kernel,
        grid=(bsz, B_KV_HEADS, nq),
        in_specs=[pl.BlockSpec((tq, gw), lambda b, kh, i: (b * nq + i, kh)),
                  pl.BlockSpec((1, gw), lambda b, kh, i: (0, 0)), tab, tab, tab, kv, kv],
        out_specs=pl.BlockSpec((tq, gw), lambda b, kh, i: (b * nq + i, kh)),
        out_shape=jax.ShapeDtypeStruct((n, B_WIDTH), F32),
        compiler_params=_cparams(("arbitrary", "arbitrary", "arbitrary")),
        name="attn",
    )(qb, gain, cos, slo, shi, k_hm, v_hm)


def _gated_residual(x, y, mod_ref, gain_ref, gate_row):
    return x + mod_ref[0, gate_row:gate_row + 1, :] * _rms(y, gain_ref[...])


def _outproj_kernel(ya_ref, yb_ref, x_ref, mod_ref, g_ref, wa_ref, wb_ref, o_ref):
    y = _dot(ya_ref[...].astype(BF16), wa_ref[...]) + _dot(yb_ref[...].astype(BF16), wb_ref[...])
    o_ref[...] = _gated_residual(x_ref[...], y, mod_ref, g_ref, 2)


def _outproj(ya, yb, x2, mod, gain, wa, wb, rows_per_mod, tm):
    n, d = x2.shape
    bpm = rows_per_mod // tm
    return pl.pallas_call(
        _outproj_kernel,
        grid=(n // tm,),
        in_specs=[pl.BlockSpec((tm, ya.shape[1]), lambda i: (i, 0)),
                  pl.BlockSpec((tm, yb.shape[1]), lambda i: (i, 0)),
                  pl.BlockSpec((tm, d), lambda i: (i, 0)),
                  pl.BlockSpec((1, 6, d), lambda i: (i // bpm, 0, 0)),
                  pl.BlockSpec((1, d), lambda i: (0, 0)),
                  pl.BlockSpec(wa.shape, lambda i: (0, 0)),
                  pl.BlockSpec(wb.shape, lambda i: (0, 0))],
        out_specs=pl.BlockSpec((tm, d), lambda i: (i, 0)),
        out_shape=jax.ShapeDtypeStruct((n, d), F32),
        compiler_params=_cparams(("arbitrary",)),
        name="outproj",
    )(ya, yb, x2, mod, gain, wa, wb)


def _ffn_kernel(x_ref, mod_ref, gpre_ref, gpost_ref, wi_ref, wo_ref, o_ref, *, hidden, hc):
    x = x_ref[...]
    a = _modulated(x, mod_ref, gpre_ref, 3).astype(BF16)
    acc = jnp.zeros(x.shape, F32)
    for c in range(hidden // hc):
        gate = _dot(a, wi_ref[:, c * hc:(c + 1) * hc])
        up = _dot(a, wi_ref[:, hidden + c * hc:hidden + (c + 1) * hc])
        acc = acc + _dot((_silu(gate) * up).astype(BF16), wo_ref[c * hc:(c + 1) * hc, :])
    o_ref[...] = _gated_residual(x, acc, mod_ref, gpost_ref, 5)


def _ffn(x2, mod, gpre, gpost, wi, wo, rows_per_mod, tm):
    n, d = x2.shape
    hidden = wo.shape[0]
    bpm = rows_per_mod // tm
    return pl.pallas_call(
        functools.partial(_ffn_kernel, hidden=hidden, hc=256),
        grid=(n // tm,),
        in_specs=[pl.BlockSpec((tm, d), lambda i: (i, 0)),
                  pl.BlockSpec((1, 6, d), lambda i: (i // bpm, 0, 0)),
                  pl.BlockSpec((1, d), lambda i: (0, 0)),
                  pl.BlockSpec((1, d), lambda i: (0, 0)),
                  pl.BlockSpec(wi.shape, lambda i: (0, 0)),
                  pl.BlockSpec(wo.shape, lambda i: (0, 0))],
        out_specs=pl.BlockSpec((tm, d), lambda i: (i, 0)),
        out_shape=jax.ShapeDtypeStruct((n, d), F32),
        compiler_params=_cparams(("arbitrary",)),
        name="ffn",
    )(x2, mod, gpre, gpost, wi, wo)


def _confin_kernel(x_ref, mod_ref, g_ref, w_ref, b_ref, o_ref, *, width):
    a = _modulated(x_ref[...], mod_ref, g_ref, 0).astype(BF16)
    val = _dot(a, w_ref[:, :width]) + b_ref[:, :width]
    gate = _dot(a, w_ref[:, width:]) + b_ref[:, width:]
    o_ref[...] = val * jax.nn.sigmoid(gate)


def _confin(x2, mod, gain, w, b, rows_per_mod, tm):
    n, d = x2.shape
    width = w.shape[1] // 2
    bpm = rows_per_mod // tm
    return pl.pallas_call(
        functools.partial(_confin_kernel, width=width),
        grid=(n // tm,),
        in_specs=[pl.BlockSpec((tm, d), lambda i: (i, 0)),
                  pl.BlockSpec((1, 6, d), lambda i: (i // bpm, 0, 0)),
                  pl.BlockSpec((1, d), lambda i: (0, 0)),
                  pl.BlockSpec(w.shape, lambda i: (0, 0)),
                  pl.BlockSpec((1, 2 * width), lambda i: (0, 0))],
        out_specs=pl.BlockSpec((tm, width), lambda i: (i, 0)),
        out_shape=jax.ShapeDtypeStruct((n, width), F32),
        compiler_params=_cparams(("arbitrary",)),
        name="confin",
    )(x2, mod, gain, w, b)


def _confout_kernel(u_ref, up_ref, un_ref, x_ref, mod_ref, g_ref, dww_ref, dwb_ref, lng_ref, lnb_ref,
                    w_ref, b_ref, o_ref, ext, conv, *, tm, tiles_per_seq):
    i = pl.program_id(0)
    pos = i % tiles_per_seq
    width = u_ref.shape[1]
    halo = CONF_HALO
    ext[pl.ds(halo, tm), :] = u_ref[...]
    ext[0:halo, :] = jnp.where(pos == 0, 0.0, up_ref[...])
    ext[pl.ds(halo + tm, halo), :] = jnp.where(pos == tiles_per_seq - 1, 0.0, un_ref[...])

    rb = 64
    pad = CONF_KERNEL // 2

    def col_body(c, carry):
        cs = pl.ds(pl.multiple_of(c * LANES, LANES), LANES)
        w = dww_ref[:, cs]
        for r in range(tm // rb):
            acc = jnp.zeros((rb, LANES), F32)
            for tap in range(CONF_KERNEL):
                acc = acc + ext[pl.ds(r * rb + halo - pad + tap, rb), cs] * w[tap:tap + 1, :]
            conv[pl.ds(r * rb, rb), cs] = acc
        return carry

    lax.fori_loop(0, width // LANES, col_body, 0)

    y = conv[...] + dwb_ref[...]
    mu = jnp.mean(y, axis=-1, keepdims=True)
    yc = y - mu
    var = jnp.mean(yc * yc, axis=-1, keepdims=True)
    y = _silu(yc * lax.rsqrt(var + EPS) * lng_ref[...] + lnb_ref[...])
    out = _dot(y.astype(BF16), w_ref[...]) + b_ref[...]
    o_ref[...] = _gated_residual(x_ref[...], out, mod_ref, g_ref, 2)


def _confout(u, x2, mod, gain, dww, dwb, lng, lnb, w, b, rows_per_mod, tm):
    n, d = x2.shape
    width = u.shape[1]
    tiles_per_seq = rows_per_mod // tm
    hb = tm // CONF_HALO
    n_halo_blocks = n // CONF_HALO
    vec = lambda wd: pl.BlockSpec((1, wd), lambda i: (0, 0))
    return pl.pallas_call(
        functools.partial(_confout_kernel, tm=tm, tiles_per_seq=tiles_per_seq),
        grid=(n // tm,),
        in_specs=[pl.BlockSpec((tm, width), lambda i: (i, 0)),
                  pl.BlockSpec((CONF_HALO, width), lambda i: (jnp.maximum(i * hb - 1, 0), 0)),
                  pl.BlockSpec((CONF_HALO, width), lambda i: (jnp.minimum((i + 1) * hb, n_halo_blocks - 1), 0)),
                  pl.BlockSpec((tm, d), lambda i: (i, 0)),
                  pl.BlockSpec((1, 6, d), lambda i: (i // tiles_per_seq, 0, 0)),
                  vec(d),
                  pl.BlockSpec(dww.shape, lambda i: (0, 0)),
                  vec(width), vec(width), vec(width),
                  pl.BlockSpec(w.shape, lambda i: (0, 0)),
                  vec(d)],
        out_specs=pl.BlockSpec((tm, d), lambda i: (i, 0)),
        out_shape=jax.ShapeDtypeStruct((n, d), F32),
        scratch_shapes=[pltpu.VMEM((tm + 2 * CONF_HALO, width), F32), pltpu.VMEM((tm, width), F32)],
        compiler_params=_cparams(("arbitrary",)),
        name="confout",
    )(u, u, u, x2, mod, gain, dww, dwb, lng, lnb, w, b)


def _rope_tables(n_tokens, n_ctx, reps):
    rows = n_tokens // GRID_W
    row = jnp.broadcast_to(jnp.arange(rows, dtype=F32)[:, None], (rows, GRID_W)).reshape(n_tokens)
    col = jnp.broadcast_to(jnp.arange(GRID_W, dtype=F32)[None, :], (rows, GRID_W)).reshape(n_tokens)
    inv_freq = ROPE_THETA ** (-jnp.arange(ROPE_AXIS_PAIRS, dtype=F32) / ROPE_AXIS_PAIRS)
    ang_r = row[:, None] * inv_freq
    ang_c = col[:, None] * inv_freq
    ang = jnp.concatenate([ang_r, ang_r, ang_c, ang_c], axis=-1)
    cos, sin = jnp.cos(ang), jnp.sin(ang)
    first_half = (jnp.arange(B_HEAD_DIM) % (2 * ROPE_AXIS_PAIRS)) < ROPE_AXIS_PAIRS
    sin_lo = jnp.where(first_half, -sin, 0.0)
    sin_hi = jnp.where(first_half, 0.0, sin)
    if n_ctx:
        cos = jnp.concatenate([jnp.ones((n_ctx, B_HEAD_DIM), F32), cos], axis=0)
        zeros = jnp.zeros((n_ctx, B_HEAD_DIM), F32)
        sin_lo = jnp.concatenate([zeros, sin_lo], axis=0)
        sin_hi = jnp.concatenate([zeros, sin_hi], axis=0)
    tile = lambda t: jnp.tile(t, (1, reps))
    return tile(cos), tile(sin_lo), tile(sin_hi)


def _hybrid_in_weight(w_in):
    off_z = 3 * A_WIDTH
    off_ba = off_z + A_WIDTH
    off_q = off_ba + 4 * A_HEADS
    off_k = off_q + B_WIDTH
    off_v = off_k + B_KV_WIDTH
    d = w_in.shape[0]
    ba = w_in[:, off_ba:off_q].reshape(d, 2, 2, A_HEADS)
    ba = ba.transpose(0, 3, 1, 2).reshape(d, A_HEADS, 4)
    ba = jnp.pad(ba, ((0, 0), (0, 0), (0, LANES - 4))).reshape(d, A_HEADS * LANES)
    w = jnp.concatenate([w_in[:, :off_z], w_in[:, off_z:off_ba], w_in[:, off_q:off_k], w_in[:, off_k:off_v],
                         w_in[:, off_v:], ba], axis=1)
    widths = (3 * A_WIDTH, A_WIDTH, B_WIDTH, B_KV_WIDTH, B_KV_WIDTH, A_HEADS * LANES)
    return w.astype(BF16), widths


def _row(v):
    return v.reshape(1, -1)


def kernel(x, c, ctx, c_ctx, w_mod, b_mod, g_mix_pre, g_mix_post, g_ffn_pre, g_ffn_post, w_ffn_in, w_ffn_out,
           hyb_w_in, hyb_conv_w, hyb_a_log, hyb_dt_bias, hyb_out_norm, hyb_q_norm, hyb_k_norm, hyb_w_out,
           conf_w_in, conf_b_in, conf_dw_w, conf_dw_b, conf_ln_g, conf_ln_b, conf_w_out, conf_b_out):
    bsz, n_lat, d = x.shape
    n_ctx = ctx.shape[1]
    depth = w_mod.shape[0]
    n = bsz * n_lat
    tm = 512

    cond = jnp.zeros((8, d), F32).at[:bsz].set(c).at[bsz].set(c_ctx)
    mods = _ada_terms(cond, w_mod, b_mod).reshape(depth, 8, 6, d)

    h = x.reshape(n, d)
    hc = ctx.reshape(bsz * n_ctx, d)
    for layer in range(depth):
        idx = layer // 2
        mod = mods[layer, :bsz]
        mod_ctx = mods[layer, bsz:bsz + 1]
        if layer % 2 == 0:
            w_in, widths = _hybrid_in_weight(hyb_w_in[idx])
            gpre = _row(g_mix_pre[layer])
            qkv, z, qb, kb, vb, ba = _inproj(h, mod, gpre, w_in, widths, n_lat, tm)
            qkv_c, _, _, kb_c, vb_c, ba_c = _inproj(hc, mod_ctx, gpre, w_in, widths, bsz * n_ctx, n_ctx)

            coef = jnp.concatenate([hyb_a_log[idx], hyb_dt_bias[idx]], axis=0)
            coef = jnp.pad(coef.T[:, :, None], ((0, 0), (0, 4), (0, 0)))
            coef = jnp.broadcast_to(coef, (A_HEADS, 8, LANES))
            ya = _delta_mixer(qkv, qkv_c, hyb_conv_w[idx], ba, ba_c, coef, z, _row(hyb_out_norm[idx]),
                              bsz, n_lat, n_ctx)

            k_all = jnp.concatenate([kb_c.reshape(bsz, n_ctx, -1), kb.reshape(bsz, n_lat, -1)], axis=1)
            v_all = jnp.concatenate([vb_c.reshape(bsz, n_ctx, -1), vb.reshape(bsz, n_lat, -1)], axis=1)
            cos_k, slo_k, shi_k = _rope_tables(n_lat, n_ctx, B_KV_HEADS)
            k_hm, v_hm = _kvprep(k_all, v_all, _row(jnp.tile(hyb_k_norm[idx], B_KV_HEADS)),
                                 cos_k, slo_k, shi_k, n_ctx)
            cos_q, slo_q, shi_q = _rope_tables(n_lat, 0, B_GROUP)
            yb = _attention(qb, _row(jnp.tile(hyb_q_norm[idx], B_GROUP)), cos_q, slo_q, shi_q,
                            k_hm, v_hm, n_lat, 256)

            w_out = hyb_w_out[idx].astype(BF16)
            h = _outproj(ya, yb, h, mod, _row(g_mix_post[layer]), w_out[:A_WIDTH], w_out[A_WIDTH:], n_lat, tm)
        else:
            u = _confin(h, mod, _row(g_mix_pre[layer]), conf_w_in[idx].astype(BF16), _row(conf_b_in[idx]),
                        n_lat, tm)
            h = _confout(u, h, mod, _row(g_mix_post[layer]), conf_dw_w[idx], _row(conf_dw_b[idx]),
                         _row(conf_ln_g[idx]), _row(conf_ln_b[idx]), conf_w_out[idx].astype(BF16),
                         _row(conf_b_out[idx]), n_lat, 256)
        h = _ffn(h, mod, _row(g_ffn_pre[layer]), _row(g_ffn_post[layer]), w_ffn_in[layer].astype(BF16),
                 w_ffn_out[layer].astype(BF16), n_lat, 256)
        assert not any(j % 2 == 0 for j in range(layer + 1, depth)), "context advance not implemented"
    return h.reshape(bsz, n_lat, d)
```

```python
import functools
import math

import jax
import jax.numpy as jnp
import numpy as np
from jax import lax
from jax.experimental import pallas as pl
from jax.experimental.pallas import tpu as pltpu

F32 = jnp.float32
BF16 = jnp.bfloat16
HIGHEST = lax.Precision.HIGHEST

EPS = 1e-6
GRID_W = 64
ROPE_THETA = 10000.0
A_HEADS = 4
A_HEAD_DIM = 128
A_WIDTH = A_HEADS * A_HEAD_DIM
SHORT_CONV_W = 5
CHUNK = 64
B_Q_HEADS = 8
B_KV_HEADS = 2
B_HEAD_DIM = 64
B_GROUP = B_Q_HEADS // B_KV_HEADS
B_WIDTH = B_Q_HEADS * B_HEAD_DIM
B_KV_WIDTH = B_KV_HEADS * B_HEAD_DIM
ROPE_AXIS_PAIRS = B_HEAD_DIM // 4
CONF_KERNEL = 31
CONF_HALO = 16
LANES = 128
VMEM_LIMIT = 56 * 1024 * 1024


def _cparams(sem):
    return pltpu.CompilerParams(dimension_semantics=sem, vmem_limit_bytes=VMEM_LIMIT)


def _silu(x):
    return x * jax.nn.sigmoid(x)


def _dot(a, b):
    return jnp.dot(a, b, preferred_element_type=F32)


def _dot_hi(a, b):
    return jnp.dot(a, b, preferred_element_type=F32, precision=HIGHEST)


def _dot_nt(a, b):
    return lax.dot_general(a, b, (((1,), (1,)), ((), ())), preferred_element_type=F32)


def _dot_tn(a, b):
    return lax.dot_general(a, b, (((0,), (0,)), ((), ())), preferred_element_type=F32)


def _rms(x, gain):
    return x * lax.rsqrt(jnp.mean(x * x, axis=-1, keepdims=True) + EPS) * gain


def _ada_kernel(c_ref, w_ref, b_ref, o_ref):
    o_ref[0] = _dot_hi(_silu(c_ref[...]), w_ref[0]) + b_ref[0]


def _ada_terms(cond, w_mod, b_mod):
    depth, d, n6 = w_mod.shape
    tn = 1536
    return pl.pallas_call(
        _ada_kernel,
        grid=(depth, n6 // tn),
        in_specs=[pl.BlockSpec((8, d), lambda l, j: (0, 0)),
                  pl.BlockSpec((1, d, tn), lambda l, j: (l, 0, j)),
                  pl.BlockSpec((1, 1, tn), lambda l, j: (l, 0, j))],
        out_specs=pl.BlockSpec((1, 8, tn), lambda l, j: (l, 0, j)),
        out_shape=jax.ShapeDtypeStruct((depth, 8, n6), F32),
        compiler_params=_cparams(("arbitrary", "arbitrary")),
        name="ada",
    )(cond, w_mod, b_mod.reshape(depth, 1, n6))


def _modulated(x, mod_ref, gain_ref, shift_row):
    y = _rms(x, gain_ref[...])
    return y * (1.0 + mod_ref[0, shift_row + 1:shift_row + 2, :]) + mod_ref[0, shift_row:shift_row + 1, :]


def _inproj_kernel(x_ref, mod_ref, g_ref, w_ref, *o_refs, widths):
    a = _modulated(x_ref[...], mod_ref, g_ref, 0).astype(BF16)
    off = 0
    for o_ref, wd in zip(o_refs, widths):
        o_ref[...] = _dot(a, w_ref[:, off:off + wd])
        off += wd


def _inproj(x2, mod, gain, w, widths, rows_per_mod, tm):
    n, d = x2.shape
    blocks_per_mod = rows_per_mod // tm
    return pl.pallas_call(
        functools.partial(_inproj_kernel, widths=widths),
        grid=(n // tm,),
        in_specs=[pl.BlockSpec((tm, d), lambda i: (i, 0)),
                  pl.BlockSpec((1, 6, d), lambda i: (i // blocks_per_mod, 0, 0)),
                  pl.BlockSpec((1, d), lambda i: (0, 0)),
                  pl.BlockSpec(w.shape, lambda i: (0, 0))],
        out_specs=[pl.BlockSpec((tm, wd), lambda i: (i, 0)) for wd in widths],
        out_shape=[jax.ShapeDtypeStruct((n, wd), F32) for wd in widths],
        compiler_params=_cparams(("arbitrary",)),
        name="inproj",
    )(x2, mod, gain, w)


def _tri_masks():
    i = lax.broadcasted_iota(jnp.int32, (CHUNK, CHUNK), 0)
    j = lax.broadcasted_iota(jnp.int32, (CHUNK, CHUNK), 1)
    return i, j


def _unit_tri_inverse(l_mat, i, j):
    eye = (i == j).astype(F32)
    same16 = jnp.right_shift(i, 4) == jnp.right_shift(j, 4)
    same32 = jnp.right_shift(i, 5) == jnp.right_shift(j, 5)
    d1 = jnp.where(same16, l_mat, 0.0)
    c1 = jnp.where(jnp.logical_and(same32, jnp.logical_not(same16)), l_mat, 0.0)
    c2 = jnp.where(same32, 0.0, l_mat)
    p = eye - d1
    d2 = _dot_hi(d1, d1)
    p = p + _dot_hi(p, d2)
    d4 = _dot_hi(d2, d2)
    p = p + _dot_hi(p, d4)
    d8 = _dot_hi(d4, d4)
    p = p + _dot_hi(p, d8)
    t1 = p - _dot_hi(p, _dot_hi(c1, p))
    return t1 - _dot_hi(t1, _dot_hi(c2, t1))


def _delta_kernel(q_ref, k_ref, v_ref, qc_ref, kc_ref, vc_ref, wq_ref, wk_ref, wv_ref,
                  ba_ref, bac_ref, coef_ref, z_ref, on_ref, o_ref,
                  xp, qs, ks, vs, bs, of_s, ob_s, *, n_lat, n_ctx):
    n_all = n_ctx + n_lat
    nch_ctx = n_ctx // CHUNK
    nch_all = n_all // CHUNK
    rb = 256

    def conv_into(src_ref, w_ref, dst, dst_off, n_rows, kind):
        xp[0:8, :] = jnp.zeros((8, A_HEAD_DIM), F32)
        xp[pl.ds(8, n_rows), :] = src_ref[...]
        xp[pl.ds(8 + n_rows, 8), :] = jnp.zeros((8, A_HEAD_DIM), F32)
        w = w_ref[...]

        def body(t, carry):
            r0 = pl.multiple_of(t * rb, rb)
            blk = xp[pl.ds(r0, rb + 16), :]
            acc = jnp.zeros((rb, A_HEAD_DIM), F32)
            for tap in range(SHORT_CONV_W):
                shift = (SHORT_CONV_W // 2 - tap) % (rb + 16)
                rolled = pltpu.roll(blk, shift, axis=0) if shift else blk
                acc = acc + rolled[8:8 + rb, :] * w[tap:tap + 1, :]
            y = _silu(acc)
            if kind < 2:
                y = y * lax.rsqrt(jnp.sum(y * y, axis=-1, keepdims=True) + EPS)
            if kind == 0:
                y = y * (A_HEAD_DIM ** -0.5)
            dst[pl.ds(pl.multiple_of(dst_off + r0, rb), rb), :] = y
            return carry

        lax.fori_loop(0, n_rows // rb, body, 0)

    for kind, (lat_ref, ctx_ref, w_ref, dst) in enumerate(
            ((q_ref, qc_ref, wq_ref, qs), (k_ref, kc_ref, wk_ref, ks), (v_ref, vc_ref, wv_ref, vs))):
        conv_into(ctx_ref, w_ref, dst, 0, n_ctx, kind)
        conv_into(lat_ref, w_ref, dst, n_ctx, n_lat, kind)

    lane = lax.broadcasted_iota(jnp.int32, (1, LANES), 1)
    coef = coef_ref[0]
    neg_a = -jnp.exp(jnp.where(lane == 2, coef[0:1, :], coef[1:2, :]))
    dtb = jnp.where(lane == 2, coef[2:3, :], coef[3:4, :])

    def beta_g(raw):
        xg = raw + dtb
        softplus = jnp.maximum(xg, 0.0) + jnp.log(1.0 + jnp.exp(-jnp.abs(xg)))
        return jnp.where(lane < 2, jax.nn.sigmoid(raw), neg_a * softplus)

    bs[pl.ds(0, n_ctx), :] = beta_g(bac_ref[...])
    bs[pl.ds(n_ctx, n_lat), :] = beta_g(ba_ref[...])

    i_idx, j_idx = _tri_masks()
    ones8 = jnp.ones((8, CHUNK), F32)

    def precompute(chunk, d):
        r = pl.multiple_of(chunk * CHUNK, CHUNK)
        q = qs[pl.ds(r, CHUNK), :]
        k = ks[pl.ds(r, CHUNK), :]
        v = vs[pl.ds(r, CHUNK), :]
        bb = bs[pl.ds(r, CHUNK), :]
        beta = bb[:, d:d + 1]
        g = bb[:, 2 + d:3 + d]
        incl = (i_idx >= j_idx) if d == 0 else (i_idx <= j_idx)
        strict = (i_idx > j_idx) if d == 0 else (i_idx < j_idx)
        g_b = jnp.broadcast_to(g, (CHUNK, CHUNK))
        gc_mat = _dot_hi(incl.astype(F32), g_b)
        seen = (i_idx <= j_idx) if d == 0 else (i_idx >= j_idx)
        gr8 = _dot_hi(ones8, jnp.where(seen, g_b, 0.0))
        gr_mat = jnp.broadcast_to(gr8[0:1, :], (CHUNK, CHUNK))
        decay = jnp.where(incl, jnp.exp(jnp.minimum(gc_mat - gr_mat, 0.0)), 0.0)
        gc = gc_mat[:, 0:1]
        g_last = gc[CHUNK - 1:CHUNK, :] if d == 0 else gc[0:1, :]
        egc = jnp.exp(gc)
        kb = k * beta
        kbf = k.astype(BF16)
        l_mat = jnp.where(strict, _dot_nt(kb.astype(BF16), kbf) * decay, 0.0)
        t_inv = _unit_tri_inverse(l_mat, i_idx, j_idx)
        rhs = jnp.concatenate([v * beta, kb * egc], axis=1)
        sol = _dot_hi(t_inv, rhs)
        u = sol[:, :A_HEAD_DIM]
        w = sol[:, A_HEAD_DIM:]
        k_tail = k * jnp.exp(g_last - gc)
        q_dec = q * egc
        intra = _dot_nt(q.astype(BF16), kbf) * decay
        return u, w, k_tail, q_dec, intra, jnp.exp(g_last)

    def scan_step(state, pre):
        u, w, k_tail, q_dec, intra, e_last = pre
        ws = _dot(jnp.concatenate([w, q_dec], axis=0).astype(BF16), state.astype(BF16))
        v_new = u - ws[:CHUNK]
        v_new_b = v_new.astype(BF16)
        o = ws[CHUNK:] + _dot(intra.astype(BF16), v_new_b)
        new_state = state * e_last + _dot_tn(k_tail.astype(BF16), v_new_b)
        return new_state, o

    def bwd_chunk(step):
        return jnp.where(step < nch_ctx, nch_ctx - 1 - step, nch_all + nch_ctx - 1 - step)

    def body(step, carry):
        s_f, s_b = carry
        cf = step
        cb = bwd_chunk(step)
        s_f, o_f = scan_step(s_f, precompute(cf, 0))
        s_b, o_b = scan_step(s_b, precompute(cb, 1))

        @pl.when(step >= nch_ctx)
        def _():
            of_s[pl.ds(pl.multiple_of((cf - nch_ctx) * CHUNK, CHUNK), CHUNK), :] = o_f
            ob_s[pl.ds(pl.multiple_of((cb - nch_ctx) * CHUNK, CHUNK), CHUNK), :] = o_b

        return s_f, s_b

    zero_state = jnp.zeros((A_HEAD_DIM, A_HEAD_DIM), F32)
    lax.fori_loop(0, nch_all, body, (zero_state, zero_state))

    o = of_s[...] + ob_s[...]
    o_ref[...] = _rms(o, on_ref[...]) * _silu(z_ref[...])


def _delta_mixer(qkv, qkv_c, conv_w, ba, ba_c, coef, z, out_norm, bsz, n_lat, n_ctx):
    h = A_HEADS
    hd = A_HEAD_DIM
    n_all = n_lat + n_ctx

    def col(kind):
        return lambda b, hh: (b, kind * h + hh)

    in_specs = (
        [pl.BlockSpec((n_lat, hd), col(kd)) for kd in range(3)]
        + [pl.BlockSpec((n_ctx, hd), col(kd)) for kd in range(3)]
        + [pl.BlockSpec((SHORT_CONV_W, hd), (lambda b, hh, kd=kd: (0, kd * h + hh))) for kd in range(3)]
        + [pl.BlockSpec((n_lat, LANES), lambda b, hh: (b, hh)),
           pl.BlockSpec((n_ctx, LANES), lambda b, hh: (b, hh)),
           pl.BlockSpec((1, 8, LANES), lambda b, hh: (hh, 0, 0)),
           pl.BlockSpec((n_lat, hd), lambda b, hh: (b, hh)),
           pl.BlockSpec((1, hd), lambda b, hh: (0, 0))])
    return pl.pallas_call(
        functools.partial(_delta_kernel, n_lat=n_lat, n_ctx=n_ctx),
        grid=(bsz, h),
        in_specs=in_specs,
        out_specs=pl.BlockSpec((n_lat, hd), lambda b, hh: (b, hh)),
        out_shape=jax.ShapeDtypeStruct((bsz * n_lat, h * hd), F32),
        scratch_shapes=[pltpu.VMEM((n_lat + 16, hd), F32),
                        pltpu.VMEM((n_all, hd), F32), pltpu.VMEM((n_all, hd), F32), pltpu.VMEM((n_all, hd), F32),
                        pltpu.VMEM((n_all, LANES), F32),
                        pltpu.VMEM((n_lat, hd), F32), pltpu.VMEM((n_lat, hd), F32)],
        compiler_params=_cparams(("arbitrary", "arbitrary")),
        name="delta",
    )(qkv, qkv, qkv, qkv_c, qkv_c, qkv_c, conv_w, conv_w, conv_w, ba, ba_c, coef, z, out_norm)


DELTA_BLOCK = 4
DELTA_ROWS = DELTA_BLOCK * CHUNK
CONV_HALO = 8


def _split_bf16(a):
    hi = a.astype(BF16)
    return hi, (a - hi.astype(F32)).astype(BF16)


def _unit_tri_inverses(l_mats, i, j):
    eye = (i == j).astype(F32)
    same16 = jnp.right_shift(i, 4) == jnp.right_shift(j, 4)
    same32 = jnp.right_shift(i, 5) == jnp.right_shift(j, 5)
    off32 = jnp.logical_and(same32, jnp.logical_not(same16))
    b = lambda a: a.astype(BF16)
    each = lambda f, *ls: [f(*xs) for xs in zip(*ls)]
    d1 = each(lambda l: b(jnp.where(same16, l, 0.0)), l_mats)
    p = each(lambda d: eye - d.astype(F32), d1)
    dk = d1
    for _ in range(3):
        dk = each(lambda d: b(_dot(d, d)), dk)
        p = each(lambda pp, d: pp + _dot(b(pp), d), p, dk)
    for sel in (off32, jnp.logical_not(same32)):
        cm = each(lambda l: b(jnp.where(sel, l, 0.0)), l_mats)
        pb = each(b, p)
        inner = each(lambda c, q: b(_dot(c, q)), cm, pb)
        p = each(lambda pp, q, m: pp - _dot(q, m), p, pb, inner)
    l_split = each(_split_bf16, l_mats)
    t_split = each(_split_bf16, p)
    resid = each(lambda t0, ls, ts: eye - t0 - (_dot(ls[0], ts[0]) + _dot(ls[0], ts[1]) + _dot(ls[1], ts[0])),
                 p, l_split, t_split)
    return each(lambda t0, ts, r: t0 + _dot(ts[0], b(r)), p, t_split, resid)


def _dprep_kernel(q_ref, k_ref, v_ref, qp_ref, kp_ref, vp_ref, qn_ref, kn_ref, vn_ref, qc_ref, kc_ref, vc_ref,
                  wq_ref, wk_ref, wv_ref, ba_ref, bac_ref, coef_ref,
                  u_ref, w_ref, kt_ref, qd_ref, in_ref, el_ref, *, n_blocks):
    blk = pl.program_id(2)
    is_ctx = blk == 0
    rows = DELTA_ROWS
    ext_rows = rows + 2 * CONV_HALO

    def conv(main_ref, ctx_ref, prev_ref, next_ref, w_ref, kind):
        x = jnp.where(is_ctx, ctx_ref[...], main_ref[...])
        prev = jnp.where(blk <= 1, 0.0, prev_ref[...])
        nxt = jnp.where(jnp.logical_or(is_ctx, blk == n_blocks - 1), 0.0, next_ref[...])
        ext = jnp.concatenate([prev, x, nxt], axis=0)
        w = w_ref[...]
        acc = jnp.zeros((rows, A_HEAD_DIM), F32)
        for tap in range(SHORT_CONV_W):
            shift = (SHORT_CONV_W // 2 - tap) % ext_rows
            rolled = pltpu.roll(ext, shift, axis=0) if shift else ext
            acc = acc + rolled[CONV_HALO:CONV_HALO + rows, :] * w[tap:tap + 1, :]
        y = _silu(acc)
        if kind < 2:
            y = y * lax.rsqrt(jnp.sum(y * y, axis=-1, keepdims=True) + EPS)
        if kind == 0:
            y = y * (A_HEAD_DIM ** -0.5)
        return y

    q_all = conv(q_ref, qc_ref, qp_ref, qn_ref, wq_ref, 0)
    k_all = conv(k_ref, kc_ref, kp_ref, kn_ref, wk_ref, 1)
    v_all = conv(v_ref, vc_ref, vp_ref, vn_ref, wv_ref, 2)

    lane = lax.broadcasted_iota(jnp.int32, (1, LANES), 1)
    coef = coef_ref[0]
    neg_a = -jnp.exp(jnp.where(lane == 2, coef[0:1, :], coef[1:2, :]))
    dtb = jnp.where(lane == 2, coef[2:3, :], coef[3:4, :])
    raw = jnp.where(is_ctx, bac_ref[...], ba_ref[...])
    xg = raw + dtb
    softplus = jnp.maximum(xg, 0.0) + jnp.log(1.0 + jnp.exp(-jnp.abs(xg)))
    bg = jnp.where(lane < 2, jax.nn.sigmoid(raw), neg_a * softplus)

    i_idx, j_idx = _tri_masks()
    i_w = lax.broadcasted_iota(jnp.int32, (CHUNK, LANES), 0)
    j_w = lax.broadcasted_iota(jnp.int32, (CHUNK, LANES), 1)
    ones8 = jnp.ones((8, CHUNK), F32)
    chunks = [slice(c * CHUNK, (c + 1) * CHUNK) for c in range(DELTA_BLOCK)]

    kk, qk = [], []
    for r in chunks:
        kbf = k_all[r].astype(BF16)
        prod = _dot_nt(jnp.concatenate([kbf, q_all[r].astype(BF16)], axis=0),
                       jnp.concatenate([kbf, jnp.zeros_like(kbf)], axis=0))
        kk.append(prod[:CHUNK])
        qk.append(prod[CHUNK:])

    insts, l_mats, rhs_all = [], [], []
    for d in range(2):
        incl = (i_idx >= j_idx) if d == 0 else (i_idx <= j_idx)
        strict = (i_idx > j_idx) if d == 0 else (i_idx < j_idx)
        in_chunk = j_w < CHUNK
        incl_w = jnp.logical_and(in_chunk, (i_w >= j_w) if d == 0 else (i_w <= j_w))
        seen_w = jnp.logical_and(in_chunk, (i_w <= j_w) if d == 0 else (i_w >= j_w))
        g_cols = [jnp.broadcast_to(bg[r, 2 + d:3 + d], (CHUNK, LANES)) for r in chunks]
        gc_all = _dot_hi(incl.astype(F32), jnp.concatenate(g_cols, axis=1))
        gr_all = _dot_hi(ones8, jnp.concatenate([jnp.where(seen_w, g, 0.0) for g in g_cols], axis=1))
        for c, r in enumerate(chunks):
            q, k, v = q_all[r], k_all[r], v_all[r]
            beta = bg[r, d:d + 1]
            gc_w = gc_all[:, c * LANES:(c + 1) * LANES]
            gr_w = jnp.broadcast_to(gr_all[0:1, c * LANES:(c + 1) * LANES], (CHUNK, LANES))
            decay = jnp.where(incl_w, jnp.exp(jnp.minimum(gc_w - gr_w, 0.0)), 0.0)
            g_last = gc_w[CHUNK - 1:CHUNK, :] if d == 0 else gc_w[0:1, :]
            egc = jnp.exp(gc_w)
            l_mats.append(jnp.where(strict, (kk[c] * beta * decay)[:, :CHUNK], 0.0))
            rhs_all.append(jnp.concatenate([v * beta, k * beta * egc], axis=1).astype(BF16))
            kt_ref[d, 0, r, :] = (k * jnp.exp(g_last - gc_w)).astype(BF16)
            qd_ref[d, 0, r, :] = (q * egc).astype(BF16)
            in_ref[d, 0, r, :] = (qk[c] * decay).astype(BF16)
            el_ref[d, 0, c * 8:(c + 1) * 8, :] = jnp.broadcast_to(jnp.exp(g_last), (8, LANES))
            insts.append((d, r))

    t_split = [_split_bf16(t) for t in _unit_tri_inverses(l_mats, i_idx, j_idx)]
    sols = [_dot(t_hi, rhs) + _dot(t_lo, rhs) for (t_hi, t_lo), rhs in zip(t_split, rhs_all)]
    for (d, r), sol in zip(insts, sols):
        u_ref[d, 0, r, :] = sol[:, :A_HEAD_DIM]
        w_ref[d, 0, r, :] = sol[:, A_HEAD_DIM:].astype(BF16)


def _delta_prep(qkv, qkv_c, conv_w, ba, ba_c, coef, bsz, n_lat, n_ctx):
    assert n_ctx == DELTA_ROWS and n_lat % DELTA_ROWS == 0
    h = A_HEADS
    hd = A_HEAD_DIM
    lat_blocks = n_lat // DELTA_ROWS
    n_blocks = lat_blocks + 1
    n_all = n_lat + n_ctx
    halo_per_block = DELTA_ROWS // CONV_HALO
    n_halo = bsz * n_lat // CONV_HALO

    def lat_blk(b, j):
        return b * lat_blocks + jnp.maximum(j - 1, 0)

    def main(kd):
        return pl.BlockSpec((DELTA_ROWS, hd), lambda b, hh, j: (lat_blk(b, j), kd * h + hh))

    def prev(kd):
        return pl.BlockSpec((CONV_HALO, hd),
                            lambda b, hh, j: (jnp.maximum(lat_blk(b, j) * halo_per_block - 1, 0), kd * h + hh))

    def nxt(kd):
        return pl.BlockSpec((CONV_HALO, hd),
                            lambda b, hh, j: (jnp.minimum((lat_blk(b, j) + 1) * halo_per_block, n_halo - 1),
                                              kd * h + hh))

    def ctxb(kd):
        return pl.BlockSpec((n_ctx, hd), lambda b, hh, j: (b, kd * h + hh))

    def tap(kd):
        return pl.BlockSpec((SHORT_CONV_W, hd), lambda b, hh, j: (0, kd * h + hh))

    in_specs = ([main(kd) for kd in range(3)] + [prev(kd) for kd in range(3)] + [nxt(kd) for kd in range(3)]
                + [ctxb(kd) for kd in range(3)] + [tap(kd) for kd in range(3)]
                + [pl.BlockSpec((DELTA_ROWS, LANES), lambda b, hh, j: (lat_blk(b, j), hh)),
                   pl.BlockSpec((n_ctx, LANES), lambda b, hh, j: (b, hh)),
                   pl.BlockSpec((1, 8, LANES), lambda b, hh, j: (hh, 0, 0))])
    big = pl.BlockSpec((2, 1, DELTA_ROWS, hd), lambda b, hh, j: (0, b, j, hh))
    small = pl.BlockSpec((2, 1, DELTA_BLOCK * 8, LANES), lambda b, hh, j: (0, b, j, hh))
    sds = lambda rows, dt: jax.ShapeDtypeStruct((2, bsz, rows, h * hd), dt)
    return pl.pallas_call(
        functools.partial(_dprep_kernel, n_blocks=n_blocks),
        grid=(bsz, h, n_blocks),
        in_specs=in_specs,
        out_specs=[big, big, big, big, big, small],
        out_shape=[sds(n_all, F32), sds(n_all, BF16), sds(n_all, BF16), sds(n_all, BF16), sds(n_all, BF16),
                   sds(n_all // CHUNK * 8, F32)],
        compiler_params=_cparams(("arbitrary", "arbitrary", "arbitrary")),
        name="dprep",
    )(*([qkv] * 9), *([qkv_c] * 3), *([conv_w] * 3), ba, ba_c, coef)


def _dscan_kernel(uf, wf, ktf, qdf, inf, elf, ub, wb, ktb, qdb, inb, elb, of_ref, ob_ref, state):
    step = pl.program_id(1)

    @pl.when(step == 0)
    def _():
        state[...] = jnp.zeros(state.shape, F32)

    dirs = ((uf, wf, ktf, qdf, inf, elf, of_ref), (ub, wb, ktb, qdb, inb, elb, ob_ref))
    chains = [(d, hh) for d in range(2) for hh in range(A_HEADS)]
    s_mats = [state[d * A_HEADS + hh] for d, hh in chains]
    for t in range(DELTA_BLOCK):
        where = []
        for d, hh in chains:
            c = t if d == 0 else DELTA_BLOCK - 1 - t
            where.append((dirs[d], c, slice(c * CHUNK, (c + 1) * CHUNK),
                          slice(hh * A_HEAD_DIM, (hh + 1) * A_HEAD_DIM)))
        ws = [_dot(jnp.concatenate([rf[1][0, 0, r, ln], rf[3][0, 0, r, ln]], axis=0), s.astype(BF16))
              for (rf, c, r, ln), s in zip(where, s_mats)]
        v_new = [(rf[0][0, 0, r, ln] - x[:CHUNK]).astype(BF16) for (rf, c, r, ln), x in zip(where, ws)]
        for (rf, c, r, ln), x, vn in zip(where, ws, v_new):
            rf[6][r, ln] = x[CHUNK:] + _dot(rf[4][0, 0, r, ln][:, :CHUNK], vn)
        s_mats = [s * rf[5][0, 0, c * 8:c * 8 + 1, ln] + _dot_tn(rf[2][0, 0, r, ln], vn)
                  for (rf, c, r, ln), s, vn in zip(where, s_mats, v_new)]
    for (d, hh), s in zip(chains, s_mats):
        state[d * A_HEADS + hh] = s


def _delta_scan(pre, bsz, n_lat, n_ctx):
    u, w, kt, qd, intra, el = pre
    lat_blocks = n_lat // DELTA_ROWS
    n_blocks = lat_blocks + 1
    width = A_WIDTH

    def fwd_idx(b, s):
        return (0, b, s, 0)

    def bwd_idx(b, s):
        return (1, b, jnp.where(s == 0, 0, n_blocks - s), 0)

    def specs(idx):
        big = pl.BlockSpec((1, 1, DELTA_ROWS, width), idx)
        return [big, big, big, big, big, pl.BlockSpec((1, 1, DELTA_BLOCK * 8, width), idx)]

    out_f = pl.BlockSpec((DELTA_ROWS, width), lambda b, s: (b * lat_blocks + jnp.maximum(s - 1, 0), 0))
    out_b = pl.BlockSpec((DELTA_ROWS, width), lambda b, s: (b * lat_blocks + lat_blocks - jnp.maximum(s, 1), 0))
    shape = jax.ShapeDtypeStruct((bsz * n_lat, width), F32)
    return pl.pallas_call(
        _dscan_kernel,
        grid=(bsz, n_blocks),
        in_specs=specs(fwd_idx) + specs(bwd_idx),
        out_specs=[out_f, out_b],
        out_shape=[shape, shape],
        scratch_shapes=[pltpu.VMEM((2 * A_HEADS, A_HEAD_DIM, A_HEAD_DIM), F32)],
        compiler_params=_cparams(("arbitrary", "arbitrary")),
        name="dscan",
    )(u, w, kt, qd, intra, el, u, w, kt, qd, intra, el)


def _head_rms(x, gain_row, head_dim):
    outs = []
    for s in range(x.shape[1] // LANES):
        xs = x[:, s * LANES:(s + 1) * LANES]
        lane = lax.broadcasted_iota(jnp.int32, xs.shape, 1)
        sq = xs * xs
        scale = jnp.zeros_like(xs)
        for part in range(LANES // head_dim):
            m = jnp.logical_and(lane >= part * head_dim, lane < (part + 1) * head_dim)
            ms = jnp.sum(jnp.where(m, sq, 0.0), axis=-1, keepdims=True) * (1.0 / head_dim)
            scale = jnp.where(m, lax.rsqrt(ms + EPS), scale)
        outs.append(xs * scale)
    y = outs[0] if len(outs) == 1 else jnp.concatenate(outs, axis=1)
    return y * gain_row


def _rope(x, cos, sin_lo, sin_hi):
    width = x.shape[1]
    fwd = pltpu.roll(x, width - ROPE_AXIS_PAIRS, axis=1)
    back = pltpu.roll(x, ROPE_AXIS_PAIRS, axis=1)
    return x * cos + fwd * sin_lo + back * sin_hi


def _kvprep_kernel(k_ref, v_ref, g_ref, cos_ref, slo_ref, shi_ref, ko_ref, vo_ref):
    k = _head_rms(k_ref[0], g_ref[...], B_HEAD_DIM)
    k = _rope(k, cos_ref[...], slo_ref[...], shi_ref[...]).astype(BF16)
    v = v_ref[0].astype(BF16)
    for hh in range(B_KV_HEADS):
        ko_ref[0, hh] = k[:, hh * B_HEAD_DIM:(hh + 1) * B_HEAD_DIM]
        vo_ref[0, hh] = v[:, hh * B_HEAD_DIM:(hh + 1) * B_HEAD_DIM]


def _kvprep(k_all, v_all, gain, cos, slo, shi, ts):
    bsz, s_len, w = k_all.shape
    row = pl.BlockSpec((1, ts, w), lambda b, i: (b, i, 0))
    tab = pl.BlockSpec((ts, w), lambda b, i: (i, 0))
    out = pl.BlockSpec((1, B_KV_HEADS, ts, B_HEAD_DIM), lambda b, i: (b, 0, i, 0))
    shape = jax.ShapeDtypeStruct((bsz, B_KV_HEADS, s_len, B_HEAD_DIM), BF16)
    return pl.pallas_call(
        _kvprep_kernel,
        grid=(bsz, s_len // ts),
        in_specs=[row, row, pl.BlockSpec((1, w), lambda b, i: (0, 0)), tab, tab, tab],
        out_specs=[out, out],
        out_shape=[shape, shape],
        compiler_params=_cparams(("arbitrary", "arbitrary")),
        name="kvprep",
    )(k_all, v_all, gain, cos, slo, shi)


def _attn_kernel(q_ref, g_ref, cos_ref, slo_ref, shi_ref, k_ref, v_ref, o_ref):
    q = _head_rms(q_ref[...], g_ref[...], B_HEAD_DIM)
    q = (_rope(q, cos_ref[...], slo_ref[...], shi_ref[...]) * (B_HEAD_DIM ** -0.5)).astype(BF16)
    k = k_ref[0, 0]
    v = v_ref[0, 0]
    outs = []
    for g in range(B_GROUP):
        s = _dot_nt(q[:, g * B_HEAD_DIM:(g + 1) * B_HEAD_DIM], k)
        p = jnp.exp(s - jnp.max(s, axis=-1, keepdims=True))
        denom = jnp.sum(p, axis=-1, keepdims=True)
        outs.append(_dot(p.astype(BF16), v) / denom)
    o_ref[...] = jnp.concatenate(outs, axis=1)


def _attention(qb, gain, cos, slo, shi, k_hm, v_hm, n_lat, tq):
    n, _ = qb.shape
    bsz, _, s_len, _ = k_hm.shape
    gw = B_GROUP * B_HEAD_DIM
    nq = n_lat // tq
    kv = pl.BlockSpec((1, 1, s_len, B_HEAD_DIM), lambda b, kh, i: (b, kh, 0, 0))
    tab = pl.BlockSpec((tq, gw), lambda b, kh, i: (i, 0))
    return pl.pallas_call(
        _attn_kernel,
        grid=(bsz, B_KV_HEADS, nq),
        in_specs=[pl.BlockSpec((tq, gw), lambda b, kh, i: (b * nq + i, kh)),
                  pl.BlockSpec((1, gw), lambda b, kh, i: (0, 0)), tab, tab, tab, kv, kv],
        out_specs=pl.BlockSpec((tq, gw), lambda b, kh, i: (b * nq + i, kh)),
        out_shape=jax.ShapeDtypeStruct((n, B_WIDTH), F32),
        compiler_params=_cparams(("arbitrary", "arbitrary", "arbitrary")),
        name="attn",
    )(qb, gain, cos, slo, shi, k_hm, v_hm)


def _gated_residual(x, y, mod_ref, gain_ref, gate_row):
    return x + mod_ref[0, gate_row:gate_row + 1, :] * _rms(y, gain_ref[...])


def _outproj_kernel(of_ref, ob_ref, z_ref, on_ref, yb_ref, x_ref, mod_ref, g_ref, wa_ref, wb_ref, o_ref):
    o = of_ref[...] + ob_ref[...]
    z = z_ref[...]
    parts = []
    for hh in range(A_HEADS):
        sl = slice(hh * A_HEAD_DIM, (hh + 1) * A_HEAD_DIM)
        parts.append((_rms(o[:, sl], on_ref[...]) * _silu(z[:, sl])).astype(BF16))
    ya = jnp.concatenate(parts, axis=1)
    y = _dot(ya, wa_ref[...]) + _dot(yb_ref[...].astype(BF16), wb_ref[...])
    o_ref[...] = _gated_residual(x_ref[...], y, mod_ref, g_ref, 2)


def _outproj(o_f, o_b, z, out_norm, yb, x2, mod, gain, wa, wb, rows_per_mod, tm):
    n, d = x2.shape
    bpm = rows_per_mod // tm
    return pl.pallas_call(
        _outproj_kernel,
        grid=(n // tm,),
        in_specs=[pl.BlockSpec((tm, o_f.shape[1]), lambda i: (i, 0)),
                  pl.BlockSpec((tm, o_b.shape[1]), lambda i: (i, 0)),
                  pl.BlockSpec((tm, z.shape[1]), lambda i: (i, 0)),
                  pl.BlockSpec((1, A_HEAD_DIM), lambda i: (0, 0)),
                  pl.BlockSpec((tm, yb.shape[1]), lambda i: (i, 0)),
                  pl.BlockSpec((tm, d), lambda i: (i, 0)),
                  pl.BlockSpec((1, 6, d), lambda i: (i // bpm, 0, 0)),
                  pl.BlockSpec((1, d), lambda i: (0, 0)),
                  pl.BlockSpec(wa.shape, lambda i: (0, 0)),
                  pl.BlockSpec(wb.shape, lambda i: (0, 0))],
        out_specs=pl.BlockSpec((tm, d), lambda i: (i, 0)),
        out_shape=jax.ShapeDtypeStruct((n, d), F32),
        compiler_params=_cparams(("arbitrary",)),
        name="outproj",
    )(o_f, o_b, z, out_norm, yb, x2, mod, gain, wa, wb)


def _ffn_kernel(x_ref, mod_ref, gpre_ref, gpost_ref, wi_ref, wo_ref, o_ref, *, hidden, hc):
    x = x_ref[...]
    a = _modulated(x, mod_ref, gpre_ref, 3).astype(BF16)
    acc = jnp.zeros(x.shape, F32)
    for c in range(hidden // hc):
        gate = _dot(a, wi_ref[:, c * hc:(c + 1) * hc])
        up = _dot(a, wi_ref[:, hidden + c * hc:hidden + (c + 1) * hc])
        acc = acc + _dot((_silu(gate) * up).astype(BF16), wo_ref[c * hc:(c + 1) * hc, :])
    o_ref[...] = _gated_residual(x, acc, mod_ref, gpost_ref, 5)


def _ffn(x2, mod, gpre, gpost, wi, wo, rows_per_mod, tm):
    n, d = x2.shape
    hidden = wo.shape[0]
    bpm = rows_per_mod // tm
    return pl.pallas_call(
        functools.partial(_ffn_kernel, hidden=hidden, hc=256),
        grid=(n // tm,),
        in_specs=[pl.BlockSpec((tm, d), lambda i: (i, 0)),
                  pl.BlockSpec((1, 6, d), lambda i: (i // bpm, 0, 0)),
                  pl.BlockSpec((1, d), lambda i: (0, 0)),
                  pl.BlockSpec((1, d), lambda i: (0, 0)),
                  pl.BlockSpec(wi.shape, lambda i: (0, 0)),
                  pl.BlockSpec(wo.shape, lambda i: (0, 0))],
        out_specs=pl.BlockSpec((tm, d), lambda i: (i, 0)),
        out_shape=jax.ShapeDtypeStruct((n, d), F32),
        compiler_params=_cparams(("arbitrary",)),
        name="ffn",
    )(x2, mod, gpre, gpost, wi, wo)


def _confin_kernel(x_ref, mod_ref, g_ref, w_ref, b_ref, o_ref, *, width):
    a = _modulated(x_ref[...], mod_ref, g_ref, 0).astype(BF16)
    val = _dot(a, w_ref[:, :width]) + b_ref[:, :width]
    gate = _dot(a, w_ref[:, width:]) + b_ref[:, width:]
    o_ref[...] = val * jax.nn.sigmoid(gate)


def _confin(x2, mod, gain, w, b, rows_per_mod, tm):
    n, d = x2.shape
    width = w.shape[1] // 2
    bpm = rows_per_mod // tm
    return pl.pallas_call(
        functools.partial(_confin_kernel, width=width),
        grid=(n // tm,),
        in_specs=[pl.BlockSpec((tm, d), lambda i: (i, 0)),
                  pl.BlockSpec((1, 6, d), lambda i: (i // bpm, 0, 0)),
                  pl.BlockSpec((1, d), lambda i: (0, 0)),
                  pl.BlockSpec(w.shape, lambda i: (0, 0)),
                  pl.BlockSpec((1, 2 * width), lambda i: (0, 0))],
        out_specs=pl.BlockSpec((tm, width), lambda i: (i, 0)),
        out_shape=jax.ShapeDtypeStruct((n, width), F32),
        compiler_params=_cparams(("arbitrary",)),
        name="confin",
    )(x2, mod, gain, w, b)


def _confout_kernel(u_ref, up_ref, un_ref, x_ref, mod_ref, g_ref, dww_ref, dwb_ref, lng_ref, lnb_ref,
                    w_ref, b_ref, o_ref, ext, conv, *, tm, tiles_per_seq):
    i = pl.program_id(0)
    pos = i % tiles_per_seq
    width = u_ref.shape[1]
    halo = CONF_HALO
    ext[pl.ds(halo, tm), :] = u_ref[...]
    ext[0:halo, :] = jnp.where(pos == 0, 0.0, up_ref[...])
    ext[pl.ds(halo + tm, halo), :] = jnp.where(pos == tiles_per_seq - 1, 0.0, un_ref[...])

    rb = 64
    pad = CONF_KERNEL // 2

    def col_body(c, carry):
        cs = pl.ds(pl.multiple_of(c * LANES, LANES), LANES)
        w = dww_ref[:, cs]
        for r in range(tm // rb):
            acc = jnp.zeros((rb, LANES), F32)
            for tap in range(CONF_KERNEL):
                acc = acc + ext[pl.ds(r * rb + halo - pad + tap, rb), cs] * w[tap:tap + 1, :]
            conv[pl.ds(r * rb, rb), cs] = acc
        return carry

    lax.fori_loop(0, width // LANES, col_body, 0)

    y = conv[...] + dwb_ref[...]
    mu = jnp.mean(y, axis=-1, keepdims=True)
    yc = y - mu
    var = jnp.mean(yc * yc, axis=-1, keepdims=True)
    y = _silu(yc * lax.rsqrt(var + EPS) * lng_ref[...] + lnb_ref[...])
    out = _dot(y.astype(BF16), w_ref[...]) + b_ref[...]
    o_ref[...] = _gated_residual(x_ref[...], out, mod_ref, g_ref, 2)


def _confout(u, x2, mod, gain, dww, dwb, lng, lnb, w, b, rows_per_mod, tm):
    n, d = x2.shape
    width = u.shape[1]
    tiles_per_seq = rows_per_mod // tm
    hb = tm // CONF_HALO
    n_halo_blocks = n // CONF_HALO
    vec = lambda wd: pl.BlockSpec((1, wd), lambda i: (0, 0))
    return pl.pallas_call(
        functools.partial(_confout_kernel, tm=tm, tiles_per_seq=tiles_per_seq),
        grid=(n // tm,),
        in_specs=[pl.BlockSpec((tm, width), lambda i: (i, 0)),
                  pl.BlockSpec((CONF_HALO, width), lambda i: (jnp.maximum(i * hb - 1, 0), 0)),
                  pl.BlockSpec((CONF_HALO, width), lambda i: (jnp.minimum((i + 1) * hb, n_halo_blocks - 1), 0)),
                  pl.BlockSpec((tm, d), lambda i: (i, 0)),
                  pl.BlockSpec((1, 6, d), lambda i: (i // tiles_per_seq, 0, 0)),
                  vec(d),
                  pl.BlockSpec(dww.shape, lambda i: (0, 0)),
                  vec(width), vec(width), vec(width),
                  pl.BlockSpec(w.shape, lambda i: (0, 0)),
                  vec(d)],
        out_specs=pl.BlockSpec((tm, d), lambda i: (i, 0)),
        out_shape=jax.ShapeDtypeStruct((n, d), F32),
        scratch_shapes=[pltpu.VMEM((tm + 2 * CONF_HALO, width), F32), pltpu.VMEM((tm, width), F32)],
        compiler_params=_cparams(("arbitrary",)),
        name="confout",
    )(u, u, u, x2, mod, gain, dww, dwb, lng, lnb, w, b)


def _rope_tables(n_tokens, n_ctx, reps):
    rows = n_tokens // GRID_W
    row = jnp.broadcast_to(jnp.arange(rows, dtype=F32)[:, None], (rows, GRID_W)).reshape(n_tokens)
    col = jnp.broadcast_to(jnp.arange(GRID_W, dtype=F32)[None, :], (rows, GRID_W)).reshape(n_tokens)
    inv_freq = ROPE_THETA ** (-jnp.arange(ROPE_AXIS_PAIRS, dtype=F32) / ROPE_AXIS_PAIRS)
    ang_r = row[:, None] * inv_freq
    ang_c = col[:, None] * inv_freq
    ang = jnp.concatenate([ang_r, ang_r, ang_c, ang_c], axis=-1)
    cos, sin = jnp.cos(ang), jnp.sin(ang)
    first_half = (jnp.arange(B_HEAD_DIM) % (2 * ROPE_AXIS_PAIRS)) < ROPE_AXIS_PAIRS
    sin_lo = jnp.where(first_half, -sin, 0.0)
    sin_hi = jnp.where(first_half, 0.0, sin)
    if n_ctx:
        cos = jnp.concatenate([jnp.ones((n_ctx, B_HEAD_DIM), F32), cos], axis=0)
        zeros = jnp.zeros((n_ctx, B_HEAD_DIM), F32)
        sin_lo = jnp.concatenate([zeros, sin_lo], axis=0)
        sin_hi = jnp.concatenate([zeros, sin_hi], axis=0)
    tile = lambda t: jnp.tile(t, (1, reps))
    return tile(cos), tile(sin_lo), tile(sin_hi)


def _hybrid_in_weight(w_in):
    off_z = 3 * A_WIDTH
    off_ba = off_z + A_WIDTH
    off_q = off_ba + 4 * A_HEADS
    off_k = off_q + B_WIDTH
    off_v = off_k + B_KV_WIDTH
    d = w_in.shape[0]
    ba = w_in[:, off_ba:off_q].reshape(d, 2, 2, A_HEADS)
    ba = ba.transpose(0, 3, 1, 2).reshape(d, A_HEADS, 4)
    ba = jnp.pad(ba, ((0, 0), (0, 0), (0, LANES - 4))).reshape(d, A_HEADS * LANES)
    w = jnp.concatenate([w_in[:, :off_z], w_in[:, off_z:off_ba], w_in[:, off_q:off_k], w_in[:, off_k:off_v],
                         w_in[:, off_v:], ba], axis=1)
    widths = (3 * A_WIDTH, A_WIDTH, B_WIDTH, B_KV_WIDTH, B_KV_WIDTH, A_HEADS * LANES)
    return w.astype(BF16), widths


def _row(v):
    return v.reshape(1, -1)


def kernel(x, c, ctx, c_ctx, w_mod, b_mod, g_mix_pre, g_mix_post, g_ffn_pre, g_ffn_post, w_ffn_in, w_ffn_out,
           hyb_w_in, hyb_conv_w, hyb_a_log, hyb_dt_bias, hyb_out_norm, hyb_q_norm, hyb_k_norm, hyb_w_out,
           conf_w_in, conf_b_in, conf_dw_w, conf_dw_b, conf_ln_g, conf_ln_b, conf_w_out, conf_b_out):
    bsz, n_lat, d = x.shape
    n_ctx = ctx.shape[1]
    depth = w_mod.shape[0]
    n = bsz * n_lat
    tm = 512

    cond = jnp.zeros((8, d), F32).at[:bsz].set(c).at[bsz].set(c_ctx)
    mods = _ada_terms(cond, w_mod, b_mod).reshape(depth, 8, 6, d)

    h = x.reshape(n, d)
    hc = ctx.reshape(bsz * n_ctx, d)
    for layer in range(depth):
        idx = layer // 2
        mod = mods[layer, :bsz]
        mod_ctx = mods[layer, bsz:bsz + 1]
        if layer % 2 == 0:
            w_in, widths = _hybrid_in_weight(hyb_w_in[idx])
            gpre = _row(g_mix_pre[layer])
            qkv, z, qb, kb, vb, ba = _inproj(h, mod, gpre, w_in, widths, n_lat, tm)
            qkv_c, _, _, kb_c, vb_c, ba_c = _inproj(hc, mod_ctx, gpre, w_in, widths, bsz * n_ctx, n_ctx)

            coef = jnp.concatenate([hyb_a_log[idx], hyb_dt_bias[idx]], axis=0)
            coef = jnp.pad(coef.T[:, :, None], ((0, 0), (0, 4), (0, 0)))
            coef = jnp.broadcast_to(coef, (A_HEADS, 8, LANES))
            o_f, o_b = _delta_scan(_delta_prep(qkv, qkv_c, hyb_conv_w[idx], ba, ba_c, coef, bsz, n_lat, n_ctx),
                                   bsz, n_lat, n_ctx)

            k_all = jnp.concatenate([kb_c.reshape(bsz, n_ctx, -1), kb.reshape(bsz, n_lat, -1)], axis=1)
            v_all = jnp.concatenate([vb_c.reshape(bsz, n_ctx, -1), vb.reshape(bsz, n_lat, -1)], axis=1)
            cos_k, slo_k, shi_k = _rope_tables(n_lat, n_ctx, B_KV_HEADS)
            k_hm, v_hm = _kvprep(k_all, v_all, _row(jnp.tile(hyb_k_norm[idx], B_KV_HEADS)),
                                 cos_k, slo_k, shi_k, n_ctx)
            cos_q, slo_q, shi_q = _rope_tables(n_lat, 0, B_GROUP)
            yb = _attention(qb, _row(jnp.tile(hyb_q_norm[idx], B_GROUP)), cos_q, slo_q, shi_q,
                            k_hm, v_hm, n_lat, 256)

            w_out = hyb_w_out[idx].astype(BF16)
            h = _outproj(o_f, o_b, z, _row(hyb_out_norm[idx]), yb, h, mod, _row(g_mix_post[layer]),
                         w_out[:A_WIDTH], w_out[A_WIDTH:], n_lat, tm)
        else:
            u = _confin(h, mod, _row(g_mix_pre[layer]), conf_w_in[idx].astype(BF16), _row(conf_b_in[idx]),
                        n_lat, tm)
            h = _confout(u, h, mod, _row(g_mix_post[layer]), conf_dw_w[idx], _row(conf_dw_b[idx]),
                         _row(conf_ln_g[idx]), _row(conf_ln_b[idx]), conf_w_out[idx].astype(BF16),
                         _row(conf_b_out[idx]), n_lat, 256)
        h = _ffn(h, mod, _row(g_ffn_pre[layer]), _row(g_ffn_post[layer]), w_ffn_in[layer].astype(BF16),
                 w_ffn_out[layer].astype(BF16), n_lat, 256)
        assert not any(j % 2 == 0 for j in range(layer + 1, depth)), "context advance not implemented"
    return h.reshape(bsz, n_lat, d)
```

```python
import functools
import math

import jax
import jax.numpy as jnp
import numpy as np
from jax import lax
from jax.experimental import pallas as pl
from jax.experimental.pallas import tpu as pltpu

F32 = jnp.float32
BF16 = jnp.bfloat16
HIGHEST = lax.Precision.HIGHEST

EPS = 1e-6
GRID_W = 64
ROPE_THETA = 10000.0
A_HEADS = 4
A_HEAD_DIM = 128
A_WIDTH = A_HEADS * A_HEAD_DIM
SHORT_CONV_W = 5
CHUNK = 64
B_Q_HEADS = 8
B_KV_HEADS = 2
B_HEAD_DIM = 64
B_GROUP = B_Q_HEADS // B_KV_HEADS
B_WIDTH = B_Q_HEADS * B_HEAD_DIM
B_KV_WIDTH = B_KV_HEADS * B_HEAD_DIM
ROPE_AXIS_PAIRS = B_HEAD_DIM // 4
CONF_KERNEL = 31
CONF_HALO = 16
LANES = 128
VMEM_LIMIT = 56 * 1024 * 1024


def _cparams(sem):
    return pltpu.CompilerParams(dimension_semantics=sem, vmem_limit_bytes=VMEM_LIMIT)


def _resident(shape):
    return pl.BlockSpec(shape, lambda i: (0, 0), pipeline_mode=pl.Buffered(1))


def _silu(x):
    return x * jax.nn.sigmoid(x)


def _dot(a, b):
    return jnp.dot(a, b, preferred_element_type=F32)


def _dot_hi(a, b):
    return jnp.dot(a, b, preferred_element_type=F32, precision=HIGHEST)


def _dot_nt(a, b):
    return lax.dot_general(a, b, (((1,), (1,)), ((), ())), preferred_element_type=F32)


def _dot_tn(a, b):
    return lax.dot_general(a, b, (((0,), (0,)), ((), ())), preferred_element_type=F32)


def _rms(x, gain):
    return x * lax.rsqrt(jnp.mean(x * x, axis=-1, keepdims=True) + EPS) * gain


def _ada_kernel(c_ref, w_ref, b_ref, o_ref):
    o_ref[0] = _dot_hi(_silu(c_ref[...]), w_ref[0]) + b_ref[0]


def _ada_terms(cond, w_mod, b_mod):
    depth, d, n6 = w_mod.shape
    tn = 1536
    return pl.pallas_call(
        _ada_kernel,
        grid=(depth, n6 // tn),
        in_specs=[pl.BlockSpec((8, d), lambda l, j: (0, 0)),
                  pl.BlockSpec((1, d, tn), lambda l, j: (l, 0, j)),
                  pl.BlockSpec((1, 1, tn), lambda l, j: (l, 0, j))],
        out_specs=pl.BlockSpec((1, 8, tn), lambda l, j: (l, 0, j)),
        out_shape=jax.ShapeDtypeStruct((depth, 8, n6), F32),
        compiler_params=_cparams(("arbitrary", "arbitrary")),
        name="ada",
    )(cond, w_mod, b_mod.reshape(depth, 1, n6))


def _modulated(x, mod_ref, gain_ref, shift_row):
    y = _rms(x, gain_ref[...])
    return y * (1.0 + mod_ref[0, shift_row + 1:shift_row + 2, :]) + mod_ref[0, shift_row:shift_row + 1, :]


def _inproj_kernel(x_ref, mod_ref, g_ref, w_ref, *o_refs, widths):
    a = _modulated(x_ref[...], mod_ref, g_ref, 0).astype(BF16)
    off = 0
    for o_ref, wd in zip(o_refs, widths):
        o_ref[...] = _dot(a, w_ref[:, off:off + wd])
        off += wd


def _inproj(x2, mod, gain, w, widths, rows_per_mod, tm):
    n, d = x2.shape
    blocks_per_mod = rows_per_mod // tm
    return pl.pallas_call(
        functools.partial(_inproj_kernel, widths=widths),
        grid=(n // tm,),
        in_specs=[pl.BlockSpec((tm, d), lambda i: (i, 0)),
                  pl.BlockSpec((1, 6, d), lambda i: (i // blocks_per_mod, 0, 0)),
                  pl.BlockSpec((1, d), lambda i: (0, 0)),
                  _resident(w.shape)],
        out_specs=[pl.BlockSpec((tm, wd), lambda i: (i, 0)) for wd in widths],
        out_shape=[jax.ShapeDtypeStruct((n, wd), F32) for wd in widths],
        compiler_params=_cparams(("arbitrary",)),
        name="inproj",
    )(x2, mod, gain, w)


def _tri_masks():
    i = lax.broadcasted_iota(jnp.int32, (CHUNK, CHUNK), 0)
    j = lax.broadcasted_iota(jnp.int32, (CHUNK, CHUNK), 1)
    return i, j


def _unit_tri_inverse(l_mat, i, j):
    eye = (i == j).astype(F32)
    same16 = jnp.right_shift(i, 4) == jnp.right_shift(j, 4)
    same32 = jnp.right_shift(i, 5) == jnp.right_shift(j, 5)
    d1 = jnp.where(same16, l_mat, 0.0)
    c1 = jnp.where(jnp.logical_and(same32, jnp.logical_not(same16)), l_mat, 0.0)
    c2 = jnp.where(same32, 0.0, l_mat)
    p = eye - d1
    d2 = _dot_hi(d1, d1)
    p = p + _dot_hi(p, d2)
    d4 = _dot_hi(d2, d2)
    p = p + _dot_hi(p, d4)
    d8 = _dot_hi(d4, d4)
    p = p + _dot_hi(p, d8)
    t1 = p - _dot_hi(p, _dot_hi(c1, p))
    return t1 - _dot_hi(t1, _dot_hi(c2, t1))


def _delta_kernel(q_ref, k_ref, v_ref, qc_ref, kc_ref, vc_ref, wq_ref, wk_ref, wv_ref,
                  ba_ref, bac_ref, coef_ref, z_ref, on_ref, o_ref,
                  xp, qs, ks, vs, bs, of_s, ob_s, *, n_lat, n_ctx):
    n_all = n_ctx + n_lat
    nch_ctx = n_ctx // CHUNK
    nch_all = n_all // CHUNK
    rb = 256

    def conv_into(src_ref, w_ref, dst, dst_off, n_rows, kind):
        xp[0:8, :] = jnp.zeros((8, A_HEAD_DIM), F32)
        xp[pl.ds(8, n_rows), :] = src_ref[...]
        xp[pl.ds(8 + n_rows, 8), :] = jnp.zeros((8, A_HEAD_DIM), F32)
        w = w_ref[...]

        def body(t, carry):
            r0 = pl.multiple_of(t * rb, rb)
            blk = xp[pl.ds(r0, rb + 16), :]
            acc = jnp.zeros((rb, A_HEAD_DIM), F32)
            for tap in range(SHORT_CONV_W):
                shift = (SHORT_CONV_W // 2 - tap) % (rb + 16)
                rolled = pltpu.roll(blk, shift, axis=0) if shift else blk
                acc = acc + rolled[8:8 + rb, :] * w[tap:tap + 1, :]
            y = _silu(acc)
            if kind < 2:
                y = y * lax.rsqrt(jnp.sum(y * y, axis=-1, keepdims=True) + EPS)
            if kind == 0:
                y = y * (A_HEAD_DIM ** -0.5)
            dst[pl.ds(pl.multiple_of(dst_off + r0, rb), rb), :] = y
            return carry

        lax.fori_loop(0, n_rows // rb, body, 0)

    for kind, (lat_ref, ctx_ref, w_ref, dst) in enumerate(
            ((q_ref, qc_ref, wq_ref, qs), (k_ref, kc_ref, wk_ref, ks), (v_ref, vc_ref, wv_ref, vs))):
        conv_into(ctx_ref, w_ref, dst, 0, n_ctx, kind)
        conv_into(lat_ref, w_ref, dst, n_ctx, n_lat, kind)

    lane = lax.broadcasted_iota(jnp.int32, (1, LANES), 1)
    coef = coef_ref[0]
    neg_a = -jnp.exp(jnp.where(lane == 2, coef[0:1, :], coef[1:2, :]))
    dtb = jnp.where(lane == 2, coef[2:3, :], coef[3:4, :])

    def beta_g(raw):
        xg = raw + dtb
        softplus = jnp.maximum(xg, 0.0) + jnp.log(1.0 + jnp.exp(-jnp.abs(xg)))
        return jnp.where(lane < 2, jax.nn.sigmoid(raw), neg_a * softplus)

    bs[pl.ds(0, n_ctx), :] = beta_g(bac_ref[...])
    bs[pl.ds(n_ctx, n_lat), :] = beta_g(ba_ref[...])

    i_idx, j_idx = _tri_masks()
    ones8 = jnp.ones((8, CHUNK), F32)

    def precompute(chunk, d):
        r = pl.multiple_of(chunk * CHUNK, CHUNK)
        q = qs[pl.ds(r, CHUNK), :]
        k = ks[pl.ds(r, CHUNK), :]
        v = vs[pl.ds(r, CHUNK), :]
        bb = bs[pl.ds(r, CHUNK), :]
        beta = bb[:, d:d + 1]
        g = bb[:, 2 + d:3 + d]
        incl = (i_idx >= j_idx) if d == 0 else (i_idx <= j_idx)
        strict = (i_idx > j_idx) if d == 0 else (i_idx < j_idx)
        g_b = jnp.broadcast_to(g, (CHUNK, CHUNK))
        gc_mat = _dot_hi(incl.astype(F32), g_b)
        seen = (i_idx <= j_idx) if d == 0 else (i_idx >= j_idx)
        gr8 = _dot_hi(ones8, jnp.where(seen, g_b, 0.0))
        gr_mat = jnp.broadcast_to(gr8[0:1, :], (CHUNK, CHUNK))
        decay = jnp.where(incl, jnp.exp(jnp.minimum(gc_mat - gr_mat, 0.0)), 0.0)
        gc = gc_mat[:, 0:1]
        g_last = gc[CHUNK - 1:CHUNK, :] if d == 0 else gc[0:1, :]
        egc = jnp.exp(gc)
        kb = k * beta
        kbf = k.astype(BF16)
        l_mat = jnp.where(strict, _dot_nt(kb.astype(BF16), kbf) * decay, 0.0)
        t_inv = _unit_tri_inverse(l_mat, i_idx, j_idx)
        rhs = jnp.concatenate([v * beta, kb * egc], axis=1)
        sol = _dot_hi(t_inv, rhs)
        u = sol[:, :A_HEAD_DIM]
        w = sol[:, A_HEAD_DIM:]
        k_tail = k * jnp.exp(g_last - gc)
        q_dec = q * egc
        intra = _dot_nt(q.astype(BF16), kbf) * decay
        return u, w, k_tail, q_dec, intra, jnp.exp(g_last)

    def scan_step(state, pre):
        u, w, k_tail, q_dec, intra, e_last = pre
        ws = _dot(jnp.concatenate([w, q_dec], axis=0).astype(BF16), state.astype(BF16))
        v_new = u - ws[:CHUNK]
        v_new_b = v_new.astype(BF16)
        o = ws[CHUNK:] + _dot(intra.astype(BF16), v_new_b)
        new_state = state * e_last + _dot_tn(k_tail.astype(BF16), v_new_b)
        return new_state, o

    def bwd_chunk(step):
        return jnp.where(step < nch_ctx, nch_ctx - 1 - step, nch_all + nch_ctx - 1 - step)

    def body(step, carry):
        s_f, s_b = carry
        cf = step
        cb = bwd_chunk(step)
        s_f, o_f = scan_step(s_f, precompute(cf, 0))
        s_b, o_b = scan_step(s_b, precompute(cb, 1))

        @pl.when(step >= nch_ctx)
        def _():
            of_s[pl.ds(pl.multiple_of((cf - nch_ctx) * CHUNK, CHUNK), CHUNK), :] = o_f
            ob_s[pl.ds(pl.multiple_of((cb - nch_ctx) * CHUNK, CHUNK), CHUNK), :] = o_b

        return s_f, s_b

    zero_state = jnp.zeros((A_HEAD_DIM, A_HEAD_DIM), F32)
    lax.fori_loop(0, nch_all, body, (zero_state, zero_state))

    o = of_s[...] + ob_s[...]
    o_ref[...] = _rms(o, on_ref[...]) * _silu(z_ref[...])


def _delta_mixer(qkv, qkv_c, conv_w, ba, ba_c, coef, z, out_norm, bsz, n_lat, n_ctx):
    h = A_HEADS
    hd = A_HEAD_DIM
    n_all = n_lat + n_ctx

    def col(kind):
        return lambda b, hh: (b, kind * h + hh)

    in_specs = (
        [pl.BlockSpec((n_lat, hd), col(kd)) for kd in range(3)]
        + [pl.BlockSpec((n_ctx, hd), col(kd)) for kd in range(3)]
        + [pl.BlockSpec((SHORT_CONV_W, hd), (lambda b, hh, kd=kd: (0, kd * h + hh))) for kd in range(3)]
        + [pl.BlockSpec((n_lat, LANES), lambda b, hh: (b, hh)),
           pl.BlockSpec((n_ctx, LANES), lambda b, hh: (b, hh)),
           pl.BlockSpec((1, 8, LANES), lambda b, hh: (hh, 0, 0)),
           pl.BlockSpec((n_lat, hd), lambda b, hh: (b, hh)),
           pl.BlockSpec((1, hd), lambda b, hh: (0, 0))])
    return pl.pallas_call(
        functools.partial(_delta_kernel, n_lat=n_lat, n_ctx=n_ctx),
        grid=(bsz, h),
        in_specs=in_specs,
        out_specs=pl.BlockSpec((n_lat, hd), lambda b, hh: (b, hh)),
        out_shape=jax.ShapeDtypeStruct((bsz * n_lat, h * hd), F32),
        scratch_shapes=[pltpu.VMEM((n_lat + 16, hd), F32),
                        pltpu.VMEM((n_all, hd), F32), pltpu.VMEM((n_all, hd), F32), pltpu.VMEM((n_all, hd), F32),
                        pltpu.VMEM((n_all, LANES), F32),
                        pltpu.VMEM((n_lat, hd), F32), pltpu.VMEM((n_lat, hd), F32)],
        compiler_params=_cparams(("arbitrary", "arbitrary")),
        name="delta",
    )(qkv, qkv, qkv, qkv_c, qkv_c, qkv_c, conv_w, conv_w, conv_w, ba, ba_c, coef, z, out_norm)


DELTA_BLOCK = 4
DELTA_ROWS = DELTA_BLOCK * CHUNK
CONV_HALO = 8


def _split_bf16(a):
    hi = a.astype(BF16)
    return hi, (a - hi.astype(F32)).astype(BF16)


def _unit_tri_inverses(l_mats, i, j):
    eye = (i == j).astype(F32)
    same16 = jnp.right_shift(i, 4) == jnp.right_shift(j, 4)
    same32 = jnp.right_shift(i, 5) == jnp.right_shift(j, 5)
    off32 = jnp.logical_and(same32, jnp.logical_not(same16))
    b = lambda a: a.astype(BF16)
    each = lambda f, *ls: [f(*xs) for xs in zip(*ls)]
    d1 = each(lambda l: b(jnp.where(same16, l, 0.0)), l_mats)
    p = each(lambda d: eye - d.astype(F32), d1)
    dk = d1
    for _ in range(3):
        dk = each(lambda d: b(_dot(d, d)), dk)
        p = each(lambda pp, d: pp + _dot(b(pp), d), p, dk)
    for sel in (off32, jnp.logical_not(same32)):
        cm = each(lambda l: b(jnp.where(sel, l, 0.0)), l_mats)
        pb = each(b, p)
        inner = each(lambda c, q: b(_dot(c, q)), cm, pb)
        p = each(lambda pp, q, m: pp - _dot(q, m), p, pb, inner)
    l_split = each(_split_bf16, l_mats)
    t_split = each(_split_bf16, p)
    resid = each(lambda t0, ls, ts: eye - t0 - (_dot(ls[0], ts[0]) + _dot(ls[0], ts[1]) + _dot(ls[1], ts[0])),
                 p, l_split, t_split)
    return each(lambda t0, ts, r: t0 + _dot(ts[0], b(r)), p, t_split, resid)


DELTA_HEADS = 4


def _dprep_kernel(q_ref, k_ref, v_ref, qp_ref, kp_ref, vp_ref, qn_ref, kn_ref, vn_ref, qc_ref, kc_ref, vc_ref,
                  wq_ref, wk_ref, wv_ref, ba_ref, bac_ref, coef_ref,
                  u_ref, w_ref, kt_ref, qd_ref, in_ref, el_ref, *, n_blocks):
    blk = pl.program_id(2)
    is_ctx = blk == 0
    rows = DELTA_ROWS
    ext_rows = rows + 2 * CONV_HALO
    heads = [slice(hh * A_HEAD_DIM, (hh + 1) * A_HEAD_DIM) for hh in range(DELTA_HEADS)]

    def conv(main_ref, ctx_ref, prev_ref, next_ref, w_ref, kind):
        x = jnp.where(is_ctx, ctx_ref[...], main_ref[...])
        prev = jnp.where(blk <= 1, 0.0, prev_ref[...])
        nxt = jnp.where(jnp.logical_or(is_ctx, blk == n_blocks - 1), 0.0, next_ref[...])
        ext = jnp.concatenate([prev, x, nxt], axis=0)
        w = w_ref[...]
        acc = jnp.zeros(x.shape, F32)
        for tap in range(SHORT_CONV_W):
            shift = (SHORT_CONV_W // 2 - tap) % ext_rows
            rolled = pltpu.roll(ext, shift, axis=0) if shift else ext
            acc = acc + rolled[CONV_HALO:CONV_HALO + rows, :] * w[tap:tap + 1, :]
        y = _silu(acc)
        per_head = []
        for ln in heads:
            yh = y[:, ln]
            if kind < 2:
                yh = yh * lax.rsqrt(jnp.sum(yh * yh, axis=-1, keepdims=True) + EPS)
            if kind == 0:
                yh = yh * (A_HEAD_DIM ** -0.5)
            per_head.append(yh)
        return per_head

    q_h = conv(q_ref, qc_ref, qp_ref, qn_ref, wq_ref, 0)
    k_h = conv(k_ref, kc_ref, kp_ref, kn_ref, wk_ref, 1)
    v_h = conv(v_ref, vc_ref, vp_ref, vn_ref, wv_ref, 2)

    lane = lax.broadcasted_iota(jnp.int32, (1, LANES), 1)
    raw_all = jnp.where(is_ctx, bac_ref[...], ba_ref[...])
    bg_h = []
    for hh, ln in enumerate(heads):
        coef = coef_ref[hh]
        neg_a = -jnp.exp(jnp.where(lane == 2, coef[0:1, :], coef[1:2, :]))
        dtb = jnp.where(lane == 2, coef[2:3, :], coef[3:4, :])
        raw = raw_all[:, ln]
        xg = raw + dtb
        softplus = jnp.maximum(xg, 0.0) + jnp.log(1.0 + jnp.exp(-jnp.abs(xg)))
        bg_h.append(jnp.where(lane < 2, jax.nn.sigmoid(raw), neg_a * softplus))

    i_idx, j_idx = _tri_masks()
    i_w = lax.broadcasted_iota(jnp.int32, (CHUNK, LANES), 0)
    j_w = lax.broadcasted_iota(jnp.int32, (CHUNK, LANES), 1)
    in_chunk = j_w < CHUNK
    ones8 = jnp.ones((8, CHUNK), F32)
    chunks = [slice(c * CHUNK, (c + 1) * CHUNK) for c in range(DELTA_BLOCK)]

    kk, qk = {}, {}
    for hh in range(DELTA_HEADS):
        for c, r in enumerate(chunks):
            kbf = k_h[hh][r].astype(BF16)
            prod = _dot_nt(jnp.concatenate([kbf, q_h[hh][r].astype(BF16)], axis=0),
                           jnp.concatenate([kbf, jnp.zeros_like(kbf)], axis=0))
            kk[hh, c] = prod[:CHUNK]
            qk[hh, c] = prod[CHUNK:]

    gc_all, gr_all = {}, {}
    for d in range(2):
        incl = (i_idx >= j_idx) if d == 0 else (i_idx <= j_idx)
        seen_w = jnp.logical_and(in_chunk, (i_w <= j_w) if d == 0 else (i_w >= j_w))
        for hh in range(DELTA_HEADS):
            g_cols = [jnp.broadcast_to(bg_h[hh][r, 2 + d:3 + d], (CHUNK, LANES)) for r in chunks]
            gc_all[hh, d] = _dot_hi(incl.astype(F32), jnp.concatenate(g_cols, axis=1))
            gr_all[hh, d] = _dot_hi(ones8, jnp.concatenate([jnp.where(seen_w, g, 0.0) for g in g_cols], axis=1))

    insts, l_mats, rhs_all = [], [], []
    for d in range(2):
        strict = (i_idx > j_idx) if d == 0 else (i_idx < j_idx)
        incl_w = jnp.logical_and(in_chunk, (i_w >= j_w) if d == 0 else (i_w <= j_w))
        for hh, ln in enumerate(heads):
            for c, r in enumerate(chunks):
                q, k, v = q_h[hh][r], k_h[hh][r], v_h[hh][r]
                beta = bg_h[hh][r, d:d + 1]
                gc_w = gc_all[hh, d][:, c * LANES:(c + 1) * LANES]
                gr_w = jnp.broadcast_to(gr_all[hh, d][0:1, c * LANES:(c + 1) * LANES], (CHUNK, LANES))
                decay = jnp.where(incl_w, jnp.exp(jnp.minimum(gc_w - gr_w, 0.0)), 0.0)
                g_last = gc_w[CHUNK - 1:CHUNK, :] if d == 0 else gc_w[0:1, :]
                egc = jnp.exp(gc_w)
                l_mats.append(jnp.where(strict, (kk[hh, c] * beta * decay)[:, :CHUNK], 0.0))
                rhs_all.append(jnp.concatenate([v * beta, k * beta * egc], axis=1).astype(BF16))
                kt_ref[d, 0, r, ln] = (k * jnp.exp(g_last - gc_w)).astype(BF16)
                qd_ref[d, 0, r, ln] = (q * egc).astype(BF16)
                in_ref[d, 0, r, ln] = (qk[hh, c] * decay).astype(BF16)
                el_ref[d, 0, c * 8:(c + 1) * 8, ln] = jnp.broadcast_to(jnp.exp(g_last), (8, LANES))
                insts.append((d, r, ln))

    t_split = [_split_bf16(t) for t in _unit_tri_inverses(l_mats, i_idx, j_idx)]
    sols = [_dot(t_hi, rhs) + _dot(t_lo, rhs) for (t_hi, t_lo), rhs in zip(t_split, rhs_all)]
    for (d, r, ln), sol in zip(insts, sols):
        u_ref[d, 0, r, ln] = sol[:, :A_HEAD_DIM]
        w_ref[d, 0, r, ln] = sol[:, A_HEAD_DIM:].astype(BF16)


def _delta_prep(qkv, qkv_c, conv_w, ba, ba_c, coef, bsz, n_lat, n_ctx):
    assert n_ctx == DELTA_ROWS and n_lat % DELTA_ROWS == 0 and A_HEADS % DELTA_HEADS == 0
    groups = A_HEADS // DELTA_HEADS
    wd = DELTA_HEADS * A_HEAD_DIM
    lat_blocks = n_lat // DELTA_ROWS
    n_blocks = lat_blocks + 1
    n_all = n_lat + n_ctx
    halo_per_block = DELTA_ROWS // CONV_HALO
    n_halo = bsz * n_lat // CONV_HALO

    def lat_blk(b, j):
        return b * lat_blocks + jnp.maximum(j - 1, 0)

    def main(kd):
        return pl.BlockSpec((DELTA_ROWS, wd), lambda b, hg, j: (lat_blk(b, j), kd * groups + hg))

    def prev(kd):
        return pl.BlockSpec((CONV_HALO, wd),
                            lambda b, hg, j: (jnp.maximum(lat_blk(b, j) * halo_per_block - 1, 0), kd * groups + hg))

    def nxt(kd):
        return pl.BlockSpec((CONV_HALO, wd),
                            lambda b, hg, j: (jnp.minimum((lat_blk(b, j) + 1) * halo_per_block, n_halo - 1),
                                              kd * groups + hg))

    def ctxb(kd):
        return pl.BlockSpec((n_ctx, wd), lambda b, hg, j: (b, kd * groups + hg))

    def tap(kd):
        return pl.BlockSpec((SHORT_CONV_W, wd), lambda b, hg, j: (0, kd * groups + hg))

    in_specs = ([main(kd) for kd in range(3)] + [prev(kd) for kd in range(3)] + [nxt(kd) for kd in range(3)]
                + [ctxb(kd) for kd in range(3)] + [tap(kd) for kd in range(3)]
                + [pl.BlockSpec((DELTA_ROWS, wd), lambda b, hg, j: (lat_blk(b, j), hg)),
                   pl.BlockSpec((n_ctx, wd), lambda b, hg, j: (b, hg)),
                   pl.BlockSpec((DELTA_HEADS, 8, LANES), lambda b, hg, j: (hg, 0, 0))])
    big = pl.BlockSpec((2, 1, DELTA_ROWS, wd), lambda b, hg, j: (0, b, j, hg))
    small = pl.BlockSpec((2, 1, DELTA_BLOCK * 8, wd), lambda b, hg, j: (0, b, j, hg))
    sds = lambda rows, dt: jax.ShapeDtypeStruct((2, bsz, rows, A_WIDTH), dt)
    return pl.pallas_call(
        functools.partial(_dprep_kernel, n_blocks=n_blocks),
        grid=(bsz, groups, n_blocks),
        in_specs=in_specs,
        out_specs=[big, big, big, big, big, small],
        out_shape=[sds(n_all, F32), sds(n_all, BF16), sds(n_all, BF16), sds(n_all, BF16), sds(n_all, BF16),
                   sds(n_all // CHUNK * 8, F32)],
        compiler_params=_cparams(("arbitrary", "arbitrary", "arbitrary")),
        name="dprep",
    )(*([qkv] * 9), *([qkv_c] * 3), *([conv_w] * 3), ba, ba_c, coef)


def _dprep_kernel_single_head(q_ref, k_ref, v_ref, qp_ref, kp_ref, vp_ref, qn_ref, kn_ref, vn_ref, qc_ref, kc_ref,
                              vc_ref, wq_ref, wk_ref, wv_ref, ba_ref, bac_ref, coef_ref,
                              u_ref, w_ref, kt_ref, qd_ref, in_ref, el_ref, *, n_blocks):
    blk = pl.program_id(2)
    is_ctx = blk == 0
    rows = DELTA_ROWS
    ext_rows = rows + 2 * CONV_HALO

    def conv(main_ref, ctx_ref, prev_ref, next_ref, w_ref, kind):
        x = jnp.where(is_ctx, ctx_ref[...], main_ref[...])
        prev = jnp.where(blk <= 1, 0.0, prev_ref[...])
        nxt = jnp.where(jnp.logical_or(is_ctx, blk == n_blocks - 1), 0.0, next_ref[...])
        ext = jnp.concatenate([prev, x, nxt], axis=0)
        w = w_ref[...]
        acc = jnp.zeros((rows, A_HEAD_DIM), F32)
        for tap in range(SHORT_CONV_W):
            shift = (SHORT_CONV_W // 2 - tap) % ext_rows
            rolled = pltpu.roll(ext, shift, axis=0) if shift else ext
            acc = acc + rolled[CONV_HALO:CONV_HALO + rows, :] * w[tap:tap + 1, :]
        y = _silu(acc)
        if kind < 2:
            y = y * lax.rsqrt(jnp.sum(y * y, axis=-1, keepdims=True) + EPS)
        if kind == 0:
            y = y * (A_HEAD_DIM ** -0.5)
        return y

    q_all = conv(q_ref, qc_ref, qp_ref, qn_ref, wq_ref, 0)
    k_all = conv(k_ref, kc_ref, kp_ref, kn_ref, wk_ref, 1)
    v_all = conv(v_ref, vc_ref, vp_ref, vn_ref, wv_ref, 2)

    lane = lax.broadcasted_iota(jnp.int32, (1, LANES), 1)
    coef = coef_ref[0]
    neg_a = -jnp.exp(jnp.where(lane == 2, coef[0:1, :], coef[1:2, :]))
    dtb = jnp.where(lane == 2, coef[2:3, :], coef[3:4, :])
    raw = jnp.where(is_ctx, bac_ref[...], ba_ref[...])
    xg = raw + dtb
    softplus = jnp.maximum(xg, 0.0) + jnp.log(1.0 + jnp.exp(-jnp.abs(xg)))
    bg = jnp.where(lane < 2, jax.nn.sigmoid(raw), neg_a * softplus)

    i_idx, j_idx = _tri_masks()
    i_w = lax.broadcasted_iota(jnp.int32, (CHUNK, LANES), 0)
    j_w = lax.broadcasted_iota(jnp.int32, (CHUNK, LANES), 1)
    ones8 = jnp.ones((8, CHUNK), F32)
    chunks = [slice(c * CHUNK, (c + 1) * CHUNK) for c in range(DELTA_BLOCK)]

    kk, qk = [], []
    for r in chunks:
        kbf = k_all[r].astype(BF16)
        prod = _dot_nt(jnp.concatenate([kbf, q_all[r].astype(BF16)], axis=0),
                       jnp.concatenate([kbf, jnp.zeros_like(kbf)], axis=0))
        kk.append(prod[:CHUNK])
        qk.append(prod[CHUNK:])

    insts, l_mats, rhs_all = [], [], []
    for d in range(2):
        incl = (i_idx >= j_idx) if d == 0 else (i_idx <= j_idx)
        strict = (i_idx > j_idx) if d == 0 else (i_idx < j_idx)
        in_chunk = j_w < CHUNK
        incl_w = jnp.logical_and(in_chunk, (i_w >= j_w) if d == 0 else (i_w <= j_w))
        seen_w = jnp.logical_and(in_chunk, (i_w <= j_w) if d == 0 else (i_w >= j_w))
        g_cols = [jnp.broadcast_to(bg[r, 2 + d:3 + d], (CHUNK, LANES)) for r in chunks]
        gc_all = _dot_hi(incl.astype(F32), jnp.concatenate(g_cols, axis=1))
        gr_all = _dot_hi(ones8, jnp.concatenate([jnp.where(seen_w, g, 0.0) for g in g_cols], axis=1))
        for c, r in enumerate(chunks):
            q, k, v = q_all[r], k_all[r], v_all[r]
            beta = bg[r, d:d + 1]
            gc_w = gc_all[:, c * LANES:(c + 1) * LANES]
            gr_w = jnp.broadcast_to(gr_all[0:1, c * LANES:(c + 1) * LANES], (CHUNK, LANES))
            decay = jnp.where(incl_w, jnp.exp(jnp.minimum(gc_w - gr_w, 0.0)), 0.0)
            g_last = gc_w[CHUNK - 1:CHUNK, :] if d == 0 else gc_w[0:1, :]
            egc = jnp.exp(gc_w)
            l_mats.append(jnp.where(strict, (kk[c] * beta * decay)[:, :CHUNK], 0.0))
            rhs_all.append(jnp.concatenate([v * beta, k * beta * egc], axis=1).astype(BF16))
            kt_ref[d, 0, r, :] = (k * jnp.exp(g_last - gc_w)).astype(BF16)
            qd_ref[d, 0, r, :] = (q * egc).astype(BF16)
            in_ref[d, 0, r, :] = (qk[c] * decay).astype(BF16)
            el_ref[d, 0, c * 8:(c + 1) * 8, :] = jnp.broadcast_to(jnp.exp(g_last), (8, LANES))
            insts.append((d, r))

    t_split = [_split_bf16(t) for t in _unit_tri_inverses(l_mats, i_idx, j_idx)]
    sols = [_dot(t_hi, rhs) + _dot(t_lo, rhs) for (t_hi, t_lo), rhs in zip(t_split, rhs_all)]
    for (d, r), sol in zip(insts, sols):
        u_ref[d, 0, r, :] = sol[:, :A_HEAD_DIM]
        w_ref[d, 0, r, :] = sol[:, A_HEAD_DIM:].astype(BF16)


def _delta_prep_single_head(qkv, qkv_c, conv_w, ba, ba_c, coef, bsz, n_lat, n_ctx):
    assert n_ctx == DELTA_ROWS and n_lat % DELTA_ROWS == 0
    h = A_HEADS
    hd = A_HEAD_DIM
    lat_blocks = n_lat // DELTA_ROWS
    n_blocks = lat_blocks + 1
    n_all = n_lat + n_ctx
    halo_per_block = DELTA_ROWS // CONV_HALO
    n_halo = bsz * n_lat // CONV_HALO

    def lat_blk(b, j):
        return b * lat_blocks + jnp.maximum(j - 1, 0)

    def main(kd):
        return pl.BlockSpec((DELTA_ROWS, hd), lambda b, hh, j: (lat_blk(b, j), kd * h + hh))

    def prev(kd):
        return pl.BlockSpec((CONV_HALO, hd),
                            lambda b, hh, j: (jnp.maximum(lat_blk(b, j) * halo_per_block - 1, 0), kd * h + hh))

    def nxt(kd):
        return pl.BlockSpec((CONV_HALO, hd),
                            lambda b, hh, j: (jnp.minimum((lat_blk(b, j) + 1) * halo_per_block, n_halo - 1),
                                              kd * h + hh))

    def ctxb(kd):
        return pl.BlockSpec((n_ctx, hd), lambda b, hh, j: (b, kd * h + hh))

    def tap(kd):
        return pl.BlockSpec((SHORT_CONV_W, hd), lambda b, hh, j: (0, kd * h + hh))

    in_specs = ([main(kd) for kd in range(3)] + [prev(kd) for kd in range(3)] + [nxt(kd) for kd in range(3)]
                + [ctxb(kd) for kd in range(3)] + [tap(kd) for kd in range(3)]
                + [pl.BlockSpec((DELTA_ROWS, LANES), lambda b, hh, j: (lat_blk(b, j), hh)),
                   pl.BlockSpec((n_ctx, LANES), lambda b, hh, j: (b, hh)),
                   pl.BlockSpec((1, 8, LANES), lambda b, hh, j: (hh, 0, 0))])
    big = pl.BlockSpec((2, 1, DELTA_ROWS, hd), lambda b, hh, j: (0, b, j, hh))
    small = pl.BlockSpec((2, 1, DELTA_BLOCK * 8, LANES), lambda b, hh, j: (0, b, j, hh))
    sds = lambda rows, dt: jax.ShapeDtypeStruct((2, bsz, rows, h * hd), dt)
    return pl.pallas_call(
        functools.partial(_dprep_kernel, n_blocks=n_blocks),
        grid=(bsz, h, n_blocks),
        in_specs=in_specs,
        out_specs=[big, big, big, big, big, small],
        out_shape=[sds(n_all, F32), sds(n_all, BF16), sds(n_all, BF16), sds(n_all, BF16), sds(n_all, BF16),
                   sds(n_all // CHUNK * 8, F32)],
        compiler_params=_cparams(("arbitrary", "arbitrary", "arbitrary")),
        name="dprep",
    )(*([qkv] * 9), *([qkv_c] * 3), *([conv_w] * 3), ba, ba_c, coef)


def _dscan_kernel(uf, wf, ktf, qdf, inf, elf, ub, wb, ktb, qdb, inb, elb, of_ref, ob_ref, state):
    step = pl.program_id(1)

    @pl.when(step == 0)
    def _():
        state[...] = jnp.zeros(state.shape, F32)

    dirs = ((uf, wf, ktf, qdf, inf, elf, of_ref), (ub, wb, ktb, qdb, inb, elb, ob_ref))
    chains = [(d, hh) for d in range(2) for hh in range(A_HEADS)]
    s_mats = [state[d * A_HEADS + hh] for d, hh in chains]
    for t in range(DELTA_BLOCK):
        where = []
        for d, hh in chains:
            c = t if d == 0 else DELTA_BLOCK - 1 - t
            where.append((dirs[d], c, slice(c * CHUNK, (c + 1) * CHUNK),
                          slice(hh * A_HEAD_DIM, (hh + 1) * A_HEAD_DIM)))
        ws = [_dot(jnp.concatenate([rf[1][0, 0, r, ln], rf[3][0, 0, r, ln]], axis=0), s.astype(BF16))
              for (rf, c, r, ln), s in zip(where, s_mats)]
        v_new = [(rf[0][0, 0, r, ln] - x[:CHUNK]).astype(BF16) for (rf, c, r, ln), x in zip(where, ws)]
        for (rf, c, r, ln), x, vn in zip(where, ws, v_new):
            rf[6][r, ln] = x[CHUNK:] + _dot(rf[4][0, 0, r, ln][:, :CHUNK], vn)
        s_mats = [s * rf[5][0, 0, c * 8:c * 8 + 1, ln] + _dot_tn(rf[2][0, 0, r, ln], vn)
                  for (rf, c, r, ln), s, vn in zip(where, s_mats, v_new)]
    for (d, hh), s in zip(chains, s_mats):
        state[d * A_HEADS + hh] = s


def _delta_scan(pre, bsz, n_lat, n_ctx):
    u, w, kt, qd, intra, el = pre
    lat_blocks = n_lat // DELTA_ROWS
    n_blocks = lat_blocks + 1
    width = A_WIDTH

    def fwd_idx(b, s):
        return (0, b, s, 0)

    def bwd_idx(b, s):
        return (1, b, jnp.where(s == 0, 0, n_blocks - s), 0)

    def specs(idx):
        big = pl.BlockSpec((1, 1, DELTA_ROWS, width), idx)
        return [big, big, big, big, big, pl.BlockSpec((1, 1, DELTA_BLOCK * 8, width), idx)]

    out_f = pl.BlockSpec((DELTA_ROWS, width), lambda b, s: (b * lat_blocks + jnp.maximum(s - 1, 0), 0))
    out_b = pl.BlockSpec((DELTA_ROWS, width), lambda b, s: (b * lat_blocks + lat_blocks - jnp.maximum(s, 1), 0))
    shape = jax.ShapeDtypeStruct((bsz * n_lat, width), F32)
    return pl.pallas_call(
        _dscan_kernel,
        grid=(bsz, n_blocks),
        in_specs=specs(fwd_idx) + specs(bwd_idx),
        out_specs=[out_f, out_b],
        out_shape=[shape, shape],
        scratch_shapes=[pltpu.VMEM((2 * A_HEADS, A_HEAD_DIM, A_HEAD_DIM), F32)],
        compiler_params=_cparams(("arbitrary", "arbitrary")),
        name="dscan",
    )(u, w, kt, qd, intra, el, u, w, kt, qd, intra, el)


def _head_rms(x, gain_row, head_dim):
    outs = []
    for s in range(x.shape[1] // LANES):
        xs = x[:, s * LANES:(s + 1) * LANES]
        lane = lax.broadcasted_iota(jnp.int32, xs.shape, 1)
        sq = xs * xs
        scale = jnp.zeros_like(xs)
        for part in range(LANES // head_dim):
            m = jnp.logical_and(lane >= part * head_dim, lane < (part + 1) * head_dim)
            ms = jnp.sum(jnp.where(m, sq, 0.0), axis=-1, keepdims=True) * (1.0 / head_dim)
            scale = jnp.where(m, lax.rsqrt(ms + EPS), scale)
        outs.append(xs * scale)
    y = outs[0] if len(outs) == 1 else jnp.concatenate(outs, axis=1)
    return y * gain_row


def _rope(x, cos, sin_lo, sin_hi):
    width = x.shape[1]
    fwd = pltpu.roll(x, width - ROPE_AXIS_PAIRS, axis=1)
    back = pltpu.roll(x, ROPE_AXIS_PAIRS, axis=1)
    return x * cos + fwd * sin_lo + back * sin_hi


def _kvprep_kernel(k_ref, v_ref, g_ref, cos_ref, slo_ref, shi_ref, ko_ref, vo_ref):
    k = _head_rms(k_ref[0], g_ref[...], B_HEAD_DIM)
    k = _rope(k, cos_ref[...], slo_ref[...], shi_ref[...]).astype(BF16)
    vt = v_ref[0].T.astype(BF16)
    for hh in range(B_KV_HEADS):
        ko_ref[0, hh] = k[:, hh * B_HEAD_DIM:(hh + 1) * B_HEAD_DIM]
        vo_ref[0, hh] = vt[hh * B_HEAD_DIM:(hh + 1) * B_HEAD_DIM, :]


def _kvprep(k_all, v_all, gain, cos, slo, shi, ts):
    bsz, s_len, w = k_all.shape
    row = pl.BlockSpec((1, ts, w), lambda b, i: (b, i, 0))
    tab = pl.BlockSpec((ts, w), lambda b, i: (i, 0))
    return pl.pallas_call(
        _kvprep_kernel,
        grid=(bsz, s_len // ts),
        in_specs=[row, row, pl.BlockSpec((1, w), lambda b, i: (0, 0)), tab, tab, tab],
        out_specs=[pl.BlockSpec((1, B_KV_HEADS, ts, B_HEAD_DIM), lambda b, i: (b, 0, i, 0)),
                   pl.BlockSpec((1, B_KV_HEADS, B_HEAD_DIM, ts), lambda b, i: (b, 0, 0, i))],
        out_shape=[jax.ShapeDtypeStruct((bsz, B_KV_HEADS, s_len, B_HEAD_DIM), BF16),
                   jax.ShapeDtypeStruct((bsz, B_KV_HEADS, B_HEAD_DIM, s_len), BF16)],
        compiler_params=_cparams(("arbitrary", "arbitrary")),
        name="kvprep",
    )(k_all, v_all, gain, cos, slo, shi)


LOG2_E = math.log2(math.e)
ATTN_SCORE_KEYS = 64
ATTN_VALUE_KEYS = 128
ATTN_Q_COLS = 256


def _attn_kernel(q_ref, g_ref, cos_ref, slo_ref, shi_ref, k_ref, vt_ref, o_ref, s_scr):
    q = _head_rms(q_ref[...], g_ref[...], B_HEAD_DIM)
    q = _rope(q, cos_ref[...], slo_ref[...], shi_ref[...]) * (B_HEAD_DIM ** -0.5 * LOG2_E)
    tq = q.shape[0]
    qt = q.T.astype(BF16)
    n_keys = k_ref.shape[2]
    halves = [slice(c, c + ATTN_Q_COLS) for c in range(0, tq, ATTN_Q_COLS)]

    def score_steps(g):
        qg = qt[g * B_HEAD_DIM:(g + 1) * B_HEAD_DIM, :]
        m8 = None
        for kb in range(0, n_keys, ATTN_SCORE_KEYS):
            st = _dot(k_ref[0, 0, kb:kb + ATTN_SCORE_KEYS, :], qg)
            s_scr[g % 2, kb:kb + ATTN_SCORE_KEYS, :] = st
            bm = jnp.max(st.reshape(ATTN_SCORE_KEYS // 8, 8, tq), axis=0)
            m8 = bm if m8 is None else jnp.maximum(m8, bm)
            if (kb // ATTN_SCORE_KEYS) % (ATTN_VALUE_KEYS // ATTN_SCORE_KEYS) == 1:
                yield None
        yield jnp.max(m8, axis=0, keepdims=True)

    def value_steps(g, m):
        l8 = [jnp.zeros((8, ATTN_Q_COLS), F32) for _ in halves]
        acc = [jnp.zeros((B_HEAD_DIM, ATTN_Q_COLS), F32) for _ in halves]
        for kb in range(0, n_keys, ATTN_VALUE_KEYS):
            vt = vt_ref[0, 0, :, kb:kb + ATTN_VALUE_KEYS]
            for i, cols in enumerate(halves):
                p = jnp.exp2(s_scr[g % 2, kb:kb + ATTN_VALUE_KEYS, cols] - m[:, cols])
                l8[i] = l8[i] + jnp.sum(p.reshape(ATTN_VALUE_KEYS // 8, 8, ATTN_Q_COLS), axis=0)
                acc[i] = acc[i] + _dot(vt, p.astype(BF16))
            yield None
        yield jnp.concatenate([a / jnp.sum(l, axis=0, keepdims=True) for a, l in zip(acc, l8)], axis=1)

    def drain(*gens):
        last = [None] * len(gens)
        live = list(range(len(gens)))
        while live:
            for i in list(live):
                try:
                    last[i] = next(gens[i])
                except StopIteration:
                    live.remove(i)
        return last

    outs = []
    (m,) = drain(score_steps(0))
    for g in range(B_GROUP):
        if g + 1 < B_GROUP:
            out, m = drain(value_steps(g, m), score_steps(g + 1))
        else:
            (out,) = drain(value_steps(g, m))
        outs.append(out)
    o_ref[...] = jnp.concatenate(outs, axis=0).T


def _attention(qb, gain, cos, slo, shi, k_hm, vt_hm, n_lat, tq):
    n, _ = qb.shape
    bsz, _, s_len, _ = k_hm.shape
    gw = B_GROUP * B_HEAD_DIM
    nq = n_lat // tq
    kv = pl.BlockSpec((1, 1, s_len, B_HEAD_DIM), lambda b, kh, i: (b, kh, 0, 0))
    vts = pl.BlockSpec((1, 1, B_HEAD_DIM, s_len), lambda b, kh, i: (b, kh, 0, 0))
    tab = pl.BlockSpec((tq, gw), lambda b, kh, i: (i, 0))
    return pl.pallas_call(
        _attn_kernel,
        grid=(bsz, B_KV_HEADS, nq),
        in_specs=[pl.BlockSpec((tq, gw), lambda b, kh, i: (b * nq + i, kh)),
                  pl.BlockSpec((1, gw), lambda b, kh, i: (0, 0)), tab, tab, tab, kv, vts],
        out_specs=pl.BlockSpec((tq, gw), lambda b, kh, i: (b * nq + i, kh)),
        out_shape=jax.ShapeDtypeStruct((n, B_WIDTH), F32),
        scratch_shapes=[pltpu.VMEM((2, s_len, tq), F32)],
        compiler_params=_cparams(("arbitrary", "arbitrary", "arbitrary")),
        name="attn",
    )(qb, gain, cos, slo, shi, k_hm, vt_hm)


def _gated_residual(x, y, mod_ref, gain_ref, gate_row):
    return x + mod_ref[0, gate_row:gate_row + 1, :] * _rms(y, gain_ref[...])


def _outproj_kernel(of_ref, ob_ref, z_ref, on_ref, yb_ref, x_ref, mod_ref, g_ref, wa_ref, wb_ref, o_ref):
    o = of_ref[...] + ob_ref[...]
    z = z_ref[...]
    parts = []
    for hh in range(A_HEADS):
        sl = slice(hh * A_HEAD_DIM, (hh + 1) * A_HEAD_DIM)
        parts.append((_rms(o[:, sl], on_ref[...]) * _silu(z[:, sl])).astype(BF16))
    ya = jnp.concatenate(parts, axis=1)
    y = _dot(ya, wa_ref[...]) + _dot(yb_ref[...].astype(BF16), wb_ref[...])
    o_ref[...] = _gated_residual(x_ref[...], y, mod_ref, g_ref, 2)


def _outproj(o_f, o_b, z, out_norm, yb, x2, mod, gain, wa, wb, rows_per_mod, tm):
    n, d = x2.shape
    bpm = rows_per_mod // tm
    return pl.pallas_call(
        _outproj_kernel,
        grid=(n // tm,),
        in_specs=[pl.BlockSpec((tm, o_f.shape[1]), lambda i: (i, 0)),
                  pl.BlockSpec((tm, o_b.shape[1]), lambda i: (i, 0)),
                  pl.BlockSpec((tm, z.shape[1]), lambda i: (i, 0)),
                  pl.BlockSpec((1, A_HEAD_DIM), lambda i: (0, 0)),
                  pl.BlockSpec((tm, yb.shape[1]), lambda i: (i, 0)),
                  pl.BlockSpec((tm, d), lambda i: (i, 0)),
                  pl.BlockSpec((1, 6, d), lambda i: (i // bpm, 0, 0)),
                  pl.BlockSpec((1, d), lambda i: (0, 0)),
                  _resident(wa.shape),
                  _resident(wb.shape)],
        out_specs=pl.BlockSpec((tm, d), lambda i: (i, 0)),
        out_shape=jax.ShapeDtypeStruct((n, d), F32),
        compiler_params=_cparams(("arbitrary",)),
        name="outproj",
    )(o_f, o_b, z, out_norm, yb, x2, mod, gain, wa, wb)


def _ffn_kernel(x_ref, mod_ref, gpre_ref, gpost_ref, wi_ref, wo_ref, o_ref, *, hidden, hc):
    x = x_ref[...]
    a = _modulated(x, mod_ref, gpre_ref, 3).astype(BF16)
    acc = jnp.zeros(x.shape, F32)
    for c in range(hidden // hc):
        gate = _dot(a, wi_ref[:, c * hc:(c + 1) * hc])
        up = _dot(a, wi_ref[:, hidden + c * hc:hidden + (c + 1) * hc])
        acc = acc + _dot((_silu(gate) * up).astype(BF16), wo_ref[c * hc:(c + 1) * hc, :])
    o_ref[...] = _gated_residual(x, acc, mod_ref, gpost_ref, 5)


def _ffn(x2, mod, gpre, gpost, wi, wo, rows_per_mod, tm):
    n, d = x2.shape
    hidden = wo.shape[0]
    bpm = rows_per_mod // tm
    return pl.pallas_call(
        functools.partial(_ffn_kernel, hidden=hidden, hc=256),
        grid=(n // tm,),
        in_specs=[pl.BlockSpec((tm, d), lambda i: (i, 0)),
                  pl.BlockSpec((1, 6, d), lambda i: (i // bpm, 0, 0)),
                  pl.BlockSpec((1, d), lambda i: (0, 0)),
                  pl.BlockSpec((1, d), lambda i: (0, 0)),
                  _resident(wi.shape),
                  _resident(wo.shape)],
        out_specs=pl.BlockSpec((tm, d), lambda i: (i, 0)),
        out_shape=jax.ShapeDtypeStruct((n, d), F32),
        compiler_params=_cparams(("arbitrary",)),
        name="ffn",
    )(x2, mod, gpre, gpost, wi, wo)


def _confin_kernel(x_ref, mod_ref, g_ref, w_ref, b_ref, o_ref, *, width):
    a = _modulated(x_ref[...], mod_ref, g_ref, 0).astype(BF16)
    val = _dot(a, w_ref[:, :width]) + b_ref[:, :width]
    gate = _dot(a, w_ref[:, width:]) + b_ref[:, width:]
    o_ref[...] = val * jax.nn.sigmoid(gate)


def _confin(x2, mod, gain, w, b, rows_per_mod, tm):
    n, d = x2.shape
    width = w.shape[1] // 2
    bpm = rows_per_mod // tm
    return pl.pallas_call(
        functools.partial(_confin_kernel, width=width),
        grid=(n // tm,),
        in_specs=[pl.BlockSpec((tm, d), lambda i: (i, 0)),
                  pl.BlockSpec((1, 6, d), lambda i: (i // bpm, 0, 0)),
                  pl.BlockSpec((1, d), lambda i: (0, 0)),
                  _resident(w.shape),
                  pl.BlockSpec((1, 2 * width), lambda i: (0, 0))],
        out_specs=pl.BlockSpec((tm, width), lambda i: (i, 0)),
        out_shape=jax.ShapeDtypeStruct((n, width), F32),
        compiler_params=_cparams(("arbitrary",)),
        name="confin",
    )(x2, mod, gain, w, b)


def _confout_kernel(u_ref, up_ref, un_ref, x_ref, mod_ref, g_ref, dww_ref, dwb_ref, lng_ref, lnb_ref,
                    w_ref, b_ref, o_ref, ext, conv, shifted, *, tm, tiles_per_seq):
    i = pl.program_id(0)
    pos = i % tiles_per_seq
    width = u_ref.shape[1]
    halo = CONF_HALO
    ext[pl.ds(halo, tm), :] = u_ref[...]
    ext[0:halo, :] = jnp.where(pos == 0, 0.0, up_ref[...])
    ext[pl.ds(halo + tm, halo), :] = jnp.where(pos == tiles_per_seq - 1, 0.0, un_ref[...])

    rb = 64
    pad = CONF_KERNEL // 2
    sub = 8
    copy_rows = tm + 2 * halo - sub

    def col_body(c, carry):
        cs = pl.ds(pl.multiple_of(c * LANES, LANES), LANES)
        w = dww_ref[:, cs]
        for s in range(sub):
            shifted[s, :, :] = ext[pl.ds(s, copy_rows), cs]
        for r in range(tm // rb):
            acc = jnp.zeros((rb, LANES), F32)
            for tap in range(CONF_KERNEL):
                off = halo - pad + tap
                acc = acc + shifted[off % sub, pl.ds(r * rb + off - off % sub, rb), :] * w[tap:tap + 1, :]
            conv[pl.ds(r * rb, rb), cs] = acc
        return carry

    lax.fori_loop(0, width // LANES, col_body, 0)

    y = conv[...] + dwb_ref[...]
    mu = jnp.mean(y, axis=-1, keepdims=True)
    yc = y - mu
    var = jnp.mean(yc * yc, axis=-1, keepdims=True)
    y = _silu(yc * lax.rsqrt(var + EPS) * lng_ref[...] + lnb_ref[...])
    out = _dot(y.astype(BF16), w_ref[...]) + b_ref[...]
    o_ref[...] = _gated_residual(x_ref[...], out, mod_ref, g_ref, 2)


def _confout(u, x2, mod, gain, dww, dwb, lng, lnb, w, b, rows_per_mod, tm):
    n, d = x2.shape
    width = u.shape[1]
    tiles_per_seq = rows_per_mod // tm
    hb = tm // CONF_HALO
    n_halo_blocks = n // CONF_HALO
    vec = lambda wd: pl.BlockSpec((1, wd), lambda i: (0, 0))
    return pl.pallas_call(
        functools.partial(_confout_kernel, tm=tm, tiles_per_seq=tiles_per_seq),
        grid=(n // tm,),
        in_specs=[pl.BlockSpec((tm, width), lambda i: (i, 0)),
                  pl.BlockSpec((CONF_HALO, width), lambda i: (jnp.maximum(i * hb - 1, 0), 0)),
                  pl.BlockSpec((CONF_HALO, width), lambda i: (jnp.minimum((i + 1) * hb, n_halo_blocks - 1), 0)),
                  pl.BlockSpec((tm, d), lambda i: (i, 0)),
                  pl.BlockSpec((1, 6, d), lambda i: (i // tiles_per_seq, 0, 0)),
                  vec(d),
                  pl.BlockSpec(dww.shape, lambda i: (0, 0)),
                  vec(width), vec(width), vec(width),
                  _resident(w.shape),
                  vec(d)],
        out_specs=pl.BlockSpec((tm, d), lambda i: (i, 0)),
        out_shape=jax.ShapeDtypeStruct((n, d), F32),
        scratch_shapes=[pltpu.VMEM((tm + 2 * CONF_HALO, width), F32), pltpu.VMEM((tm, width), F32),
                        pltpu.VMEM((8, tm + 2 * CONF_HALO - 8, LANES), F32)],
        compiler_params=_cparams(("arbitrary",)),
        name="confout",
    )(u, u, u, x2, mod, gain, dww, dwb, lng, lnb, w, b)


def _rope_tables(n_tokens, n_ctx, reps):
    rows = n_tokens // GRID_W
    row = jnp.broadcast_to(jnp.arange(rows, dtype=F32)[:, None], (rows, GRID_W)).reshape(n_tokens)
    col = jnp.broadcast_to(jnp.arange(GRID_W, dtype=F32)[None, :], (rows, GRID_W)).reshape(n_tokens)
    inv_freq = ROPE_THETA ** (-jnp.arange(ROPE_AXIS_PAIRS, dtype=F32) / ROPE_AXIS_PAIRS)
    ang_r = row[:, None] * inv_freq
    ang_c = col[:, None] * inv_freq
    ang = jnp.concatenate([ang_r, ang_r, ang_c, ang_c], axis=-1)
    cos, sin = jnp.cos(ang), jnp.sin(ang)
    first_half = (jnp.arange(B_HEAD_DIM) % (2 * ROPE_AXIS_PAIRS)) < ROPE_AXIS_PAIRS
    sin_lo = jnp.where(first_half, -sin, 0.0)
    sin_hi = jnp.where(first_half, 0.0, sin)
    if n_ctx:
        cos = jnp.concatenate([jnp.ones((n_ctx, B_HEAD_DIM), F32), cos], axis=0)
        zeros = jnp.zeros((n_ctx, B_HEAD_DIM), F32)
        sin_lo = jnp.concatenate([zeros, sin_lo], axis=0)
        sin_hi = jnp.concatenate([zeros, sin_hi], axis=0)
    tile = lambda t: jnp.tile(t, (1, reps))
    return tile(cos), tile(sin_lo), tile(sin_hi)


def _hybrid_in_weight(w_in):
    off_z = 3 * A_WIDTH
    off_ba = off_z + A_WIDTH
    off_q = off_ba + 4 * A_HEADS
    off_k = off_q + B_WIDTH
    off_v = off_k + B_KV_WIDTH
    d = w_in.shape[0]
    ba = w_in[:, off_ba:off_q].reshape(d, 2, 2, A_HEADS)
    ba = ba.transpose(0, 3, 1, 2).reshape(d, A_HEADS, 4)
    ba = jnp.pad(ba, ((0, 0), (0, 0), (0, LANES - 4))).reshape(d, A_HEADS * LANES)
    w = jnp.concatenate([w_in[:, :off_z], w_in[:, off_z:off_ba], w_in[:, off_q:off_k], w_in[:, off_k:off_v],
                         w_in[:, off_v:], ba], axis=1)
    widths = (3 * A_WIDTH, A_WIDTH, B_WIDTH, B_KV_WIDTH, B_KV_WIDTH, A_HEADS * LANES)
    return w.astype(BF16), widths


def _row(v):
    return v.reshape(1, -1)


def kernel(x, c, ctx, c_ctx, w_mod, b_mod, g_mix_pre, g_mix_post, g_ffn_pre, g_ffn_post, w_ffn_in, w_ffn_out,
           hyb_w_in, hyb_conv_w, hyb_a_log, hyb_dt_bias, hyb_out_norm, hyb_q_norm, hyb_k_norm, hyb_w_out,
           conf_w_in, conf_b_in, conf_dw_w, conf_dw_b, conf_ln_g, conf_ln_b, conf_w_out, conf_b_out):
    bsz, n_lat, d = x.shape
    n_ctx = ctx.shape[1]
    depth = w_mod.shape[0]
    n = bsz * n_lat
    tm = 1024

    cond =jnp.zeros((8, d), F32).at[:bsz].set(c).at[bsz].set(c_ctx)
    mods = _ada_terms(cond, w_mod, b_mod).reshape(depth, 8, 6, d)

    h = x.reshape(n, d)
    hc = ctx.reshape(bsz * n_ctx, d)
    for layer in range(depth):
        idx = layer // 2
        mod = mods[layer, :bsz]
        mod_ctx = mods[layer, bsz:bsz + 1]
        if layer % 2 == 0:
            w_in, widths = _hybrid_in_weight(hyb_w_in[idx])
            gpre = _row(g_mix_pre[layer])
            qkv, z, qb, kb, vb, ba = _inproj(h, mod, gpre, w_in, widths, n_lat, tm)
            qkv_c, _, _, kb_c, vb_c, ba_c = _inproj(hc, mod_ctx, gpre, w_in, widths, bsz * n_ctx, n_ctx)

            coef = jnp.concatenate([hyb_a_log[idx], hyb_dt_bias[idx]], axis=0)
            coef = jnp.pad(coef.T[:, :, None], ((0, 0), (0, 4), (0, 0)))
            coef = jnp.broadcast_to(coef, (A_HEADS, 8, LANES))
            o_f, o_b = _delta_scan(_delta_prep(qkv, qkv_c, hyb_conv_w[idx], ba, ba_c, coef, bsz, n_lat, n_ctx),
                                   bsz, n_lat, n_ctx)

            k_all = jnp.concatenate([kb_c.reshape(bsz, n_ctx, -1), kb.reshape(bsz, n_lat, -1)], axis=1)
            v_all = jnp.concatenate([vb_c.reshape(bsz, n_ctx, -1), vb.reshape(bsz, n_lat, -1)], axis=1)
            cos_k, slo_k, shi_k = _rope_tables(n_lat, n_ctx, B_KV_HEADS)
            k_hm, vt_hm = _kvprep(k_all, v_all, _row(jnp.tile(hyb_k_norm[idx], B_KV_HEADS)),
                                  cos_k, slo_k, shi_k, n_ctx)
            cos_q, slo_q, shi_q = _rope_tables(n_lat, 0, B_GROUP)
            yb = _attention(qb, _row(jnp.tile(hyb_q_norm[idx], B_GROUP)), cos_q, slo_q, shi_q,
                            k_hm, vt_hm, n_lat, 512)

            w_out = hyb_w_out[idx].astype(BF16)
            h = _outproj(o_f, o_b, z, _row(hyb_out_norm[idx]), yb, h, mod, _row(g_mix_post[layer]),
                         w_out[:A_WIDTH], w_out[A_WIDTH:], n_lat, tm)
        else:
            u = _confin(h, mod, _row(g_mix_pre[layer]), conf_w_in[idx].astype(BF16), _row(conf_b_in[idx]),
                        n_lat, tm)
            h = _confout(u, h, mod, _row(g_mix_post[layer]), conf_dw_w[idx], _row(conf_dw_b[idx]),
                         _row(conf_ln_g[idx]), _row(conf_ln_b[idx]), conf_w_out[idx].astype(BF16),
                         _row(conf_b_out[idx]), n_lat, 256)
        h = _ffn(h, mod, _row(g_ffn_pre[layer]), _row(g_ffn_post[layer]), w_ffn_in[layer].astype(BF16),
                 w_ffn_out[layer].astype(BF16), n_lat, 1024)
        assert not any(j % 2 == 0 for j in range(layer + 1, depth)), "context advance not implemented"
    return h.reshape(bsz, n_lat, d)
```

```python
import functools
import math

import jax
import jax.numpy as jnp
import numpy as np
from jax import lax
from jax.experimental import pallas as pl
from jax.experimental.pallas import tpu as pltpu

F32 = jnp.float32
BF16 = jnp.bfloat16
HIGHEST = lax.Precision.HIGHEST

EPS = 1e-6
GRID_W = 64
ROPE_THETA = 10000.0
A_HEADS = 4
A_HEAD_DIM = 128
A_WIDTH = A_HEADS * A_HEAD_DIM
SHORT_CONV_W = 5
CHUNK = 64
B_Q_HEADS = 8
B_KV_HEADS = 2
B_HEAD_DIM = 64
B_GROUP = B_Q_HEADS // B_KV_HEADS
B_WIDTH = B_Q_HEADS * B_HEAD_DIM
B_KV_WIDTH = B_KV_HEADS * B_HEAD_DIM
ROPE_AXIS_PAIRS = B_HEAD_DIM // 4
CONF_KERNEL = 31
CONF_HALO = 16
LANES = 128
VMEM_LIMIT = 56 * 1024 * 1024


def _cparams(sem):
    return pltpu.CompilerParams(dimension_semantics=sem, vmem_limit_bytes=VMEM_LIMIT)


def _resident(shape):
    return pl.BlockSpec(shape, lambda i: (0, 0), pipeline_mode=pl.Buffered(1))


def _silu(x):
    return x * jax.nn.sigmoid(x)


def _dot(a, b):
    return jnp.dot(a, b, preferred_element_type=F32)


def _dot_hi(a, b):
    return jnp.dot(a, b, preferred_element_type=F32, precision=HIGHEST)


def _dot_nt(a, b):
    return lax.dot_general(a, b, (((1,), (1,)), ((), ())), preferred_element_type=F32)


def _dot_tn(a, b):
    return lax.dot_general(a, b, (((0,), (0,)), ((), ())), preferred_element_type=F32)


def _rms(x, gain):
    return x * lax.rsqrt(jnp.mean(x * x, axis=-1, keepdims=True) + EPS) * gain


def _ada_kernel(c_ref, w_ref, b_ref, o_ref):
    o_ref[0] = _dot_hi(_silu(c_ref[...]), w_ref[0]) + b_ref[0]


def _ada_terms(cond, w_mod, b_mod):
    depth, d, n6 = w_mod.shape
    tn = 1536
    return pl.pallas_call(
        _ada_kernel,
        grid=(depth, n6 // tn),
        in_specs=[pl.BlockSpec((8, d), lambda l, j: (0, 0)),
                  pl.BlockSpec((1, d, tn), lambda l, j: (l, 0, j)),
                  pl.BlockSpec((1, 1, tn), lambda l, j: (l, 0, j))],
        out_specs=pl.BlockSpec((1, 8, tn), lambda l, j: (l, 0, j)),
        out_shape=jax.ShapeDtypeStruct((depth, 8, n6), F32),
        compiler_params=_cparams(("arbitrary", "arbitrary")),
        name="ada",
    )(cond, w_mod, b_mod.reshape(depth, 1, n6))


def _modulated(x, mod_ref, gain_ref, shift_row):
    y = _rms(x, gain_ref[...])
    return y * (1.0 + mod_ref[0, shift_row + 1:shift_row + 2, :]) + mod_ref[0, shift_row:shift_row + 1, :]


PROJ_HALO = 16


def _hyb_inproj_kernel(x_ref, xp_ref, xn_ref, mod_ref, g_ref, w_ref, cw_ref, kg_ref, cos_ref, slo_ref, shi_ref,
                       qkv_ref, z_ref, qb_ref, ba_ref, ko_ref, vo_ref, *, tiles_per_seq):
    pos = pl.program_id(0) % tiles_per_seq
    tm = x_ref.shape[0]
    halo = PROJ_HALO
    qkv_w = 3 * A_WIDTH
    a_prev = jnp.where(pos > 0, _modulated(xp_ref[...], mod_ref, g_ref, 0), 0.0).astype(BF16)
    a_next = jnp.where(pos < tiles_per_seq - 1, _modulated(xn_ref[...], mod_ref, g_ref, 0), 0.0).astype(BF16)
    a = _modulated(x_ref[...], mod_ref, g_ref, 0).astype(BF16)
    a_ext = jnp.concatenate([a_prev, a, a_next], axis=0)

    off_z = qkv_w
    off_q = off_z + A_WIDTH
    off_k = off_q + B_WIDTH
    off_v = off_k + B_KV_WIDTH
    off_ba = off_v + B_KV_WIDTH

    def gate_proj():
        z_ref[...] = _dot(a, w_ref[:, off_z:off_q])

    def query_proj():
        qb_ref[...] = _dot(a, w_ref[:, off_q:off_k])

    def key_value_proj():
        kb = _dot(a, w_ref[:, off_k:off_v])
        vb = _dot(a, w_ref[:, off_v:off_ba])
        k = _rope(_head_rms(kb, kg_ref[...], B_HEAD_DIM), cos_ref[...], slo_ref[...], shi_ref[...]).astype(BF16)
        vt = vb.T.astype(BF16)
        for hh in range(B_KV_HEADS):
            ko_ref[0, hh] = k[:, hh * B_HEAD_DIM:(hh + 1) * B_HEAD_DIM]
            vo_ref[0, hh, :B_HEAD_DIM, :] = vt[hh * B_HEAD_DIM:(hh + 1) * B_HEAD_DIM, :]
            vo_ref[0, hh, B_HEAD_DIM:, :] = jnp.ones((ATTN_VT_ROWS - B_HEAD_DIM, tm), BF16)

    def logit_proj():
        ba_ref[...] = _dot(a, w_ref[:, off_ba:off_ba + A_HEADS * LANES])

    others = [gate_proj, query_proj, key_value_proj, logit_proj]
    ext_rows = tm + 2 * halo
    pair = 2 * A_HEAD_DIM
    n_pairs = qkv_w // pair
    ext_next = _dot(a_ext, w_ref[:, :pair])
    for cb in range(n_pairs):
        cols = slice(cb * pair, (cb + 1) * pair)
        ext = ext_next
        if cb + 1 < n_pairs:
            ext_next = _dot(a_ext, w_ref[:, (cb + 1) * pair:(cb + 2) * pair])
        if cb < len(others):
            others[cb]()
        w = cw_ref[:, cols]
        acc = jnp.zeros((tm, pair), F32)
        for tap in range(SHORT_CONV_W):
            shift = (SHORT_CONV_W // 2 - tap) % ext_rows
            rolled = pltpu.roll(ext, shift, axis=0) if shift else ext
            acc = acc + rolled[halo:halo + tm, :] * w[tap:tap + 1, :]
        y = _silu(acc)
        for part in range(2):
            head = 2 * cb + part
            yh = y[:, part * A_HEAD_DIM:(part + 1) * A_HEAD_DIM]
            if head < 2 * A_HEADS:
                yh = yh * lax.rsqrt(jnp.sum(yh * yh, axis=-1, keepdims=True) + EPS)
            if head < A_HEADS:
                yh = yh * (A_HEAD_DIM ** -0.5)
            qkv_ref[:, head * A_HEAD_DIM:(head + 1) * A_HEAD_DIM] = yh


def _hyb_inproj(x2, mod, gain, w, conv_w, k_gain, cos, slo, shi, bsz, rows_per_seq, tm):
    n, d = x2.shape
    tiles_per_seq = rows_per_seq // tm
    hb = tm // PROJ_HALO
    n_halo = n // PROJ_HALO
    f32_out = lambda wd: (pl.BlockSpec((tm, wd), lambda i: (i, 0)), jax.ShapeDtypeStruct((n, wd), F32))
    outs = [f32_out(3 * A_WIDTH), f32_out(A_WIDTH), f32_out(B_WIDTH), f32_out(A_HEADS * LANES),
            (pl.BlockSpec((1, B_KV_HEADS, tm, B_HEAD_DIM), lambda i: (i // tiles_per_seq, 0, i % tiles_per_seq, 0)),
             jax.ShapeDtypeStruct((bsz, B_KV_HEADS, rows_per_seq, B_HEAD_DIM), BF16)),
            (pl.BlockSpec((1, B_KV_HEADS, ATTN_VT_ROWS, tm), lambda i: (i // tiles_per_seq, 0, 0, i % tiles_per_seq)),
             jax.ShapeDtypeStruct((bsz, B_KV_HEADS, ATTN_VT_ROWS, rows_per_seq), BF16))]
    tab = pl.BlockSpec((tm, B_KV_WIDTH), lambda i: (i % tiles_per_seq, 0))
    return pl.pallas_call(
        functools.partial(_hyb_inproj_kernel, tiles_per_seq=tiles_per_seq),
        grid=(n // tm,),
        in_specs=[pl.BlockSpec((tm, d), lambda i: (i, 0)),
                  pl.BlockSpec((PROJ_HALO, d), lambda i: (jnp.maximum(i * hb - 1, 0), 0)),
                  pl.BlockSpec((PROJ_HALO, d), lambda i: (jnp.minimum((i + 1) * hb, n_halo - 1), 0)),
                  pl.BlockSpec((1, 6, d), lambda i: (i // tiles_per_seq if mod.shape[0] > 1 else 0, 0, 0)),
                  pl.BlockSpec((1, d), lambda i: (0, 0)),
                  _resident(w.shape),
                  _resident(conv_w.shape),
                  pl.BlockSpec((1, B_KV_WIDTH), lambda i: (0, 0)), tab, tab, tab],
        out_specs=[o[0] for o in outs],
        out_shape=[o[1] for o in outs],
        compiler_params=_cparams(("arbitrary",)),
        name="inproj",
    )(x2, x2, x2, mod, gain, w, conv_w, k_gain, cos, slo, shi)


def _tri_masks():
    i = lax.broadcasted_iota(jnp.int32, (CHUNK, CHUNK), 0)
    j = lax.broadcasted_iota(jnp.int32, (CHUNK, CHUNK), 1)
    return i, j


def _unit_tri_inverse(l_mat, i, j):
    eye = (i == j).astype(F32)
    same16 = jnp.right_shift(i, 4) == jnp.right_shift(j, 4)
    same32 = jnp.right_shift(i, 5) == jnp.right_shift(j, 5)
    d1 = jnp.where(same16, l_mat, 0.0)
    c1 = jnp.where(jnp.logical_and(same32, jnp.logical_not(same16)), l_mat, 0.0)
    c2 = jnp.where(same32, 0.0, l_mat)
    p = eye - d1
    d2 = _dot_hi(d1, d1)
    p = p + _dot_hi(p, d2)
    d4 = _dot_hi(d2, d2)
    p = p + _dot_hi(p, d4)
    d8 = _dot_hi(d4, d4)
    p = p + _dot_hi(p, d8)
    t1 = p - _dot_hi(p, _dot_hi(c1, p))
    return t1 - _dot_hi(t1, _dot_hi(c2, t1))


def _delta_kernel(q_ref, k_ref, v_ref, qc_ref, kc_ref, vc_ref, wq_ref, wk_ref, wv_ref,
                  ba_ref, bac_ref, coef_ref, z_ref, on_ref, o_ref,
                  xp, qs, ks, vs, bs, of_s, ob_s, *, n_lat, n_ctx):
    n_all = n_ctx + n_lat
    nch_ctx = n_ctx // CHUNK
    nch_all = n_all // CHUNK
    rb = 256

    def conv_into(src_ref, w_ref, dst, dst_off, n_rows, kind):
        xp[0:8, :] = jnp.zeros((8, A_HEAD_DIM), F32)
        xp[pl.ds(8, n_rows), :] = src_ref[...]
        xp[pl.ds(8 + n_rows, 8), :] = jnp.zeros((8, A_HEAD_DIM), F32)
        w = w_ref[...]

        def body(t, carry):
            r0 = pl.multiple_of(t * rb, rb)
            blk = xp[pl.ds(r0, rb + 16), :]
            acc = jnp.zeros((rb, A_HEAD_DIM), F32)
            for tap in range(SHORT_CONV_W):
                shift = (SHORT_CONV_W // 2 - tap) % (rb + 16)
                rolled = pltpu.roll(blk, shift, axis=0) if shift else blk
                acc = acc + rolled[8:8 + rb, :] * w[tap:tap + 1, :]
            y = _silu(acc)
            if kind < 2:
                y = y * lax.rsqrt(jnp.sum(y * y, axis=-1, keepdims=True) + EPS)
            if kind == 0:
                y = y * (A_HEAD_DIM ** -0.5)
            dst[pl.ds(pl.multiple_of(dst_off + r0, rb), rb), :] = y
            return carry

        lax.fori_loop(0, n_rows // rb, body, 0)

    for kind, (lat_ref, ctx_ref, w_ref, dst) in enumerate(
            ((q_ref, qc_ref, wq_ref, qs), (k_ref, kc_ref, wk_ref, ks), (v_ref, vc_ref, wv_ref, vs))):
        conv_into(ctx_ref, w_ref, dst, 0, n_ctx, kind)
        conv_into(lat_ref, w_ref, dst, n_ctx, n_lat, kind)

    lane = lax.broadcasted_iota(jnp.int32, (1, LANES), 1)
    coef = coef_ref[0]
    neg_a = -jnp.exp(jnp.where(lane == 2, coef[0:1, :], coef[1:2, :]))
    dtb = jnp.where(lane == 2, coef[2:3, :], coef[3:4, :])

    def beta_g(raw):
        xg = raw + dtb
        softplus = jnp.maximum(xg, 0.0) + jnp.log(1.0 + jnp.exp(-jnp.abs(xg)))
        return jnp.where(lane < 2, jax.nn.sigmoid(raw), neg_a * softplus)

    bs[pl.ds(0, n_ctx), :] = beta_g(bac_ref[...])
    bs[pl.ds(n_ctx, n_lat), :] = beta_g(ba_ref[...])

    i_idx, j_idx = _tri_masks()
    ones8 = jnp.ones((8, CHUNK), F32)

    def precompute(chunk, d):
        r = pl.multiple_of(chunk * CHUNK, CHUNK)
        q = qs[pl.ds(r, CHUNK), :]
        k = ks[pl.ds(r, CHUNK), :]
        v = vs[pl.ds(r, CHUNK), :]
        bb = bs[pl.ds(r, CHUNK), :]
        beta = bb[:, d:d + 1]
        g = bb[:, 2 + d:3 + d]
        incl = (i_idx >= j_idx) if d == 0 else (i_idx <= j_idx)
        strict = (i_idx > j_idx) if d == 0 else (i_idx < j_idx)
        g_b = jnp.broadcast_to(g, (CHUNK, CHUNK))
        gc_mat = _dot_hi(incl.astype(F32), g_b)
        seen = (i_idx <= j_idx) if d == 0 else (i_idx >= j_idx)
        gr8 = _dot_hi(ones8, jnp.where(seen, g_b, 0.0))
        gr_mat = jnp.broadcast_to(gr8[0:1, :], (CHUNK, CHUNK))
        decay = jnp.where(incl, jnp.exp(jnp.minimum(gc_mat - gr_mat, 0.0)), 0.0)
        gc = gc_mat[:, 0:1]
        g_last = gc[CHUNK - 1:CHUNK, :] if d == 0 else gc[0:1, :]
        egc = jnp.exp(gc)
        kb = k * beta
        kbf = k.astype(BF16)
        l_mat = jnp.where(strict, _dot_nt(kb.astype(BF16), kbf) * decay, 0.0)
        t_inv = _unit_tri_inverse(l_mat, i_idx, j_idx)
        rhs = jnp.concatenate([v * beta, kb * egc], axis=1)
        sol = _dot_hi(t_inv, rhs)
        u = sol[:, :A_HEAD_DIM]
        w = sol[:, A_HEAD_DIM:]
        k_tail = k * jnp.exp(g_last - gc)
        q_dec = q * egc
        intra = _dot_nt(q.astype(BF16), kbf) * decay
        return u, w, k_tail, q_dec, intra, jnp.exp(g_last)

    def scan_step(state, pre):
        u, w, k_tail, q_dec, intra, e_last = pre
        ws = _dot(jnp.concatenate([w, q_dec], axis=0).astype(BF16), state.astype(BF16))
        v_new = u - ws[:CHUNK]
        v_new_b = v_new.astype(BF16)
        o = ws[CHUNK:] + _dot(intra.astype(BF16), v_new_b)
        new_state = state * e_last + _dot_tn(k_tail.astype(BF16), v_new_b)
        return new_state, o

    def bwd_chunk(step):
        return jnp.where(step < nch_ctx, nch_ctx - 1 - step, nch_all + nch_ctx - 1 - step)

    def body(step, carry):
        s_f, s_b = carry
        cf = step
        cb = bwd_chunk(step)
        s_f, o_f = scan_step(s_f, precompute(cf, 0))
        s_b, o_b = scan_step(s_b, precompute(cb, 1))

        @pl.when(step >= nch_ctx)
        def _():
            of_s[pl.ds(pl.multiple_of((cf - nch_ctx) * CHUNK, CHUNK), CHUNK), :] = o_f
            ob_s[pl.ds(pl.multiple_of((cb - nch_ctx) * CHUNK, CHUNK), CHUNK), :] = o_b

        return s_f, s_b

    zero_state = jnp.zeros((A_HEAD_DIM, A_HEAD_DIM), F32)
    lax.fori_loop(0, nch_all, body, (zero_state, zero_state))

    o = of_s[...] + ob_s[...]
    o_ref[...] = _rms(o, on_ref[...]) * _silu(z_ref[...])


def _delta_mixer(qkv, qkv_c, conv_w, ba, ba_c, coef, z, out_norm, bsz, n_lat, n_ctx):
    h = A_HEADS
    hd = A_HEAD_DIM
    n_all = n_lat + n_ctx

    def col(kind):
        return lambda b, hh: (b, kind * h + hh)

    in_specs = (
        [pl.BlockSpec((n_lat, hd), col(kd)) for kd in range(3)]
        + [pl.BlockSpec((n_ctx, hd), col(kd)) for kd in range(3)]
        + [pl.BlockSpec((SHORT_CONV_W, hd), (lambda b, hh, kd=kd: (0, kd * h + hh))) for kd in range(3)]
        + [pl.BlockSpec((n_lat, LANES), lambda b, hh: (b, hh)),
           pl.BlockSpec((n_ctx, LANES), lambda b, hh: (b, hh)),
           pl.BlockSpec((1, 8, LANES), lambda b, hh: (hh, 0, 0)),
           pl.BlockSpec((n_lat, hd), lambda b, hh: (b, hh)),
           pl.BlockSpec((1, hd), lambda b, hh: (0, 0))])
    return pl.pallas_call(
        functools.partial(_delta_kernel, n_lat=n_lat, n_ctx=n_ctx),
        grid=(bsz, h),
        in_specs=in_specs,
        out_specs=pl.BlockSpec((n_lat, hd), lambda b, hh: (b, hh)),
        out_shape=jax.ShapeDtypeStruct((bsz * n_lat, h * hd), F32),
        scratch_shapes=[pltpu.VMEM((n_lat + 16, hd), F32),
                        pltpu.VMEM((n_all, hd), F32), pltpu.VMEM((n_all, hd), F32), pltpu.VMEM((n_all, hd), F32),
                        pltpu.VMEM((n_all, LANES), F32),
                        pltpu.VMEM((n_lat, hd), F32), pltpu.VMEM((n_lat, hd), F32)],
        compiler_params=_cparams(("arbitrary", "arbitrary")),
        name="delta",
    )(qkv, qkv, qkv, qkv_c, qkv_c, qkv_c, conv_w, conv_w, conv_w, ba, ba_c, coef, z, out_norm)


DELTA_BLOCK = 4
DELTA_ROWS = DELTA_BLOCK * CHUNK
CONV_HALO = 8


def _split_bf16(a):
    hi = a.astype(BF16)
    return hi, (a - hi.astype(F32)).astype(BF16)


def _unit_tri_inverses(l_mats, i, j):
    eye = (i == j).astype(F32)
    same16 = jnp.right_shift(i, 4) == jnp.right_shift(j, 4)
    same32 = jnp.right_shift(i, 5) == jnp.right_shift(j, 5)
    off32 = jnp.logical_and(same32, jnp.logical_not(same16))
    b = lambda a: a.astype(BF16)
    each = lambda f, *ls: [f(*xs) for xs in zip(*ls)]
    d1 = each(lambda l: b(jnp.where(same16, l, 0.0)), l_mats)
    p = each(lambda d: eye - d.astype(F32), d1)
    dk = d1
    for _ in range(3):
        dk = each(lambda d: b(_dot(d, d)), dk)
        p = each(lambda pp, d: pp + _dot(b(pp), d), p, dk)
    for sel in (off32, jnp.logical_not(same32)):
        cm = each(lambda l: b(jnp.where(sel, l, 0.0)), l_mats)
        pb = each(b, p)
        inner = each(lambda c, q: b(_dot(c, q)), cm, pb)
        p = each(lambda pp, q, m: pp - _dot(q, m), p, pb, inner)
    l_split = each(_split_bf16, l_mats)
    t_split = each(_split_bf16, p)
    resid = each(lambda t0, ls, ts: eye - t0 - (_dot(ls[0], ts[0]) + _dot(ls[0], ts[1]) + _dot(ls[1], ts[0])),
                 p, l_split, t_split)
    return each(lambda t0, ts, r: t0 + _dot(ts[0], b(r)), p, t_split, resid)


DELTA_HEADS = 4


def _dprep_kernel(q_ref, k_ref, v_ref, qc_ref, kc_ref, vc_ref, ba_ref, bac_ref, coef_ref,
                  u_ref, w_ref, kt_ref, qd_ref, in_ref, el_ref):
    blk = pl.program_id(2)
    is_ctx = blk == 0
    heads = [slice(hh * A_HEAD_DIM, (hh + 1) * A_HEAD_DIM) for hh in range(DELTA_HEADS)]

    def per_head(main_ref, ctx_ref):
        x = jnp.where(is_ctx, ctx_ref[...], main_ref[...])
        return [x[:, ln] for ln in heads]

    q_h = per_head(q_ref, qc_ref)
    k_h = per_head(k_ref, kc_ref)
    v_h = per_head(v_ref, vc_ref)

    lane = lax.broadcasted_iota(jnp.int32, (1, LANES), 1)
    raw_all = jnp.where(is_ctx, bac_ref[...], ba_ref[...])
    bg_h = []
    for hh, ln in enumerate(heads):
        coef = coef_ref[hh]
        neg_a = -jnp.exp(jnp.where(lane == 2, coef[0:1, :], coef[1:2, :]))
        dtb = jnp.where(lane == 2, coef[2:3, :], coef[3:4, :])
        raw = raw_all[:, ln]
        xg = raw + dtb
        softplus = jnp.maximum(xg, 0.0) + jnp.log(1.0 + jnp.exp(-jnp.abs(xg)))
        bg_h.append(jnp.where(lane < 2, jax.nn.sigmoid(raw), neg_a * softplus))

    i_idx, j_idx = _tri_masks()
    i_w = lax.broadcasted_iota(jnp.int32, (CHUNK, LANES), 0)
    j_w = lax.broadcasted_iota(jnp.int32, (CHUNK, LANES), 1)
    in_chunk = j_w < CHUNK
    ones8 = jnp.ones((8, CHUNK), F32)
    chunks = [slice(c * CHUNK, (c + 1) * CHUNK) for c in range(DELTA_BLOCK)]

    kk, qk = {}, {}
    for hh in range(DELTA_HEADS):
        for c, r in enumerate(chunks):
            kbf = k_h[hh][r].astype(BF16)
            prod = _dot_nt(jnp.concatenate([kbf, q_h[hh][r].astype(BF16)], axis=0),
                           jnp.concatenate([kbf, jnp.zeros_like(kbf)], axis=0))
            kk[hh, c] = prod[:CHUNK]
            qk[hh, c] = prod[CHUNK:]

    gc_all, gr_all = {}, {}
    for d in range(2):
        incl = (i_idx >= j_idx) if d == 0 else (i_idx <= j_idx)
        seen_w = jnp.logical_and(in_chunk, (i_w <= j_w) if d == 0 else (i_w >= j_w))
        for hh in range(DELTA_HEADS):
            g_cols = [jnp.broadcast_to(bg_h[hh][r, 2 + d:3 + d], (CHUNK, LANES)) for r in chunks]
            gc_all[hh, d] = _dot_hi(incl.astype(F32), jnp.concatenate(g_cols, axis=1))
            gr_all[hh, d] = _dot_hi(ones8, jnp.concatenate([jnp.where(seen_w, g, 0.0) for g in g_cols], axis=1))

    insts, l_mats, rhs_all = [], [], []
    for d in range(2):
        strict = (i_idx > j_idx) if d == 0 else (i_idx < j_idx)
        incl_w = jnp.logical_and(in_chunk, (i_w >= j_w) if d == 0 else (i_w <= j_w))
        for hh, ln in enumerate(heads):
            for c, r in enumerate(chunks):
                q, k, v = q_h[hh][r], k_h[hh][r], v_h[hh][r]
                beta = bg_h[hh][r, d:d + 1]
                gc_w = gc_all[hh, d][:, c * LANES:(c + 1) * LANES]
                gr_w = jnp.broadcast_to(gr_all[hh, d][0:1, c * LANES:(c + 1) * LANES], (CHUNK, LANES))
                decay = jnp.where(incl_w, jnp.exp(jnp.minimum(gc_w - gr_w, 0.0)), 0.0)
                g_last = gc_w[CHUNK - 1:CHUNK, :] if d == 0 else gc_w[0:1, :]
                egc = jnp.exp(gc_w)
                l_mats.append(jnp.where(strict, (kk[hh, c] * beta * decay)[:, :CHUNK], 0.0))
                rhs_all.append(jnp.concatenate([v * beta, k * beta * egc], axis=1).astype(BF16))
                kt_ref[d, 0, r, ln] = (k * jnp.exp(g_last - gc_w)).astype(BF16)
                qd_ref[d, 0, r, ln] = (q * egc).astype(BF16)
                in_ref[d, 0, r, ln] = (qk[hh, c] * decay).astype(BF16)
                el_ref[d, 0, c * 8:(c + 1) * 8, ln] = jnp.broadcast_to(jnp.exp(g_last), (8, LANES))
                insts.append((d, r, ln))

    t_split = [_split_bf16(t) for t in _unit_tri_inverses(l_mats, i_idx, j_idx)]
    sols = [_dot(t_hi, rhs) + _dot(t_lo, rhs) for (t_hi, t_lo), rhs in zip(t_split, rhs_all)]
    for (d, r, ln), sol in zip(insts, sols):
        u_ref[d, 0, r, ln] = sol[:, :A_HEAD_DIM]
        w_ref[d, 0, r, ln] = sol[:, A_HEAD_DIM:].astype(BF16)


def _delta_prep(qkv, qkv_c, ba, ba_c, coef, bsz, n_lat, n_ctx):
    assert n_ctx == DELTA_ROWS and n_lat % DELTA_ROWS == 0 and A_HEADS % DELTA_HEADS == 0
    groups = A_HEADS // DELTA_HEADS
    wd = DELTA_HEADS * A_HEAD_DIM
    lat_blocks = n_lat // DELTA_ROWS
    n_blocks = lat_blocks + 1
    n_all = n_lat + n_ctx

    def lat_blk(b, j):
        return b * lat_blocks + jnp.maximum(j - 1, 0)

    def main(kd):
        return pl.BlockSpec((DELTA_ROWS, wd), lambda b, hg, j: (lat_blk(b, j), kd * groups + hg))

    def ctxb(kd):
        return pl.BlockSpec((n_ctx, wd), lambda b, hg, j: (b, kd * groups + hg))

    in_specs = ([main(kd) for kd in range(3)] + [ctxb(kd) for kd in range(3)]
                + [pl.BlockSpec((DELTA_ROWS, wd), lambda b, hg, j: (lat_blk(b, j), hg)),
                   pl.BlockSpec((n_ctx, wd), lambda b, hg, j: (b, hg)),
                   pl.BlockSpec((DELTA_HEADS, 8, LANES), lambda b, hg, j: (hg, 0, 0))])
    big = pl.BlockSpec((2, 1, DELTA_ROWS, wd), lambda b, hg, j: (0, b, j, hg))
    small = pl.BlockSpec((2, 1, DELTA_BLOCK * 8, wd), lambda b, hg, j: (0, b, j, hg))
    sds = lambda rows, dt: jax.ShapeDtypeStruct((2, bsz, rows, A_WIDTH), dt)
    return pl.pallas_call(
        _dprep_kernel,
        grid=(bsz, groups, n_blocks),
        in_specs=in_specs,
        out_specs=[big, big, big, big, big, small],
        out_shape=[sds(n_all, F32), sds(n_all, BF16), sds(n_all, BF16), sds(n_all, BF16), sds(n_all, BF16),
                   sds(n_all // CHUNK * 8, F32)],
        compiler_params=_cparams(("arbitrary", "arbitrary", "arbitrary")),
        name="dprep",
    )(*([qkv] * 3), *([qkv_c] * 3), ba, ba_c, coef)


def _dprep_kernel_single_head(q_ref, k_ref, v_ref, qp_ref, kp_ref, vp_ref, qn_ref, kn_ref, vn_ref, qc_ref, kc_ref,
                              vc_ref, wq_ref, wk_ref, wv_ref, ba_ref, bac_ref, coef_ref,
                              u_ref, w_ref, kt_ref, qd_ref, in_ref, el_ref, *, n_blocks):
    blk = pl.program_id(2)
    is_ctx = blk == 0
    rows = DELTA_ROWS
    ext_rows = rows + 2 * CONV_HALO

    def conv(main_ref, ctx_ref, prev_ref, next_ref, w_ref, kind):
        x = jnp.where(is_ctx, ctx_ref[...], main_ref[...])
        prev = jnp.where(blk <= 1, 0.0, prev_ref[...])
        nxt = jnp.where(jnp.logical_or(is_ctx, blk == n_blocks - 1), 0.0, next_ref[...])
        ext = jnp.concatenate([prev, x, nxt], axis=0)
        w = w_ref[...]
        acc = jnp.zeros((rows, A_HEAD_DIM), F32)
        for tap in range(SHORT_CONV_W):
            shift = (SHORT_CONV_W // 2 - tap) % ext_rows
            rolled = pltpu.roll(ext, shift, axis=0) if shift else ext
            acc = acc + rolled[CONV_HALO:CONV_HALO + rows, :] * w[tap:tap + 1, :]
        y = _silu(acc)
        if kind < 2:
            y = y * lax.rsqrt(jnp.sum(y * y, axis=-1, keepdims=True) + EPS)
        if kind == 0:
            y = y * (A_HEAD_DIM ** -0.5)
        return y

    q_all = conv(q_ref, qc_ref, qp_ref, qn_ref, wq_ref, 0)
    k_all = conv(k_ref, kc_ref, kp_ref, kn_ref, wk_ref, 1)
    v_all = conv(v_ref, vc_ref, vp_ref, vn_ref, wv_ref, 2)

    lane = lax.broadcasted_iota(jnp.int32, (1, LANES), 1)
    coef = coef_ref[0]
    neg_a = -jnp.exp(jnp.where(lane == 2, coef[0:1, :], coef[1:2, :]))
    dtb = jnp.where(lane == 2, coef[2:3, :], coef[3:4, :])
    raw = jnp.where(is_ctx, bac_ref[...], ba_ref[...])
    xg = raw + dtb
    softplus = jnp.maximum(xg, 0.0) + jnp.log(1.0 + jnp.exp(-jnp.abs(xg)))
    bg = jnp.where(lane < 2, jax.nn.sigmoid(raw), neg_a * softplus)

    i_idx, j_idx = _tri_masks()
    i_w = lax.broadcasted_iota(jnp.int32, (CHUNK, LANES), 0)
    j_w = lax.broadcasted_iota(jnp.int32, (CHUNK, LANES), 1)
    ones8 = jnp.ones((8, CHUNK), F32)
    chunks = [slice(c * CHUNK, (c + 1) * CHUNK) for c in range(DELTA_BLOCK)]

    kk, qk = [], []
    for r in chunks:
        kbf = k_all[r].astype(BF16)
        prod = _dot_nt(jnp.concatenate([kbf, q_all[r].astype(BF16)], axis=0),
                       jnp.concatenate([kbf, jnp.zeros_like(kbf)], axis=0))
        kk.append(prod[:CHUNK])
        qk.append(prod[CHUNK:])

    insts, l_mats, rhs_all = [], [], []
    for d in range(2):
        incl = (i_idx >= j_idx) if d == 0 else (i_idx <= j_idx)
        strict = (i_idx > j_idx) if d == 0 else (i_idx < j_idx)
        in_chunk = j_w < CHUNK
        incl_w = jnp.logical_and(in_chunk, (i_w >= j_w) if d == 0 else (i_w <= j_w))
        seen_w = jnp.logical_and(in_chunk, (i_w <= j_w) if d == 0 else (i_w >= j_w))
        g_cols = [jnp.broadcast_to(bg[r, 2 + d:3 + d], (CHUNK, LANES)) for r in chunks]
        gc_all = _dot_hi(incl.astype(F32), jnp.concatenate(g_cols, axis=1))
        gr_all = _dot_hi(ones8, jnp.concatenate([jnp.where(seen_w, g, 0.0) for g in g_cols], axis=1))
        for c, r in enumerate(chunks):
            q, k, v = q_all[r], k_all[r], v_all[r]
            beta = bg[r, d:d + 1]
            gc_w = gc_all[:, c * LANES:(c + 1) * LANES]
            gr_w = jnp.broadcast_to(gr_all[0:1, c * LANES:(c + 1) * LANES], (CHUNK, LANES))
            decay = jnp.where(incl_w, jnp.exp(jnp.minimum(gc_w - gr_w, 0.0)), 0.0)
            g_last = gc_w[CHUNK - 1:CHUNK, :] if d == 0 else gc_w[0:1, :]
            egc = jnp.exp(gc_w)
            l_mats.append(jnp.where(strict, (kk[c] * beta * decay)[:, :CHUNK], 0.0))
            rhs_all.append(jnp.concatenate([v * beta, k * beta * egc], axis=1).astype(BF16))
            kt_ref[d, 0, r, :] = (k * jnp.exp(g_last - gc_w)).astype(BF16)
            qd_ref[d, 0, r, :] = (q * egc).astype(BF16)
            in_ref[d, 0, r, :] = (qk[c] * decay).astype(BF16)
            el_ref[d, 0, c * 8:(c + 1) * 8, :] = jnp.broadcast_to(jnp.exp(g_last), (8, LANES))
            insts.append((d, r))

    t_split = [_split_bf16(t) for t in _unit_tri_inverses(l_mats, i_idx, j_idx)]
    sols = [_dot(t_hi, rhs) + _dot(t_lo, rhs) for (t_hi, t_lo), rhs in zip(t_split, rhs_all)]
    for (d, r), sol in zip(insts, sols):
        u_ref[d, 0, r, :] = sol[:, :A_HEAD_DIM]
        w_ref[d, 0, r, :] = sol[:, A_HEAD_DIM:].astype(BF16)


def _delta_prep_single_head(qkv, qkv_c, conv_w, ba, ba_c, coef, bsz, n_lat, n_ctx):
    assert n_ctx == DELTA_ROWS and n_lat % DELTA_ROWS == 0
    h = A_HEADS
    hd = A_HEAD_DIM
    lat_blocks = n_lat // DELTA_ROWS
    n_blocks = lat_blocks + 1
    n_all = n_lat + n_ctx
    halo_per_block = DELTA_ROWS // CONV_HALO
    n_halo = bsz * n_lat // CONV_HALO

    def lat_blk(b, j):
        return b * lat_blocks + jnp.maximum(j - 1, 0)

    def main(kd):
        return pl.BlockSpec((DELTA_ROWS, hd), lambda b, hh, j: (lat_blk(b, j), kd * h + hh))

    def prev(kd):
        return pl.BlockSpec((CONV_HALO, hd),
                            lambda b, hh, j: (jnp.maximum(lat_blk(b, j) * halo_per_block - 1, 0), kd * h + hh))

    def nxt(kd):
        return pl.BlockSpec((CONV_HALO, hd),
                            lambda b, hh, j: (jnp.minimum((lat_blk(b, j) + 1) * halo_per_block, n_halo - 1),
                                              kd * h + hh))

    def ctxb(kd):
        return pl.BlockSpec((n_ctx, hd), lambda b, hh, j: (b, kd * h + hh))

    def tap(kd):
        return pl.BlockSpec((SHORT_CONV_W, hd), lambda b, hh, j: (0, kd * h + hh))

    in_specs = ([main(kd) for kd in range(3)] + [prev(kd) for kd in range(3)] + [nxt(kd) for kd in range(3)]
                + [ctxb(kd) for kd in range(3)] + [tap(kd) for kd in range(3)]
                + [pl.BlockSpec((DELTA_ROWS, LANES), lambda b, hh, j: (lat_blk(b, j), hh)),
                   pl.BlockSpec((n_ctx, LANES), lambda b, hh, j: (b, hh)),
                   pl.BlockSpec((1, 8, LANES), lambda b, hh, j: (hh, 0, 0))])
    big = pl.BlockSpec((2, 1, DELTA_ROWS, hd), lambda b, hh, j: (0, b, j, hh))
    small = pl.BlockSpec((2, 1, DELTA_BLOCK * 8, LANES), lambda b, hh, j: (0, b, j, hh))
    sds = lambda rows, dt: jax.ShapeDtypeStruct((2, bsz, rows, h * hd), dt)
    return pl.pallas_call(
        functools.partial(_dprep_kernel, n_blocks=n_blocks),
        grid=(bsz, h, n_blocks),
        in_specs=in_specs,
        out_specs=[big, big, big, big, big, small],
        out_shape=[sds(n_all, F32), sds(n_all, BF16), sds(n_all, BF16), sds(n_all, BF16), sds(n_all, BF16),
                   sds(n_all // CHUNK * 8, F32)],
        compiler_params=_cparams(("arbitrary", "arbitrary", "arbitrary")),
        name="dprep",
    )(*([qkv] * 9), *([qkv_c] * 3), *([conv_w] * 3), ba, ba_c, coef)


def _dscan_kernel(uf, wf, ktf, qdf, inf, elf, ub, wb, ktb, qdb, inb, elb, of_ref, ob_ref, state):
    step = pl.program_id(1)

    @pl.when(step == 0)
    def _():
        state[...] = jnp.zeros(state.shape, F32)

    dirs = ((uf, wf, ktf, qdf, inf, elf, of_ref), (ub, wb, ktb, qdb, inb, elb, ob_ref))
    chains = [(d, hh) for d in range(2) for hh in range(A_HEADS)]
    s_mats = [state[d * A_HEADS + hh] for d, hh in chains]
    for t in range(DELTA_BLOCK):
        where = []
        for d, hh in chains:
            c = t if d == 0 else DELTA_BLOCK - 1 - t
            where.append((dirs[d], c, slice(c * CHUNK, (c + 1) * CHUNK),
                          slice(hh * A_HEAD_DIM, (hh + 1) * A_HEAD_DIM)))
        ws = [_dot(jnp.concatenate([rf[1][0, 0, r, ln], rf[3][0, 0, r, ln]], axis=0), s.astype(BF16))
              for (rf, c, r, ln), s in zip(where, s_mats)]
        v_new = [(rf[0][0, 0, r, ln] - x[:CHUNK]).astype(BF16) for (rf, c, r, ln), x in zip(where, ws)]
        for (rf, c, r, ln), x, vn in zip(where, ws, v_new):
            rf[6][r, ln] = x[CHUNK:] + _dot(rf[4][0, 0, r, ln][:, :CHUNK], vn)
        s_mats = [s * rf[5][0, 0, c * 8:c * 8 + 1, ln] + _dot_tn(rf[2][0, 0, r, ln], vn)
                  for (rf, c, r, ln), s, vn in zip(where, s_mats, v_new)]
    for (d, hh), s in zip(chains, s_mats):
        state[d * A_HEADS + hh] = s


def _delta_scan(pre, bsz, n_lat, n_ctx):
    u, w, kt, qd, intra, el = pre
    lat_blocks = n_lat // DELTA_ROWS
    n_blocks = lat_blocks + 1
    width = A_WIDTH

    def fwd_idx(b, s):
        return (0, b, s, 0)

    def bwd_idx(b, s):
        return (1, b, jnp.where(s == 0, 0, n_blocks - s), 0)

    def specs(idx):
        big = pl.BlockSpec((1, 1, DELTA_ROWS, width), idx)
        return [big, big, big, big, big, pl.BlockSpec((1, 1, DELTA_BLOCK * 8, width), idx)]

    out_f = pl.BlockSpec((DELTA_ROWS, width), lambda b, s: (b * lat_blocks + jnp.maximum(s - 1, 0), 0))
    out_b = pl.BlockSpec((DELTA_ROWS, width), lambda b, s: (b * lat_blocks + lat_blocks - jnp.maximum(s, 1), 0))
    shape = jax.ShapeDtypeStruct((bsz * n_lat, width), F32)
    return pl.pallas_call(
        _dscan_kernel,
        grid=(bsz, n_blocks),
        in_specs=specs(fwd_idx) + specs(bwd_idx),
        out_specs=[out_f, out_b],
        out_shape=[shape, shape],
        scratch_shapes=[pltpu.VMEM((2 * A_HEADS, A_HEAD_DIM, A_HEAD_DIM), F32)],
        compiler_params=_cparams(("arbitrary", "arbitrary")),
        name="dscan",
    )(u, w, kt, qd, intra, el, u, w, kt, qd, intra, el)


def _head_rms(x, gain_row, head_dim):
    outs = []
    for s in range(x.shape[1] // LANES):
        xs = x[:, s * LANES:(s + 1) * LANES]
        lane = lax.broadcasted_iota(jnp.int32, xs.shape, 1)
        sq = xs * xs
        scale = jnp.zeros_like(xs)
        for part in range(LANES // head_dim):
            m = jnp.logical_and(lane >= part * head_dim, lane < (part + 1) * head_dim)
            ms = jnp.sum(jnp.where(m, sq, 0.0), axis=-1, keepdims=True) * (1.0 / head_dim)
            scale = jnp.where(m, lax.rsqrt(ms + EPS), scale)
        outs.append(xs * scale)
    y = outs[0] if len(outs) == 1 else jnp.concatenate(outs, axis=1)
    return y * gain_row


def _rope(x, cos, sin_lo, sin_hi):
    width = x.shape[1]
    fwd = pltpu.roll(x, width - ROPE_AXIS_PAIRS, axis=1)
    back = pltpu.roll(x, ROPE_AXIS_PAIRS, axis=1)
    return x * cos + fwd * sin_lo + back * sin_hi


def _kvprep_kernel(k_ref, v_ref, g_ref, cos_ref, slo_ref, shi_ref, ko_ref, vo_ref):
    k = _head_rms(k_ref[0], g_ref[...], B_HEAD_DIM)
    k = _rope(k, cos_ref[...], slo_ref[...], shi_ref[...]).astype(BF16)
    vt = v_ref[0].T.astype(BF16)
    for hh in range(B_KV_HEADS):
        ko_ref[0, hh] = k[:, hh * B_HEAD_DIM:(hh + 1) * B_HEAD_DIM]
        vo_ref[0, hh, :B_HEAD_DIM, :] = vt[hh * B_HEAD_DIM:(hh + 1) * B_HEAD_DIM, :]
        vo_ref[0, hh, B_HEAD_DIM:, :] = jnp.ones((ATTN_VT_ROWS - B_HEAD_DIM, vt.shape[1]), BF16)


def _kvprep(k_all, v_all, gain, cos, slo, shi, ts):
    bsz, s_len, w = k_all.shape
    row = pl.BlockSpec((1, ts, w), lambda b, i: (b, i, 0))
    tab = pl.BlockSpec((ts, w), lambda b, i: (i, 0))
    return pl.pallas_call(
        _kvprep_kernel,
        grid=(bsz, s_len // ts),
        in_specs=[row, row, pl.BlockSpec((1, w), lambda b, i: (0, 0)), tab, tab, tab],
        out_specs=[pl.BlockSpec((1, B_KV_HEADS, ts, B_HEAD_DIM), lambda b, i: (b, 0, i, 0)),
                   pl.BlockSpec((1, B_KV_HEADS, ATTN_VT_ROWS, ts), lambda b, i: (b, 0, 0, i))],
        out_shape=[jax.ShapeDtypeStruct((bsz, B_KV_HEADS, s_len, B_HEAD_DIM), BF16),
                   jax.ShapeDtypeStruct((bsz, B_KV_HEADS, ATTN_VT_ROWS, s_len), BF16)],
        compiler_params=_cparams(("arbitrary", "arbitrary")),
        name="kvprep",
    )(k_all, v_all, gain, cos, slo, shi)


LOG2_E = math.log2(math.e)
ATTN_SCORE_KEYS = 64
ATTN_VALUE_KEYS = 256
ATTN_Q_COLS = 256
ATTN_VT_ROWS = B_HEAD_DIM + 16


def _attn_kernel(q_ref, g_ref, cos_ref, slo_ref, shi_ref, k_ref, vt_ref, o_ref):
    q = _head_rms(q_ref[...], g_ref[...], B_HEAD_DIM)
    q = _rope(q, cos_ref[...], slo_ref[...], shi_ref[...]) * (B_HEAD_DIM ** -0.5 * LOG2_E)
    tq = q.shape[0]
    qt = q.T.astype(BF16)
    n_keys = k_ref.shape[2]
    q_s = [qt[g * B_HEAD_DIM:(g + 1) * B_HEAD_DIM, c:c + ATTN_Q_COLS]
           for g in range(B_GROUP) for c in range(0, tq, ATTN_Q_COLS)]
    m = [jnp.full((1, ATTN_Q_COLS), -1e30, F32) for _ in q_s]
    acc = [jnp.zeros((ATTN_VT_ROWS, ATTN_Q_COLS), F32) for _ in q_s]

    def scores(kb):
        k_blk = k_ref[0, 0, kb:kb + ATTN_VALUE_KEYS, :]
        return [_dot(k_blk, qs) for qs in q_s]

    def value_update(acc, pending):
        alpha, p, kb = pending
        vt = vt_ref[0, 0, :, kb:kb + ATTN_VALUE_KEYS]
        return [a * al + _dot(vt, pp) for a, al, pp in zip(acc, alpha, p)]

    st_next = scores(0)
    pending = None
    for kb in range(0, n_keys, ATTN_VALUE_KEYS):
        st = st_next
        if kb + ATTN_VALUE_KEYS < n_keys:
            st_next = scores(kb + ATTN_VALUE_KEYS)
        m_new = [jnp.maximum(mo, jnp.max(s, axis=0, keepdims=True)) for mo, s in zip(m, st)]
        alpha = [jnp.exp2(mo - mn) for mo, mn in zip(m, m_new)]
        p = [jnp.exp2((s - mn).astype(BF16)) for s, mn in zip(st, m_new)]
        if pending is not None:
            acc = value_update(acc, pending)
        pending = (alpha, p, kb)
        m = m_new
    acc = value_update(acc, pending)
    outs = [a[:B_HEAD_DIM] / a[B_HEAD_DIM:B_HEAD_DIM + 1] for a in acc]
    n_col = tq // ATTN_Q_COLS
    o_ref[...] = jnp.concatenate([jnp.concatenate(outs[g * n_col:(g + 1) * n_col], axis=1)
                                  for g in range(B_GROUP)], axis=0).T


def _attn_kernel_two_pass(q_ref, g_ref, cos_ref, slo_ref, shi_ref, k_ref, vt_ref, o_ref, s_scr):
    q = _head_rms(q_ref[...], g_ref[...], B_HEAD_DIM)
    q = _rope(q, cos_ref[...], slo_ref[...], shi_ref[...]) * (B_HEAD_DIM ** -0.5 * LOG2_E)
    tq = q.shape[0]
    qt = q.T.astype(BF16)
    n_keys = k_ref.shape[2]
    halves = [slice(c, c + ATTN_Q_COLS) for c in range(0, tq, ATTN_Q_COLS)]

    def score_steps(g):
        qg = qt[g * B_HEAD_DIM:(g + 1) * B_HEAD_DIM, :]
        m8 = None
        for kb in range(0, n_keys, ATTN_SCORE_KEYS):
            st = _dot(k_ref[0, 0, kb:kb + ATTN_SCORE_KEYS, :], qg)
            s_scr[g % 2, kb:kb + ATTN_SCORE_KEYS, :] = st
            bm = jnp.max(st.reshape(ATTN_SCORE_KEYS // 8, 8, tq), axis=0)
            m8 = bm if m8 is None else jnp.maximum(m8, bm)
            if (kb // ATTN_SCORE_KEYS) % (ATTN_VALUE_KEYS // ATTN_SCORE_KEYS) == 1:
                yield None
        yield jnp.max(m8, axis=0, keepdims=True)

    def value_steps(g, m):
        acc = [jnp.zeros((ATTN_VT_ROWS, ATTN_Q_COLS), F32) for _ in halves]
        for kb in range(0, n_keys, ATTN_VALUE_KEYS):
            vt = vt_ref[0, 0, :, kb:kb + ATTN_VALUE_KEYS]
            for i, cols in enumerate(halves):
                p = jnp.exp2((s_scr[g % 2, kb:kb + ATTN_VALUE_KEYS, cols] - m[:, cols]).astype(BF16))
                acc[i] = acc[i] + _dot(vt, p)
            yield None
        yield jnp.concatenate([a[:B_HEAD_DIM] / a[B_HEAD_DIM:B_HEAD_DIM + 1] for a in acc], axis=1)

    def drain(*gens):
        last = [None] * len(gens)
        live = list(range(len(gens)))
        while live:
            for i in list(live):
                try:
                    last[i] = next(gens[i])
                except StopIteration:
                    live.remove(i)
        return last

    outs = []
    (m,) = drain(score_steps(0))
    for g in range(B_GROUP):
        if g + 1 < B_GROUP:
            out, m = drain(value_steps(g, m), score_steps(g + 1))
        else:
            (out,) = drain(value_steps(g, m))
        outs.append(out)
    o_ref[...] = jnp.concatenate(outs, axis=0).T


def _attention(qb, gain, cos, slo, shi, k_hm, vt_hm, n_lat, tq):
    n, _ = qb.shape
    bsz, _, s_len, _ = k_hm.shape
    gw = B_GROUP * B_HEAD_DIM
    nq = n_lat // tq
    kv = pl.BlockSpec((1, 1, s_len, B_HEAD_DIM), lambda b, kh, i: (b, kh, 0, 0))
    vts = pl.BlockSpec((1, 1, ATTN_VT_ROWS, s_len), lambda b, kh, i: (b, kh, 0, 0))
    tab = pl.BlockSpec((tq, gw), lambda b, kh, i: (i, 0))
    return pl.pallas_call(
        _attn_kernel,
        grid=(bsz, B_KV_HEADS, nq),
        in_specs=[pl.BlockSpec((tq, gw), lambda b, kh, i: (b * nq + i, kh)),
                  pl.BlockSpec((1, gw), lambda b, kh, i: (0, 0)), tab, tab, tab, kv, vts],
        out_specs=pl.BlockSpec((tq, gw), lambda b, kh, i: (b * nq + i, kh)),
        out_shape=jax.ShapeDtypeStruct((n, B_WIDTH), F32),
        compiler_params=_cparams(("arbitrary", "arbitrary", "arbitrary")),
        name="attn",
    )(qb, gain, cos, slo, shi, k_hm, vt_hm)


def _gated_residual(x, y, mod_ref, gain_ref, gate_row):
    return x + mod_ref[0, gate_row:gate_row + 1, :] * _rms(y, gain_ref[...])


def _outproj_kernel(of_ref, ob_ref, z_ref, on_ref, yb_ref, x_ref, mod_ref, g_ref, wa_ref, wb_ref, o_ref):
    o = of_ref[...] + ob_ref[...]
    z = z_ref[...]
    parts = []
    for hh in range(A_HEADS):
        sl = slice(hh * A_HEAD_DIM, (hh + 1) * A_HEAD_DIM)
        parts.append((_rms(o[:, sl], on_ref[...]) * _silu(z[:, sl])).astype(BF16))
    ya = jnp.concatenate(parts, axis=1)
    y = _dot(ya, wa_ref[...]) + _dot(yb_ref[...].astype(BF16), wb_ref[...])
    o_ref[...] = _gated_residual(x_ref[...], y, mod_ref, g_ref, 2)


def _outproj(o_f, o_b, z, out_norm, yb, x2, mod, gain, wa, wb, rows_per_mod, tm):
    n, d = x2.shape
    bpm = rows_per_mod // tm
    return pl.pallas_call(
        _outproj_kernel,
        grid=(n // tm,),
        in_specs=[pl.BlockSpec((tm, o_f.shape[1]), lambda i: (i, 0)),
                  pl.BlockSpec((tm, o_b.shape[1]), lambda i: (i, 0)),
                  pl.BlockSpec((tm, z.shape[1]), lambda i: (i, 0)),
                  pl.BlockSpec((1, A_HEAD_DIM), lambda i: (0, 0)),
                  pl.BlockSpec((tm, yb.shape[1]), lambda i: (i, 0)),
                  pl.BlockSpec((tm, d), lambda i: (i, 0)),
                  pl.BlockSpec((1, 6, d), lambda i: (i // bpm, 0, 0)),
                  pl.BlockSpec((1, d), lambda i: (0, 0)),
                  _resident(wa.shape),
                  _resident(wb.shape)],
        out_specs=pl.BlockSpec((tm, d), lambda i: (i, 0)),
        out_shape=jax.ShapeDtypeStruct((n, d), F32),
        compiler_params=_cparams(("arbitrary",)),
        name="outproj",
    )(o_f, o_b, z, out_norm, yb, x2, mod, gain, wa, wb)


def _ffn_kernel(x_ref, mod_ref, gpre_ref, gpost_ref, wi_ref, wo_ref, o_ref, *, hidden, hc):
    x = x_ref[...]
    a = _modulated(x, mod_ref, gpre_ref, 3).astype(BF16)
    acc = jnp.zeros(x.shape, F32)
    for c in range(hidden // hc):
        gate = _dot(a, wi_ref[:, c * hc:(c + 1) * hc])
        up = _dot(a, wi_ref[:, hidden + c * hc:hidden + (c + 1) * hc])
        acc = acc + _dot((_silu(gate) * up).astype(BF16), wo_ref[c * hc:(c + 1) * hc, :])
    o_ref[...] = _gated_residual(x, acc, mod_ref, gpost_ref, 5)


def _ffn(x2, mod, gpre, gpost, wi, wo, rows_per_mod, tm):
    n, d = x2.shape
    hidden = wo.shape[0]
    bpm = rows_per_mod // tm
    return pl.pallas_call(
        functools.partial(_ffn_kernel, hidden=hidden, hc=256),
        grid=(n // tm,),
        in_specs=[pl.BlockSpec((tm, d), lambda i: (i, 0)),
                  pl.BlockSpec((1, 6, d), lambda i: (i // bpm, 0, 0)),
                  pl.BlockSpec((1, d), lambda i: (0, 0)),
                  pl.BlockSpec((1, d), lambda i: (0, 0)),
                  _resident(wi.shape),
                  _resident(wo.shape)],
        out_specs=pl.BlockSpec((tm, d), lambda i: (i, 0)),
        out_shape=jax.ShapeDtypeStruct((n, d), F32),
        compiler_params=_cparams(("arbitrary",)),
        name="ffn",
    )(x2, mod, gpre, gpost, wi, wo)


def _confin_kernel(x_ref, mod_ref, g_ref, w_ref, b_ref, o_ref, *, width):
    a = _modulated(x_ref[...], mod_ref, g_ref, 0).astype(BF16)
    val = _dot(a, w_ref[:, :width]) + b_ref[:, :width]
    gate = _dot(a, w_ref[:, width:]) + b_ref[:, width:]
    o_ref[...] = val * jax.nn.sigmoid(gate)


def _confin(x2, mod, gain, w, b, rows_per_mod, tm):
    n, d = x2.shape
    width = w.shape[1] // 2
    bpm = rows_per_mod // tm
    return pl.pallas_call(
        functools.partial(_confin_kernel, width=width),
        grid=(n // tm,),
        in_specs=[pl.BlockSpec((tm, d), lambda i: (i, 0)),
                  pl.BlockSpec((1, 6, d), lambda i: (i // bpm, 0, 0)),
                  pl.BlockSpec((1, d), lambda i: (0, 0)),
                  _resident(w.shape),
                  pl.BlockSpec((1, 2 * width), lambda i: (0, 0))],
        out_specs=pl.BlockSpec((tm, width), lambda i: (i, 0)),
        out_shape=jax.ShapeDtypeStruct((n, width), F32),
        compiler_params=_cparams(("arbitrary",)),
        name="confin",
    )(x2, mod, gain, w, b)


def _confout_kernel(u_ref, up_ref, un_ref, x_ref, mod_ref, g_ref, dww_ref, dwb_ref, lng_ref, lnb_ref,
                    w_ref, b_ref, o_ref, ext, conv, shifted, *, tm, tiles_per_seq):
    i = pl.program_id(0)
    pos = i % tiles_per_seq
    width = u_ref.shape[1]
    halo = CONF_HALO
    ext[pl.ds(halo, tm), :] = u_ref[...]
    ext[0:halo, :] = jnp.where(pos == 0, 0.0, up_ref[...])
    ext[pl.ds(halo + tm, halo), :] = jnp.where(pos == tiles_per_seq - 1, 0.0, un_ref[...])

    rb = 64
    pad = CONF_KERNEL // 2
    sub = 8
    copy_rows = tm + 2 * halo - sub

    def col_body(c, carry):
        cs = pl.ds(pl.multiple_of(c * LANES, LANES), LANES)
        w = dww_ref[:, cs]
        for s in range(sub):
            shifted[s, :, :] = ext[pl.ds(s, copy_rows), cs]
        for r in range(tm // rb):
            acc = jnp.zeros((rb, LANES), F32)
            for tap in range(CONF_KERNEL):
                off = halo - pad + tap
                acc = acc + shifted[off % sub, pl.ds(r * rb + off - off % sub, rb), :] * w[tap:tap + 1, :]
            conv[pl.ds(r * rb, rb), cs] = acc
        return carry

    lax.fori_loop(0, width // LANES, col_body, 0)

    y = conv[...] + dwb_ref[...]
    mu = jnp.mean(y, axis=-1, keepdims=True)
    yc = y - mu
    var = jnp.mean(yc * yc, axis=-1, keepdims=True)
    y = _silu(yc * lax.rsqrt(var + EPS) * lng_ref[...] + lnb_ref[...])
    out = _dot(y.astype(BF16), w_ref[...]) + b_ref[...]
    o_ref[...] = _gated_residual(x_ref[...], out, mod_ref, g_ref, 2)


def _confout(u, x2, mod, gain, dww, dwb, lng, lnb, w, b, rows_per_mod, tm):
    n, d = x2.shape
    width = u.shape[1]
    tiles_per_seq = rows_per_mod // tm
    hb = tm // CONF_HALO
    n_halo_blocks = n // CONF_HALO
    vec = lambda wd: pl.BlockSpec((1, wd), lambda i: (0, 0))
    return pl.pallas_call(
        functools.partial(_confout_kernel, tm=tm, tiles_per_seq=tiles_per_seq),
        grid=(n // tm,),
        in_specs=[pl.BlockSpec((tm, width), lambda i: (i, 0)),
                  pl.BlockSpec((CONF_HALO, width), lambda i: (jnp.maximum(i * hb - 1, 0), 0)),
                  pl.BlockSpec((CONF_HALO, width), lambda i: (jnp.minimum((i + 1) * hb, n_halo_blocks - 1), 0)),
                  pl.BlockSpec((tm, d), lambda i: (i, 0)),
                  pl.BlockSpec((1, 6, d), lambda i: (i // tiles_per_seq, 0, 0)),
                  vec(d),
                  pl.BlockSpec(dww.shape, lambda i: (0, 0)),
                  vec(width), vec(width), vec(width),
                  _resident(w.shape),
                  vec(d)],
        out_specs=pl.BlockSpec((tm, d), lambda i: (i, 0)),
        out_shape=jax.ShapeDtypeStruct((n, d), F32),
        scratch_shapes=[pltpu.VMEM((tm + 2 * CONF_HALO, width), F32), pltpu.VMEM((tm, width), F32),
                        pltpu.VMEM((8, tm + 2 * CONF_HALO - 8, LANES), F32)],
        compiler_params=_cparams(("arbitrary",)),
        name="confout",
    )(u, u, u, x2, mod, gain, dww, dwb, lng, lnb, w, b)


def _rope_tables(n_tokens, reps):
    rows = n_tokens // GRID_W
    row = jnp.broadcast_to(jnp.arange(rows, dtype=F32)[:, None], (rows, GRID_W)).reshape(n_tokens)
    col = jnp.broadcast_to(jnp.arange(GRID_W, dtype=F32)[None, :], (rows, GRID_W)).reshape(n_tokens)
    inv_freq = ROPE_THETA ** (-jnp.arange(ROPE_AXIS_PAIRS, dtype=F32) / ROPE_AXIS_PAIRS)
    ang_r = row[:, None] * inv_freq
    ang_c = col[:, None] * inv_freq
    ang = jnp.concatenate([ang_r, ang_r, ang_c, ang_c], axis=-1)
    cos, sin = jnp.cos(ang), jnp.sin(ang)
    first_half = (jnp.arange(B_HEAD_DIM) % (2 * ROPE_AXIS_PAIRS)) < ROPE_AXIS_PAIRS
    sin_lo = jnp.where(first_half, -sin, 0.0)
    sin_hi = jnp.where(first_half, 0.0, sin)
    tile = lambda t: jnp.tile(t, (1, reps))
    return tile(cos), tile(sin_lo), tile(sin_hi)


def _hybrid_in_weight(w_in):
    off_z = 3 * A_WIDTH
    off_ba = off_z + A_WIDTH
    off_q = off_ba + 4 * A_HEADS
    off_k = off_q + B_WIDTH
    off_v = off_k + B_KV_WIDTH
    d = w_in.shape[0]
    ba = w_in[:, off_ba:off_q].reshape(d, 2, 2, A_HEADS)
    ba = ba.transpose(0, 3, 1, 2).reshape(d, A_HEADS, 4)
    ba = jnp.pad(ba, ((0, 0), (0, 0), (0, LANES - 4))).reshape(d, A_HEADS * LANES)
    w = jnp.concatenate([w_in[:, :off_z], w_in[:, off_z:off_ba], w_in[:, off_q:off_k], w_in[:, off_k:off_v],
                         w_in[:, off_v:], ba], axis=1)
    return w.astype(BF16)


def _row(v):
    return v.reshape(1, -1)


def kernel(x, c, ctx, c_ctx, w_mod, b_mod, g_mix_pre, g_mix_post, g_ffn_pre, g_ffn_post, w_ffn_in, w_ffn_out,
           hyb_w_in, hyb_conv_w, hyb_a_log, hyb_dt_bias, hyb_out_norm, hyb_q_norm, hyb_k_norm, hyb_w_out,
           conf_w_in, conf_b_in, conf_dw_w, conf_dw_b, conf_ln_g, conf_ln_b, conf_w_out, conf_b_out):
    bsz, n_lat, d = x.shape
    n_ctx = ctx.shape[1]
    depth = w_mod.shape[0]
    n = bsz * n_lat
    tm = 1024

    cond =jnp.zeros((8, d), F32).at[:bsz].set(c).at[bsz].set(c_ctx)
    mods = _ada_terms(cond, w_mod, b_mod).reshape(depth, 8, 6, d)

    h = x.reshape(n, d)
    hc = ctx.reshape(bsz * n_ctx, d)
    for layer in range(depth):
        idx = layer // 2
        mod = mods[layer, :bsz]
        mod_ctx = mods[layer, bsz:bsz + 1]
        if layer % 2 == 0:
            w_in = _hybrid_in_weight(hyb_w_in[idx])
            gpre = _row(g_mix_pre[layer])
            k_gain = _row(jnp.tile(hyb_k_norm[idx], B_KV_HEADS))
            cos_k, slo_k, shi_k = _rope_tables(n_lat, B_KV_HEADS)
            no_pos = (jnp.ones((n_ctx, B_KV_WIDTH), F32), jnp.zeros((n_ctx, B_KV_WIDTH), F32),
                      jnp.zeros((n_ctx, B_KV_WIDTH), F32))
            qkv, z, qb, ba, k_lat, vt_lat = _hyb_inproj(h, mod, gpre, w_in, hyb_conv_w[idx], k_gain,
                                                        cos_k, slo_k, shi_k, bsz, n_lat, tm)
            qkv_c, _, _, ba_c, k_ctx, vt_ctx = _hyb_inproj(hc, mod_ctx, gpre, w_in, hyb_conv_w[idx], k_gain,
                                                           *no_pos, bsz, n_ctx, n_ctx)

            coef = jnp.concatenate([hyb_a_log[idx], hyb_dt_bias[idx]], axis=0)
            coef = jnp.pad(coef.T[:, :, None], ((0, 0), (0, 4), (0, 0)))
            coef = jnp.broadcast_to(coef, (A_HEADS, 8, LANES))
            o_f, o_b = _delta_scan(_delta_prep(qkv, qkv_c, ba, ba_c, coef, bsz, n_lat, n_ctx), bsz, n_lat, n_ctx)

            k_hm = jnp.concatenate([k_lat, k_ctx], axis=2)
            vt_hm = jnp.concatenate([vt_lat, vt_ctx], axis=3)
            cos_q, slo_q, shi_q = _rope_tables(n_lat, B_GROUP)
            yb = _attention(qb, _row(jnp.tile(hyb_q_norm[idx], B_GROUP)), cos_q, slo_q, shi_q,
                            k_hm, vt_hm, n_lat, 512)

            w_out = hyb_w_out[idx].astype(BF16)
            h = _outproj(o_f, o_b, z, _row(hyb_out_norm[idx]), yb, h, mod, _row(g_mix_post[layer]),
                         w_out[:A_WIDTH], w_out[A_WIDTH:], n_lat, tm)
        else:
            u = _confin(h, mod, _row(g_mix_pre[layer]), conf_w_in[idx].astype(BF16), _row(conf_b_in[idx]),
                        n_lat, tm)
            h = _confout(u, h, mod, _row(g_mix_post[layer]), conf_dw_w[idx], _row(conf_dw_b[idx]),
                         _row(conf_ln_g[idx]), _row(conf_ln_b[idx]), conf_w_out[idx].astype(BF16),
                         _row(conf_b_out[idx]), n_lat, 256)
        h = _ffn(h, mod, _row(g_ffn_pre[layer]), _row(g_ffn_post[layer]), w_ffn_in[layer].astype(BF16),
                 w_ffn_out[layer].astype(BF16), n_lat, 1024)
        assert not any(j % 2 == 0 for j in range(layer + 1, depth)), "context advance not implemented"
    return h.reshape(bsz, n_lat, d)
```

```python
import functools
import math

import jax
import jax.numpy as jnp
import numpy as np
from jax import lax
from jax.experimental import pallas as pl
from jax.experimental.pallas import tpu as pltpu

F32 = jnp.float32
BF16 = jnp.bfloat16
HIGHEST = lax.Precision.HIGHEST

EPS = 1e-6
GRID_W = 64
ROPE_THETA = 10000.0
A_HEADS = 4
A_HEAD_DIM = 128
A_WIDTH = A_HEADS * A_HEAD_DIM
SHORT_CONV_W = 5
CHUNK = 64
B_Q_HEADS = 8
B_KV_HEADS = 2
B_HEAD_DIM = 64
B_GROUP = B_Q_HEADS // B_KV_HEADS
B_WIDTH = B_Q_HEADS * B_HEAD_DIM
B_KV_WIDTH = B_KV_HEADS * B_HEAD_DIM
ROPE_AXIS_PAIRS = B_HEAD_DIM // 4
CONF_KERNEL = 31
CONF_HALO = 16
LANES = 128
VMEM_LIMIT = 56 * 1024 * 1024


def _cparams(sem):
    return pltpu.CompilerParams(dimension_semantics=sem, vmem_limit_bytes=VMEM_LIMIT)


def _resident(shape):
    return pl.BlockSpec(shape, lambda i: (0, 0), pipeline_mode=pl.Buffered(1))


def _silu(x):
    return x * jax.nn.sigmoid(x)


def _dot(a, b):
    return jnp.dot(a, b, preferred_element_type=F32)


def _dot_hi(a, b):
    return jnp.dot(a, b, preferred_element_type=F32, precision=HIGHEST)


def _dot_nt(a, b):
    return lax.dot_general(a, b, (((1,), (1,)), ((), ())), preferred_element_type=F32)


def _dot_tn(a, b):
    return lax.dot_general(a, b, (((0,), (0,)), ((), ())), preferred_element_type=F32)


def _rms(x, gain):
    return x * lax.rsqrt(jnp.mean(x * x, axis=-1, keepdims=True) + EPS) * gain


def _ada_kernel(c_ref, w_ref, b_ref, o_ref):
    o_ref[0] = _dot_hi(_silu(c_ref[...]), w_ref[0]) + b_ref[0]


def _ada_terms(cond, w_mod, b_mod):
    depth, d, n6 = w_mod.shape
    tn = 1536
    return pl.pallas_call(
        _ada_kernel,
        grid=(depth, n6 // tn),
        in_specs=[pl.BlockSpec((8, d), lambda l, j: (0, 0)),
                  pl.BlockSpec((1, d, tn), lambda l, j: (l, 0, j)),
                  pl.BlockSpec((1, 1, tn), lambda l, j: (l, 0, j))],
        out_specs=pl.BlockSpec((1, 8, tn), lambda l, j: (l, 0, j)),
        out_shape=jax.ShapeDtypeStruct((depth, 8, n6), F32),
        compiler_params=_cparams(("arbitrary", "arbitrary")),
        name="ada",
    )(cond, w_mod, b_mod.reshape(depth, 1, n6))


def _modulated(x, mod_ref, gain_ref, shift_row):
    y = _rms(x, gain_ref[...])
    return y * (1.0 + mod_ref[0, shift_row + 1:shift_row + 2, :]) + mod_ref[0, shift_row:shift_row + 1, :]


PROJ_HALO = 16


def _hyb_inproj_kernel(x_ref, xp_ref, xn_ref, mod_ref, g_ref, w_ref, cw_ref, kg_ref, cos_ref, slo_ref, shi_ref,
                       qkv_ref, z_ref, qb_ref, ba_ref, ko_ref, vo_ref, *, tiles_per_seq):
    pos = pl.program_id(0) % tiles_per_seq
    tm = x_ref.shape[0]
    halo = PROJ_HALO
    qkv_w = 3 * A_WIDTH
    a_prev = jnp.where(pos > 0, _modulated(xp_ref[...], mod_ref, g_ref, 0), 0.0).astype(BF16)
    a_next = jnp.where(pos < tiles_per_seq - 1, _modulated(xn_ref[...], mod_ref, g_ref, 0), 0.0).astype(BF16)
    a = _modulated(x_ref[...], mod_ref, g_ref, 0).astype(BF16)
    a_ext = jnp.concatenate([a_prev, a, a_next], axis=0)

    off_z = qkv_w
    off_q = off_z + A_WIDTH
    off_k = off_q + B_WIDTH
    off_v = off_k + B_KV_WIDTH
    off_ba = off_v + B_KV_WIDTH

    def gate_proj():
        z_ref[...] = _dot(a, w_ref[:, off_z:off_q])

    def query_proj():
        qb_ref[...] = _dot(a, w_ref[:, off_q:off_k])

    def key_value_proj():
        kb = _dot(a, w_ref[:, off_k:off_v])
        vb = _dot(a, w_ref[:, off_v:off_ba])
        k = _rope(_head_rms(kb, kg_ref[...], B_HEAD_DIM), cos_ref[...], slo_ref[...], shi_ref[...]).astype(BF16)
        vt = vb.T.astype(BF16)
        for hh in range(B_KV_HEADS):
            ko_ref[0, hh] = k[:, hh * B_HEAD_DIM:(hh + 1) * B_HEAD_DIM]
            vo_ref[0, hh, :B_HEAD_DIM, :] = vt[hh * B_HEAD_DIM:(hh + 1) * B_HEAD_DIM, :]
            vo_ref[0, hh, B_HEAD_DIM:, :] = jnp.ones((ATTN_VT_ROWS - B_HEAD_DIM, tm), BF16)

    def logit_proj():
        ba_ref[...] = _dot(a, w_ref[:, off_ba:off_ba + A_HEADS * LANES])

    others = [gate_proj, query_proj, key_value_proj, logit_proj]
    ext_rows = tm + 2 * halo
    pair = 2 * A_HEAD_DIM
    n_pairs = qkv_w // pair
    ext_next = _dot(a_ext, w_ref[:, :pair])
    for cb in range(n_pairs):
        cols = slice(cb * pair, (cb + 1) * pair)
        ext = ext_next
        if cb + 1 < n_pairs:
            ext_next = _dot(a_ext, w_ref[:, (cb + 1) * pair:(cb + 2) * pair])
        if cb < len(others):
            others[cb]()
        w = cw_ref[:, cols]
        acc = jnp.zeros((tm, pair), F32)
        for tap in range(SHORT_CONV_W):
            shift = (SHORT_CONV_W // 2 - tap) % ext_rows
            rolled = pltpu.roll(ext, shift, axis=0) if shift else ext
            acc = acc + rolled[halo:halo + tm, :] * w[tap:tap + 1, :]
        y = _silu(acc)
        for part in range(2):
            head = 2 * cb + part
            yh = y[:, part * A_HEAD_DIM:(part + 1) * A_HEAD_DIM]
            if head < 2 * A_HEADS:
                yh = yh * lax.rsqrt(jnp.sum(yh * yh, axis=-1, keepdims=True) + EPS)
            if head < A_HEADS:
                yh = yh * (A_HEAD_DIM ** -0.5)
            qkv_ref[:, head * A_HEAD_DIM:(head + 1) * A_HEAD_DIM] = yh


def _hyb_inproj(x2, mod, gain, w, conv_w, k_gain, cos, slo, shi, bsz, rows_per_seq, tm):
    n, d = x2.shape
    tiles_per_seq = rows_per_seq // tm
    hb = tm // PROJ_HALO
    n_halo = n // PROJ_HALO
    f32_out = lambda wd: (pl.BlockSpec((tm, wd), lambda i: (i, 0)), jax.ShapeDtypeStruct((n, wd), F32))
    outs = [f32_out(3 * A_WIDTH), f32_out(A_WIDTH), f32_out(B_WIDTH), f32_out(A_HEADS * LANES),
            (pl.BlockSpec((1, B_KV_HEADS, tm, B_HEAD_DIM), lambda i: (i // tiles_per_seq, 0, i % tiles_per_seq, 0)),
             jax.ShapeDtypeStruct((bsz, B_KV_HEADS, rows_per_seq, B_HEAD_DIM), BF16)),
            (pl.BlockSpec((1, B_KV_HEADS, ATTN_VT_ROWS, tm), lambda i: (i // tiles_per_seq, 0, 0, i % tiles_per_seq)),
             jax.ShapeDtypeStruct((bsz, B_KV_HEADS, ATTN_VT_ROWS, rows_per_seq), BF16))]
    tab = pl.BlockSpec((tm, B_KV_WIDTH), lambda i: (i % tiles_per_seq, 0))
    return pl.pallas_call(
        functools.partial(_hyb_inproj_kernel, tiles_per_seq=tiles_per_seq),
        grid=(n // tm,),
        in_specs=[pl.BlockSpec((tm, d), lambda i: (i, 0)),
                  pl.BlockSpec((PROJ_HALO, d), lambda i: (jnp.maximum(i * hb - 1, 0), 0)),
                  pl.BlockSpec((PROJ_HALO, d), lambda i: (jnp.minimum((i + 1) * hb, n_halo - 1), 0)),
                  pl.BlockSpec((1, 6, d), lambda i: (i // tiles_per_seq if mod.shape[0] > 1 else 0, 0, 0)),
                  pl.BlockSpec((1, d), lambda i: (0, 0)),
                  _resident(w.shape),
                  _resident(conv_w.shape),
                  pl.BlockSpec((1, B_KV_WIDTH), lambda i: (0, 0)), tab, tab, tab],
        out_specs=[o[0] for o in outs],
        out_shape=[o[1] for o in outs],
        compiler_params=_cparams(("arbitrary",)),
        name="inproj",
    )(x2, x2, x2, mod, gain, w, conv_w, k_gain, cos, slo, shi)


def _tri_masks():
    i = lax.broadcasted_iota(jnp.int32, (CHUNK, CHUNK), 0)
    j = lax.broadcasted_iota(jnp.int32, (CHUNK, CHUNK), 1)
    return i, j


def _unit_tri_inverse(l_mat, i, j):
    eye = (i == j).astype(F32)
    same16 = jnp.right_shift(i, 4) == jnp.right_shift(j, 4)
    same32 = jnp.right_shift(i, 5) == jnp.right_shift(j, 5)
    d1 = jnp.where(same16, l_mat, 0.0)
    c1 = jnp.where(jnp.logical_and(same32, jnp.logical_not(same16)), l_mat, 0.0)
    c2 = jnp.where(same32, 0.0, l_mat)
    p = eye - d1
    d2 = _dot_hi(d1, d1)
    p = p + _dot_hi(p, d2)
    d4 = _dot_hi(d2, d2)
    p = p + _dot_hi(p, d4)
    d8 = _dot_hi(d4, d4)
    p = p + _dot_hi(p, d8)
    t1 = p - _dot_hi(p, _dot_hi(c1, p))
    return t1 - _dot_hi(t1, _dot_hi(c2, t1))


def _delta_kernel(q_ref, k_ref, v_ref, qc_ref, kc_ref, vc_ref, wq_ref, wk_ref, wv_ref,
                  ba_ref, bac_ref, coef_ref, z_ref, on_ref, o_ref,
                  xp, qs, ks, vs, bs, of_s, ob_s, *, n_lat, n_ctx):
    n_all = n_ctx + n_lat
    nch_ctx = n_ctx // CHUNK
    nch_all = n_all // CHUNK
    rb = 256

    def conv_into(src_ref, w_ref, dst, dst_off, n_rows, kind):
        xp[0:8, :] = jnp.zeros((8, A_HEAD_DIM), F32)
        xp[pl.ds(8, n_rows), :] = src_ref[...]
        xp[pl.ds(8 + n_rows, 8), :] = jnp.zeros((8, A_HEAD_DIM), F32)
        w = w_ref[...]

        def body(t, carry):
            r0 = pl.multiple_of(t * rb, rb)
            blk = xp[pl.ds(r0, rb + 16), :]
            acc = jnp.zeros((rb, A_HEAD_DIM), F32)
            for tap in range(SHORT_CONV_W):
                shift = (SHORT_CONV_W // 2 - tap) % (rb + 16)
                rolled = pltpu.roll(blk, shift, axis=0) if shift else blk
                acc = acc + rolled[8:8 + rb, :] * w[tap:tap + 1, :]
            y = _silu(acc)
            if kind < 2:
                y = y * lax.rsqrt(jnp.sum(y * y, axis=-1, keepdims=True) + EPS)
            if kind == 0:
                y = y * (A_HEAD_DIM ** -0.5)
            dst[pl.ds(pl.multiple_of(dst_off + r0, rb), rb), :] = y
            return carry

        lax.fori_loop(0, n_rows // rb, body, 0)

    for kind, (lat_ref, ctx_ref, w_ref, dst) in enumerate(
            ((q_ref, qc_ref, wq_ref, qs), (k_ref, kc_ref, wk_ref, ks), (v_ref, vc_ref, wv_ref, vs))):
        conv_into(ctx_ref, w_ref, dst, 0, n_ctx, kind)
        conv_into(lat_ref, w_ref, dst, n_ctx, n_lat, kind)

    lane = lax.broadcasted_iota(jnp.int32, (1, LANES), 1)
    coef = coef_ref[0]
    neg_a = -jnp.exp(jnp.where(lane == 2, coef[0:1, :], coef[1:2, :]))
    dtb = jnp.where(lane == 2, coef[2:3, :], coef[3:4, :])

    def beta_g(raw):
        xg = raw + dtb
        softplus = jnp.maximum(xg, 0.0) + jnp.log(1.0 + jnp.exp(-jnp.abs(xg)))
        return jnp.where(lane < 2, jax.nn.sigmoid(raw), neg_a * softplus)

    bs[pl.ds(0, n_ctx), :] = beta_g(bac_ref[...])
    bs[pl.ds(n_ctx, n_lat), :] = beta_g(ba_ref[...])

    i_idx, j_idx = _tri_masks()
    ones8 = jnp.ones((8, CHUNK), F32)

    def precompute(chunk, d):
        r = pl.multiple_of(chunk * CHUNK, CHUNK)
        q = qs[pl.ds(r, CHUNK), :]
        k = ks[pl.ds(r, CHUNK), :]
        v = vs[pl.ds(r, CHUNK), :]
        bb = bs[pl.ds(r, CHUNK), :]
        beta = bb[:, d:d + 1]
        g = bb[:, 2 + d:3 + d]
        incl = (i_idx >= j_idx) if d == 0 else (i_idx <= j_idx)
        strict = (i_idx > j_idx) if d == 0 else (i_idx < j_idx)
        g_b = jnp.broadcast_to(g, (CHUNK, CHUNK))
        gc_mat = _dot_hi(incl.astype(F32), g_b)
        seen = (i_idx <= j_idx) if d == 0 else (i_idx >= j_idx)
        gr8 = _dot_hi(ones8, jnp.where(seen, g_b, 0.0))
        gr_mat = jnp.broadcast_to(gr8[0:1, :], (CHUNK, CHUNK))
        decay = jnp.where(incl, jnp.exp(jnp.minimum(gc_mat - gr_mat, 0.0)), 0.0)
        gc = gc_mat[:, 0:1]
        g_last = gc[CHUNK - 1:CHUNK, :] if d == 0 else gc[0:1, :]
        egc = jnp.exp(gc)
        kb = k * beta
        kbf = k.astype(BF16)
        l_mat = jnp.where(strict, _dot_nt(kb.astype(BF16), kbf) * decay, 0.0)
        t_inv = _unit_tri_inverse(l_mat, i_idx, j_idx)
        rhs = jnp.concatenate([v * beta, kb * egc], axis=1)
        sol = _dot_hi(t_inv, rhs)
        u = sol[:, :A_HEAD_DIM]
        w = sol[:, A_HEAD_DIM:]
        k_tail = k * jnp.exp(g_last - gc)
        q_dec = q * egc
        intra = _dot_nt(q.astype(BF16), kbf) * decay
        return u, w, k_tail, q_dec, intra, jnp.exp(g_last)

    def scan_step(state, pre):
        u, w, k_tail, q_dec, intra, e_last = pre
        ws = _dot(jnp.concatenate([w, q_dec], axis=0).astype(BF16), state.astype(BF16))
        v_new = u - ws[:CHUNK]
        v_new_b = v_new.astype(BF16)
        o = ws[CHUNK:] + _dot(intra.astype(BF16), v_new_b)
        new_state = state * e_last + _dot_tn(k_tail.astype(BF16), v_new_b)
        return new_state, o

    def bwd_chunk(step):
        return jnp.where(step < nch_ctx, nch_ctx - 1 - step, nch_all + nch_ctx - 1 - step)

    def body(step, carry):
        s_f, s_b = carry
        cf = step
        cb = bwd_chunk(step)
        s_f, o_f = scan_step(s_f, precompute(cf, 0))
        s_b, o_b = scan_step(s_b, precompute(cb, 1))

        @pl.when(step >= nch_ctx)
        def _():
            of_s[pl.ds(pl.multiple_of((cf - nch_ctx) * CHUNK, CHUNK), CHUNK), :] = o_f
            ob_s[pl.ds(pl.multiple_of((cb - nch_ctx) * CHUNK, CHUNK), CHUNK), :] = o_b

        return s_f, s_b

    zero_state = jnp.zeros((A_HEAD_DIM, A_HEAD_DIM), F32)
    lax.fori_loop(0, nch_all, body, (zero_state, zero_state))

    o = of_s[...] + ob_s[...]
    o_ref[...] = _rms(o, on_ref[...]) * _silu(z_ref[...])


def _delta_mixer(qkv, qkv_c, conv_w, ba, ba_c, coef, z, out_norm, bsz, n_lat, n_ctx):
    h = A_HEADS
    hd = A_HEAD_DIM
    n_all = n_lat + n_ctx

    def col(kind):
        return lambda b, hh: (b, kind * h + hh)

    in_specs = (
        [pl.BlockSpec((n_lat, hd), col(kd)) for kd in range(3)]
        + [pl.BlockSpec((n_ctx, hd), col(kd)) for kd in range(3)]
        + [pl.BlockSpec((SHORT_CONV_W, hd), (lambda b, hh, kd=kd: (0, kd * h + hh))) for kd in range(3)]
        + [pl.BlockSpec((n_lat, LANES), lambda b, hh: (b, hh)),
           pl.BlockSpec((n_ctx, LANES), lambda b, hh: (b, hh)),
           pl.BlockSpec((1, 8, LANES), lambda b, hh: (hh, 0, 0)),
           pl.BlockSpec((n_lat, hd), lambda b, hh: (b, hh)),
           pl.BlockSpec((1, hd), lambda b, hh: (0, 0))])
    return pl.pallas_call(
        functools.partial(_delta_kernel, n_lat=n_lat, n_ctx=n_ctx),
        grid=(bsz, h),
        in_specs=in_specs,
        out_specs=pl.BlockSpec((n_lat, hd), lambda b, hh: (b, hh)),
        out_shape=jax.ShapeDtypeStruct((bsz * n_lat, h * hd), F32),
        scratch_shapes=[pltpu.VMEM((n_lat + 16, hd), F32),
                        pltpu.VMEM((n_all, hd), F32), pltpu.VMEM((n_all, hd), F32), pltpu.VMEM((n_all, hd), F32),
                        pltpu.VMEM((n_all, LANES), F32),
                        pltpu.VMEM((n_lat, hd), F32), pltpu.VMEM((n_lat, hd), F32)],
        compiler_params=_cparams(("arbitrary", "arbitrary")),
        name="delta",
    )(qkv, qkv, qkv, qkv_c, qkv_c, qkv_c, conv_w, conv_w, conv_w, ba, ba_c, coef, z, out_norm)


DELTA_BLOCK = 4
DELTA_ROWS = DELTA_BLOCK * CHUNK
CONV_HALO = 8


def _split_bf16(a):
    hi = a.astype(BF16)
    return hi, (a - hi.astype(F32)).astype(BF16)


def _drain(*gens):
    last = [None] * len(gens)
    live = list(range(len(gens)))
    while live:
        for g in list(live):
            try:
                last[g] = next(gens[g])
            except StopIteration:
                live.remove(g)
    return last


def _unit_tri_inverse_steps(l_mats, i, j):
    eye = (i == j).astype(F32)
    same16 = jnp.right_shift(i, 4) == jnp.right_shift(j, 4)
    same32 = jnp.right_shift(i, 5) == jnp.right_shift(j, 5)
    off32 = jnp.logical_and(same32, jnp.logical_not(same16))
    b = lambda a: a.astype(BF16)
    each = lambda f, *ls: [f(*xs) for xs in zip(*ls)]
    d1 = each(lambda l: b(jnp.where(same16, l, 0.0)), l_mats)
    p = each(lambda d: eye - d.astype(F32), d1)
    dk = d1
    for _ in range(3):
        dk = each(lambda d: b(_dot(d, d)), dk)
        yield None
        p = each(lambda pp, d: pp + _dot(b(pp), d), p, dk)
        yield None
    for sel in (off32, jnp.logical_not(same32)):
        cm = each(lambda l: b(jnp.where(sel, l, 0.0)), l_mats)
        pb = each(b, p)
        inner = each(lambda c, q: b(_dot(c, q)), cm, pb)
        yield None
        p = each(lambda pp, q, m: pp - _dot(q, m), p, pb, inner)
        yield None
    l_split = each(_split_bf16, l_mats)
    t_split = each(_split_bf16, p)
    resid = each(lambda t0, ls, ts: eye - t0 - (_dot(ls[0], ts[0]) + _dot(ls[0], ts[1]) + _dot(ls[1], ts[0])),
                 p, l_split, t_split)
    yield None
    yield each(lambda t0, ts, r: t0 + _dot(ts[0], b(r)), p, t_split, resid)


def _unit_tri_inverses(l_mats, i, j):
    return _drain(_unit_tri_inverse_steps(l_mats, i, j))[0]


DELTA_HEADS = 4


def _dprep_kernel(q_ref, k_ref, v_ref, qc_ref, kc_ref, vc_ref, ba_ref, bac_ref, coef_ref,
                  u_ref, w_ref, kt_ref, qd_ref, in_ref, el_ref):
    blk = pl.program_id(2)
    is_ctx = blk == 0
    heads = [slice(hh * A_HEAD_DIM, (hh + 1) * A_HEAD_DIM) for hh in range(DELTA_HEADS)]

    def per_head(main_ref, ctx_ref):
        x = jnp.where(is_ctx, ctx_ref[...], main_ref[...])
        return [x[:, ln] for ln in heads]

    q_h = per_head(q_ref, qc_ref)
    k_h = per_head(k_ref, kc_ref)
    v_h = per_head(v_ref, vc_ref)

    lane = lax.broadcasted_iota(jnp.int32, (1, LANES), 1)
    raw_all = jnp.where(is_ctx, bac_ref[...], ba_ref[...])
    bg_h = []
    for hh, ln in enumerate(heads):
        coef = coef_ref[hh]
        neg_a = -jnp.exp(jnp.where(lane == 2, coef[0:1, :], coef[1:2, :]))
        dtb = jnp.where(lane == 2, coef[2:3, :], coef[3:4, :])
        raw = raw_all[:, ln]
        xg = raw + dtb
        softplus = jnp.maximum(xg, 0.0) + jnp.log(1.0 + jnp.exp(-jnp.abs(xg)))
        bg_h.append(jnp.where(lane < 2, jax.nn.sigmoid(raw), neg_a * softplus))

    i_idx, j_idx = _tri_masks()
    i_w = lax.broadcasted_iota(jnp.int32, (CHUNK, LANES), 0)
    j_w = lax.broadcasted_iota(jnp.int32, (CHUNK, LANES), 1)
    in_chunk = j_w < CHUNK
    ones8 = jnp.ones((8, CHUNK), F32)
    chunks = [slice(c * CHUNK, (c + 1) * CHUNK) for c in range(DELTA_BLOCK)]

    kk, qk = {}, {}
    for hh in range(DELTA_HEADS):
        for c, r in enumerate(chunks):
            kbf = k_h[hh][r].astype(BF16)
            prod = _dot_nt(jnp.concatenate([kbf, q_h[hh][r].astype(BF16)], axis=0),
                           jnp.concatenate([kbf, jnp.zeros_like(kbf)], axis=0))
            kk[hh, c] = prod[:CHUNK]
            qk[hh, c] = prod[CHUNK:]

    gc_all, gr_all = {}, {}
    for d in range(2):
        incl = (i_idx >= j_idx) if d == 0 else (i_idx <= j_idx)
        seen_w = jnp.logical_and(in_chunk, (i_w <= j_w) if d == 0 else (i_w >= j_w))
        for hh in range(DELTA_HEADS):
            g_cols = [jnp.broadcast_to(bg_h[hh][r, 2 + d:3 + d], (CHUNK, LANES)) for r in chunks]
            gc_all[hh, d] = _dot_hi(incl.astype(F32), jnp.concatenate(g_cols, axis=1))
            gr_all[hh, d] = _dot_hi(ones8, jnp.concatenate([jnp.where(seen_w, g, 0.0) for g in g_cols], axis=1))

    insts, l_mats, rhs_all = [], [], []
    for d in range(2):
        strict = (i_idx > j_idx) if d == 0 else (i_idx < j_idx)
        incl_w = jnp.logical_and(in_chunk, (i_w >= j_w) if d == 0 else (i_w <= j_w))
        for hh, ln in enumerate(heads):
            for c, r in enumerate(chunks):
                q, k, v = q_h[hh][r], k_h[hh][r], v_h[hh][r]
                beta = bg_h[hh][r, d:d + 1]
                gc_w = gc_all[hh, d][:, c * LANES:(c + 1) * LANES]
                gr_w = jnp.broadcast_to(gr_all[hh, d][0:1, c * LANES:(c + 1) * LANES], (CHUNK, LANES))
                decay = jnp.where(incl_w, jnp.exp(jnp.minimum(gc_w - gr_w, 0.0)), 0.0)
                g_last = gc_w[CHUNK - 1:CHUNK, :] if d == 0 else gc_w[0:1, :]
                egc = jnp.exp(gc_w)
                l_mats.append(jnp.where(strict, (kk[hh, c] * beta * decay)[:, :CHUNK], 0.0))
                rhs_all.append(jnp.concatenate([v * beta, k * beta * egc], axis=1).astype(BF16))
                kt_ref[d, 0, r, ln] = (k * jnp.exp(g_last - gc_w)).astype(BF16)
                qd_ref[d, 0, r, ln] = (q * egc).astype(BF16)
                in_ref[d, 0, r, ln] = (qk[hh, c] * decay).astype(BF16)
                el_ref[d, 0, c * 8:(c + 1) * 8, ln] = jnp.broadcast_to(jnp.exp(g_last), (8, LANES))
                insts.append((d, r, ln))

    t_split = [_split_bf16(t) for t in _unit_tri_inverses(l_mats, i_idx, j_idx)]
    sols = [_dot(t_hi, rhs) + _dot(t_lo, rhs) for (t_hi, t_lo), rhs in zip(t_split, rhs_all)]
    for (d, r, ln), sol in zip(insts, sols):
        u_ref[d, 0, r, ln] = sol[:, :A_HEAD_DIM]
        w_ref[d, 0, r, ln] = sol[:, A_HEAD_DIM:].astype(BF16)


def _delta_prep(qkv, qkv_c, ba, ba_c, coef, bsz, n_lat, n_ctx):
    assert n_ctx == DELTA_ROWS and n_lat % DELTA_ROWS == 0 and A_HEADS % DELTA_HEADS == 0
    groups = A_HEADS // DELTA_HEADS
    wd = DELTA_HEADS * A_HEAD_DIM
    lat_blocks = n_lat // DELTA_ROWS
    n_blocks = lat_blocks + 1
    n_all = n_lat + n_ctx

    def lat_blk(b, j):
        return b * lat_blocks + jnp.maximum(j - 1, 0)

    def main(kd):
        return pl.BlockSpec((DELTA_ROWS, wd), lambda b, hg, j: (lat_blk(b, j), kd * groups + hg))

    def ctxb(kd):
        return pl.BlockSpec((n_ctx, wd), lambda b, hg, j: (b, kd * groups + hg))

    in_specs = ([main(kd) for kd in range(3)] + [ctxb(kd) for kd in range(3)]
                + [pl.BlockSpec((DELTA_ROWS, wd), lambda b, hg, j: (lat_blk(b, j), hg)),
                   pl.BlockSpec((n_ctx, wd), lambda b, hg, j: (b, hg)),
                   pl.BlockSpec((DELTA_HEADS, 8, LANES), lambda b, hg, j: (hg, 0, 0))])
    big = pl.BlockSpec((2, 1, DELTA_ROWS, wd), lambda b, hg, j: (0, b, j, hg))
    small = pl.BlockSpec((2, 1, DELTA_BLOCK * 8, wd), lambda b, hg, j: (0, b, j, hg))
    sds = lambda rows, dt: jax.ShapeDtypeStruct((2, bsz, rows, A_WIDTH), dt)
    return pl.pallas_call(
        _dprep_kernel,
        grid=(bsz, groups, n_blocks),
        in_specs=in_specs,
        out_specs=[big, big, big, big, big, small],
        out_shape=[sds(n_all, F32), sds(n_all, BF16), sds(n_all, BF16), sds(n_all, BF16), sds(n_all, BF16),
                   sds(n_all // CHUNK * 8, F32)],
        compiler_params=_cparams(("arbitrary", "arbitrary", "arbitrary")),
        name="dprep",
    )(*([qkv] * 3), *([qkv_c] * 3), ba, ba_c, coef)


def _dprep_kernel_single_head(q_ref, k_ref, v_ref, qp_ref, kp_ref, vp_ref, qn_ref, kn_ref, vn_ref, qc_ref, kc_ref,
                              vc_ref, wq_ref, wk_ref, wv_ref, ba_ref, bac_ref, coef_ref,
                              u_ref, w_ref, kt_ref, qd_ref, in_ref, el_ref, *, n_blocks):
    blk = pl.program_id(2)
    is_ctx = blk == 0
    rows = DELTA_ROWS
    ext_rows = rows + 2 * CONV_HALO

    def conv(main_ref, ctx_ref, prev_ref, next_ref, w_ref, kind):
        x = jnp.where(is_ctx, ctx_ref[...], main_ref[...])
        prev = jnp.where(blk <= 1, 0.0, prev_ref[...])
        nxt = jnp.where(jnp.logical_or(is_ctx, blk == n_blocks - 1), 0.0, next_ref[...])
        ext = jnp.concatenate([prev, x, nxt], axis=0)
        w = w_ref[...]
        acc = jnp.zeros((rows, A_HEAD_DIM), F32)
        for tap in range(SHORT_CONV_W):
            shift = (SHORT_CONV_W // 2 - tap) % ext_rows
            rolled = pltpu.roll(ext, shift, axis=0) if shift else ext
            acc = acc + rolled[CONV_HALO:CONV_HALO + rows, :] * w[tap:tap + 1, :]
        y = _silu(acc)
        if kind < 2:
            y = y * lax.rsqrt(jnp.sum(y * y, axis=-1, keepdims=True) + EPS)
        if kind == 0:
            y = y * (A_HEAD_DIM ** -0.5)
        return y

    q_all = conv(q_ref, qc_ref, qp_ref, qn_ref, wq_ref, 0)
    k_all = conv(k_ref, kc_ref, kp_ref, kn_ref, wk_ref, 1)
    v_all = conv(v_ref, vc_ref, vp_ref, vn_ref, wv_ref, 2)

    lane = lax.broadcasted_iota(jnp.int32, (1, LANES), 1)
    coef = coef_ref[0]
    neg_a = -jnp.exp(jnp.where(lane == 2, coef[0:1, :], coef[1:2, :]))
    dtb = jnp.where(lane == 2, coef[2:3, :], coef[3:4, :])
    raw = jnp.where(is_ctx, bac_ref[...], ba_ref[...])
    xg = raw + dtb
    softplus = jnp.maximum(xg, 0.0) + jnp.log(1.0 + jnp.exp(-jnp.abs(xg)))
    bg = jnp.where(lane < 2, jax.nn.sigmoid(raw), neg_a * softplus)

    i_idx, j_idx = _tri_masks()
    i_w = lax.broadcasted_iota(jnp.int32, (CHUNK, LANES), 0)
    j_w = lax.broadcasted_iota(jnp.int32, (CHUNK, LANES), 1)
    ones8 = jnp.ones((8, CHUNK), F32)
    chunks = [slice(c * CHUNK, (c + 1) * CHUNK) for c in range(DELTA_BLOCK)]

    kk, qk = [], []
    for r in chunks:
        kbf = k_all[r].astype(BF16)
        prod = _dot_nt(jnp.concatenate([kbf, q_all[r].astype(BF16)], axis=0),
                       jnp.concatenate([kbf, jnp.zeros_like(kbf)], axis=0))
        kk.append(prod[:CHUNK])
        qk.append(prod[CHUNK:])

    insts, l_mats, rhs_all = [], [], []
    for d in range(2):
        incl = (i_idx >= j_idx) if d == 0 else (i_idx <= j_idx)
        strict = (i_idx > j_idx) if d == 0 else (i_idx < j_idx)
        in_chunk = j_w < CHUNK
        incl_w = jnp.logical_and(in_chunk, (i_w >= j_w) if d == 0 else (i_w <= j_w))
        seen_w = jnp.logical_and(in_chunk, (i_w <= j_w) if d == 0 else (i_w >= j_w))
        g_cols = [jnp.broadcast_to(bg[r, 2 + d:3 + d], (CHUNK, LANES)) for r in chunks]
        gc_all = _dot_hi(incl.astype(F32), jnp.concatenate(g_cols, axis=1))
        gr_all = _dot_hi(ones8, jnp.concatenate([jnp.where(seen_w, g, 0.0) for g in g_cols], axis=1))
        for c, r in enumerate(chunks):
            q, k, v = q_all[r], k_all[r], v_all[r]
            beta = bg[r, d:d + 1]
            gc_w = gc_all[:, c * LANES:(c + 1) * LANES]
            gr_w = jnp.broadcast_to(gr_all[0:1, c * LANES:(c + 1) * LANES], (CHUNK, LANES))
            decay = jnp.where(incl_w, jnp.exp(jnp.minimum(gc_w - gr_w, 0.0)), 0.0)
            g_last = gc_w[CHUNK - 1:CHUNK, :] if d == 0 else gc_w[0:1, :]
            egc = jnp.exp(gc_w)
            l_mats.append(jnp.where(strict, (kk[c] * beta * decay)[:, :CHUNK], 0.0))
            rhs_all.append(jnp.concatenate([v * beta, k * beta * egc], axis=1).astype(BF16))
            kt_ref[d, 0, r, :] = (k * jnp.exp(g_last - gc_w)).astype(BF16)
            qd_ref[d, 0, r, :] = (q * egc).astype(BF16)
            in_ref[d, 0, r, :] = (qk[c] * decay).astype(BF16)
            el_ref[d, 0, c * 8:(c + 1) * 8, :] = jnp.broadcast_to(jnp.exp(g_last), (8, LANES))
            insts.append((d, r))

    t_split = [_split_bf16(t) for t in _unit_tri_inverses(l_mats, i_idx, j_idx)]
    sols = [_dot(t_hi, rhs) + _dot(t_lo, rhs) for (t_hi, t_lo), rhs in zip(t_split, rhs_all)]
    for (d, r), sol in zip(insts, sols):
        u_ref[d, 0, r, :] = sol[:, :A_HEAD_DIM]
        w_ref[d, 0, r, :] = sol[:, A_HEAD_DIM:].astype(BF16)


def _delta_prep_single_head(qkv, qkv_c, conv_w, ba, ba_c, coef, bsz, n_lat, n_ctx):
    assert n_ctx == DELTA_ROWS and n_lat % DELTA_ROWS == 0
    h = A_HEADS
    hd = A_HEAD_DIM
    lat_blocks = n_lat // DELTA_ROWS
    n_blocks = lat_blocks + 1
    n_all = n_lat + n_ctx
    halo_per_block = DELTA_ROWS // CONV_HALO
    n_halo = bsz * n_lat // CONV_HALO

    def lat_blk(b, j):
        return b * lat_blocks + jnp.maximum(j - 1, 0)

    def main(kd):
        return pl.BlockSpec((DELTA_ROWS, hd), lambda b, hh, j: (lat_blk(b, j), kd * h + hh))

    def prev(kd):
        return pl.BlockSpec((CONV_HALO, hd),
                            lambda b, hh, j: (jnp.maximum(lat_blk(b, j) * halo_per_block - 1, 0), kd * h + hh))

    def nxt(kd):
        return pl.BlockSpec((CONV_HALO, hd),
                            lambda b, hh, j: (jnp.minimum((lat_blk(b, j) + 1) * halo_per_block, n_halo - 1),
                                              kd * h + hh))

    def ctxb(kd):
        return pl.BlockSpec((n_ctx, hd), lambda b, hh, j: (b, kd * h + hh))

    def tap(kd):
        return pl.BlockSpec((SHORT_CONV_W, hd), lambda b, hh, j: (0, kd * h + hh))

    in_specs = ([main(kd) for kd in range(3)] + [prev(kd) for kd in range(3)] + [nxt(kd) for kd in range(3)]
                + [ctxb(kd) for kd in range(3)] + [tap(kd) for kd in range(3)]
                + [pl.BlockSpec((DELTA_ROWS, LANES), lambda b, hh, j: (lat_blk(b, j), hh)),
                   pl.BlockSpec((n_ctx, LANES), lambda b, hh, j: (b, hh)),
                   pl.BlockSpec((1, 8, LANES), lambda b, hh, j: (hh, 0, 0))])
    big = pl.BlockSpec((2, 1, DELTA_ROWS, hd), lambda b, hh, j: (0, b, j, hh))
    small = pl.BlockSpec((2, 1, DELTA_BLOCK * 8, LANES), lambda b, hh, j: (0, b, j, hh))
    sds = lambda rows, dt: jax.ShapeDtypeStruct((2, bsz, rows, h * hd), dt)
    return pl.pallas_call(
        functools.partial(_dprep_kernel, n_blocks=n_blocks),
        grid=(bsz, h, n_blocks),
        in_specs=in_specs,
        out_specs=[big, big, big, big, big, small],
        out_shape=[sds(n_all, F32), sds(n_all, BF16), sds(n_all, BF16), sds(n_all, BF16), sds(n_all, BF16),
                   sds(n_all // CHUNK * 8, F32)],
        compiler_params=_cparams(("arbitrary", "arbitrary", "arbitrary")),
        name="dprep",
    )(*([qkv] * 9), *([qkv_c] * 3), *([conv_w] * 3), ba, ba_c, coef)


def _dscan_kernel(uf, wf, ktf, qdf, inf, elf, ub, wb, ktb, qdb, inb, elb, of_ref, ob_ref, state):
    step = pl.program_id(1)

    @pl.when(step == 0)
    def _():
        state[...] = jnp.zeros(state.shape, F32)

    dirs = ((uf, wf, ktf, qdf, inf, elf, of_ref), (ub, wb, ktb, qdb, inb, elb, ob_ref))
    chains = [(d, hh) for d in range(2) for hh in range(A_HEADS)]
    s_mats = [state[d * A_HEADS + hh] for d, hh in chains]
    for t in range(DELTA_BLOCK):
        where = []
        for d, hh in chains:
            c = t if d == 0 else DELTA_BLOCK - 1 - t
            where.append((dirs[d], c, slice(c * CHUNK, (c + 1) * CHUNK),
                          slice(hh * A_HEAD_DIM, (hh + 1) * A_HEAD_DIM)))
        ws = [_dot(jnp.concatenate([rf[1][0, 0, r, ln], rf[3][0, 0, r, ln]], axis=0), s.astype(BF16))
              for (rf, c, r, ln), s in zip(where, s_mats)]
        v_new = [(rf[0][0, 0, r, ln] - x[:CHUNK]).astype(BF16) for (rf, c, r, ln), x in zip(where, ws)]
        for (rf, c, r, ln), x, vn in zip(where, ws, v_new):
            rf[6][r, ln] = x[CHUNK:] + _dot(rf[4][0, 0, r, ln][:, :CHUNK], vn)
        s_mats = [s * rf[5][0, 0, c * 8:c * 8 + 1, ln] + _dot_tn(rf[2][0, 0, r, ln], vn)
                  for (rf, c, r, ln), s, vn in zip(where, s_mats, v_new)]
    for (d, hh), s in zip(chains, s_mats):
        state[d * A_HEADS + hh] = s


def _delta_scan(pre, bsz, n_lat, n_ctx):
    u, w, kt, qd, intra, el = pre
    lat_blocks = n_lat // DELTA_ROWS
    n_blocks = lat_blocks + 1
    width = A_WIDTH

    def fwd_idx(b, s):
        return (0, b, s, 0)

    def bwd_idx(b, s):
        return (1, b, jnp.where(s == 0, 0, n_blocks - s), 0)

    def specs(idx):
        big = pl.BlockSpec((1, 1, DELTA_ROWS, width), idx)
        return [big, big, big, big, big, pl.BlockSpec((1, 1, DELTA_BLOCK * 8, width), idx)]

    out_f = pl.BlockSpec((DELTA_ROWS, width), lambda b, s: (b * lat_blocks + jnp.maximum(s - 1, 0), 0))
    out_b = pl.BlockSpec((DELTA_ROWS, width), lambda b, s: (b * lat_blocks + lat_blocks - jnp.maximum(s, 1), 0))
    shape = jax.ShapeDtypeStruct((bsz * n_lat, width), F32)
    return pl.pallas_call(
        _dscan_kernel,
        grid=(bsz, n_blocks),
        in_specs=specs(fwd_idx) + specs(bwd_idx),
        out_specs=[out_f, out_b],
        out_shape=[shape, shape],
        scratch_shapes=[pltpu.VMEM((2 * A_HEADS, A_HEAD_DIM, A_HEAD_DIM), F32)],
        compiler_params=_cparams(("arbitrary", "arbitrary")),
        name="dscan",
    )(u, w, kt, qd, intra, el, u, w, kt, qd, intra, el)


def _delta_kernel(qf_ref, kf_ref, vf_ref, qr_ref, kr_ref, vr_ref, qc_ref, kc_ref, vc_ref,
                  baf_ref, bar_ref, bac_ref, coef_ref, of_ref, ob_ref,
                  u_s, w_s, kt_s, qd_s, in_s, el_s, state):
    step = pl.program_id(1)
    is_ctx = step == 0
    wslot = step % 2
    rslot = 1 - wslot
    heads = [slice(hh * A_HEAD_DIM, (hh + 1) * A_HEAD_DIM) for hh in range(A_HEADS)]
    chunks = [slice(c * CHUNK, (c + 1) * CHUNK) for c in range(DELTA_BLOCK)]

    @pl.when(step == 0)
    def _():
        for scr in (u_s, w_s, kt_s, qd_s, in_s, el_s):
            scr[1] = jnp.zeros(scr.shape[1:], scr.dtype)
        state[...] = jnp.zeros(state.shape, F32)

    chains = [(d, hh) for d in range(2) for hh in range(A_HEADS)]
    s_mats = [state[d * A_HEADS + hh] for d, hh in chains]
    out_refs = (of_ref, ob_ref)

    def scan_substep(t, s_mats):
        where = []
        for d, hh in chains:
            c = t if d == 0 else DELTA_BLOCK - 1 - t
            where.append((d, c, chunks[c], heads[hh]))
        ws = [_dot(jnp.concatenate([w_s[rslot, d, r, ln], qd_s[rslot, d, r, ln]], axis=0), s.astype(BF16))
              for (d, c, r, ln), s in zip(where, s_mats)]
        v_new = [(u_s[rslot, d, r, ln] - x[:CHUNK]).astype(BF16) for (d, c, r, ln), x in zip(where, ws)]
        for (d, c, r, ln), x, vn in zip(where, ws, v_new):
            out_refs[d][r, ln] = x[CHUNK:] + _dot(in_s[rslot, d, r, ln][:, :CHUNK], vn)
        return [s * el_s[rslot, d, c * 8:c * 8 + 1, ln] + _dot_tn(kt_s[rslot, d, r, ln], vn)
                for (d, c, r, ln), s, vn in zip(where, s_mats, v_new)]

    def per_head(main_ref, ctx_ref):
        x = jnp.where(is_ctx, ctx_ref[...], main_ref[...])
        return [x[:, ln] for ln in heads]

    q_h = (per_head(qf_ref, qc_ref), per_head(qr_ref, qc_ref))
    k_h = (per_head(kf_ref, kc_ref), per_head(kr_ref, kc_ref))
    v_h = (per_head(vf_ref, vc_ref), per_head(vr_ref, vc_ref))

    lane = lax.broadcasted_iota(jnp.int32, (1, LANES), 1)
    raw_d = (jnp.where(is_ctx, bac_ref[...], baf_ref[...]), jnp.where(is_ctx, bac_ref[...], bar_ref[...]))
    bg = {}
    for hh, ln in enumerate(heads):
        coef = coef_ref[hh]
        neg_a = -jnp.exp(jnp.where(lane == 2, coef[0:1, :], coef[1:2, :]))
        dtb = jnp.where(lane == 2, coef[2:3, :], coef[3:4, :])
        for d in range(2):
            raw = raw_d[d][:, ln]
            xg = raw + dtb
            softplus = jnp.maximum(xg, 0.0) + jnp.log(1.0 + jnp.exp(-jnp.abs(xg)))
            bg[d, hh] = jnp.where(lane < 2, jax.nn.sigmoid(raw), neg_a * softplus)

    i_idx, j_idx = _tri_masks()
    i_w = lax.broadcasted_iota(jnp.int32, (CHUNK, LANES), 0)
    j_w = lax.broadcasted_iota(jnp.int32, (CHUNK, LANES), 1)
    in_chunk = j_w < CHUNK
    ones8 = jnp.ones((8, CHUNK), F32)

    def front_steps(d):
        incl = (i_idx >= j_idx) if d == 0 else (i_idx <= j_idx)
        strict = (i_idx > j_idx) if d == 0 else (i_idx < j_idx)
        incl_w = jnp.logical_and(in_chunk, (i_w >= j_w) if d == 0 else (i_w <= j_w))
        seen_w = jnp.logical_and(in_chunk, (i_w <= j_w) if d == 0 else (i_w >= j_w))
        insts, l_mats, rhs_all = [], [], []
        for hh, ln in enumerate(heads):
            g_cols = [jnp.broadcast_to(bg[d, hh][r, 2 + d:3 + d], (CHUNK, LANES)) for r in chunks]
            gc_all = _dot_hi(incl.astype(F32), jnp.concatenate(g_cols, axis=1))
            gr_all = _dot_hi(ones8, jnp.concatenate([jnp.where(seen_w, g, 0.0) for g in g_cols], axis=1))
            for c, r in enumerate(chunks):
                q, k, v = q_h[d][hh][r], k_h[d][hh][r], v_h[d][hh][r]
                kbf = k.astype(BF16)
                prod = _dot_nt(jnp.concatenate([kbf, q.astype(BF16)], axis=0),
                               jnp.concatenate([kbf, jnp.zeros_like(kbf)], axis=0))
                beta = bg[d, hh][r, d:d + 1]
                gc_w = gc_all[:, c * LANES:(c + 1) * LANES]
                gr_w = jnp.broadcast_to(gr_all[0:1, c * LANES:(c + 1) * LANES], (CHUNK, LANES))
                decay = jnp.where(incl_w, jnp.exp(jnp.minimum(gc_w - gr_w, 0.0)), 0.0)
                g_last = gc_w[CHUNK - 1:CHUNK, :] if d == 0 else gc_w[0:1, :]
                egc = jnp.exp(gc_w)
                l_mats.append(jnp.where(strict, (prod[:CHUNK] * beta * decay)[:, :CHUNK], 0.0))
                rhs_all.append(jnp.concatenate([v * beta, k * beta * egc], axis=1).astype(BF16))
                kt_s[wslot, d, r, ln] = (k * jnp.exp(g_last - gc_w)).astype(BF16)
                qd_s[wslot, d, r, ln] = (q * egc).astype(BF16)
                in_s[wslot, d, r, ln] = (prod[CHUNK:] * decay).astype(BF16)
                el_s[wslot, d, c * 8:(c + 1) * 8, ln] = jnp.broadcast_to(jnp.exp(g_last), (8, LANES))
                insts.append((r, ln))
                yield None
        yield insts, l_mats, rhs_all

    def scan_steps(s_mats):
        for t in range(DELTA_BLOCK):
            s_mats = scan_substep(t, s_mats)
            yield s_mats
            yield s_mats
            yield s_mats

    front0, front1, s_mats = _drain(front_steps(0), front_steps(1), scan_steps(s_mats))
    l_mats = front0[1] + front1[1]
    t_split = [_split_bf16(t) for t in _unit_tri_inverses(l_mats, i_idx, j_idx)]
    sols = [_dot(t_hi, rhs) + _dot(t_lo, rhs) for (t_hi, t_lo), rhs in zip(t_split, front0[2] + front1[2])]
    dests = [(0, r, ln) for r, ln in front0[0]] + [(1, r, ln) for r, ln in front1[0]]
    for (d, r, ln), sol in zip(dests, sols):
        u_s[wslot, d, r, ln] = sol[:, :A_HEAD_DIM]
        w_s[wslot, d, r, ln] = sol[:, A_HEAD_DIM:].astype(BF16)
    for (d, hh), s in zip(chains, s_mats):
        state[d * A_HEADS + hh] = s


def _delta_mixer(qkv, qkv_c, ba, ba_c, coef, bsz, n_lat, n_ctx):
    assert n_ctx == DELTA_ROWS and n_lat % DELTA_ROWS == 0 and DELTA_BLOCK == 4
    wd = A_WIDTH
    lat_blocks = n_lat // DELTA_ROWS
    n_steps = lat_blocks + 2

    def fwd_blk(b, s):
        return b * lat_blocks + jnp.clip(s - 1, 0, lat_blocks - 1)

    def bwd_blk(b, s):
        return b * lat_blocks + jnp.clip(lat_blocks - s, 0, lat_blocks - 1)

    def lat(blk_fn, kd):
        return pl.BlockSpec((DELTA_ROWS, wd), lambda b, s: (blk_fn(b, s), kd))

    def ctxb(kd):
        return pl.BlockSpec((n_ctx, wd), lambda b, s: (b, kd))

    in_specs = ([lat(fwd_blk, kd) for kd in range(3)] + [lat(bwd_blk, kd) for kd in range(3)]
                + [ctxb(kd) for kd in range(3)]
                + [lat(fwd_blk, 0), lat(bwd_blk, 0), ctxb(0),
                   pl.BlockSpec((A_HEADS, 8, LANES), lambda b, s: (0, 0, 0))])
    out_f = pl.BlockSpec((DELTA_ROWS, wd), lambda b, s: (b * lat_blocks + jnp.maximum(s - 2, 0), 0))
    out_b = pl.BlockSpec((DELTA_ROWS, wd), lambda b, s: (b * lat_blocks + lat_blocks - jnp.maximum(s - 1, 1), 0))
    shape = jax.ShapeDtypeStruct((bsz * n_lat, wd), F32)
    slots = lambda rows, dt: pltpu.VMEM((2, 2, rows, wd), dt)
    return pl.pallas_call(
        _delta_kernel,
        grid=(bsz, n_steps),
        in_specs=in_specs,
        out_specs=[out_f, out_b],
        out_shape=[shape, shape],
        scratch_shapes=[slots(DELTA_ROWS, F32), slots(DELTA_ROWS, BF16), slots(DELTA_ROWS, BF16),
                        slots(DELTA_ROWS, BF16), slots(DELTA_ROWS, BF16), slots(DELTA_BLOCK * 8, F32),
                        pltpu.VMEM((2 * A_HEADS, A_HEAD_DIM, A_HEAD_DIM), F32)],
        compiler_params=_cparams(("arbitrary", "arbitrary")),
        name="delta",
    )(*([qkv] * 6), *([qkv_c] * 3), ba, ba, ba_c, coef)


def _head_rms(x, gain_row, head_dim):
    outs = []
    for s in range(x.shape[1] // LANES):
        xs = x[:, s * LANES:(s + 1) * LANES]
        lane = lax.broadcasted_iota(jnp.int32, xs.shape, 1)
        sq = xs * xs
        scale = jnp.zeros_like(xs)
        for part in range(LANES // head_dim):
            m = jnp.logical_and(lane >= part * head_dim, lane < (part + 1) * head_dim)
            ms = jnp.sum(jnp.where(m, sq, 0.0), axis=-1, keepdims=True) * (1.0 / head_dim)
            scale = jnp.where(m, lax.rsqrt(ms + EPS), scale)
        outs.append(xs * scale)
    y = outs[0] if len(outs) == 1 else jnp.concatenate(outs, axis=1)
    return y * gain_row


def _rope(x, cos, sin_lo, sin_hi):
    width = x.shape[1]
    fwd = pltpu.roll(x, width - ROPE_AXIS_PAIRS, axis=1)
    back = pltpu.roll(x, ROPE_AXIS_PAIRS, axis=1)
    return x * cos + fwd * sin_lo + back * sin_hi


def _kvprep_kernel(k_ref, v_ref, g_ref, cos_ref, slo_ref, shi_ref, ko_ref, vo_ref):
    k = _head_rms(k_ref[0], g_ref[...], B_HEAD_DIM)
    k = _rope(k, cos_ref[...], slo_ref[...], shi_ref[...]).astype(BF16)
    vt = v_ref[0].T.astype(BF16)
    for hh in range(B_KV_HEADS):
        ko_ref[0, hh] = k[:, hh * B_HEAD_DIM:(hh + 1) * B_HEAD_DIM]
        vo_ref[0, hh, :B_HEAD_DIM, :] = vt[hh * B_HEAD_DIM:(hh + 1) * B_HEAD_DIM, :]
        vo_ref[0, hh, B_HEAD_DIM:, :] = jnp.ones((ATTN_VT_ROWS - B_HEAD_DIM, vt.shape[1]), BF16)


def _kvprep(k_all, v_all, gain, cos, slo, shi, ts):
    bsz, s_len, w = k_all.shape
    row = pl.BlockSpec((1, ts, w), lambda b, i: (b, i, 0))
    tab = pl.BlockSpec((ts, w), lambda b, i: (i, 0))
    return pl.pallas_call(
        _kvprep_kernel,
        grid=(bsz, s_len // ts),
        in_specs=[row, row, pl.BlockSpec((1, w), lambda b, i: (0, 0)), tab, tab, tab],
        out_specs=[pl.BlockSpec((1, B_KV_HEADS, ts, B_HEAD_DIM), lambda b, i: (b, 0, i, 0)),
                   pl.BlockSpec((1, B_KV_HEADS, ATTN_VT_ROWS, ts), lambda b, i: (b, 0, 0, i))],
        out_shape=[jax.ShapeDtypeStruct((bsz, B_KV_HEADS, s_len, B_HEAD_DIM), BF16),
                   jax.ShapeDtypeStruct((bsz, B_KV_HEADS, ATTN_VT_ROWS, s_len), BF16)],
        compiler_params=_cparams(("arbitrary", "arbitrary")),
        name="kvprep",
    )(k_all, v_all, gain, cos, slo, shi)


LOG2_E = math.log2(math.e)
ATTN_SCORE_KEYS = 64
ATTN_VALUE_KEYS = 256
ATTN_Q_COLS = 256
ATTN_VT_ROWS = B_HEAD_DIM + 16


def _attn_kernel(q_ref, g_ref, cos_ref, slo_ref, shi_ref, k_ref, vt_ref, o_ref):
    q = _head_rms(q_ref[...], g_ref[...], B_HEAD_DIM)
    q = _rope(q, cos_ref[...], slo_ref[...], shi_ref[...]) * (B_HEAD_DIM ** -0.5 * LOG2_E)
    tq = q.shape[0]
    qt = q.T.astype(BF16)
    n_keys = k_ref.shape[2]
    q_s = [qt[g * B_HEAD_DIM:(g + 1) * B_HEAD_DIM, c:c + ATTN_Q_COLS]
           for g in range(B_GROUP) for c in range(0, tq, ATTN_Q_COLS)]
    m = [jnp.full((1, ATTN_Q_COLS), -1e30, F32) for _ in q_s]
    acc = [jnp.zeros((ATTN_VT_ROWS, ATTN_Q_COLS), F32) for _ in q_s]

    def scores(kb):
        k_blk = k_ref[0, 0, kb:kb + ATTN_VALUE_KEYS, :]
        return [_dot(k_blk, qs) for qs in q_s]

    def value_update(acc, pending):
        alpha, p, kb = pending
        vt = vt_ref[0, 0, :, kb:kb + ATTN_VALUE_KEYS]
        return [a * al + _dot(vt, pp) for a, al, pp in zip(acc, alpha, p)]

    st_next = scores(0)
    pending = None
    for kb in range(0, n_keys, ATTN_VALUE_KEYS):
        st = st_next
        if kb + ATTN_VALUE_KEYS < n_keys:
            st_next = scores(kb + ATTN_VALUE_KEYS)
        m_new = [jnp.maximum(mo, jnp.max(s, axis=0, keepdims=True)) for mo, s in zip(m, st)]
        alpha = [jnp.exp2(mo - mn) for mo, mn in zip(m, m_new)]
        p = [jnp.exp2((s - mn).astype(BF16)) for s, mn in zip(st, m_new)]
        if pending is not None:
            acc = value_update(acc, pending)
        pending = (alpha, p, kb)
        m = m_new
    acc = value_update(acc, pending)
    outs = [a[:B_HEAD_DIM] / a[B_HEAD_DIM:B_HEAD_DIM + 1] for a in acc]
    n_col = tq // ATTN_Q_COLS
    o_ref[...] = jnp.concatenate([jnp.concatenate(outs[g * n_col:(g + 1) * n_col], axis=1)
                                  for g in range(B_GROUP)], axis=0).T


def _attn_kernel_two_pass(q_ref, g_ref, cos_ref, slo_ref, shi_ref, k_ref, vt_ref, o_ref, s_scr):
    q = _head_rms(q_ref[...], g_ref[...], B_HEAD_DIM)
    q = _rope(q, cos_ref[...], slo_ref[...], shi_ref[...]) * (B_HEAD_DIM ** -0.5 * LOG2_E)
    tq = q.shape[0]
    qt = q.T.astype(BF16)
    n_keys = k_ref.shape[2]
    halves = [slice(c, c + ATTN_Q_COLS) for c in range(0, tq, ATTN_Q_COLS)]

    def score_steps(g):
        qg = qt[g * B_HEAD_DIM:(g + 1) * B_HEAD_DIM, :]
        m8 = None
        for kb in range(0, n_keys, ATTN_SCORE_KEYS):
            st = _dot(k_ref[0, 0, kb:kb + ATTN_SCORE_KEYS, :], qg)
            s_scr[g % 2, kb:kb + ATTN_SCORE_KEYS, :] = st
            bm = jnp.max(st.reshape(ATTN_SCORE_KEYS // 8, 8, tq), axis=0)
            m8 = bm if m8 is None else jnp.maximum(m8, bm)
            if (kb // ATTN_SCORE_KEYS) % (ATTN_VALUE_KEYS // ATTN_SCORE_KEYS) == 1:
                yield None
        yield jnp.max(m8, axis=0, keepdims=True)

    def value_steps(g, m):
        acc = [jnp.zeros((ATTN_VT_ROWS, ATTN_Q_COLS), F32) for _ in halves]
        for kb in range(0, n_keys, ATTN_VALUE_KEYS):
            vt = vt_ref[0, 0, :, kb:kb + ATTN_VALUE_KEYS]
            for i, cols in enumerate(halves):
                p = jnp.exp2((s_scr[g % 2, kb:kb + ATTN_VALUE_KEYS, cols] - m[:, cols]).astype(BF16))
                acc[i] = acc[i] + _dot(vt, p)
            yield None
        yield jnp.concatenate([a[:B_HEAD_DIM] / a[B_HEAD_DIM:B_HEAD_DIM + 1] for a in acc], axis=1)

    def drain(*gens):
        last = [None] * len(gens)
        live = list(range(len(gens)))
        while live:
            for i in list(live):
                try:
                    last[i] = next(gens[i])
                except StopIteration:
                    live.remove(i)
        return last

    outs = []
    (m,) = drain(score_steps(0))
    for g in range(B_GROUP):
        if g + 1 < B_GROUP:
            out, m = drain(value_steps(g, m), score_steps(g + 1))
        else:
            (out,) = drain(value_steps(g, m))
        outs.append(out)
    o_ref[...] = jnp.concatenate(outs, axis=0).T


def _attention(qb, gain, cos, slo, shi, k_hm, vt_hm, n_lat, tq):
    n, _ = qb.shape
    bsz, _, s_len, _ = k_hm.shape
    gw = B_GROUP * B_HEAD_DIM
    nq = n_lat // tq
    kv = pl.BlockSpec((1, 1, s_len, B_HEAD_DIM), lambda b, kh, i: (b, kh, 0, 0))
    vts = pl.BlockSpec((1, 1, ATTN_VT_ROWS, s_len), lambda b, kh, i: (b, kh, 0, 0))
    tab = pl.BlockSpec((tq, gw), lambda b, kh, i: (i, 0))
    return pl.pallas_call(
        _attn_kernel,
        grid=(bsz, B_KV_HEADS, nq),
        in_specs=[pl.BlockSpec((tq, gw), lambda b, kh, i: (b * nq + i, kh)),
                  pl.BlockSpec((1, gw), lambda b, kh, i: (0, 0)), tab, tab, tab, kv, vts],
        out_specs=pl.BlockSpec((tq, gw), lambda b, kh, i: (b * nq + i, kh)),
        out_shape=jax.ShapeDtypeStruct((n, B_WIDTH), F32),
        compiler_params=_cparams(("arbitrary", "arbitrary", "arbitrary")),
        name="attn",
    )(qb, gain, cos, slo, shi, k_hm, vt_hm)


def _gated_residual(x, y, mod_ref, gain_ref, gate_row):
    return x + mod_ref[0, gate_row:gate_row + 1, :] * _rms(y, gain_ref[...])


def _outproj_kernel(of_ref, ob_ref, z_ref, on_ref, yb_ref, x_ref, mod_ref, g_ref, wa_ref, wb_ref, o_ref):
    o = of_ref[...] + ob_ref[...]
    z = z_ref[...]
    parts = []
    for hh in range(A_HEADS):
        sl = slice(hh * A_HEAD_DIM, (hh + 1) * A_HEAD_DIM)
        parts.append((_rms(o[:, sl], on_ref[...]) * _silu(z[:, sl])).astype(BF16))
    ya = jnp.concatenate(parts, axis=1)
    y = _dot(ya, wa_ref[...]) + _dot(yb_ref[...].astype(BF16), wb_ref[...])
    o_ref[...] = _gated_residual(x_ref[...], y, mod_ref, g_ref, 2)


def _outproj(o_f, o_b, z, out_norm, yb, x2, mod, gain, wa, wb, rows_per_mod, tm):
    n, d = x2.shape
    bpm = rows_per_mod // tm
    return pl.pallas_call(
        _outproj_kernel,
        grid=(n // tm,),
        in_specs=[pl.BlockSpec((tm, o_f.shape[1]), lambda i: (i, 0)),
                  pl.BlockSpec((tm, o_b.shape[1]), lambda i: (i, 0)),
                  pl.BlockSpec((tm, z.shape[1]), lambda i: (i, 0)),
                  pl.BlockSpec((1, A_HEAD_DIM), lambda i: (0, 0)),
                  pl.BlockSpec((tm, yb.shape[1]), lambda i: (i, 0)),
                  pl.BlockSpec((tm, d), lambda i: (i, 0)),
                  pl.BlockSpec((1, 6, d), lambda i: (i // bpm, 0, 0)),
                  pl.BlockSpec((1, d), lambda i: (0, 0)),
                  _resident(wa.shape),
                  _resident(wb.shape)],
        out_specs=pl.BlockSpec((tm, d), lambda i: (i, 0)),
        out_shape=jax.ShapeDtypeStruct((n, d), F32),
        compiler_params=_cparams(("arbitrary",)),
        name="outproj",
    )(o_f, o_b, z, out_norm, yb, x2, mod, gain, wa, wb)


def _ffn_kernel(x_ref, mod_ref, gpre_ref, gpost_ref, wi_ref, wo_ref, o_ref, *, hidden, hc):
    x = x_ref[...]
    a = _modulated(x, mod_ref, gpre_ref, 3).astype(BF16)
    acc = jnp.zeros(x.shape, F32)
    for c in range(hidden // hc):
        gate = _dot(a, wi_ref[:, c * hc:(c + 1) * hc])
        up = _dot(a, wi_ref[:, hidden + c * hc:hidden + (c + 1) * hc])
        acc = acc + _dot((_silu(gate) * up).astype(BF16), wo_ref[c * hc:(c + 1) * hc, :])
    o_ref[...] = _gated_residual(x, acc, mod_ref, gpost_ref, 5)


def _ffn(x2, mod, gpre, gpost, wi, wo, rows_per_mod, tm):
    n, d = x2.shape
    hidden = wo.shape[0]
    bpm = rows_per_mod // tm
    return pl.pallas_call(
        functools.partial(_ffn_kernel, hidden=hidden, hc=256),
        grid=(n // tm,),
        in_specs=[pl.BlockSpec((tm, d), lambda i: (i, 0)),
                  pl.BlockSpec((1, 6, d), lambda i: (i // bpm, 0, 0)),
                  pl.BlockSpec((1, d), lambda i: (0, 0)),
                  pl.BlockSpec((1, d), lambda i: (0, 0)),
                  _resident(wi.shape),
                  _resident(wo.shape)],
        out_specs=pl.BlockSpec((tm, d), lambda i: (i, 0)),
        out_shape=jax.ShapeDtypeStruct((n, d), F32),
        compiler_params=_cparams(("arbitrary",)),
        name="ffn",
    )(x2, mod, gpre, gpost, wi, wo)


def _confin_kernel(x_ref, mod_ref, g_ref, w_ref, b_ref, o_ref, *, width):
    a = _modulated(x_ref[...], mod_ref, g_ref, 0).astype(BF16)
    val = _dot(a, w_ref[:, :width]) + b_ref[:, :width]
    gate = _dot(a, w_ref[:, width:]) + b_ref[:, width:]
    o_ref[...] = val * jax.nn.sigmoid(gate)


def _confin(x2, mod, gain, w, b, rows_per_mod, tm):
    n, d = x2.shape
    width = w.shape[1] // 2
    bpm = rows_per_mod // tm
    return pl.pallas_call(
        functools.partial(_confin_kernel, width=width),
        grid=(n // tm,),
        in_specs=[pl.BlockSpec((tm, d), lambda i: (i, 0)),
                  pl.BlockSpec((1, 6, d), lambda i: (i // bpm, 0, 0)),
                  pl.BlockSpec((1, d), lambda i: (0, 0)),
                  _resident(w.shape),
                  pl.BlockSpec((1, 2 * width), lambda i: (0, 0))],
        out_specs=pl.BlockSpec((tm, width), lambda i: (i, 0)),
        out_shape=jax.ShapeDtypeStruct((n, width), F32),
        compiler_params=_cparams(("arbitrary",)),
        name="confin",
    )(x2, mod, gain, w, b)


def _confout_kernel(u_ref, up_ref, un_ref, x_ref, mod_ref, g_ref, dww_ref, dwb_ref, lng_ref, lnb_ref,
                    w_ref, b_ref, o_ref, ext, conv, shifted, *, tm, tiles_per_seq):
    i = pl.program_id(0)
    pos = i % tiles_per_seq
    width = u_ref.shape[1]
    halo = CONF_HALO
    ext[pl.ds(halo, tm), :] = u_ref[...]
    ext[0:halo, :] = jnp.where(pos == 0, 0.0, up_ref[...])
    ext[pl.ds(halo + tm, halo), :] = jnp.where(pos == tiles_per_seq - 1, 0.0, un_ref[...])

    rb = 64
    pad = CONF_KERNEL // 2
    sub = 8
    copy_rows = tm + 2 * halo - sub

    def col_body(c, carry):
        cs = pl.ds(pl.multiple_of(c * LANES, LANES), LANES)
        w = dww_ref[:, cs]
        for s in range(sub):
            shifted[s, :, :] = ext[pl.ds(s, copy_rows), cs]
        for r in range(tm // rb):
            acc = jnp.zeros((rb, LANES), F32)
            for tap in range(CONF_KERNEL):
                off = halo - pad + tap
                acc = acc + shifted[off % sub, pl.ds(r * rb + off - off % sub, rb), :] * w[tap:tap + 1, :]
            conv[pl.ds(r * rb, rb), cs] = acc
        return carry

    lax.fori_loop(0, width // LANES, col_body, 0)

    y = conv[...] + dwb_ref[...]
    mu = jnp.mean(y, axis=-1, keepdims=True)
    yc = y - mu
    var = jnp.mean(yc * yc, axis=-1, keepdims=True)
    y = _silu(yc * lax.rsqrt(var + EPS) * lng_ref[...] + lnb_ref[...])
    out = _dot(y.astype(BF16), w_ref[...]) + b_ref[...]
    o_ref[...] = _gated_residual(x_ref[...], out, mod_ref, g_ref, 2)


def _confout(u, x2, mod, gain, dww, dwb, lng, lnb, w, b, rows_per_mod, tm):
    n, d = x2.shape
    width = u.shape[1]
    tiles_per_seq = rows_per_mod // tm
    hb = tm // CONF_HALO
    n_halo_blocks = n // CONF_HALO
    vec = lambda wd: pl.BlockSpec((1, wd), lambda i: (0, 0))
    return pl.pallas_call(
        functools.partial(_confout_kernel, tm=tm, tiles_per_seq=tiles_per_seq),
        grid=(n // tm,),
        in_specs=[pl.BlockSpec((tm, width), lambda i: (i, 0)),
                  pl.BlockSpec((CONF_HALO, width), lambda i: (jnp.maximum(i * hb - 1, 0), 0)),
                  pl.BlockSpec((CONF_HALO, width), lambda i: (jnp.minimum((i + 1) * hb, n_halo_blocks - 1), 0)),
                  pl.BlockSpec((tm, d), lambda i: (i, 0)),
                  pl.BlockSpec((1, 6, d), lambda i: (i // tiles_per_seq, 0, 0)),
                  vec(d),
                  pl.BlockSpec(dww.shape, lambda i: (0, 0)),
                  vec(width), vec(width), vec(width),
                  _resident(w.shape),
                  vec(d)],
        out_specs=pl.BlockSpec((tm, d), lambda i: (i, 0)),
        out_shape=jax.ShapeDtypeStruct((n, d), F32),
        scratch_shapes=[pltpu.VMEM((tm + 2 * CONF_HALO, width), F32), pltpu.VMEM((tm, width), F32),
                        pltpu.VMEM((8, tm + 2 * CONF_HALO - 8, LANES), F32)],
        compiler_params=_cparams(("arbitrary",)),
        name="confout",
    )(u, u, u, x2, mod, gain, dww, dwb, lng, lnb, w, b)


def _rope_tables(n_tokens, reps):
    rows = n_tokens // GRID_W
    row = jnp.broadcast_to(jnp.arange(rows, dtype=F32)[:, None], (rows, GRID_W)).reshape(n_tokens)
    col = jnp.broadcast_to(jnp.arange(GRID_W, dtype=F32)[None, :], (rows, GRID_W)).reshape(n_tokens)
    inv_freq = ROPE_THETA ** (-jnp.arange(ROPE_AXIS_PAIRS, dtype=F32) / ROPE_AXIS_PAIRS)
    ang_r = row[:, None] * inv_freq
    ang_c = col[:, None] * inv_freq
    ang = jnp.concatenate([ang_r, ang_r, ang_c, ang_c], axis=-1)
    cos, sin = jnp.cos(ang), jnp.sin(ang)
    first_half = (jnp.arange(B_HEAD_DIM) % (2 * ROPE_AXIS_PAIRS)) < ROPE_AXIS_PAIRS
    sin_lo = jnp.where(first_half, -sin, 0.0)
    sin_hi = jnp.where(first_half, 0.0, sin)
    tile = lambda t: jnp.tile(t, (1, reps))
    return tile(cos), tile(sin_lo), tile(sin_hi)


def _hybrid_in_weight(w_in):
    off_z = 3 * A_WIDTH
    off_ba = off_z + A_WIDTH
    off_q = off_ba + 4 * A_HEADS
    off_k = off_q + B_WIDTH
    off_v = off_k + B_KV_WIDTH
    d = w_in.shape[0]
    ba = w_in[:, off_ba:off_q].reshape(d, 2, 2, A_HEADS)
    ba = ba.transpose(0, 3, 1, 2).reshape(d, A_HEADS, 4)
    ba = jnp.pad(ba, ((0, 0), (0, 0), (0, LANES - 4))).reshape(d, A_HEADS * LANES)
    w = jnp.concatenate([w_in[:, :off_z], w_in[:, off_z:off_ba], w_in[:, off_q:off_k], w_in[:, off_k:off_v],
                         w_in[:, off_v:], ba], axis=1)
    return w.astype(BF16)


def _row(v):
    return v.reshape(1, -1)


def kernel(x, c, ctx, c_ctx, w_mod, b_mod, g_mix_pre, g_mix_post, g_ffn_pre, g_ffn_post, w_ffn_in, w_ffn_out,
           hyb_w_in, hyb_conv_w, hyb_a_log, hyb_dt_bias, hyb_out_norm, hyb_q_norm, hyb_k_norm, hyb_w_out,
           conf_w_in, conf_b_in, conf_dw_w, conf_dw_b, conf_ln_g, conf_ln_b, conf_w_out, conf_b_out):
    bsz, n_lat, d = x.shape
    n_ctx = ctx.shape[1]
    depth = w_mod.shape[0]
    n = bsz * n_lat
    tm = 1024

    cond =jnp.zeros((8, d), F32).at[:bsz].set(c).at[bsz].set(c_ctx)
    mods = _ada_terms(cond, w_mod, b_mod).reshape(depth, 8, 6, d)

    h = x.reshape(n, d)
    hc = ctx.reshape(bsz * n_ctx, d)
    for layer in range(depth):
        idx = layer // 2
        mod = mods[layer, :bsz]
        mod_ctx = mods[layer, bsz:bsz + 1]
        if layer % 2 == 0:
            w_in = _hybrid_in_weight(hyb_w_in[idx])
            gpre = _row(g_mix_pre[layer])
            k_gain = _row(jnp.tile(hyb_k_norm[idx], B_KV_HEADS))
            cos_k, slo_k, shi_k = _rope_tables(n_lat, B_KV_HEADS)
            no_pos = (jnp.ones((n_ctx, B_KV_WIDTH), F32), jnp.zeros((n_ctx, B_KV_WIDTH), F32),
                      jnp.zeros((n_ctx, B_KV_WIDTH), F32))
            qkv, z, qb, ba, k_lat, vt_lat = _hyb_inproj(h, mod, gpre, w_in, hyb_conv_w[idx], k_gain,
                                                        cos_k, slo_k, shi_k, bsz, n_lat, tm)
            qkv_c, _, _, ba_c, k_ctx, vt_ctx = _hyb_inproj(hc, mod_ctx, gpre, w_in, hyb_conv_w[idx], k_gain,
                                                           *no_pos, bsz, n_ctx, n_ctx)

            coef = jnp.concatenate([hyb_a_log[idx], hyb_dt_bias[idx]], axis=0)
            coef = jnp.pad(coef.T[:, :, None], ((0, 0), (0, 4), (0, 0)))
            coef = jnp.broadcast_to(coef, (A_HEADS, 8, LANES))
            o_f, o_b = _delta_mixer(qkv, qkv_c, ba, ba_c, coef, bsz, n_lat, n_ctx)

            k_hm = jnp.concatenate([k_lat, k_ctx], axis=2)
            vt_hm = jnp.concatenate([vt_lat, vt_ctx], axis=3)
            cos_q, slo_q, shi_q = _rope_tables(n_lat, B_GROUP)
            yb = _attention(qb, _row(jnp.tile(hyb_q_norm[idx], B_GROUP)), cos_q, slo_q, shi_q,
                            k_hm, vt_hm, n_lat, 512)

            w_out = hyb_w_out[idx].astype(BF16)
            h = _outproj(o_f, o_b, z, _row(hyb_out_norm[idx]), yb, h, mod, _row(g_mix_post[layer]),
                         w_out[:A_WIDTH], w_out[A_WIDTH:], n_lat, tm)
        else:
            u = _confin(h, mod, _row(g_mix_pre[layer]), conf_w_in[idx].astype(BF16), _row(conf_b_in[idx]),
                        n_lat, tm)
            h = _confout(u, h, mod, _row(g_mix_post[layer]), conf_dw_w[idx], _row(conf_dw_b[idx]),
                         _row(conf_ln_g[idx]), _row(conf_ln_b[idx]), conf_w_out[idx].astype(BF16),
                         _row(conf_b_out[idx]), n_lat, 256)
        h = _ffn(h, mod, _row(g_ffn_pre[layer]), _row(g_ffn_post[layer]), w_ffn_in[layer].astype(BF16),
                 w_ffn_out[layer].astype(BF16), n_lat, 1024)
        assert not any(j % 2 == 0 for j in range(layer + 1, depth)), "context advance not implemented"
    return h.reshape(bsz, n_lat, d)
```

```python
import functools
import math

import jax
import jax.numpy as jnp
import numpy as np
from jax import lax
from jax.experimental import pallas as pl
from jax.experimental.pallas import tpu as pltpu

F32 = jnp.float32
BF16 = jnp.bfloat16
HIGHEST = lax.Precision.HIGHEST

EPS = 1e-6
GRID_W = 64
ROPE_THETA = 10000.0
A_HEADS = 4
A_HEAD_DIM = 128
A_WIDTH = A_HEADS * A_HEAD_DIM
SHORT_CONV_W = 5
CHUNK = 64
B_Q_HEADS = 8
B_KV_HEADS = 2
B_HEAD_DIM = 64
B_GROUP = B_Q_HEADS // B_KV_HEADS
B_WIDTH = B_Q_HEADS * B_HEAD_DIM
B_KV_WIDTH = B_KV_HEADS * B_HEAD_DIM
ROPE_AXIS_PAIRS = B_HEAD_DIM // 4
CONF_KERNEL = 31
CONF_HALO = 16
LANES = 128
VMEM_LIMIT = 56 * 1024 * 1024


def _cparams(sem):
    return pltpu.CompilerParams(dimension_semantics=sem, vmem_limit_bytes=VMEM_LIMIT)


def _resident(shape):
    return pl.BlockSpec(shape, lambda i: (0, 0), pipeline_mode=pl.Buffered(1))


def _silu(x):
    return x * jax.nn.sigmoid(x)


def _dot(a, b):
    return jnp.dot(a, b, preferred_element_type=F32)


def _dot_hi(a, b):
    return jnp.dot(a, b, preferred_element_type=F32, precision=HIGHEST)


def _dot_nt(a, b):
    return lax.dot_general(a, b, (((1,), (1,)), ((), ())), preferred_element_type=F32)


def _dot_tn(a, b):
    return lax.dot_general(a, b, (((0,), (0,)), ((), ())), preferred_element_type=F32)


def _rms(x, gain):
    return x * lax.rsqrt(jnp.mean(x * x, axis=-1, keepdims=True) + EPS) * gain


def _ada_kernel(c_ref, w_ref, b_ref, o_ref):
    o_ref[0] = _dot_hi(_silu(c_ref[...]), w_ref[0]) + b_ref[0]


def _ada_terms(cond, w_mod, b_mod):
    depth, d, n6 = w_mod.shape
    tn = 1536
    return pl.pallas_call(
        _ada_kernel,
        grid=(depth, n6 // tn),
        in_specs=[pl.BlockSpec((8, d), lambda l, j: (0, 0)),
                  pl.BlockSpec((1, d, tn), lambda l, j: (l, 0, j)),
                  pl.BlockSpec((1, 1, tn), lambda l, j: (l, 0, j))],
        out_specs=pl.BlockSpec((1, 8, tn), lambda l, j: (l, 0, j)),
        out_shape=jax.ShapeDtypeStruct((depth, 8, n6), F32),
        compiler_params=_cparams(("arbitrary", "arbitrary")),
        name="ada",
    )(cond, w_mod, b_mod.reshape(depth, 1, n6))


def _modulated(x, mod_ref, gain_ref, shift_row):
    y = _rms(x, gain_ref[...])
    return y * (1.0 + mod_ref[0, shift_row + 1:shift_row + 2, :]) + mod_ref[0, shift_row:shift_row + 1, :]


PROJ_HALO = 16


def _hyb_inproj_kernel(x_ref, xp_ref, xn_ref, mod_ref, g_ref, w_ref, cw_ref, kg_ref, cos_ref, slo_ref, shi_ref,
                       qkv_ref, z_ref, qb_ref, ba_ref, ko_ref, vo_ref, *, tiles_per_seq):
    pos = pl.program_id(0) % tiles_per_seq
    tm = x_ref.shape[0]
    halo = PROJ_HALO
    qkv_w = 3 * A_WIDTH
    a_prev = jnp.where(pos > 0, _modulated(xp_ref[...], mod_ref, g_ref, 0), 0.0).astype(BF16)
    a_next = jnp.where(pos < tiles_per_seq - 1, _modulated(xn_ref[...], mod_ref, g_ref, 0), 0.0).astype(BF16)
    a = _modulated(x_ref[...], mod_ref, g_ref, 0).astype(BF16)
    a_ext = jnp.concatenate([a_prev, a, a_next], axis=0)

    off_z = qkv_w
    off_q = off_z + A_WIDTH
    off_k = off_q + B_WIDTH
    off_v = off_k + B_KV_WIDTH
    off_ba = off_v + B_KV_WIDTH

    def gate_proj():
        z_ref[...] = _dot(a, w_ref[:, off_z:off_q])

    def query_proj():
        qb_ref[...] = _dot(a, w_ref[:, off_q:off_k])

    def key_value_proj():
        kb = _dot(a, w_ref[:, off_k:off_v])
        vb = _dot(a, w_ref[:, off_v:off_ba])
        k = _rope(_head_rms(kb, kg_ref[...], B_HEAD_DIM), cos_ref[...], slo_ref[...], shi_ref[...]).astype(BF16)
        vt = vb.T.astype(BF16)
        for hh in range(B_KV_HEADS):
            ko_ref[0, hh] = k[:, hh * B_HEAD_DIM:(hh + 1) * B_HEAD_DIM]
            vo_ref[0, hh, :B_HEAD_DIM, :] = vt[hh * B_HEAD_DIM:(hh + 1) * B_HEAD_DIM, :]
            vo_ref[0, hh, B_HEAD_DIM:, :] = jnp.ones((ATTN_VT_ROWS - B_HEAD_DIM, tm), BF16)

    def logit_proj():
        ba_ref[...] = _dot(a, w_ref[:, off_ba:off_ba + A_HEADS * LANES])

    others = [gate_proj, query_proj, key_value_proj, logit_proj]
    ext_rows = tm + 2 * halo
    pair = 2 * A_HEAD_DIM
    n_pairs = qkv_w // pair
    ext_next = _dot(a_ext, w_ref[:, :pair])
    for cb in range(n_pairs):
        cols = slice(cb * pair, (cb + 1) * pair)
        ext = ext_next
        if cb + 1 < n_pairs:
            ext_next = _dot(a_ext, w_ref[:, (cb + 1) * pair:(cb + 2) * pair])
        if cb < len(others):
            others[cb]()
        w = cw_ref[:, cols]
        acc = jnp.zeros((tm, pair), F32)
        for tap in range(SHORT_CONV_W):
            shift = (SHORT_CONV_W // 2 - tap) % ext_rows
            rolled = pltpu.roll(ext, shift, axis=0) if shift else ext
            acc = acc + rolled[halo:halo + tm, :] * w[tap:tap + 1, :]
        y = _silu(acc)
        for part in range(2):
            head = 2 * cb + part
            yh = y[:, part * A_HEAD_DIM:(part + 1) * A_HEAD_DIM]
            if head < 2 * A_HEADS:
                yh = yh * lax.rsqrt(jnp.sum(yh * yh, axis=-1, keepdims=True) + EPS)
            if head < A_HEADS:
                yh = yh * (A_HEAD_DIM ** -0.5)
            qkv_ref[:, head * A_HEAD_DIM:(head + 1) * A_HEAD_DIM] = yh


def _hyb_inproj(x2, mod, gain, w, conv_w, k_gain, cos, slo, shi, bsz, rows_per_seq, tm):
    n, d = x2.shape
    tiles_per_seq = rows_per_seq // tm
    hb = tm // PROJ_HALO
    n_halo = n // PROJ_HALO
    f32_out = lambda wd: (pl.BlockSpec((tm, wd), lambda i: (i, 0)), jax.ShapeDtypeStruct((n, wd), F32))
    outs = [f32_out(3 * A_WIDTH), f32_out(A_WIDTH), f32_out(B_WIDTH), f32_out(A_HEADS * LANES),
            (pl.BlockSpec((1, B_KV_HEADS, tm, B_HEAD_DIM), lambda i: (i // tiles_per_seq, 0, i % tiles_per_seq, 0)),
             jax.ShapeDtypeStruct((bsz, B_KV_HEADS, rows_per_seq, B_HEAD_DIM), BF16)),
            (pl.BlockSpec((1, B_KV_HEADS, ATTN_VT_ROWS, tm), lambda i: (i // tiles_per_seq, 0, 0, i % tiles_per_seq)),
             jax.ShapeDtypeStruct((bsz, B_KV_HEADS, ATTN_VT_ROWS, rows_per_seq), BF16))]
    tab = pl.BlockSpec((tm, B_KV_WIDTH), lambda i: (i % tiles_per_seq, 0))
    return pl.pallas_call(
        functools.partial(_hyb_inproj_kernel, tiles_per_seq=tiles_per_seq),
        grid=(n // tm,),
        in_specs=[pl.BlockSpec((tm, d), lambda i: (i, 0)),
                  pl.BlockSpec((PROJ_HALO, d), lambda i: (jnp.maximum(i * hb - 1, 0), 0)),
                  pl.BlockSpec((PROJ_HALO, d), lambda i: (jnp.minimum((i + 1) * hb, n_halo - 1), 0)),
                  pl.BlockSpec((1, 6, d), lambda i: (i // tiles_per_seq if mod.shape[0] > 1 else 0, 0, 0)),
                  pl.BlockSpec((1, d), lambda i: (0, 0)),
                  _resident(w.shape),
                  _resident(conv_w.shape),
                  pl.BlockSpec((1, B_KV_WIDTH), lambda i: (0, 0)), tab, tab, tab],
        out_specs=[o[0] for o in outs],
        out_shape=[o[1] for o in outs],
        compiler_params=_cparams(("arbitrary",)),
        name="inproj",
    )(x2, x2, x2, mod, gain, w, conv_w, k_gain, cos, slo, shi)


def _tri_masks():
    i = lax.broadcasted_iota(jnp.int32, (CHUNK, CHUNK), 0)
    j = lax.broadcasted_iota(jnp.int32, (CHUNK, CHUNK), 1)
    return i, j


def _unit_tri_inverse(l_mat, i, j):
    eye = (i == j).astype(F32)
    same16 = jnp.right_shift(i, 4) == jnp.right_shift(j, 4)
    same32 = jnp.right_shift(i, 5) == jnp.right_shift(j, 5)
    d1 = jnp.where(same16, l_mat, 0.0)
    c1 = jnp.where(jnp.logical_and(same32, jnp.logical_not(same16)), l_mat, 0.0)
    c2 = jnp.where(same32, 0.0, l_mat)
    p = eye - d1
    d2 = _dot_hi(d1, d1)
    p = p + _dot_hi(p, d2)
    d4 = _dot_hi(d2, d2)
    p = p + _dot_hi(p, d4)
    d8 = _dot_hi(d4, d4)
    p = p + _dot_hi(p, d8)
    t1 = p - _dot_hi(p, _dot_hi(c1, p))
    return t1 - _dot_hi(t1, _dot_hi(c2, t1))


def _delta_kernel(q_ref, k_ref, v_ref, qc_ref, kc_ref, vc_ref, wq_ref, wk_ref, wv_ref,
                  ba_ref, bac_ref, coef_ref, z_ref, on_ref, o_ref,
                  xp, qs, ks, vs, bs, of_s, ob_s, *, n_lat, n_ctx):
    n_all = n_ctx + n_lat
    nch_ctx = n_ctx // CHUNK
    nch_all = n_all // CHUNK
    rb = 256

    def conv_into(src_ref, w_ref, dst, dst_off, n_rows, kind):
        xp[0:8, :] = jnp.zeros((8, A_HEAD_DIM), F32)
        xp[pl.ds(8, n_rows), :] = src_ref[...]
        xp[pl.ds(8 + n_rows, 8), :] = jnp.zeros((8, A_HEAD_DIM), F32)
        w = w_ref[...]

        def body(t, carry):
            r0 = pl.multiple_of(t * rb, rb)
            blk = xp[pl.ds(r0, rb + 16), :]
            acc = jnp.zeros((rb, A_HEAD_DIM), F32)
            for tap in range(SHORT_CONV_W):
                shift = (SHORT_CONV_W // 2 - tap) % (rb + 16)
                rolled = pltpu.roll(blk, shift, axis=0) if shift else blk
                acc = acc + rolled[8:8 + rb, :] * w[tap:tap + 1, :]
            y = _silu(acc)
            if kind < 2:
                y = y * lax.rsqrt(jnp.sum(y * y, axis=-1, keepdims=True) + EPS)
            if kind == 0:
                y = y * (A_HEAD_DIM ** -0.5)
            dst[pl.ds(pl.multiple_of(dst_off + r0, rb), rb), :] = y
            return carry

        lax.fori_loop(0, n_rows // rb, body, 0)

    for kind, (lat_ref, ctx_ref, w_ref, dst) in enumerate(
            ((q_ref, qc_ref, wq_ref, qs), (k_ref, kc_ref, wk_ref, ks), (v_ref, vc_ref, wv_ref, vs))):
        conv_into(ctx_ref, w_ref, dst, 0, n_ctx, kind)
        conv_into(lat_ref, w_ref, dst, n_ctx, n_lat, kind)

    lane = lax.broadcasted_iota(jnp.int32, (1, LANES), 1)
    coef = coef_ref[0]
    neg_a = -jnp.exp(jnp.where(lane == 2, coef[0:1, :], coef[1:2, :]))
    dtb = jnp.where(lane == 2, coef[2:3, :], coef[3:4, :])

    def beta_g(raw):
        xg = raw + dtb
        softplus = jnp.maximum(xg, 0.0) + jnp.log(1.0 + jnp.exp(-jnp.abs(xg)))
        return jnp.where(lane < 2, jax.nn.sigmoid(raw), neg_a * softplus)

    bs[pl.ds(0, n_ctx), :] = beta_g(bac_ref[...])
    bs[pl.ds(n_ctx, n_lat), :] = beta_g(ba_ref[...])

    i_idx, j_idx = _tri_masks()
    ones8 = jnp.ones((8, CHUNK), F32)

    def precompute(chunk, d):
        r = pl.multiple_of(chunk * CHUNK, CHUNK)
        q = qs[pl.ds(r, CHUNK), :]
        k = ks[pl.ds(r, CHUNK), :]
        v = vs[pl.ds(r, CHUNK), :]
        bb = bs[pl.ds(r, CHUNK), :]
        beta = bb[:, d:d + 1]
        g = bb[:, 2 + d:3 + d]
        incl = (i_idx >= j_idx) if d == 0 else (i_idx <= j_idx)
        strict = (i_idx > j_idx) if d == 0 else (i_idx < j_idx)
        g_b = jnp.broadcast_to(g, (CHUNK, CHUNK))
        gc_mat = _dot_hi(incl.astype(F32), g_b)
        seen = (i_idx <= j_idx) if d == 0 else (i_idx >= j_idx)
        gr8 = _dot_hi(ones8, jnp.where(seen, g_b, 0.0))
        gr_mat = jnp.broadcast_to(gr8[0:1, :], (CHUNK, CHUNK))
        decay = jnp.where(incl, jnp.exp(jnp.minimum(gc_mat - gr_mat, 0.0)), 0.0)
        gc = gc_mat[:, 0:1]
        g_last = gc[CHUNK - 1:CHUNK, :] if d == 0 else gc[0:1, :]
        egc = jnp.exp(gc)
        kb = k * beta
        kbf = k.astype(BF16)
        l_mat = jnp.where(strict, _dot_nt(kb.astype(BF16), kbf) * decay, 0.0)
        t_inv = _unit_tri_inverse(l_mat, i_idx, j_idx)
        rhs = jnp.concatenate([v * beta, kb * egc], axis=1)
        sol = _dot_hi(t_inv, rhs)
        u = sol[:, :A_HEAD_DIM]
        w = sol[:, A_HEAD_DIM:]
        k_tail = k * jnp.exp(g_last - gc)
        q_dec = q * egc
        intra = _dot_nt(q.astype(BF16), kbf) * decay
        return u, w, k_tail, q_dec, intra, jnp.exp(g_last)

    def scan_step(state, pre):
        u, w, k_tail, q_dec, intra, e_last = pre
        ws = _dot(jnp.concatenate([w, q_dec], axis=0).astype(BF16), state.astype(BF16))
        v_new = u - ws[:CHUNK]
        v_new_b = v_new.astype(BF16)
        o = ws[CHUNK:] + _dot(intra.astype(BF16), v_new_b)
        new_state = state * e_last + _dot_tn(k_tail.astype(BF16), v_new_b)
        return new_state, o

    def bwd_chunk(step):
        return jnp.where(step < nch_ctx, nch_ctx - 1 - step, nch_all + nch_ctx - 1 - step)

    def body(step, carry):
        s_f, s_b = carry
        cf = step
        cb = bwd_chunk(step)
        s_f, o_f = scan_step(s_f, precompute(cf, 0))
        s_b, o_b = scan_step(s_b, precompute(cb, 1))

        @pl.when(step >= nch_ctx)
        def _():
            of_s[pl.ds(pl.multiple_of((cf - nch_ctx) * CHUNK, CHUNK), CHUNK), :] = o_f
            ob_s[pl.ds(pl.multiple_of((cb - nch_ctx) * CHUNK, CHUNK), CHUNK), :] = o_b

        return s_f, s_b

    zero_state = jnp.zeros((A_HEAD_DIM, A_HEAD_DIM), F32)
    lax.fori_loop(0, nch_all, body, (zero_state, zero_state))

    o = of_s[...] + ob_s[...]
    o_ref[...] = _rms(o, on_ref[...]) * _silu(z_ref[...])


def _delta_mixer(qkv, qkv_c, conv_w, ba, ba_c, coef, z, out_norm, bsz, n_lat, n_ctx):
    h = A_HEADS
    hd = A_HEAD_DIM
    n_all = n_lat + n_ctx

    def col(kind):
        return lambda b, hh: (b, kind * h + hh)

    in_specs = (
        [pl.BlockSpec((n_lat, hd), col(kd)) for kd in range(3)]
        + [pl.BlockSpec((n_ctx, hd), col(kd)) for kd in range(3)]
        + [pl.BlockSpec((SHORT_CONV_W, hd), (lambda b, hh, kd=kd: (0, kd * h + hh))) for kd in range(3)]
        + [pl.BlockSpec((n_lat, LANES), lambda b, hh: (b, hh)),
           pl.BlockSpec((n_ctx, LANES), lambda b, hh: (b, hh)),
           pl.BlockSpec((1, 8, LANES), lambda b, hh: (hh, 0, 0)),
           pl.BlockSpec((n_lat, hd), lambda b, hh: (b, hh)),
           pl.BlockSpec((1, hd), lambda b, hh: (0, 0))])
    return pl.pallas_call(
        functools.partial(_delta_kernel, n_lat=n_lat, n_ctx=n_ctx),
        grid=(bsz, h),
        in_specs=in_specs,
        out_specs=pl.BlockSpec((n_lat, hd), lambda b, hh: (b, hh)),
        out_shape=jax.ShapeDtypeStruct((bsz * n_lat, h * hd), F32),
        scratch_shapes=[pltpu.VMEM((n_lat + 16, hd), F32),
                        pltpu.VMEM((n_all, hd), F32), pltpu.VMEM((n_all, hd), F32), pltpu.VMEM((n_all, hd), F32),
                        pltpu.VMEM((n_all, LANES), F32),
                        pltpu.VMEM((n_lat, hd), F32), pltpu.VMEM((n_lat, hd), F32)],
        compiler_params=_cparams(("arbitrary", "arbitrary")),
        name="delta",
    )(qkv, qkv, qkv, qkv_c, qkv_c, qkv_c, conv_w, conv_w, conv_w, ba, ba_c, coef, z, out_norm)


DELTA_BLOCK = 4
DELTA_ROWS = DELTA_BLOCK * CHUNK
CONV_HALO = 8


def _split_bf16(a):
    hi = a.astype(BF16)
    return hi, (a - hi.astype(F32)).astype(BF16)


def _drain(*gens):
    last = [None] * len(gens)
    live = list(range(len(gens)))
    while live:
        for g in list(live):
            try:
                last[g] = next(gens[g])
            except StopIteration:
                live.remove(g)
    return last


def _unit_tri_inverse_steps(l_mats, i, j):
    eye = (i == j).astype(F32)
    same16 = jnp.right_shift(i, 4) == jnp.right_shift(j, 4)
    same32 = jnp.right_shift(i, 5) == jnp.right_shift(j, 5)
    off32 = jnp.logical_and(same32, jnp.logical_not(same16))
    b = lambda a: a.astype(BF16)
    each = lambda f, *ls: [f(*xs) for xs in zip(*ls)]
    d1 = each(lambda l: b(jnp.where(same16, l, 0.0)), l_mats)
    p = each(lambda d: eye - d.astype(F32), d1)
    dk = d1
    for _ in range(3):
        dk = each(lambda d: b(_dot(d, d)), dk)
        yield None
        p = each(lambda pp, d: pp + _dot(b(pp), d), p, dk)
        yield None
    for sel in (off32, jnp.logical_not(same32)):
        cm = each(lambda l: b(jnp.where(sel, l, 0.0)), l_mats)
        pb = each(b, p)
        inner = each(lambda c, q: b(_dot(c, q)), cm, pb)
        yield None
        p = each(lambda pp, q, m: pp - _dot(q, m), p, pb, inner)
        yield None
    l_split = each(_split_bf16, l_mats)
    t_split = each(_split_bf16, p)
    resid = each(lambda t0, ls, ts: eye - t0 - (_dot(ls[0], ts[0]) + _dot(ls[0], ts[1]) + _dot(ls[1], ts[0])),
                 p, l_split, t_split)
    yield None
    yield each(lambda t0, ts, r: t0 + _dot(ts[0], b(r)), p, t_split, resid)


def _unit_tri_inverses(l_mats, i, j):
    return _drain(_unit_tri_inverse_steps(l_mats, i, j))[0]


DELTA_HEADS = 4


def _dprep_kernel(q_ref, k_ref, v_ref, qc_ref, kc_ref, vc_ref, ba_ref, bac_ref, coef_ref,
                  u_ref, w_ref, kt_ref, qd_ref, in_ref, el_ref):
    blk = pl.program_id(2)
    is_ctx = blk == 0
    heads = [slice(hh * A_HEAD_DIM, (hh + 1) * A_HEAD_DIM) for hh in range(DELTA_HEADS)]

    def per_head(main_ref, ctx_ref):
        x = jnp.where(is_ctx, ctx_ref[...], main_ref[...])
        return [x[:, ln] for ln in heads]

    q_h = per_head(q_ref, qc_ref)
    k_h = per_head(k_ref, kc_ref)
    v_h = per_head(v_ref, vc_ref)

    lane = lax.broadcasted_iota(jnp.int32, (1, LANES), 1)
    raw_all = jnp.where(is_ctx, bac_ref[...], ba_ref[...])
    bg_h = []
    for hh, ln in enumerate(heads):
        coef = coef_ref[hh]
        neg_a = -jnp.exp(jnp.where(lane == 2, coef[0:1, :], coef[1:2, :]))
        dtb = jnp.where(lane == 2, coef[2:3, :], coef[3:4, :])
        raw = raw_all[:, ln]
        xg = raw + dtb
        softplus = jnp.maximum(xg, 0.0) + jnp.log(1.0 + jnp.exp(-jnp.abs(xg)))
        bg_h.append(jnp.where(lane < 2, jax.nn.sigmoid(raw), neg_a * softplus))

    i_idx, j_idx = _tri_masks()
    i_w = lax.broadcasted_iota(jnp.int32, (CHUNK, LANES), 0)
    j_w = lax.broadcasted_iota(jnp.int32, (CHUNK, LANES), 1)
    in_chunk = j_w < CHUNK
    ones8 = jnp.ones((8, CHUNK), F32)
    chunks = [slice(c * CHUNK, (c + 1) * CHUNK) for c in range(DELTA_BLOCK)]

    kk, qk = {}, {}
    for hh in range(DELTA_HEADS):
        for c, r in enumerate(chunks):
            kbf = k_h[hh][r].astype(BF16)
            prod = _dot_nt(jnp.concatenate([kbf, q_h[hh][r].astype(BF16)], axis=0),
                           jnp.concatenate([kbf, jnp.zeros_like(kbf)], axis=0))
            kk[hh, c] = prod[:CHUNK]
            qk[hh, c] = prod[CHUNK:]

    gc_all, gr_all = {}, {}
    for d in range(2):
        incl = (i_idx >= j_idx) if d == 0 else (i_idx <= j_idx)
        seen_w = jnp.logical_and(in_chunk, (i_w <= j_w) if d == 0 else (i_w >= j_w))
        for hh in range(DELTA_HEADS):
            g_cols = [jnp.broadcast_to(bg_h[hh][r, 2 + d:3 + d], (CHUNK, LANES)) for r in chunks]
            gc_all[hh, d] = _dot_hi(incl.astype(F32), jnp.concatenate(g_cols, axis=1))
            gr_all[hh, d] = _dot_hi(ones8, jnp.concatenate([jnp.where(seen_w, g, 0.0) for g in g_cols], axis=1))

    insts, l_mats, rhs_all = [], [], []
    for d in range(2):
        strict = (i_idx > j_idx) if d == 0 else (i_idx < j_idx)
        incl_w = jnp.logical_and(in_chunk, (i_w >= j_w) if d == 0 else (i_w <= j_w))
        for hh, ln in enumerate(heads):
            for c, r in enumerate(chunks):
                q, k, v = q_h[hh][r], k_h[hh][r], v_h[hh][r]
                beta = bg_h[hh][r, d:d + 1]
                gc_w = gc_all[hh, d][:, c * LANES:(c + 1) * LANES]
                gr_w = jnp.broadcast_to(gr_all[hh, d][0:1, c * LANES:(c + 1) * LANES], (CHUNK, LANES))
                decay = jnp.where(incl_w, jnp.exp(jnp.minimum(gc_w - gr_w, 0.0)), 0.0)
                g_last = gc_w[CHUNK - 1:CHUNK, :] if d == 0 else gc_w[0:1, :]
                egc = jnp.exp(gc_w)
                l_mats.append(jnp.where(strict, (kk[hh, c] * beta * decay)[:, :CHUNK], 0.0))
                rhs_all.append(jnp.concatenate([v * beta, k * beta * egc], axis=1).astype(BF16))
                kt_ref[d, 0, r, ln] = (k * jnp.exp(g_last - gc_w)).astype(BF16)
                qd_ref[d, 0, r, ln] = (q * egc).astype(BF16)
                in_ref[d, 0, r, ln] = (qk[hh, c] * decay).astype(BF16)
                el_ref[d, 0, c * 8:(c + 1) * 8, ln] = jnp.broadcast_to(jnp.exp(g_last), (8, LANES))
                insts.append((d, r, ln))

    t_split = [_split_bf16(t) for t in _unit_tri_inverses(l_mats, i_idx, j_idx)]
    sols = [_dot(t_hi, rhs) + _dot(t_lo, rhs) for (t_hi, t_lo), rhs in zip(t_split, rhs_all)]
    for (d, r, ln), sol in zip(insts, sols):
        u_ref[d, 0, r, ln] = sol[:, :A_HEAD_DIM]
        w_ref[d, 0, r, ln] = sol[:, A_HEAD_DIM:].astype(BF16)


def _delta_prep(qkv, qkv_c, ba, ba_c, coef, bsz, n_lat, n_ctx):
    assert n_ctx == DELTA_ROWS and n_lat % DELTA_ROWS == 0 and A_HEADS % DELTA_HEADS == 0
    groups = A_HEADS // DELTA_HEADS
    wd = DELTA_HEADS * A_HEAD_DIM
    lat_blocks = n_lat // DELTA_ROWS
    n_blocks = lat_blocks + 1
    n_all = n_lat + n_ctx

    def lat_blk(b, j):
        return b * lat_blocks + jnp.maximum(j - 1, 0)

    def main(kd):
        return pl.BlockSpec((DELTA_ROWS, wd), lambda b, hg, j: (lat_blk(b, j), kd * groups + hg))

    def ctxb(kd):
        return pl.BlockSpec((n_ctx, wd), lambda b, hg, j: (b, kd * groups + hg))

    in_specs = ([main(kd) for kd in range(3)] + [ctxb(kd) for kd in range(3)]
                + [pl.BlockSpec((DELTA_ROWS, wd), lambda b, hg, j: (lat_blk(b, j), hg)),
                   pl.BlockSpec((n_ctx, wd), lambda b, hg, j: (b, hg)),
                   pl.BlockSpec((DELTA_HEADS, 8, LANES), lambda b, hg, j: (hg, 0, 0))])
    big = pl.BlockSpec((2, 1, DELTA_ROWS, wd), lambda b, hg, j: (0, b, j, hg))
    small = pl.BlockSpec((2, 1, DELTA_BLOCK * 8, wd), lambda b, hg, j: (0, b, j, hg))
    sds = lambda rows, dt: jax.ShapeDtypeStruct((2, bsz, rows, A_WIDTH), dt)
    return pl.pallas_call(
        _dprep_kernel,
        grid=(bsz, groups, n_blocks),
        in_specs=in_specs,
        out_specs=[big, big, big, big, big, small],
        out_shape=[sds(n_all, F32), sds(n_all, BF16), sds(n_all, BF16), sds(n_all, BF16), sds(n_all, BF16),
                   sds(n_all // CHUNK * 8, F32)],
        compiler_params=_cparams(("arbitrary", "arbitrary", "arbitrary")),
        name="dprep",
    )(*([qkv] * 3), *([qkv_c] * 3), ba, ba_c, coef)


def _dprep_kernel_single_head(q_ref, k_ref, v_ref, qp_ref, kp_ref, vp_ref, qn_ref, kn_ref, vn_ref, qc_ref, kc_ref,
                              vc_ref, wq_ref, wk_ref, wv_ref, ba_ref, bac_ref, coef_ref,
                              u_ref, w_ref, kt_ref, qd_ref, in_ref, el_ref, *, n_blocks):
    blk = pl.program_id(2)
    is_ctx = blk == 0
    rows = DELTA_ROWS
    ext_rows = rows + 2 * CONV_HALO

    def conv(main_ref, ctx_ref, prev_ref, next_ref, w_ref, kind):
        x = jnp.where(is_ctx, ctx_ref[...], main_ref[...])
        prev = jnp.where(blk <= 1, 0.0, prev_ref[...])
        nxt = jnp.where(jnp.logical_or(is_ctx, blk == n_blocks - 1), 0.0, next_ref[...])
        ext = jnp.concatenate([prev, x, nxt], axis=0)
        w = w_ref[...]
        acc = jnp.zeros((rows, A_HEAD_DIM), F32)
        for tap in range(SHORT_CONV_W):
            shift = (SHORT_CONV_W // 2 - tap) % ext_rows
            rolled = pltpu.roll(ext, shift, axis=0) if shift else ext
            acc = acc + rolled[CONV_HALO:CONV_HALO + rows, :] * w[tap:tap + 1, :]
        y = _silu(acc)
        if kind < 2:
            y = y * lax.rsqrt(jnp.sum(y * y, axis=-1, keepdims=True) + EPS)
        if kind == 0:
            y = y * (A_HEAD_DIM ** -0.5)
        return y

    q_all = conv(q_ref, qc_ref, qp_ref, qn_ref, wq_ref, 0)
    k_all = conv(k_ref, kc_ref, kp_ref, kn_ref, wk_ref, 1)
    v_all = conv(v_ref, vc_ref, vp_ref, vn_ref, wv_ref, 2)

    lane = lax.broadcasted_iota(jnp.int32, (1, LANES), 1)
    coef = coef_ref[0]
    neg_a = -jnp.exp(jnp.where(lane == 2, coef[0:1, :], coef[1:2, :]))
    dtb = jnp.where(lane == 2, coef[2:3, :], coef[3:4, :])
    raw = jnp.where(is_ctx, bac_ref[...], ba_ref[...])
    xg = raw + dtb
    softplus = jnp.maximum(xg, 0.0) + jnp.log(1.0 + jnp.exp(-jnp.abs(xg)))
    bg = jnp.where(lane < 2, jax.nn.sigmoid(raw), neg_a * softplus)

    i_idx, j_idx = _tri_masks()
    i_w = lax.broadcasted_iota(jnp.int32, (CHUNK, LANES), 0)
    j_w = lax.broadcasted_iota(jnp.int32, (CHUNK, LANES), 1)
    ones8 = jnp.ones((8, CHUNK), F32)
    chunks = [slice(c * CHUNK, (c + 1) * CHUNK) for c in range(DELTA_BLOCK)]

    kk, qk = [], []
    for r in chunks:
        kbf = k_all[r].astype(BF16)
        prod = _dot_nt(jnp.concatenate([kbf, q_all[r].astype(BF16)], axis=0),
                       jnp.concatenate([kbf, jnp.zeros_like(kbf)], axis=0))
        kk.append(prod[:CHUNK])
        qk.append(prod[CHUNK:])

    insts, l_mats, rhs_all = [], [], []
    for d in range(2):
        incl = (i_idx >= j_idx) if d == 0 else (i_idx <= j_idx)
        strict = (i_idx > j_idx) if d == 0 else (i_idx < j_idx)
        in_chunk = j_w < CHUNK
        incl_w = jnp.logical_and(in_chunk, (i_w >= j_w) if d == 0 else (i_w <= j_w))
        seen_w = jnp.logical_and(in_chunk, (i_w <= j_w) if d == 0 else (i_w >= j_w))
        g_cols = [jnp.broadcast_to(bg[r, 2 + d:3 + d], (CHUNK, LANES)) for r in chunks]
        gc_all = _dot_hi(incl.astype(F32), jnp.concatenate(g_cols, axis=1))
        gr_all = _dot_hi(ones8, jnp.concatenate([jnp.where(seen_w, g, 0.0) for g in g_cols], axis=1))
        for c, r in enumerate(chunks):
            q, k, v = q_all[r], k_all[r], v_all[r]
            beta = bg[r, d:d + 1]
            gc_w = gc_all[:, c * LANES:(c + 1) * LANES]
            gr_w = jnp.broadcast_to(gr_all[0:1, c * LANES:(c + 1) * LANES], (CHUNK, LANES))
            decay = jnp.where(incl_w, jnp.exp(jnp.minimum(gc_w - gr_w, 0.0)), 0.0)
            g_last = gc_w[CHUNK - 1:CHUNK, :] if d == 0 else gc_w[0:1, :]
            egc = jnp.exp(gc_w)
            l_mats.append(jnp.where(strict, (kk[c] * beta * decay)[:, :CHUNK], 0.0))
            rhs_all.append(jnp.concatenate([v * beta, k * beta * egc], axis=1).astype(BF16))
            kt_ref[d, 0, r, :] = (k * jnp.exp(g_last - gc_w)).astype(BF16)
            qd_ref[d, 0, r, :] = (q * egc).astype(BF16)
            in_ref[d, 0, r, :] = (qk[c] * decay).astype(BF16)
            el_ref[d, 0, c * 8:(c + 1) * 8, :] = jnp.broadcast_to(jnp.exp(g_last), (8, LANES))
            insts.append((d, r))

    t_split = [_split_bf16(t) for t in _unit_tri_inverses(l_mats, i_idx, j_idx)]
    sols = [_dot(t_hi, rhs) + _dot(t_lo, rhs) for (t_hi, t_lo), rhs in zip(t_split, rhs_all)]
    for (d, r), sol in zip(insts, sols):
        u_ref[d, 0, r, :] = sol[:, :A_HEAD_DIM]
        w_ref[d, 0, r, :] = sol[:, A_HEAD_DIM:].astype(BF16)


def _delta_prep_single_head(qkv, qkv_c, conv_w, ba, ba_c, coef, bsz, n_lat, n_ctx):
    assert n_ctx == DELTA_ROWS and n_lat % DELTA_ROWS == 0
    h = A_HEADS
    hd = A_HEAD_DIM
    lat_blocks = n_lat // DELTA_ROWS
    n_blocks = lat_blocks + 1
    n_all = n_lat + n_ctx
    halo_per_block = DELTA_ROWS // CONV_HALO
    n_halo = bsz * n_lat // CONV_HALO

    def lat_blk(b, j):
        return b * lat_blocks + jnp.maximum(j - 1, 0)

    def main(kd):
        return pl.BlockSpec((DELTA_ROWS, hd), lambda b, hh, j: (lat_blk(b, j), kd * h + hh))

    def prev(kd):
        return pl.BlockSpec((CONV_HALO, hd),
                            lambda b, hh, j: (jnp.maximum(lat_blk(b, j) * halo_per_block - 1, 0), kd * h + hh))

    def nxt(kd):
        return pl.BlockSpec((CONV_HALO, hd),
                            lambda b, hh, j: (jnp.minimum((lat_blk(b, j) + 1) * halo_per_block, n_halo - 1),
                                              kd * h + hh))

    def ctxb(kd):
        return pl.BlockSpec((n_ctx, hd), lambda b, hh, j: (b, kd * h + hh))

    def tap(kd):
        return pl.BlockSpec((SHORT_CONV_W, hd), lambda b, hh, j: (0, kd * h + hh))

    in_specs = ([main(kd) for kd in range(3)] + [prev(kd) for kd in range(3)] + [nxt(kd) for kd in range(3)]
                + [ctxb(kd) for kd in range(3)] + [tap(kd) for kd in range(3)]
                + [pl.BlockSpec((DELTA_ROWS, LANES), lambda b, hh, j: (lat_blk(b, j), hh)),
                   pl.BlockSpec((n_ctx, LANES), lambda b, hh, j: (b, hh)),
                   pl.BlockSpec((1, 8, LANES), lambda b, hh, j: (hh, 0, 0))])
    big = pl.BlockSpec((2, 1, DELTA_ROWS, hd), lambda b, hh, j: (0, b, j, hh))
    small = pl.BlockSpec((2, 1, DELTA_BLOCK * 8, LANES), lambda b, hh, j: (0, b, j, hh))
    sds = lambda rows, dt: jax.ShapeDtypeStruct((2, bsz, rows, h * hd), dt)
    return pl.pallas_call(
        functools.partial(_dprep_kernel, n_blocks=n_blocks),
        grid=(bsz, h, n_blocks),
        in_specs=in_specs,
        out_specs=[big, big, big, big, big, small],
        out_shape=[sds(n_all, F32), sds(n_all, BF16), sds(n_all, BF16), sds(n_all, BF16), sds(n_all, BF16),
                   sds(n_all // CHUNK * 8, F32)],
        compiler_params=_cparams(("arbitrary", "arbitrary", "arbitrary")),
        name="dprep",
    )(*([qkv] * 9), *([qkv_c] * 3), *([conv_w] * 3), ba, ba_c, coef)


def _dscan_kernel(uf, wf, ktf, qdf, inf, elf, ub, wb, ktb, qdb, inb, elb, of_ref, ob_ref, state):
    step = pl.program_id(1)

    @pl.when(step == 0)
    def _():
        state[...] = jnp.zeros(state.shape, F32)

    dirs = ((uf, wf, ktf, qdf, inf, elf, of_ref), (ub, wb, ktb, qdb, inb, elb, ob_ref))
    chains = [(d, hh) for d in range(2) for hh in range(A_HEADS)]
    s_mats = [state[d * A_HEADS + hh] for d, hh in chains]
    for t in range(DELTA_BLOCK):
        where = []
        for d, hh in chains:
            c = t if d == 0 else DELTA_BLOCK - 1 - t
            where.append((dirs[d], c, slice(c * CHUNK, (c + 1) * CHUNK),
                          slice(hh * A_HEAD_DIM, (hh + 1) * A_HEAD_DIM)))
        ws = [_dot(jnp.concatenate([rf[1][0, 0, r, ln], rf[3][0, 0, r, ln]], axis=0), s.astype(BF16))
              for (rf, c, r, ln), s in zip(where, s_mats)]
        v_new = [(rf[0][0, 0, r, ln] - x[:CHUNK]).astype(BF16) for (rf, c, r, ln), x in zip(where, ws)]
        for (rf, c, r, ln), x, vn in zip(where, ws, v_new):
            rf[6][r, ln] = x[CHUNK:] + _dot(rf[4][0, 0, r, ln][:, :CHUNK], vn)
        s_mats = [s * rf[5][0, 0, c * 8:c * 8 + 1, ln] + _dot_tn(rf[2][0, 0, r, ln], vn)
                  for (rf, c, r, ln), s, vn in zip(where, s_mats, v_new)]
    for (d, hh), s in zip(chains, s_mats):
        state[d * A_HEADS + hh] = s


def _delta_scan(pre, bsz, n_lat, n_ctx):
    u, w, kt, qd, intra, el = pre
    lat_blocks = n_lat // DELTA_ROWS
    n_blocks = lat_blocks + 1
    width = A_WIDTH

    def fwd_idx(b, s):
        return (0, b, s, 0)

    def bwd_idx(b, s):
        return (1, b, jnp.where(s == 0, 0, n_blocks - s), 0)

    def specs(idx):
        big = pl.BlockSpec((1, 1, DELTA_ROWS, width), idx)
        return [big, big, big, big, big, pl.BlockSpec((1, 1, DELTA_BLOCK * 8, width), idx)]

    out_f = pl.BlockSpec((DELTA_ROWS, width), lambda b, s: (b * lat_blocks + jnp.maximum(s - 1, 0), 0))
    out_b = pl.BlockSpec((DELTA_ROWS, width), lambda b, s: (b * lat_blocks + lat_blocks - jnp.maximum(s, 1), 0))
    shape = jax.ShapeDtypeStruct((bsz * n_lat, width), F32)
    return pl.pallas_call(
        _dscan_kernel,
        grid=(bsz, n_blocks),
        in_specs=specs(fwd_idx) + specs(bwd_idx),
        out_specs=[out_f, out_b],
        out_shape=[shape, shape],
        scratch_shapes=[pltpu.VMEM((2 * A_HEADS, A_HEAD_DIM, A_HEAD_DIM), F32)],
        compiler_params=_cparams(("arbitrary", "arbitrary")),
        name="dscan",
    )(u, w, kt, qd, intra, el, u, w, kt, qd, intra, el)


def _delta_kernel(qf_ref, kf_ref, vf_ref, qr_ref, kr_ref, vr_ref, qc_ref, kc_ref, vc_ref,
                  baf_ref, bar_ref, bac_ref, coef_ref, of_ref, ob_ref,
                  u_s, w_s, kt_s, qd_s, in_s, el_s, l_s, rhs_s, state):
    step = pl.program_id(1)
    is_ctx = step == 0
    front_slot = lax.rem(step, 3)
    scan_slot = lax.rem(step + 1, 3)
    lr_write = step % 2
    lr_read = 1 - lr_write
    uw_write = lr_read
    uw_read = lr_write
    heads = [slice(hh * A_HEAD_DIM, (hh + 1) * A_HEAD_DIM) for hh in range(A_HEADS)]
    chunks = [slice(c * CHUNK, (c + 1) * CHUNK) for c in range(DELTA_BLOCK)]

    @pl.when(step == 0)
    def _():
        for scr in (kt_s, qd_s, in_s, el_s):
            scr[1] = jnp.zeros(scr.shape[1:], scr.dtype)
            scr[2] = jnp.zeros(scr.shape[1:], scr.dtype)
        u_s[0] = jnp.zeros(u_s.shape[1:], u_s.dtype)
        w_s[0] = jnp.zeros(w_s.shape[1:], w_s.dtype)
        l_s[1] = jnp.zeros(l_s.shape[1:], l_s.dtype)
        rhs_s[1] = jnp.zeros(rhs_s.shape[1:], rhs_s.dtype)
        state[...] = jnp.zeros(state.shape, F32)

    chains = [(d, hh) for d in range(2) for hh in range(A_HEADS)]
    s_mats = [state[d * A_HEADS + hh] for d, hh in chains]
    out_refs = (of_ref, ob_ref)

    def scan_substep(t, s_mats):
        where = []
        for d, hh in chains:
            c = t if d == 0 else DELTA_BLOCK - 1 - t
            where.append((d, c, chunks[c], heads[hh]))
        ws = [_dot(jnp.concatenate([w_s[uw_read, d, r, ln], qd_s[scan_slot, d, r, ln]], axis=0), s.astype(BF16))
              for (d, c, r, ln), s in zip(where, s_mats)]
        v_new = [(u_s[uw_read, d, r, ln] - x[:CHUNK]).astype(BF16) for (d, c, r, ln), x in zip(where, ws)]
        for (d, c, r, ln), x, vn in zip(where, ws, v_new):
            out_refs[d][r, ln] = x[CHUNK:] + _dot(in_s[scan_slot, d, r, ln][:, :CHUNK], vn)
        return [s * el_s[scan_slot, d, c * 8:c * 8 + 1, ln] + _dot_tn(kt_s[scan_slot, d, r, ln], vn)
                for (d, c, r, ln), s, vn in zip(where, s_mats, v_new)]

    def per_head(main_ref, ctx_ref):
        x = jnp.where(is_ctx, ctx_ref[...], main_ref[...])
        return [x[:, ln] for ln in heads]

    q_h = (per_head(qf_ref, qc_ref), per_head(qr_ref, qc_ref))
    k_h = (per_head(kf_ref, kc_ref), per_head(kr_ref, kc_ref))
    v_h = (per_head(vf_ref, vc_ref), per_head(vr_ref, vc_ref))

    lane = lax.broadcasted_iota(jnp.int32, (1, LANES), 1)
    raw_d = (jnp.where(is_ctx, bac_ref[...], baf_ref[...]), jnp.where(is_ctx, bac_ref[...], bar_ref[...]))
    bg = {}
    for hh, ln in enumerate(heads):
        coef = coef_ref[hh]
        neg_a = -jnp.exp(jnp.where(lane == 2, coef[0:1, :], coef[1:2, :]))
        dtb = jnp.where(lane == 2, coef[2:3, :], coef[3:4, :])
        for d in range(2):
            raw = raw_d[d][:, ln]
            xg = raw + dtb
            softplus = jnp.maximum(xg, 0.0) + jnp.log(1.0 + jnp.exp(-jnp.abs(xg)))
            bg[d, hh] = jnp.where(lane < 2, jax.nn.sigmoid(raw), neg_a * softplus)

    i_idx, j_idx = _tri_masks()
    i_w = lax.broadcasted_iota(jnp.int32, (CHUNK, LANES), 0)
    j_w = lax.broadcasted_iota(jnp.int32, (CHUNK, LANES), 1)
    in_chunk = j_w < CHUNK
    ones8 = jnp.ones((16, CHUNK), BF16)

    def front_steps(d):
        incl = (i_idx >= j_idx) if d == 0 else (i_idx <= j_idx)
        strict = (i_idx > j_idx) if d == 0 else (i_idx < j_idx)
        incl_w = jnp.logical_and(in_chunk, (i_w >= j_w) if d == 0 else (i_w <= j_w))
        seen_w = jnp.logical_and(in_chunk, (i_w <= j_w) if d == 0 else (i_w >= j_w))
        seen_cat = jnp.concatenate([seen_w] * DELTA_BLOCK, axis=1)
        incl_b = incl.astype(BF16)
        for hh, ln in enumerate(heads):
            g_cat = jnp.concatenate([jnp.broadcast_to(bg[d, hh][r, 2 + d:3 + d], (CHUNK, LANES)) for r in chunks],
                                    axis=1)
            g_hi = g_cat.astype(BF16)
            g_rest = g_cat - g_hi.astype(F32)
            g_mid = g_rest.astype(BF16)
            pieces = (g_hi, g_mid, (g_rest - g_mid.astype(F32)).astype(BF16))
            gc_all = sum(_dot(incl_b, p) for p in pieces)
            gr_all = sum(_dot(ones8, jnp.where(seen_cat, p, jnp.zeros_like(p))) for p in pieces)
            for c, r in enumerate(chunks):
                q, k, v = q_h[d][hh][r], k_h[d][hh][r], v_h[d][hh][r]
                kbf = k.astype(BF16)
                prod = _dot_nt(jnp.concatenate([kbf, q.astype(BF16)], axis=0),
                               jnp.concatenate([kbf, jnp.zeros_like(kbf)], axis=0))
                beta = bg[d, hh][r, d:d + 1]
                gc_w = gc_all[:, c * LANES:(c + 1) * LANES]
                gr_w = jnp.broadcast_to(gr_all[0:1, c * LANES:(c + 1) * LANES], (CHUNK, LANES))
                decay = jnp.where(incl_w, jnp.exp(jnp.minimum(gc_w - gr_w, 0.0)), 0.0)
                g_last = gc_w[CHUNK - 1:CHUNK, :] if d == 0 else gc_w[0:1, :]
                egc = jnp.exp(gc_w)
                n = (d * A_HEADS + hh) * DELTA_BLOCK + c
                l_s[lr_write, n] = jnp.where(strict, (prod[:CHUNK] * beta * decay)[:, :CHUNK], 0.0)
                rhs_s[lr_write, n] = jnp.concatenate([v * beta, k * beta * egc], axis=1).astype(BF16)
                kt_s[front_slot, d, r, ln] = (k * jnp.exp(g_last - gc_w)).astype(BF16)
                qd_s[front_slot, d, r, ln] = (q * egc).astype(BF16)
                in_s[front_slot, d, r, ln] = (prod[CHUNK:] * decay).astype(BF16)
                el_s[front_slot, d, c * 8:(c + 1) * 8, ln] = jnp.broadcast_to(jnp.exp(g_last), (8, LANES))
                yield None

    def back_steps():
        n_inst = 2 * A_HEADS * DELTA_BLOCK
        inverse = None
        for inverse in _unit_tri_inverse_steps([l_s[lr_read, n] for n in range(n_inst)], i_idx, j_idx):
            yield None
        t_split = [_split_bf16(t) for t in inverse]
        rhs = [rhs_s[lr_read, n] for n in range(n_inst)]
        sols = [_dot(t_hi, rr) + _dot(t_lo, rr) for (t_hi, t_lo), rr in zip(t_split, rhs)]
        for n, sol in enumerate(sols):
            d, hh, c = n // (A_HEADS * DELTA_BLOCK), (n // DELTA_BLOCK) % A_HEADS, n % DELTA_BLOCK
            u_s[uw_write, d, chunks[c], heads[hh]] = sol[:, :A_HEAD_DIM]
            w_s[uw_write, d, chunks[c], heads[hh]] = sol[:, A_HEAD_DIM:].astype(BF16)
        yield None

    def scan_steps(s_mats):
        for t in range(DELTA_BLOCK):
            s_mats = scan_substep(t, s_mats)
            yield s_mats
            yield s_mats
            yield s_mats

    _, _, _, s_mats = _drain(front_steps(0), front_steps(1), back_steps(), scan_steps(s_mats))
    for (d, hh), s in zip(chains, s_mats):
        state[d * A_HEADS + hh] = s


def _delta_mixer(qkv, qkv_c, ba, ba_c, coef, bsz, n_lat, n_ctx):
    assert n_ctx == DELTA_ROWS and n_lat % DELTA_ROWS == 0 and DELTA_BLOCK == 4
    wd = A_WIDTH
    lat_blocks = n_lat // DELTA_ROWS
    n_steps = lat_blocks + 3

    def fwd_blk(b, s):
        return b * lat_blocks + jnp.clip(s - 1, 0, lat_blocks - 1)

    def bwd_blk(b, s):
        return b * lat_blocks + jnp.clip(lat_blocks - s, 0, lat_blocks - 1)

    def lat(blk_fn, kd):
        return pl.BlockSpec((DELTA_ROWS, wd), lambda b, s: (blk_fn(b, s), kd))

    def ctxb(kd):
        return pl.BlockSpec((n_ctx, wd), lambda b, s: (b, kd))

    in_specs = ([lat(fwd_blk, kd) for kd in range(3)] + [lat(bwd_blk, kd) for kd in range(3)]
                + [ctxb(kd) for kd in range(3)]
                + [lat(fwd_blk, 0), lat(bwd_blk, 0), ctxb(0),
                   pl.BlockSpec((A_HEADS, 8, LANES), lambda b, s: (0, 0, 0))])
    out_f = pl.BlockSpec((DELTA_ROWS, wd), lambda b, s: (b * lat_blocks + jnp.maximum(s - 3, 0), 0))
    out_b = pl.BlockSpec((DELTA_ROWS, wd), lambda b, s: (b * lat_blocks + lat_blocks - jnp.maximum(s - 2, 1), 0))
    shape = jax.ShapeDtypeStruct((bsz * n_lat, wd), F32)
    slots = lambda n, rows, dt: pltpu.VMEM((n, 2, rows, wd), dt)
    n_inst = 2 * A_HEADS * DELTA_BLOCK
    return pl.pallas_call(
        _delta_kernel,
        grid=(bsz, n_steps),
        in_specs=in_specs,
        out_specs=[out_f, out_b],
        out_shape=[shape, shape],
        scratch_shapes=[slots(2, DELTA_ROWS, F32), slots(2, DELTA_ROWS, BF16), slots(3, DELTA_ROWS, BF16),
                        slots(3, DELTA_ROWS, BF16), slots(3, DELTA_ROWS, BF16), slots(3, DELTA_BLOCK * 8, F32),
                        pltpu.VMEM((2, n_inst, CHUNK, CHUNK), F32),
                        pltpu.VMEM((2, n_inst, CHUNK, 2 * A_HEAD_DIM), BF16),
                        pltpu.VMEM((2 * A_HEADS, A_HEAD_DIM, A_HEAD_DIM), F32)],
        compiler_params=_cparams(("arbitrary", "arbitrary")),
        name="delta",
    )(*([qkv] * 6), *([qkv_c] * 3), ba, ba, ba_c, coef)


def _head_rms(x, gain_row, head_dim):
    outs = []
    for s in range(x.shape[1] // LANES):
        xs = x[:, s * LANES:(s + 1) * LANES]
        lane = lax.broadcasted_iota(jnp.int32, xs.shape, 1)
        sq = xs * xs
        scale = jnp.zeros_like(xs)
        for part in range(LANES // head_dim):
            m = jnp.logical_and(lane >= part * head_dim, lane < (part + 1) * head_dim)
            ms = jnp.sum(jnp.where(m, sq, 0.0), axis=-1, keepdims=True) * (1.0 / head_dim)
            scale = jnp.where(m, lax.rsqrt(ms + EPS), scale)
        outs.append(xs * scale)
    y = outs[0] if len(outs) == 1 else jnp.concatenate(outs, axis=1)
    return y * gain_row


def _rope(x, cos, sin_lo, sin_hi):
    width = x.shape[1]
    fwd = pltpu.roll(x, width - ROPE_AXIS_PAIRS, axis=1)
    back = pltpu.roll(x, ROPE_AXIS_PAIRS, axis=1)
    return x * cos + fwd * sin_lo + back * sin_hi


def _kvprep_kernel(k_ref, v_ref, g_ref, cos_ref, slo_ref, shi_ref, ko_ref, vo_ref):
    k = _head_rms(k_ref[0], g_ref[...], B_HEAD_DIM)
    k = _rope(k, cos_ref[...], slo_ref[...], shi_ref[...]).astype(BF16)
    vt = v_ref[0].T.astype(BF16)
    for hh in range(B_KV_HEADS):
        ko_ref[0, hh] = k[:, hh * B_HEAD_DIM:(hh + 1) * B_HEAD_DIM]
        vo_ref[0, hh, :B_HEAD_DIM, :] = vt[hh * B_HEAD_DIM:(hh + 1) * B_HEAD_DIM, :]
        vo_ref[0, hh, B_HEAD_DIM:, :] = jnp.ones((ATTN_VT_ROWS - B_HEAD_DIM, vt.shape[1]), BF16)


def _kvprep(k_all, v_all, gain, cos, slo, shi, ts):
    bsz, s_len, w = k_all.shape
    row = pl.BlockSpec((1, ts, w), lambda b, i: (b, i, 0))
    tab = pl.BlockSpec((ts, w), lambda b, i: (i, 0))
    return pl.pallas_call(
        _kvprep_kernel,
        grid=(bsz, s_len // ts),
        in_specs=[row, row, pl.BlockSpec((1, w), lambda b, i: (0, 0)), tab, tab, tab],
        out_specs=[pl.BlockSpec((1, B_KV_HEADS, ts, B_HEAD_DIM), lambda b, i: (b, 0, i, 0)),
                   pl.BlockSpec((1, B_KV_HEADS, ATTN_VT_ROWS, ts), lambda b, i: (b, 0, 0, i))],
        out_shape=[jax.ShapeDtypeStruct((bsz, B_KV_HEADS, s_len, B_HEAD_DIM), BF16),
                   jax.ShapeDtypeStruct((bsz, B_KV_HEADS, ATTN_VT_ROWS, s_len), BF16)],
        compiler_params=_cparams(("arbitrary", "arbitrary")),
        name="kvprep",
    )(k_all, v_all, gain, cos, slo, shi)


LOG2_E = math.log2(math.e)
ATTN_SCORE_KEYS = 64
ATTN_VALUE_KEYS = 256
ATTN_Q_COLS = 256
ATTN_VT_ROWS = B_HEAD_DIM + 16


def _attn_kernel(q_ref, g_ref, cos_ref, slo_ref, shi_ref, k_ref, vt_ref, o_ref):
    q = _head_rms(q_ref[...], g_ref[...], B_HEAD_DIM)
    q = _rope(q, cos_ref[...], slo_ref[...], shi_ref[...]) * (B_HEAD_DIM ** -0.5 * LOG2_E)
    tq = q.shape[0]
    qt = q.T.astype(BF16)
    n_keys = k_ref.shape[2]
    q_s = [qt[g * B_HEAD_DIM:(g + 1) * B_HEAD_DIM, c:c + ATTN_Q_COLS]
           for g in range(B_GROUP) for c in range(0, tq, ATTN_Q_COLS)]
    m = [jnp.full((1, ATTN_Q_COLS), -1e30, F32) for _ in q_s]
    acc = [jnp.zeros((ATTN_VT_ROWS, ATTN_Q_COLS), F32) for _ in q_s]

    def scores(kb):
        k_blk = k_ref[0, 0, kb:kb + ATTN_VALUE_KEYS, :]
        return [_dot(k_blk, qs) for qs in q_s]

    def value_update(acc, pending):
        alpha, p, kb = pending
        vt = vt_ref[0, 0, :, kb:kb + ATTN_VALUE_KEYS]
        return [a * al + _dot(vt, pp) for a, al, pp in zip(acc, alpha, p)]

    st_next = scores(0)
    pending = None
    for kb in range(0, n_keys, ATTN_VALUE_KEYS):
        st = st_next
        if kb + ATTN_VALUE_KEYS < n_keys:
            st_next = scores(kb + ATTN_VALUE_KEYS)
        m_new = [jnp.maximum(mo, jnp.max(s, axis=0, keepdims=True)) for mo, s in zip(m, st)]
        alpha = [jnp.exp2(mo - mn) for mo, mn in zip(m, m_new)]
        p = [jnp.exp2((s - mn).astype(BF16)) for s, mn in zip(st, m_new)]
        if pending is not None:
            acc = value_update(acc, pending)
        pending = (alpha, p, kb)
        m = m_new
    acc = value_update(acc, pending)
    outs = [a[:B_HEAD_DIM] / a[B_HEAD_DIM:B_HEAD_DIM + 1] for a in acc]
    n_col = tq // ATTN_Q_COLS
    o_ref[...] = jnp.concatenate([jnp.concatenate(outs[g * n_col:(g + 1) * n_col], axis=1)
                                  for g in range(B_GROUP)], axis=0).T


def _attn_kernel_two_pass(q_ref, g_ref, cos_ref, slo_ref, shi_ref, k_ref, vt_ref, o_ref, s_scr):
    q = _head_rms(q_ref[...], g_ref[...], B_HEAD_DIM)
    q = _rope(q, cos_ref[...], slo_ref[...], shi_ref[...]) * (B_HEAD_DIM ** -0.5 * LOG2_E)
    tq = q.shape[0]
    qt = q.T.astype(BF16)
    n_keys = k_ref.shape[2]
    halves = [slice(c, c + ATTN_Q_COLS) for c in range(0, tq, ATTN_Q_COLS)]

    def score_steps(g):
        qg = qt[g * B_HEAD_DIM:(g + 1) * B_HEAD_DIM, :]
        m8 = None
        for kb in range(0, n_keys, ATTN_SCORE_KEYS):
            st = _dot(k_ref[0, 0, kb:kb + ATTN_SCORE_KEYS, :], qg)
            s_scr[g % 2, kb:kb + ATTN_SCORE_KEYS, :] = st
            bm = jnp.max(st.reshape(ATTN_SCORE_KEYS // 8, 8, tq), axis=0)
            m8 = bm if m8 is None else jnp.maximum(m8, bm)
            if (kb // ATTN_SCORE_KEYS) % (ATTN_VALUE_KEYS // ATTN_SCORE_KEYS) == 1:
                yield None
        yield jnp.max(m8, axis=0, keepdims=True)

    def value_steps(g, m):
        acc = [jnp.zeros((ATTN_VT_ROWS, ATTN_Q_COLS), F32) for _ in halves]
        for kb in range(0, n_keys, ATTN_VALUE_KEYS):
            vt = vt_ref[0, 0, :, kb:kb + ATTN_VALUE_KEYS]
            for i, cols in enumerate(halves):
                p = jnp.exp2((s_scr[g % 2, kb:kb + ATTN_VALUE_KEYS, cols] - m[:, cols]).astype(BF16))
                acc[i] = acc[i] + _dot(vt, p)
            yield None
        yield jnp.concatenate([a[:B_HEAD_DIM] / a[B_HEAD_DIM:B_HEAD_DIM + 1] for a in acc], axis=1)

    def drain(*gens):
        last = [None] * len(gens)
        live = list(range(len(gens)))
        while live:
            for i in list(live):
                try:
                    last[i] = next(gens[i])
                except StopIteration:
                    live.remove(i)
        return last

    outs = []
    (m,) = drain(score_steps(0))
    for g in range(B_GROUP):
        if g + 1 < B_GROUP:
            out, m = drain(value_steps(g, m), score_steps(g + 1))
        else:
            (out,) = drain(value_steps(g, m))
        outs.append(out)
    o_ref[...] = jnp.concatenate(outs, axis=0).T


def _attention(qb, gain, cos, slo, shi, k_hm, vt_hm, n_lat, tq):
    n, _ = qb.shape
    bsz, _, s_len, _ = k_hm.shape
    gw = B_GROUP * B_HEAD_DIM
    nq = n_lat // tq
    kv = pl.BlockSpec((1, 1, s_len, B_HEAD_DIM), lambda b, kh, i: (b, kh, 0, 0))
    vts = pl.BlockSpec((1, 1, ATTN_VT_ROWS, s_len), lambda b, kh, i: (b, kh, 0, 0))
    tab = pl.BlockSpec((tq, gw), lambda b, kh, i: (i, 0))
    return pl.pallas_call(
        _attn_kernel,
        grid=(bsz, B_KV_HEADS, nq),
        in_specs=[pl.BlockSpec((tq, gw), lambda b, kh, i: (b * nq + i, kh)),
                  pl.BlockSpec((1, gw), lambda b, kh, i: (0, 0)), tab, tab, tab, kv, vts],
        out_specs=pl.BlockSpec((tq, gw), lambda b, kh, i: (b * nq + i, kh)),
        out_shape=jax.ShapeDtypeStruct((n, B_WIDTH), F32),
        compiler_params=_cparams(("arbitrary", "arbitrary", "arbitrary")),
        name="attn",
    )(qb, gain, cos, slo, shi, k_hm, vt_hm)


def _gated_residual(x, y, mod_ref, gain_ref, gate_row):
    return x + mod_ref[0, gate_row:gate_row + 1, :] * _rms(y, gain_ref[...])


def _outproj_kernel(of_ref, ob_ref, z_ref, on_ref, yb_ref, x_ref, mod_ref, g_ref, wa_ref, wb_ref, o_ref):
    o = of_ref[...] + ob_ref[...]
    z = z_ref[...]
    parts = []
    for hh in range(A_HEADS):
        sl = slice(hh * A_HEAD_DIM, (hh + 1) * A_HEAD_DIM)
        parts.append((_rms(o[:, sl], on_ref[...]) * _silu(z[:, sl])).astype(BF16))
    ya = jnp.concatenate(parts, axis=1)
    y = _dot(ya, wa_ref[...]) + _dot(yb_ref[...].astype(BF16), wb_ref[...])
    o_ref[...] = _gated_residual(x_ref[...], y, mod_ref, g_ref, 2)


def _outproj(o_f, o_b, z, out_norm, yb, x2, mod, gain, wa, wb, rows_per_mod, tm):
    n, d = x2.shape
    bpm = rows_per_mod // tm
    return pl.pallas_call(
        _outproj_kernel,
        grid=(n // tm,),
        in_specs=[pl.BlockSpec((tm, o_f.shape[1]), lambda i: (i, 0)),
                  pl.BlockSpec((tm, o_b.shape[1]), lambda i: (i, 0)),
                  pl.BlockSpec((tm, z.shape[1]), lambda i: (i, 0)),
                  pl.BlockSpec((1, A_HEAD_DIM), lambda i: (0, 0)),
                  pl.BlockSpec((tm, yb.shape[1]), lambda i: (i, 0)),
                  pl.BlockSpec((tm, d), lambda i: (i, 0)),
                  pl.BlockSpec((1, 6, d), lambda i: (i // bpm, 0, 0)),
                  pl.BlockSpec((1, d), lambda i: (0, 0)),
                  _resident(wa.shape),
                  _resident(wb.shape)],
        out_specs=pl.BlockSpec((tm, d), lambda i: (i, 0)),
        out_shape=jax.ShapeDtypeStruct((n, d), F32),
        compiler_params=_cparams(("arbitrary",)),
        name="outproj",
    )(o_f, o_b, z, out_norm, yb, x2, mod, gain, wa, wb)


def _ffn_kernel(x_ref, mod_ref, gpre_ref, gpost_ref, wi_ref, wo_ref, o_ref, *, hidden, hc):
    x = x_ref[...]
    a = _modulated(x, mod_ref, gpre_ref, 3).astype(BF16)
    acc = jnp.zeros(x.shape, F32)
    for c in range(hidden // hc):
        gate = _dot(a, wi_ref[:, c * hc:(c + 1) * hc])
        up = _dot(a, wi_ref[:, hidden + c * hc:hidden + (c + 1) * hc])
        acc = acc + _dot((_silu(gate) * up).astype(BF16), wo_ref[c * hc:(c + 1) * hc, :])
    o_ref[...] = _gated_residual(x, acc, mod_ref, gpost_ref, 5)


def _ffn(x2, mod, gpre, gpost, wi, wo, rows_per_mod, tm):
    n, d = x2.shape
    hidden = wo.shape[0]
    bpm = rows_per_mod // tm
    return pl.pallas_call(
        functools.partial(_ffn_kernel, hidden=hidden, hc=256),
        grid=(n // tm,),
        in_specs=[pl.BlockSpec((tm, d), lambda i: (i, 0)),
                  pl.BlockSpec((1, 6, d), lambda i: (i // bpm, 0, 0)),
                  pl.BlockSpec((1, d), lambda i: (0, 0)),
                  pl.BlockSpec((1, d), lambda i: (0, 0)),
                  _resident(wi.shape),
                  _resident(wo.shape)],
        out_specs=pl.BlockSpec((tm, d), lambda i: (i, 0)),
        out_shape=jax.ShapeDtypeStruct((n, d), F32),
        compiler_params=_cparams(("arbitrary",)),
        name="ffn",
    )(x2, mod, gpre, gpost, wi, wo)


def _confin_kernel(x_ref, mod_ref, g_ref, w_ref, b_ref, o_ref, *, width):
    a = _modulated(x_ref[...], mod_ref, g_ref, 0).astype(BF16)
    val = _dot(a, w_ref[:, :width]) + b_ref[:, :width]
    gate = _dot(a, w_ref[:, width:]) + b_ref[:, width:]
    o_ref[...] = val * jax.nn.sigmoid(gate)


def _confin(x2, mod, gain, w, b, rows_per_mod, tm):
    n, d = x2.shape
    width = w.shape[1] // 2
    bpm = rows_per_mod // tm
    return pl.pallas_call(
        functools.partial(_confin_kernel, width=width),
        grid=(n // tm,),
        in_specs=[pl.BlockSpec((tm, d), lambda i: (i, 0)),
                  pl.BlockSpec((1, 6, d), lambda i: (i // bpm, 0, 0)),
                  pl.BlockSpec((1, d), lambda i: (0, 0)),
                  _resident(w.shape),
                  pl.BlockSpec((1, 2 * width), lambda i: (0, 0))],
        out_specs=pl.BlockSpec((tm, width), lambda i: (i, 0)),
        out_shape=jax.ShapeDtypeStruct((n, width), F32),
        compiler_params=_cparams(("arbitrary",)),
        name="confin",
    )(x2, mod, gain, w, b)


def _confout_kernel(u_ref, up_ref, un_ref, x_ref, mod_ref, g_ref, dww_ref, dwb_ref, lng_ref, lnb_ref,
                    w_ref, b_ref, o_ref, ext, conv, shifted, *, tm, tiles_per_seq):
    i = pl.program_id(0)
    pos = i % tiles_per_seq
    width = u_ref.shape[1]
    halo = CONF_HALO
    ext[pl.ds(halo, tm), :] = u_ref[...]
    ext[0:halo, :] = jnp.where(pos == 0, 0.0, up_ref[...])
    ext[pl.ds(halo + tm, halo), :] = jnp.where(pos == tiles_per_seq - 1, 0.0, un_ref[...])

    rb = 64
    pad = CONF_KERNEL // 2
    sub = 8
    copy_rows = tm + 2 * halo - sub

    def col_body(c, carry):
        cs = pl.ds(pl.multiple_of(c * LANES, LANES), LANES)
        w = dww_ref[:, cs]
        for s in range(sub):
            shifted[s, :, :] = ext[pl.ds(s, copy_rows), cs]
        for r in range(tm // rb):
            acc = jnp.zeros((rb, LANES), F32)
            for tap in range(CONF_KERNEL):
                off = halo - pad + tap
                acc = acc + shifted[off % sub, pl.ds(r * rb + off - off % sub, rb), :] * w[tap:tap + 1, :]
            conv[pl.ds(r * rb, rb), cs] = acc
        return carry

    lax.fori_loop(0, width // LANES, col_body, 0)

    y = conv[...] + dwb_ref[...]
    mu = jnp.mean(y, axis=-1, keepdims=True)
    yc = y - mu
    var = jnp.mean(yc * yc, axis=-1, keepdims=True)
    y = _silu(yc * lax.rsqrt(var + EPS) * lng_ref[...] + lnb_ref[...])
    out = _dot(y.astype(BF16), w_ref[...]) + b_ref[...]
    o_ref[...] = _gated_residual(x_ref[...], out, mod_ref, g_ref, 2)


def _confout(u, x2, mod, gain, dww, dwb, lng, lnb, w, b, rows_per_mod, tm):
    n, d = x2.shape
    width = u.shape[1]
    tiles_per_seq = rows_per_mod // tm
    hb = tm // CONF_HALO
    n_halo_blocks = n // CONF_HALO
    vec = lambda wd: pl.BlockSpec((1, wd), lambda i: (0, 0))
    return pl.pallas_call(
        functools.partial(_confout_kernel, tm=tm, tiles_per_seq=tiles_per_seq),
        grid=(n // tm,),
        in_specs=[pl.BlockSpec((tm, width), lambda i: (i, 0)),
                  pl.BlockSpec((CONF_HALO, width), lambda i: (jnp.maximum(i * hb - 1, 0), 0)),
                  pl.BlockSpec((CONF_HALO, width), lambda i: (jnp.minimum((i + 1) * hb, n_halo_blocks - 1), 0)),
                  pl.BlockSpec((tm, d), lambda i: (i, 0)),
                  pl.BlockSpec((1, 6, d), lambda i: (i // tiles_per_seq, 0, 0)),
                  vec(d),
                  pl.BlockSpec(dww.shape, lambda i: (0, 0)),
                  vec(width), vec(width), vec(width),
                  _resident(w.shape),
                  vec(d)],
        out_specs=pl.BlockSpec((tm, d), lambda i: (i, 0)),
        out_shape=jax.ShapeDtypeStruct((n, d), F32),
        scratch_shapes=[pltpu.VMEM((tm + 2 * CONF_HALO, width), F32), pltpu.VMEM((tm, width), F32),
                        pltpu.VMEM((8, tm + 2 * CONF_HALO - 8, LANES), F32)],
        compiler_params=_cparams(("arbitrary",)),
        name="confout",
    )(u, u, u, x2, mod, gain, dww, dwb, lng, lnb, w, b)


def _rope_tables(n_tokens, reps):
    rows = n_tokens // GRID_W
    row = jnp.broadcast_to(jnp.arange(rows, dtype=F32)[:, None], (rows, GRID_W)).reshape(n_tokens)
    col = jnp.broadcast_to(jnp.arange(GRID_W, dtype=F32)[None, :], (rows, GRID_W)).reshape(n_tokens)
    inv_freq = ROPE_THETA ** (-jnp.arange(ROPE_AXIS_PAIRS, dtype=F32) / ROPE_AXIS_PAIRS)
    ang_r = row[:, None] * inv_freq
    ang_c = col[:, None] * inv_freq
    ang = jnp.concatenate([ang_r, ang_r, ang_c, ang_c], axis=-1)
    cos, sin = jnp.cos(ang), jnp.sin(ang)
    first_half = (jnp.arange(B_HEAD_DIM) % (2 * ROPE_AXIS_PAIRS)) < ROPE_AXIS_PAIRS
    sin_lo = jnp.where(first_half, -sin, 0.0)
    sin_hi = jnp.where(first_half, 0.0, sin)
    tile = lambda t: jnp.tile(t, (1, reps))
    return tile(cos), tile(sin_lo), tile(sin_hi)


def _hybrid_in_weight(w_in):
    off_z = 3 * A_WIDTH
    off_ba = off_z + A_WIDTH
    off_q = off_ba + 4 * A_HEADS
    off_k = off_q + B_WIDTH
    off_v = off_k + B_KV_WIDTH
    d = w_in.shape[0]
    ba = w_in[:, off_ba:off_q].reshape(d, 2, 2, A_HEADS)
    ba = ba.transpose(0, 3, 1, 2).reshape(d, A_HEADS, 4)
    ba = jnp.pad(ba, ((0, 0), (0, 0), (0, LANES - 4))).reshape(d, A_HEADS * LANES)
    w = jnp.concatenate([w_in[:, :off_z], w_in[:, off_z:off_ba], w_in[:, off_q:off_k], w_in[:, off_k:off_v],
                         w_in[:, off_v:], ba], axis=1)
    return w.astype(BF16)


def _row(v):
    return v.reshape(1, -1)


def kernel(x, c, ctx, c_ctx, w_mod, b_mod, g_mix_pre, g_mix_post, g_ffn_pre, g_ffn_post, w_ffn_in, w_ffn_out,
           hyb_w_in, hyb_conv_w, hyb_a_log, hyb_dt_bias, hyb_out_norm, hyb_q_norm, hyb_k_norm, hyb_w_out,
           conf_w_in, conf_b_in, conf_dw_w, conf_dw_b, conf_ln_g, conf_ln_b, conf_w_out, conf_b_out):
    bsz, n_lat, d = x.shape
    n_ctx = ctx.shape[1]
    depth = w_mod.shape[0]
    n = bsz * n_lat
    tm = 1024

    cond =jnp.zeros((8, d), F32).at[:bsz].set(c).at[bsz].set(c_ctx)
    mods = _ada_terms(cond, w_mod, b_mod).reshape(depth, 8, 6, d)

    h = x.reshape(n, d)
    hc = ctx.reshape(bsz * n_ctx, d)
    for layer in range(depth):
        idx = layer // 2
        mod = mods[layer, :bsz]
        mod_ctx = mods[layer, bsz:bsz + 1]
        if layer % 2 == 0:
            w_in = _hybrid_in_weight(hyb_w_in[idx])
            gpre = _row(g_mix_pre[layer])
            k_gain = _row(jnp.tile(hyb_k_norm[idx], B_KV_HEADS))
            cos_k, slo_k, shi_k = _rope_tables(n_lat, B_KV_HEADS)
            no_pos = (jnp.ones((n_ctx, B_KV_WIDTH), F32), jnp.zeros((n_ctx, B_KV_WIDTH), F32),
                      jnp.zeros((n_ctx, B_KV_WIDTH), F32))
            qkv, z, qb, ba, k_lat, vt_lat = _hyb_inproj(h, mod, gpre, w_in, hyb_conv_w[idx], k_gain,
                                                        cos_k, slo_k, shi_k, bsz, n_lat, tm)
            qkv_c, _, _, ba_c, k_ctx, vt_ctx = _hyb_inproj(hc, mod_ctx, gpre, w_in, hyb_conv_w[idx], k_gain,
                                                           *no_pos, bsz, n_ctx, n_ctx)

            coef = jnp.concatenate([hyb_a_log[idx], hyb_dt_bias[idx]], axis=0)
            coef = jnp.pad(coef.T[:, :, None], ((0, 0), (0, 4), (0, 0)))
            coef = jnp.broadcast_to(coef, (A_HEADS, 8, LANES))
            o_f, o_b = _delta_mixer(qkv, qkv_c, ba, ba_c, coef, bsz, n_lat, n_ctx)

            k_hm = jnp.concatenate([k_lat, k_ctx], axis=2)
            vt_hm = jnp.concatenate([vt_lat, vt_ctx], axis=3)
            cos_q, slo_q, shi_q = _rope_tables(n_lat, B_GROUP)
            yb = _attention(qb, _row(jnp.tile(hyb_q_norm[idx], B_GROUP)), cos_q, slo_q, shi_q,
                            k_hm, vt_hm, n_lat, 512)

            w_out = hyb_w_out[idx].astype(BF16)
            h = _outproj(o_f, o_b, z, _row(hyb_out_norm[idx]), yb, h, mod, _row(g_mix_post[layer]),
                         w_out[:A_WIDTH], w_out[A_WIDTH:], n_lat, tm)
        else:
            u = _confin(h, mod, _row(g_mix_pre[layer]), conf_w_in[idx].astype(BF16), _row(conf_b_in[idx]),
                        n_lat, tm)
            h = _confout(u, h, mod, _row(g_mix_post[layer]), conf_dw_w[idx], _row(conf_dw_b[idx]),
                         _row(conf_ln_g[idx]), _row(conf_ln_b[idx]), conf_w_out[idx].astype(BF16),
                         _row(conf_b_out[idx]), n_lat, 256)
        h = _ffn(h, mod, _row(g_ffn_pre[layer]), _row(g_ffn_post[layer]), w_ffn_in[layer].astype(BF16),
                 w_ffn_out[layer].astype(BF16), n_lat, 1024)
        assert not any(j % 2 == 0 for j in range(layer + 1, depth)), "context advance not implemented"
    return h.reshape(bsz, n_lat, d)
```

```python
import functools
import math

import jax
import jax.numpy as jnp
from jax import lax
from jax.experimental import pallas as pl
from jax.experimental.pallas import tpu as pltpu

F32 = jnp.float32
BF16 = jnp.bfloat16
HIGHEST = lax.Precision.HIGHEST

EPS = 1e-6
GRID_W = 64
ROPE_THETA = 10000.0
A_HEADS = 4
A_HEAD_DIM = 128
A_WIDTH = A_HEADS * A_HEAD_DIM
SHORT_CONV_W = 5
CHUNK = 64
B_Q_HEADS = 8
B_KV_HEADS = 2
B_HEAD_DIM = 64
B_GROUP = B_Q_HEADS // B_KV_HEADS
B_WIDTH = B_Q_HEADS * B_HEAD_DIM
B_KV_WIDTH = B_KV_HEADS * B_HEAD_DIM
ROPE_AXIS_PAIRS = B_HEAD_DIM // 4
CONF_KERNEL = 31
LOG2_E = math.log2(math.e)

LANES = 128
SUBLANES = 8
VMEM_LIMIT = 56 * 1024 * 1024
ROW_TILE = 1024
CONF_TILE = 512
CONF_HALO = 16
PROJ_HALO = 16
DELTA_BLOCK = 4
DELTA_ROWS = DELTA_BLOCK * CHUNK
ATTN_Q_TILE = 512
ATTN_KEY_BLOCK = 256
ATTN_Q_COLS = 256
ATTN_VT_ROWS = B_HEAD_DIM + 16


def _cparams(sem):
    return pltpu.CompilerParams(dimension_semantics=sem, vmem_limit_bytes=VMEM_LIMIT)


def _resident(shape):
    return pl.BlockSpec(shape, lambda i: (0, 0), pipeline_mode=pl.Buffered(1))


def _silu(x):
    return x * jax.nn.sigmoid(x)


def _dot(a, b):
    return jnp.dot(a, b, preferred_element_type=F32)


def _dot_nt(a, b):
    return lax.dot_general(a, b, (((1,), (1,)), ((), ())), preferred_element_type=F32)


def _dot_tn(a, b):
    return lax.dot_general(a, b, (((0,), (0,)), ((), ())), preferred_element_type=F32)


def _rms(x, gain):
    return x * lax.rsqrt(jnp.mean(x * x, axis=-1, keepdims=True) + EPS) * gain


def _split_bf16(a):
    hi = a.astype(BF16)
    return hi, (a - hi.astype(F32)).astype(BF16)


def _drain(*gens):
    last = [None] * len(gens)
    live = list(range(len(gens)))
    while live:
        for g in list(live):
            try:
                last[g] = next(gens[g])
            except StopIteration:
                live.remove(g)
    return last


def _ada_kernel(c_ref, w_ref, b_ref, o_ref):
    o_ref[0] = jnp.dot(_silu(c_ref[...]), w_ref[0], preferred_element_type=F32, precision=HIGHEST) + b_ref[0]


def _ada_terms(cond, w_mod, b_mod):
    depth, d, n6 = w_mod.shape
    tn = n6 // 4
    return pl.pallas_call(
        _ada_kernel,
        grid=(depth, n6 // tn),
        in_specs=[pl.BlockSpec((SUBLANES, d), lambda l, j: (0, 0)),
                  pl.BlockSpec((1, d, tn), lambda l, j: (l, 0, j)),
                  pl.BlockSpec((1, 1, tn), lambda l, j: (l, 0, j))],
        out_specs=pl.BlockSpec((1, SUBLANES, tn), lambda l, j: (l, 0, j)),
        out_shape=jax.ShapeDtypeStruct((depth, SUBLANES, n6), F32),
        compiler_params=_cparams(("arbitrary", "arbitrary")),
        name="ada",
    )(cond, w_mod, b_mod.reshape(depth, 1, n6))


def _modulated(x, mod_ref, gain_ref, shift_row):
    y = _rms(x, gain_ref[...])
    return y * (1.0 + mod_ref[0, shift_row + 1:shift_row + 2, :]) + mod_ref[0, shift_row:shift_row + 1, :]


def _gated_residual(x, y, mod_ref, gain_ref, gate_row):
    return x + mod_ref[0, gate_row:gate_row + 1, :] * _rms(y, gain_ref[...])


def _head_rms(x, gain_row, head_dim):
    outs = []
    for s in range(x.shape[1] // LANES):
        xs = x[:, s * LANES:(s + 1) * LANES]
        lane = lax.broadcasted_iota(jnp.int32, xs.shape, 1)
        sq = xs * xs
        scale = jnp.zeros_like(xs)
        for part in range(LANES // head_dim):
            m = jnp.logical_and(lane >= part * head_dim, lane < (part + 1) * head_dim)
            ms = jnp.sum(jnp.where(m, sq, 0.0), axis=-1, keepdims=True) * (1.0 / head_dim)
            scale = jnp.where(m, lax.rsqrt(ms + EPS), scale)
        outs.append(xs * scale)
    y = outs[0] if len(outs) == 1 else jnp.concatenate(outs, axis=1)
    return y * gain_row


def _rope(x, cos, sin_lo, sin_hi):
    width = x.shape[1]
    fwd = pltpu.roll(x, width - ROPE_AXIS_PAIRS, axis=1)
    back = pltpu.roll(x, ROPE_AXIS_PAIRS, axis=1)
    return x * cos + fwd * sin_lo + back * sin_hi


def _hyb_inproj_kernel(x_ref, xp_ref, xn_ref, mod_ref, g_ref, w_ref, cw_ref, kg_ref, cos_ref, slo_ref, shi_ref,
                       qkv_ref, z_ref, qb_ref, ba_ref, ko_ref, vo_ref, *, tiles_per_seq):
    pos = pl.program_id(0) % tiles_per_seq
    tm = x_ref.shape[0]
    halo = PROJ_HALO
    qkv_w = 3 * A_WIDTH
    a_prev = jnp.where(pos > 0, _modulated(xp_ref[...], mod_ref, g_ref, 0), 0.0).astype(BF16)
    a_next = jnp.where(pos < tiles_per_seq - 1, _modulated(xn_ref[...], mod_ref, g_ref, 0), 0.0).astype(BF16)
    a = _modulated(x_ref[...], mod_ref, g_ref, 0).astype(BF16)
    a_ext = jnp.concatenate([a_prev, a, a_next], axis=0)

    off_z = qkv_w
    off_q = off_z + A_WIDTH
    off_k = off_q + B_WIDTH
    off_v = off_k + B_KV_WIDTH
    off_ba = off_v + B_KV_WIDTH

    def gate_proj():
        z_ref[...] = _dot(a, w_ref[:, off_z:off_q])

    def query_proj():
        qb_ref[...] = _dot(a, w_ref[:, off_q:off_k])

    def key_value_proj():
        kb = _dot(a, w_ref[:, off_k:off_v])
        vb = _dot(a, w_ref[:, off_v:off_ba])
        k = _rope(_head_rms(kb, kg_ref[...], B_HEAD_DIM), cos_ref[...], slo_ref[...], shi_ref[...]).astype(BF16)
        vt = vb.T.astype(BF16)
        for hh in range(B_KV_HEADS):
            ko_ref[0, hh] = k[:, hh * B_HEAD_DIM:(hh + 1) * B_HEAD_DIM]
            vo_ref[0, hh, :B_HEAD_DIM, :] = vt[hh * B_HEAD_DIM:(hh + 1) * B_HEAD_DIM, :]
            vo_ref[0, hh, B_HEAD_DIM:, :] = jnp.ones((ATTN_VT_ROWS - B_HEAD_DIM, tm), BF16)

    def logit_proj():
        ba_ref[...] = _dot(a, w_ref[:, off_ba:off_ba + A_HEADS * LANES])

    others = [gate_proj, query_proj, key_value_proj, logit_proj]
    ext_rows = tm + 2 * halo
    pair = 2 * A_HEAD_DIM
    n_pairs = qkv_w // pair
    ext_next = _dot(a_ext, w_ref[:, :pair])
    for cb in range(n_pairs):
        cols = slice(cb * pair, (cb + 1) * pair)
        ext = ext_next
        if cb + 1 < n_pairs:
            ext_next = _dot(a_ext, w_ref[:, (cb + 1) * pair:(cb + 2) * pair])
        if cb < len(others):
            others[cb]()
        w = cw_ref[:, cols]
        acc = jnp.zeros((tm, pair), F32)
        for tap in range(SHORT_CONV_W):
            shift = (SHORT_CONV_W // 2 - tap) % ext_rows
            rolled = pltpu.roll(ext, shift, axis=0) if shift else ext
            acc = acc + rolled[halo:halo + tm, :] * w[tap:tap + 1, :]
        y = _silu(acc)
        for part in range(2):
            head = 2 * cb + part
            yh = y[:, part * A_HEAD_DIM:(part + 1) * A_HEAD_DIM]
            if head < 2 * A_HEADS:
                yh = yh * lax.rsqrt(jnp.sum(yh * yh, axis=-1, keepdims=True) + EPS)
            if head < A_HEADS:
                yh = yh * (A_HEAD_DIM ** -0.5)
            qkv_ref[:, head * A_HEAD_DIM:(head + 1) * A_HEAD_DIM] = yh


def _hyb_inproj(x2, mod, gain, w, conv_w, k_gain, cos, slo, shi, bsz, rows_per_seq, tm):
    n, d = x2.shape
    tiles_per_seq = rows_per_seq // tm
    hb = tm // PROJ_HALO
    n_halo = n // PROJ_HALO
    f32_out = lambda wd: (pl.BlockSpec((tm, wd), lambda i: (i, 0)), jax.ShapeDtypeStruct((n, wd), F32))
    outs = [f32_out(3 * A_WIDTH), f32_out(A_WIDTH), f32_out(B_WIDTH), f32_out(A_HEADS * LANES),
            (pl.BlockSpec((1, B_KV_HEADS, tm, B_HEAD_DIM), lambda i: (i // tiles_per_seq, 0, i % tiles_per_seq, 0)),
             jax.ShapeDtypeStruct((bsz, B_KV_HEADS, rows_per_seq, B_HEAD_DIM), BF16)),
            (pl.BlockSpec((1, B_KV_HEADS, ATTN_VT_ROWS, tm), lambda i: (i // tiles_per_seq, 0, 0, i % tiles_per_seq)),
             jax.ShapeDtypeStruct((bsz, B_KV_HEADS, ATTN_VT_ROWS, rows_per_seq), BF16))]
    tab = pl.BlockSpec((tm, B_KV_WIDTH), lambda i: (i % tiles_per_seq, 0))
    return pl.pallas_call(
        functools.partial(_hyb_inproj_kernel, tiles_per_seq=tiles_per_seq),
        grid=(n // tm,),
        in_specs=[pl.BlockSpec((tm, d), lambda i: (i, 0)),
                  pl.BlockSpec((PROJ_HALO, d), lambda i: (jnp.maximum(i * hb - 1, 0), 0)),
                  pl.BlockSpec((PROJ_HALO, d), lambda i: (jnp.minimum((i + 1) * hb, n_halo - 1), 0)),
                  pl.BlockSpec((1, 6, d), lambda i: (i // tiles_per_seq if mod.shape[0] > 1 else 0, 0, 0)),
                  pl.BlockSpec((1, d), lambda i: (0, 0)),
                  _resident(w.shape),
                  _resident(conv_w.shape),
                  pl.BlockSpec((1, B_KV_WIDTH), lambda i: (0, 0)), tab, tab, tab],
        out_specs=[o[0] for o in outs],
        out_shape=[o[1] for o in outs],
        compiler_params=_cparams(("arbitrary",)),
        name="inproj",
    )(x2, x2, x2, mod, gain, w, conv_w, k_gain, cos, slo, shi)


def _unit_tri_inverse_steps(l_mats, i, j):
    eye = (i == j).astype(F32)
    same16 = jnp.right_shift(i, 4) == jnp.right_shift(j, 4)
    same32 = jnp.right_shift(i, 5) == jnp.right_shift(j, 5)
    off32 = jnp.logical_and(same32, jnp.logical_not(same16))
    b = lambda a: a.astype(BF16)
    each = lambda f, *ls: [f(*xs) for xs in zip(*ls)]
    d1 = each(lambda l: b(jnp.where(same16, l, 0.0)), l_mats)
    p = each(lambda d: eye - d.astype(F32), d1)
    dk = d1
    for _ in range(3):
        dk = each(lambda d: b(_dot(d, d)), dk)
        yield None
        p = each(lambda pp, d: pp + _dot(b(pp), d), p, dk)
        yield None
    for sel in (off32, jnp.logical_not(same32)):
        cm = each(lambda l: b(jnp.where(sel, l, 0.0)), l_mats)
        pb = each(b, p)
        inner = each(lambda c, q: b(_dot(c, q)), cm, pb)
        yield None
        p = each(lambda pp, q, m: pp - _dot(q, m), p, pb, inner)
        yield None
    l_split = each(_split_bf16, l_mats)
    t_split = each(_split_bf16, p)
    resid = each(lambda t0, ls, ts: eye - t0 - (_dot(ls[0], ts[0]) + _dot(ls[0], ts[1]) + _dot(ls[1], ts[0])),
                 p, l_split, t_split)
    yield None
    yield each(lambda t0, ts, r: t0 + _dot(ts[0], b(r)), p, t_split, resid)


def _delta_kernel(qf_ref, kf_ref, vf_ref, qr_ref, kr_ref, vr_ref, qc_ref, kc_ref, vc_ref,
                  baf_ref, bar_ref, bac_ref, coef_ref, of_ref, ob_ref,
                  u_s, w_s, kt_s, qd_s, in_s, el_s, l_s, rhs_s, state):
    step = pl.program_id(1)
    is_ctx = step == 0
    front_slot = lax.rem(step, 3)
    scan_slot = lax.rem(step + 1, 3)
    lr_write = step % 2
    lr_read = 1 - lr_write
    uw_write = lr_read
    uw_read = lr_write
    heads = [slice(hh * A_HEAD_DIM, (hh + 1) * A_HEAD_DIM) for hh in range(A_HEADS)]
    chunks = [slice(c * CHUNK, (c + 1) * CHUNK) for c in range(DELTA_BLOCK)]

    @pl.when(step == 0)
    def _():
        for scr in (kt_s, qd_s, in_s, el_s):
            scr[1] = jnp.zeros(scr.shape[1:], scr.dtype)
            scr[2] = jnp.zeros(scr.shape[1:], scr.dtype)
        u_s[0] = jnp.zeros(u_s.shape[1:], u_s.dtype)
        w_s[0] = jnp.zeros(w_s.shape[1:], w_s.dtype)
        l_s[1] = jnp.zeros(l_s.shape[1:], l_s.dtype)
        rhs_s[1] = jnp.zeros(rhs_s.shape[1:], rhs_s.dtype)
        state[...] = jnp.zeros(state.shape, F32)

    chains = [(d, hh) for d in range(2) for hh in range(A_HEADS)]
    s_mats = [state[d * A_HEADS + hh] for d, hh in chains]
    out_refs = (of_ref, ob_ref)

    def scan_substep(t, s_mats):
        where = []
        for d, hh in chains:
            c = t if d == 0 else DELTA_BLOCK - 1 - t
            where.append((d, c, chunks[c], heads[hh]))
        ws = [_dot(jnp.concatenate([w_s[uw_read, d, r, ln], qd_s[scan_slot, d, r, ln]], axis=0), s.astype(BF16))
              for (d, c, r, ln), s in zip(where, s_mats)]
        v_new = [(u_s[uw_read, d, r, ln] - x[:CHUNK]).astype(BF16) for (d, c, r, ln), x in zip(where, ws)]
        for (d, c, r, ln), x, vn in zip(where, ws, v_new):
            out_refs[d][r, ln] = x[CHUNK:] + _dot(in_s[scan_slot, d, r, ln][:, :CHUNK], vn)
        return [s * el_s[scan_slot, d, c * SUBLANES:c * SUBLANES + 1, ln] + _dot_tn(kt_s[scan_slot, d, r, ln], vn)
                for (d, c, r, ln), s, vn in zip(where, s_mats, v_new)]

    def per_head(main_ref, ctx_ref):
        x = jnp.where(is_ctx, ctx_ref[...], main_ref[...])
        return [x[:, ln] for ln in heads]

    q_h = (per_head(qf_ref, qc_ref), per_head(qr_ref, qc_ref))
    k_h = (per_head(kf_ref, kc_ref), per_head(kr_ref, kc_ref))
    v_h = (per_head(vf_ref, vc_ref), per_head(vr_ref, vc_ref))

    lane = lax.broadcasted_iota(jnp.int32, (1, LANES), 1)
    raw_d = (jnp.where(is_ctx, bac_ref[...], baf_ref[...]), jnp.where(is_ctx, bac_ref[...], bar_ref[...]))
    bg = {}
    for hh, ln in enumerate(heads):
        coef = coef_ref[hh]
        neg_a = -jnp.exp(jnp.where(lane == 2, coef[0:1, :], coef[1:2, :]))
        dtb = jnp.where(lane == 2, coef[2:3, :], coef[3:4, :])
        for d in range(2):
            raw = raw_d[d][:, ln]
            xg = raw + dtb
            softplus = jnp.maximum(xg, 0.0) + jnp.log(1.0 + jnp.exp(-jnp.abs(xg)))
            bg[d, hh] = jnp.where(lane < 2, jax.nn.sigmoid(raw), neg_a * softplus)

    i_idx = lax.broadcasted_iota(jnp.int32, (CHUNK, CHUNK), 0)
    j_idx = lax.broadcasted_iota(jnp.int32, (CHUNK, CHUNK), 1)
    i_w = lax.broadcasted_iota(jnp.int32, (CHUNK, LANES), 0)
    j_w = lax.broadcasted_iota(jnp.int32, (CHUNK, LANES), 1)
    in_chunk = j_w < CHUNK
    ones_rows = jnp.ones((2 * SUBLANES, CHUNK), BF16)

    def front_steps(d):
        incl = (i_idx >= j_idx) if d == 0 else (i_idx <= j_idx)
        strict = (i_idx > j_idx) if d == 0 else (i_idx < j_idx)
        incl_w = jnp.logical_and(in_chunk, (i_w >= j_w) if d == 0 else (i_w <= j_w))
        seen_w = jnp.logical_and(in_chunk, (i_w <= j_w) if d == 0 else (i_w >= j_w))
        seen_cat = jnp.concatenate([seen_w] * DELTA_BLOCK, axis=1)
        incl_b = incl.astype(BF16)
        for hh, ln in enumerate(heads):
            g_cat = jnp.concatenate([jnp.broadcast_to(bg[d, hh][r, 2 + d:3 + d], (CHUNK, LANES)) for r in chunks],
                                    axis=1)
            g_hi = g_cat.astype(BF16)
            g_rest = g_cat - g_hi.astype(F32)
            g_mid = g_rest.astype(BF16)
            pieces = (g_hi, g_mid, (g_rest - g_mid.astype(F32)).astype(BF16))
            gc_all = sum(_dot(incl_b, p) for p in pieces)
            gr_all = sum(_dot(ones_rows, jnp.where(seen_cat, p, jnp.zeros_like(p))) for p in pieces)
            for c, r in enumerate(chunks):
                q, k, v = q_h[d][hh][r], k_h[d][hh][r], v_h[d][hh][r]
                kbf = k.astype(BF16)
                prod = _dot_nt(jnp.concatenate([kbf, q.astype(BF16)], axis=0),
                               jnp.concatenate([kbf, jnp.zeros_like(kbf)], axis=0))
                beta = bg[d, hh][r, d:d + 1]
                gc_w = gc_all[:, c * LANES:(c + 1) * LANES]
                gr_w = jnp.broadcast_to(gr_all[0:1, c * LANES:(c + 1) * LANES], (CHUNK, LANES))
                decay = jnp.where(incl_w, jnp.exp(jnp.minimum(gc_w - gr_w, 0.0)), 0.0)
                g_last = gc_w[CHUNK - 1:CHUNK, :] if d == 0 else gc_w[0:1, :]
                egc = jnp.exp(gc_w)
                n = (d * A_HEADS + hh) * DELTA_BLOCK + c
                l_s[lr_write, n] = jnp.where(strict, (prod[:CHUNK] * beta * decay)[:, :CHUNK], 0.0)
                rhs_s[lr_write, n] = jnp.concatenate([v * beta, k * beta * egc], axis=1).astype(BF16)
                kt_s[front_slot, d, r, ln] = (k * jnp.exp(g_last - gc_w)).astype(BF16)
                qd_s[front_slot, d, r, ln] = (q * egc).astype(BF16)
                in_s[front_slot, d, r, ln] = (prod[CHUNK:] * decay).astype(BF16)
                el_s[front_slot, d, c * SUBLANES:(c + 1) * SUBLANES, ln] = jnp.broadcast_to(jnp.exp(g_last),
                                                                                            (SUBLANES, LANES))
                yield None

    def back_steps():
        n_inst = 2 * A_HEADS * DELTA_BLOCK
        inverse = None
        for inverse in _unit_tri_inverse_steps([l_s[lr_read, n] for n in range(n_inst)], i_idx, j_idx):
            yield None
        t_split = [_split_bf16(t) for t in inverse]
        rhs = [rhs_s[lr_read, n] for n in range(n_inst)]
        sols = [_dot(t_hi, rr) + _dot(t_lo, rr) for (t_hi, t_lo), rr in zip(t_split, rhs)]
        for n, sol in enumerate(sols):
            d, hh, c = n // (A_HEADS * DELTA_BLOCK), (n // DELTA_BLOCK) % A_HEADS, n % DELTA_BLOCK
            u_s[uw_write, d, chunks[c], heads[hh]] = sol[:, :A_HEAD_DIM]
            w_s[uw_write, d, chunks[c], heads[hh]] = sol[:, A_HEAD_DIM:].astype(BF16)
        yield None

    def scan_steps(s_mats):
        for t in range(DELTA_BLOCK):
            s_mats = scan_substep(t, s_mats)
            yield s_mats
            yield s_mats
            yield s_mats

    _, _, _, s_mats = _drain(front_steps(0), front_steps(1), back_steps(), scan_steps(s_mats))
    for (d, hh), s in zip(chains, s_mats):
        state[d * A_HEADS + hh] = s


def _delta_mixer(qkv, qkv_c, ba, ba_c, coef, bsz, n_lat, n_ctx):
    assert n_ctx == DELTA_ROWS and n_lat % DELTA_ROWS == 0
    wd = A_WIDTH
    lat_blocks = n_lat // DELTA_ROWS
    n_steps = lat_blocks + 3

    def fwd_blk(b, s):
        return b * lat_blocks + jnp.clip(s - 1, 0, lat_blocks - 1)

    def bwd_blk(b, s):
        return b * lat_blocks + jnp.clip(lat_blocks - s, 0, lat_blocks - 1)

    def lat(blk_fn, kd):
        return pl.BlockSpec((DELTA_ROWS, wd), lambda b, s: (blk_fn(b, s), kd))

    def ctxb(kd):
        return pl.BlockSpec((n_ctx, wd), lambda b, s: (b, kd))

    in_specs = ([lat(fwd_blk, kd) for kd in range(3)] + [lat(bwd_blk, kd) for kd in range(3)]
                + [ctxb(kd) for kd in range(3)]
                + [lat(fwd_blk, 0), lat(bwd_blk, 0), ctxb(0),
                   pl.BlockSpec((A_HEADS, SUBLANES, LANES), lambda b, s: (0, 0, 0))])
    out_f = pl.BlockSpec((DELTA_ROWS, wd), lambda b, s: (b * lat_blocks + jnp.maximum(s - 3, 0), 0))
    out_b = pl.BlockSpec((DELTA_ROWS, wd), lambda b, s: (b * lat_blocks + lat_blocks - jnp.maximum(s - 2, 1), 0))
    shape = jax.ShapeDtypeStruct((bsz * n_lat, wd), F32)
    slots = lambda n, rows, dt: pltpu.VMEM((n, 2, rows, wd), dt)
    n_inst = 2 * A_HEADS * DELTA_BLOCK
    return pl.pallas_call(
        _delta_kernel,
        grid=(bsz, n_steps),
        in_specs=in_specs,
        out_specs=[out_f, out_b],
        out_shape=[shape, shape],
        scratch_shapes=[slots(2, DELTA_ROWS, F32), slots(2, DELTA_ROWS, BF16), slots(3, DELTA_ROWS, BF16),
                        slots(3, DELTA_ROWS, BF16), slots(3, DELTA_ROWS, BF16),
                        slots(3, DELTA_BLOCK * SUBLANES, F32),
                        pltpu.VMEM((2, n_inst, CHUNK, CHUNK), F32),
                        pltpu.VMEM((2, n_inst, CHUNK, 2 * A_HEAD_DIM), BF16),
                        pltpu.VMEM((2 * A_HEADS, A_HEAD_DIM, A_HEAD_DIM), F32)],
        compiler_params=_cparams(("arbitrary", "arbitrary")),
        name="delta",
    )(*([qkv] * 6), *([qkv_c] * 3), ba, ba, ba_c, coef)


def _attn_kernel(q_ref, g_ref, cos_ref, slo_ref, shi_ref, k_ref, vt_ref, o_ref):
    q = _head_rms(q_ref[...], g_ref[...], B_HEAD_DIM)
    q = _rope(q, cos_ref[...], slo_ref[...], shi_ref[...]) * (B_HEAD_DIM ** -0.5 * LOG2_E)
    tq = q.shape[0]
    qt = q.T.astype(BF16)
    n_keys = k_ref.shape[2]
    q_s = [qt[g * B_HEAD_DIM:(g + 1) * B_HEAD_DIM, c:c + ATTN_Q_COLS]
           for g in range(B_GROUP) for c in range(0, tq, ATTN_Q_COLS)]
    m = [jnp.full((1, ATTN_Q_COLS), -1e30, F32) for _ in q_s]
    acc = [jnp.zeros((ATTN_VT_ROWS, ATTN_Q_COLS), F32) for _ in q_s]

    def scores(kb):
        k_blk = k_ref[0, 0, kb:kb + ATTN_KEY_BLOCK, :]
        return [_dot(k_blk, qs) for qs in q_s]

    def value_update(acc, pending):
        alpha, p, kb = pending
        vt = vt_ref[0, 0, :, kb:kb + ATTN_KEY_BLOCK]
        return [a * al + _dot(vt, pp) for a, al, pp in zip(acc, alpha, p)]

    st_next = scores(0)
    pending = None
    for kb in range(0, n_keys, ATTN_KEY_BLOCK):
        st = st_next
        if kb + ATTN_KEY_BLOCK < n_keys:
            st_next = scores(kb + ATTN_KEY_BLOCK)
        m_new = [jnp.maximum(mo, jnp.max(s, axis=0, keepdims=True)) for mo, s in zip(m, st)]
        alpha = [jnp.exp2(mo - mn) for mo, mn in zip(m, m_new)]
        p = [jnp.exp2((s - mn).astype(BF16)) for s, mn in zip(st, m_new)]
        if pending is not None:
            acc = value_update(acc, pending)
        pending = (alpha, p, kb)
        m = m_new
    acc = value_update(acc, pending)
    outs = [a[:B_HEAD_DIM] / a[B_HEAD_DIM:B_HEAD_DIM + 1] for a in acc]
    n_col = tq // ATTN_Q_COLS
    o_ref[...] = jnp.concatenate([jnp.concatenate(outs[g * n_col:(g + 1) * n_col], axis=1)
                                  for g in range(B_GROUP)], axis=0).T


def _attention(qb, gain, cos, slo, shi, k_hm, vt_hm, n_lat, tq):
    n, _ = qb.shape
    bsz, _, s_len, _ = k_hm.shape
    assert s_len % ATTN_KEY_BLOCK == 0 and tq % ATTN_Q_COLS == 0
    gw = B_GROUP * B_HEAD_DIM
    nq = n_lat // tq
    kv = pl.BlockSpec((1, 1, s_len, B_HEAD_DIM), lambda b, kh, i: (b, kh, 0, 0))
    vts = pl.BlockSpec((1, 1, ATTN_VT_ROWS, s_len), lambda b, kh, i: (b, kh, 0, 0))
    tab = pl.BlockSpec((tq, gw), lambda b, kh, i: (i, 0))
    return pl.pallas_call(
        _attn_kernel,
        grid=(bsz, B_KV_HEADS, nq),
        in_specs=[pl.BlockSpec((tq, gw), lambda b, kh, i: (b * nq + i, kh)),
                  pl.BlockSpec((1, gw), lambda b, kh, i: (0, 0)), tab, tab, tab, kv, vts],
        out_specs=pl.BlockSpec((tq, gw), lambda b, kh, i: (b * nq + i, kh)),
        out_shape=jax.ShapeDtypeStruct((n, B_WIDTH), F32),
        compiler_params=_cparams(("arbitrary", "arbitrary", "arbitrary")),
        name="attn",
    )(qb, gain, cos, slo, shi, k_hm, vt_hm)


def _outproj_kernel(of_ref, ob_ref, z_ref, on_ref, yb_ref, x_ref, mod_ref, g_ref, wa_ref, wb_ref, o_ref):
    o = of_ref[...] + ob_ref[...]
    z = z_ref[...]
    parts = []
    for hh in range(A_HEADS):
        sl = slice(hh * A_HEAD_DIM, (hh + 1) * A_HEAD_DIM)
        parts.append((_rms(o[:, sl], on_ref[...]) * _silu(z[:, sl])).astype(BF16))
    ya = jnp.concatenate(parts, axis=1)
    y = _dot(ya, wa_ref[...]) + _dot(yb_ref[...].astype(BF16), wb_ref[...])
    o_ref[...] = _gated_residual(x_ref[...], y, mod_ref, g_ref, 2)


def _outproj(o_f, o_b, z, out_norm, yb, x2, mod, gain, wa, wb, rows_per_mod, tm):
    n, d = x2.shape
    bpm = rows_per_mod // tm
    return pl.pallas_call(
        _outproj_kernel,
        grid=(n // tm,),
        in_specs=[pl.BlockSpec((tm, o_f.shape[1]), lambda i: (i, 0)),
                  pl.BlockSpec((tm, o_b.shape[1]), lambda i: (i, 0)),
                  pl.BlockSpec((tm, z.shape[1]), lambda i: (i, 0)),
                  pl.BlockSpec((1, A_HEAD_DIM), lambda i: (0, 0)),
                  pl.BlockSpec((tm, yb.shape[1]), lambda i: (i, 0)),
                  pl.BlockSpec((tm, d), lambda i: (i, 0)),
                  pl.BlockSpec((1, 6, d), lambda i: (i // bpm, 0, 0)),
                  pl.BlockSpec((1, d), lambda i: (0, 0)),
                  _resident(wa.shape),
                  _resident(wb.shape)],
        out_specs=pl.BlockSpec((tm, d), lambda i: (i, 0)),
        out_shape=jax.ShapeDtypeStruct((n, d), F32),
        compiler_params=_cparams(("arbitrary",)),
        name="outproj",
    )(o_f, o_b, z, out_norm, yb, x2, mod, gain, wa, wb)


def _ffn_kernel(x_ref, mod_ref, gpre_ref, gpost_ref, wi_ref, wo_ref, o_ref, *, hidden, hc):
    x = x_ref[...]
    a = _modulated(x, mod_ref, gpre_ref, 3).astype(BF16)
    acc = jnp.zeros(x.shape, F32)
    for c in range(hidden // hc):
        gate = _dot(a, wi_ref[:, c * hc:(c + 1) * hc])
        up = _dot(a, wi_ref[:, hidden + c * hc:hidden + (c + 1) * hc])
        acc = acc + _dot((_silu(gate) * up).astype(BF16), wo_ref[c * hc:(c + 1) * hc, :])
    o_ref[...] = _gated_residual(x, acc, mod_ref, gpost_ref, 5)


def _ffn(x2, mod, gpre, gpost, wi, wo, rows_per_mod, tm):
    n, d = x2.shape
    hidden = wo.shape[0]
    bpm = rows_per_mod // tm
    return pl.pallas_call(
        functools.partial(_ffn_kernel, hidden=hidden, hc=2 * LANES),
        grid=(n // tm,),
        in_specs=[pl.BlockSpec((tm, d), lambda i: (i, 0)),
                  pl.BlockSpec((1, 6, d), lambda i: (i // bpm, 0, 0)),
                  pl.BlockSpec((1, d), lambda i: (0, 0)),
                  pl.BlockSpec((1, d), lambda i: (0, 0)),
                  _resident(wi.shape),
                  _resident(wo.shape)],
        out_specs=pl.BlockSpec((tm, d), lambda i: (i, 0)),
        out_shape=jax.ShapeDtypeStruct((n, d), F32),
        compiler_params=_cparams(("arbitrary",)),
        name="ffn",
    )(x2, mod, gpre, gpost, wi, wo)


def _confin_kernel(x_ref, mod_ref, g_ref, w_ref, b_ref, o_ref, *, width):
    a = _modulated(x_ref[...], mod_ref, g_ref, 0).astype(BF16)
    val = _dot(a, w_ref[:, :width]) + b_ref[:, :width]
    gate = _dot(a, w_ref[:, width:]) + b_ref[:, width:]
    o_ref[...] = val * jax.nn.sigmoid(gate)


def _confin(x2, mod, gain, w, b, rows_per_mod, tm):
    n, d = x2.shape
    width = w.shape[1] // 2
    bpm = rows_per_mod // tm
    return pl.pallas_call(
        functools.partial(_confin_kernel, width=width),
        grid=(n // tm,),
        in_specs=[pl.BlockSpec((tm, d), lambda i: (i, 0)),
                  pl.BlockSpec((1, 6, d), lambda i: (i // bpm, 0, 0)),
                  pl.BlockSpec((1, d), lambda i: (0, 0)),
                  _resident(w.shape),
                  pl.BlockSpec((1, 2 * width), lambda i: (0, 0))],
        out_specs=pl.BlockSpec((tm, width), lambda i: (i, 0)),
        out_shape=jax.ShapeDtypeStruct((n, width), F32),
        compiler_params=_cparams(("arbitrary",)),
        name="confin",
    )(x2, mod, gain, w, b)


def _confout_kernel(u_ref, up_ref, un_ref, x_ref, mod_ref, g_ref, dww_ref, dwb_ref, lng_ref, lnb_ref,
                    w_ref, b_ref, o_ref, ext, conv, shifted, *, tm, tiles_per_seq):
    pos = pl.program_id(0) % tiles_per_seq
    width = u_ref.shape[1]
    halo = CONF_HALO
    ext[pl.ds(halo, tm), :] = u_ref[...]
    ext[0:halo, :] = jnp.where(pos == 0, 0.0, up_ref[...])
    ext[pl.ds(halo + tm, halo), :] = jnp.where(pos == tiles_per_seq - 1, 0.0, un_ref[...])

    rb = 64
    pad = CONF_KERNEL // 2
    copy_rows = tm + 2 * halo - SUBLANES

    def col_body(c, carry):
        cs = pl.ds(pl.multiple_of(c * LANES, LANES), LANES)
        w = dww_ref[:, cs]
        for s in range(SUBLANES):
            shifted[s, :, :] = ext[pl.ds(s, copy_rows), cs]
        for r in range(tm // rb):
            acc = jnp.zeros((rb, LANES), F32)
            for tap in range(CONF_KERNEL):
                off = halo - pad + tap
                acc = acc + shifted[off % SUBLANES, pl.ds(r * rb + off - off % SUBLANES, rb), :] * w[tap:tap + 1, :]
            conv[pl.ds(r * rb, rb), cs] = acc
        return carry

    lax.fori_loop(0, width // LANES, col_body, 0)

    y = conv[...] + dwb_ref[...]
    mu = jnp.mean(y, axis=-1, keepdims=True)
    yc = y - mu
    var = jnp.mean(yc * yc, axis=-1, keepdims=True)
    y = _silu(yc * lax.rsqrt(var + EPS) * lng_ref[...] + lnb_ref[...])
    out = _dot(y.astype(BF16), w_ref[...]) + b_ref[...]
    o_ref[...] = _gated_residual(x_ref[...], out, mod_ref, g_ref, 2)


def _confout(u, x2, mod, gain, dww, dwb, lng, lnb, w, b, rows_per_mod, tm):
    n, d = x2.shape
    width = u.shape[1]
    tiles_per_seq = rows_per_mod // tm
    hb = tm // CONF_HALO
    n_halo_blocks = n // CONF_HALO
    vec = lambda wd: pl.BlockSpec((1, wd), lambda i: (0, 0))
    return pl.pallas_call(
        functools.partial(_confout_kernel, tm=tm, tiles_per_seq=tiles_per_seq),
        grid=(n // tm,),
        in_specs=[pl.BlockSpec((tm, width), lambda i: (i, 0)),
                  pl.BlockSpec((CONF_HALO, width), lambda i: (jnp.maximum(i * hb - 1, 0), 0)),
                  pl.BlockSpec((CONF_HALO, width), lambda i: (jnp.minimum((i + 1) * hb, n_halo_blocks - 1), 0)),
                  pl.BlockSpec((tm, d), lambda i: (i, 0)),
                  pl.BlockSpec((1, 6, d), lambda i: (i // tiles_per_seq, 0, 0)),
                  vec(d),
                  pl.BlockSpec(dww.shape, lambda i: (0, 0)),
                  vec(width), vec(width), vec(width),
                  _resident(w.shape),
                  vec(d)],
        out_specs=pl.BlockSpec((tm, d), lambda i: (i, 0)),
        out_shape=jax.ShapeDtypeStruct((n, d), F32),
        scratch_shapes=[pltpu.VMEM((tm + 2 * CONF_HALO, width), F32), pltpu.VMEM((tm, width), F32),
                        pltpu.VMEM((SUBLANES, tm + 2 * CONF_HALO - SUBLANES, LANES), F32)],
        compiler_params=_cparams(("arbitrary",)),
        name="confout",
    )(u, u, u, x2, mod, gain, dww, dwb, lng, lnb, w, b)


def _rope_tables(n_tokens, reps):
    rows = n_tokens // GRID_W
    row = jnp.broadcast_to(jnp.arange(rows, dtype=F32)[:, None], (rows, GRID_W)).reshape(n_tokens)
    col = jnp.broadcast_to(jnp.arange(GRID_W, dtype=F32)[None, :], (rows, GRID_W)).reshape(n_tokens)
    inv_freq = ROPE_THETA ** (-jnp.arange(ROPE_AXIS_PAIRS, dtype=F32) / ROPE_AXIS_PAIRS)
    ang_r = row[:, None] * inv_freq
    ang_c = col[:, None] * inv_freq
    ang = jnp.concatenate([ang_r, ang_r, ang_c, ang_c], axis=-1)
    cos, sin = jnp.cos(ang), jnp.sin(ang)
    first_half = (jnp.arange(B_HEAD_DIM) % (2 * ROPE_AXIS_PAIRS)) < ROPE_AXIS_PAIRS
    sin_lo = jnp.where(first_half, -sin, 0.0)
    sin_hi = jnp.where(first_half, 0.0, sin)
    tile = lambda t: jnp.tile(t, (1, reps))
    return tile(cos), tile(sin_lo), tile(sin_hi)


def _hybrid_in_weight(w_in):
    off_z = 3 * A_WIDTH
    off_ba = off_z + A_WIDTH
    off_q = off_ba + 4 * A_HEADS
    off_k = off_q + B_WIDTH
    off_v = off_k + B_KV_WIDTH
    d = w_in.shape[0]
    ba = w_in[:, off_ba:off_q].reshape(d, 2, 2, A_HEADS)
    ba = ba.transpose(0, 3, 1, 2).reshape(d, A_HEADS, 4)
    ba = jnp.pad(ba, ((0, 0), (0, 0), (0, LANES - 4))).reshape(d, A_HEADS * LANES)
    w = jnp.concatenate([w_in[:, :off_z], w_in[:, off_z:off_ba], w_in[:, off_q:off_k], w_in[:, off_k:off_v],
                         w_in[:, off_v:], ba], axis=1)
    return w.astype(BF16)


def _row(v):
    return v.reshape(1, -1)


def kernel(x, c, ctx, c_ctx, w_mod, b_mod, g_mix_pre, g_mix_post, g_ffn_pre, g_ffn_post, w_ffn_in, w_ffn_out,
           hyb_w_in, hyb_conv_w, hyb_a_log, hyb_dt_bias, hyb_out_norm, hyb_q_norm, hyb_k_norm, hyb_w_out,
           conf_w_in, conf_b_in, conf_dw_w, conf_dw_b, conf_ln_g, conf_ln_b, conf_w_out, conf_b_out):
    bsz, n_lat, d = x.shape
    n_ctx = ctx.shape[1]
    depth = w_mod.shape[0]
    n = bsz * n_lat
    tm = ROW_TILE

    cond = jnp.zeros((SUBLANES, d), F32).at[:bsz].set(c).at[bsz].set(c_ctx)
    mods = _ada_terms(cond, w_mod, b_mod).reshape(depth, SUBLANES, 6, d)

    h = x.reshape(n, d)
    hc = ctx.reshape(bsz * n_ctx, d)
    for layer in range(depth):
        idx = layer // 2
        mod = mods[layer, :bsz]
        mod_ctx = mods[layer, bsz:bsz + 1]
        if layer % 2 == 0:
            w_in = _hybrid_in_weight(hyb_w_in[idx])
            gpre = _row(g_mix_pre[layer])
            k_gain = _row(jnp.tile(hyb_k_norm[idx], B_KV_HEADS))
            cos_k, slo_k, shi_k = _rope_tables(n_lat, B_KV_HEADS)
            no_pos = (jnp.ones((n_ctx, B_KV_WIDTH), F32), jnp.zeros((n_ctx, B_KV_WIDTH), F32),
                      jnp.zeros((n_ctx, B_KV_WIDTH), F32))
            qkv, z, qb, ba, k_lat, vt_lat = _hyb_inproj(h, mod, gpre, w_in, hyb_conv_w[idx], k_gain,
                                                        cos_k, slo_k, shi_k, bsz, n_lat, tm)
            qkv_c, _, _, ba_c, k_ctx, vt_ctx = _hyb_inproj(hc, mod_ctx, gpre, w_in, hyb_conv_w[idx], k_gain,
                                                           *no_pos, bsz, n_ctx, n_ctx)

            coef = jnp.concatenate([hyb_a_log[idx], hyb_dt_bias[idx]], axis=0)
            coef = jnp.pad(coef.T[:, :, None], ((0, 0), (0, SUBLANES - 4), (0, 0)))
            coef = jnp.broadcast_to(coef, (A_HEADS, SUBLANES, LANES))
            o_f, o_b = _delta_mixer(qkv, qkv_c, ba, ba_c, coef, bsz, n_lat, n_ctx)

            k_hm = jnp.concatenate([k_lat, k_ctx], axis=2)
            vt_hm = jnp.concatenate([vt_lat, vt_ctx], axis=3)
            cos_q, slo_q, shi_q = _rope_tables(n_lat, B_GROUP)
            yb = _attention(qb, _row(jnp.tile(hyb_q_norm[idx], B_GROUP)), cos_q, slo_q, shi_q,
                            k_hm, vt_hm, n_lat, ATTN_Q_TILE)

            w_out = hyb_w_out[idx].astype(BF16)
            h = _outproj(o_f, o_b, z, _row(hyb_out_norm[idx]), yb, h, mod, _row(g_mix_post[layer]),
                         w_out[:A_WIDTH], w_out[A_WIDTH:], n_lat, tm)
        else:
            u = _confin(h, mod, _row(g_mix_pre[layer]), conf_w_in[idx].astype(BF16), _row(conf_b_in[idx]),
                        n_lat, tm)
            h = _confout(u, h, mod, _row(g_mix_post[layer]), conf_dw_w[idx], _row(conf_dw_b[idx]),
                         _row(conf_ln_g[idx]), _row(conf_ln_b[idx]), conf_w_out[idx].astype(BF16),
                         _row(conf_b_out[idx]), n_lat, CONF_TILE)
        h = _ffn(h, mod, _row(g_ffn_pre[layer]), _row(g_ffn_post[layer]), w_ffn_in[layer].astype(BF16),
                 w_ffn_out[layer].astype(BF16), n_lat, tm)
        assert not any(j % 2 == 0 for j in range(layer + 1, depth)), "context advance not implemented"
    return h.reshape(bsz, n_lat, d)
```

```python
import functools
import math

import jax
import jax.numpy as jnp
from jax import lax
from jax.experimental import pallas as pl
from jax.experimental.pallas import tpu as pltpu

F32 = jnp.float32
BF16 = jnp.bfloat16

EPS = 1e-6
GRID_W = 64
ROPE_THETA = 10000.0
A_HEADS = 4
A_HEAD_DIM = 128
A_WIDTH = A_HEADS * A_HEAD_DIM
SHORT_CONV_W = 5
CHUNK = 64
B_Q_HEADS = 8
B_KV_HEADS = 2
B_HEAD_DIM = 64
B_GROUP = B_Q_HEADS // B_KV_HEADS
B_WIDTH = B_Q_HEADS * B_HEAD_DIM
B_KV_WIDTH = B_KV_HEADS * B_HEAD_DIM
ROPE_AXIS_PAIRS = B_HEAD_DIM // 4
CONF_KERNEL = 31
LOG2_E = math.log2(math.e)

LANES = 128
SUBLANES = 8
VMEM_LIMIT = 56 * 1024 * 1024
ROW_TILE = 1024
CONF_TILE = 512
CONF_HALO = 16
PROJ_HALO = 16
DELTA_BLOCK = 4
DELTA_ROWS = DELTA_BLOCK * CHUNK
ATTN_Q_TILE = 512
ATTN_KEY_BLOCK = 256
ATTN_Q_COLS = 256
ATTN_VT_ROWS = B_HEAD_DIM + 16


def _cparams(sem):
    return pltpu.CompilerParams(dimension_semantics=sem, vmem_limit_bytes=VMEM_LIMIT)


def _resident(shape):
    return pl.BlockSpec(shape, lambda i: (0, 0), pipeline_mode=pl.Buffered(1))


def _silu(x):
    return x * jax.nn.sigmoid(x)


def _dot(a, b):
    return jnp.dot(a, b, preferred_element_type=F32)


def _dot_nt(a, b):
    return lax.dot_general(a, b, (((1,), (1,)), ((), ())), preferred_element_type=F32)


def _dot_tn(a, b):
    return lax.dot_general(a, b, (((0,), (0,)), ((), ())), preferred_element_type=F32)


def _rms(x, gain):
    return x * lax.rsqrt(jnp.mean(x * x, axis=-1, keepdims=True) + EPS) * gain


def _split_bf16(a):
    hi = a.astype(BF16)
    return hi, (a - hi.astype(F32)).astype(BF16)


def _drain(*gens):
    last = [None] * len(gens)
    live = list(range(len(gens)))
    while live:
        for g in list(live):
            try:
                last[g] = next(gens[g])
            except StopIteration:
                live.remove(g)
    return last


def _ada_kernel(c_ref, w_ref, b_ref, o_ref):
    s_hi, s_lo = _split_bf16(_silu(c_ref[...]))
    w_hi, w_lo = _split_bf16(w_ref[0])
    o_ref[0] = _dot(s_hi, w_hi) + _dot(s_hi, w_lo) + _dot(s_lo, w_hi) + b_ref[0]


def _ada_terms(cond, w_mod, b_mod):
    depth, d, n6 = w_mod.shape
    tn = n6 // 4
    return pl.pallas_call(
        _ada_kernel,
        grid=(depth, n6 // tn),
        in_specs=[pl.BlockSpec((SUBLANES, d), lambda l, j: (0, 0)),
                  pl.BlockSpec((1, d, tn), lambda l, j: (l, 0, j)),
                  pl.BlockSpec((1, 1, tn), lambda l, j: (l, 0, j))],
        out_specs=pl.BlockSpec((1, SUBLANES, tn), lambda l, j: (l, 0, j)),
        out_shape=jax.ShapeDtypeStruct((depth, SUBLANES, n6), F32),
        compiler_params=_cparams(("arbitrary", "arbitrary")),
        name="ada",
    )(cond, w_mod, b_mod.reshape(depth, 1, n6))


def _modulated(x, mod_ref, gain_ref, shift_row):
    y = _rms(x, gain_ref[...])
    return y * (1.0 + mod_ref[0, shift_row + 1:shift_row + 2, :]) + mod_ref[0, shift_row:shift_row + 1, :]


def _gated_residual(x, y, mod_ref, gain_ref, gate_row):
    return x + mod_ref[0, gate_row:gate_row + 1, :] * _rms(y, gain_ref[...])


def _head_rms(x, gain_row, head_dim):
    outs = []
    for s in range(x.shape[1] // LANES):
        xs = x[:, s * LANES:(s + 1) * LANES]
        lane = lax.broadcasted_iota(jnp.int32, xs.shape, 1)
        sq = xs * xs
        scale = jnp.zeros_like(xs)
        for part in range(LANES // head_dim):
            m = jnp.logical_and(lane >= part * head_dim, lane < (part + 1) * head_dim)
            ms = jnp.sum(jnp.where(m, sq, 0.0), axis=-1, keepdims=True) * (1.0 / head_dim)
            scale = jnp.where(m, lax.rsqrt(ms + EPS), scale)
        outs.append(xs * scale)
    y = outs[0] if len(outs) == 1 else jnp.concatenate(outs, axis=1)
    return y * gain_row


def _rope(x, cos, sin_lo, sin_hi):
    width = x.shape[1]
    fwd = pltpu.roll(x, width - ROPE_AXIS_PAIRS, axis=1)
    back = pltpu.roll(x, ROPE_AXIS_PAIRS, axis=1)
    return x * cos + fwd * sin_lo + back * sin_hi


def _hyb_inproj_kernel(x_ref, xp_ref, xn_ref, mod_ref, g_ref, w_ref, cw_ref, kg_ref, cos_ref, slo_ref, shi_ref,
                       qkv_ref, z_ref, qb_ref, ba_ref, ko_ref, vo_ref, *, tiles_per_seq):
    pos = pl.program_id(0) % tiles_per_seq
    tm = x_ref.shape[0]
    halo = PROJ_HALO
    qkv_w = 3 * A_WIDTH
    a_prev = jnp.where(pos > 0, _modulated(xp_ref[...], mod_ref, g_ref, 0), 0.0).astype(BF16)
    a_next = jnp.where(pos < tiles_per_seq - 1, _modulated(xn_ref[...], mod_ref, g_ref, 0), 0.0).astype(BF16)
    a = _modulated(x_ref[...], mod_ref, g_ref, 0).astype(BF16)
    a_ext = jnp.concatenate([a_prev, a, a_next], axis=0)

    off_z = qkv_w
    off_q = off_z + A_WIDTH
    off_k = off_q + B_WIDTH
    off_v = off_k + B_KV_WIDTH
    off_ba = off_v + B_KV_WIDTH

    def gate_proj():
        z_ref[...] = _dot(a, w_ref[:, off_z:off_q])

    def query_proj():
        qb_ref[...] = _dot(a, w_ref[:, off_q:off_k])

    def key_value_proj():
        kb = _dot(a, w_ref[:, off_k:off_v])
        vb = _dot(a, w_ref[:, off_v:off_ba])
        k = _rope(_head_rms(kb, kg_ref[...], B_HEAD_DIM), cos_ref[...], slo_ref[...], shi_ref[...]).astype(BF16)
        vt = vb.T.astype(BF16)
        for hh in range(B_KV_HEADS):
            ko_ref[0, hh] = k[:, hh * B_HEAD_DIM:(hh + 1) * B_HEAD_DIM]
            vo_ref[0, hh, :B_HEAD_DIM, :] = vt[hh * B_HEAD_DIM:(hh + 1) * B_HEAD_DIM, :]
            vo_ref[0, hh, B_HEAD_DIM:, :] = jnp.ones((ATTN_VT_ROWS - B_HEAD_DIM, tm), BF16)

    def logit_proj():
        ba_ref[...] = _dot(a, w_ref[:, off_ba:off_ba + LANES])

    others = [gate_proj, query_proj, key_value_proj, logit_proj]
    ext_rows = tm + 2 * halo
    pair = 2 * A_HEAD_DIM
    n_pairs = qkv_w // pair
    ext_next = _dot(a_ext, w_ref[:, :pair])
    for cb in range(n_pairs):
        cols = slice(cb * pair, (cb + 1) * pair)
        ext = ext_next
        if cb + 1 < n_pairs:
            ext_next = _dot(a_ext, w_ref[:, (cb + 1) * pair:(cb + 2) * pair])
        if cb < len(others):
            others[cb]()
        w = cw_ref[:, cols]
        acc = jnp.zeros((tm, pair), F32)
        for tap in range(SHORT_CONV_W):
            shift = (SHORT_CONV_W // 2 - tap) % ext_rows
            rolled = pltpu.roll(ext, shift, axis=0) if shift else ext
            acc = acc + rolled[halo:halo + tm, :] * w[tap:tap + 1, :]
        y = _silu(acc)
        for part in range(2):
            head = 2 * cb + part
            yh = y[:, part * A_HEAD_DIM:(part + 1) * A_HEAD_DIM]
            if head < 2 * A_HEADS:
                yh = yh * lax.rsqrt(jnp.sum(yh * yh, axis=-1, keepdims=True) + EPS)
            if head < A_HEADS:
                yh = yh * (A_HEAD_DIM ** -0.5)
            qkv_ref[:, head * A_HEAD_DIM:(head + 1) * A_HEAD_DIM] = yh


def _hyb_inproj(x2, mod, gain, w, conv_w, k_gain, cos, slo, shi, bsz, rows_per_seq, tm):
    n, d = x2.shape
    tiles_per_seq = rows_per_seq // tm
    hb = tm // PROJ_HALO
    n_halo = n // PROJ_HALO
    f32_out = lambda wd: (pl.BlockSpec((tm, wd), lambda i: (i, 0)), jax.ShapeDtypeStruct((n, wd), F32))
    outs = [f32_out(3 * A_WIDTH), f32_out(A_WIDTH), f32_out(B_WIDTH), f32_out(LANES),
            (pl.BlockSpec((1, B_KV_HEADS, tm, B_HEAD_DIM), lambda i: (i // tiles_per_seq, 0, i % tiles_per_seq, 0)),
             jax.ShapeDtypeStruct((bsz, B_KV_HEADS, rows_per_seq, B_HEAD_DIM), BF16)),
            (pl.BlockSpec((1, B_KV_HEADS, ATTN_VT_ROWS, tm), lambda i: (i // tiles_per_seq, 0, 0, i % tiles_per_seq)),
             jax.ShapeDtypeStruct((bsz, B_KV_HEADS, ATTN_VT_ROWS, rows_per_seq), BF16))]
    tab = pl.BlockSpec((tm, B_KV_WIDTH), lambda i: (i % tiles_per_seq, 0))
    return pl.pallas_call(
        functools.partial(_hyb_inproj_kernel, tiles_per_seq=tiles_per_seq),
        grid=(n // tm,),
        in_specs=[pl.BlockSpec((tm, d), lambda i: (i, 0)),
                  pl.BlockSpec((PROJ_HALO, d), lambda i: (jnp.maximum(i * hb - 1, 0), 0)),
                  pl.BlockSpec((PROJ_HALO, d), lambda i: (jnp.minimum((i + 1) * hb, n_halo - 1), 0)),
                  pl.BlockSpec((1, 6, d), lambda i: (i // tiles_per_seq if mod.shape[0] > 1 else 0, 0, 0)),
                  pl.BlockSpec((1, d), lambda i: (0, 0)),
                  _resident(w.shape),
                  _resident(conv_w.shape),
                  pl.BlockSpec((1, B_KV_WIDTH), lambda i: (0, 0)), tab, tab, tab],
        out_specs=[o[0] for o in outs],
        out_shape=[o[1] for o in outs],
        compiler_params=_cparams(("arbitrary",)),
        name="inproj",
    )(x2, x2, x2, mod, gain, w, conv_w, k_gain, cos, slo, shi)


def _unit_tri_inverse_steps(l_mats, i, j):
    eye = (i == j).astype(F32)
    same16 = jnp.right_shift(i, 4) == jnp.right_shift(j, 4)
    same32 = jnp.right_shift(i, 5) == jnp.right_shift(j, 5)
    off32 = jnp.logical_and(same32, jnp.logical_not(same16))
    b = lambda a: a.astype(BF16)
    each = lambda f, *ls: [f(*xs) for xs in zip(*ls)]
    d1 = each(lambda l: b(jnp.where(same16, l, 0.0)), l_mats)
    p = each(lambda d: eye - d.astype(F32), d1)
    dk = d1
    for _ in range(3):
        dk = each(lambda d: b(_dot(d, d)), dk)
        yield None
        p = each(lambda pp, d: pp + _dot(b(pp), d), p, dk)
        yield None
    for sel in (off32, jnp.logical_not(same32)):
        cm = each(lambda l: b(jnp.where(sel, l, 0.0)), l_mats)
        pb = each(b, p)
        inner = each(lambda c, q: b(_dot(c, q)), cm, pb)
        yield None
        p = each(lambda pp, q, m: pp - _dot(q, m), p, pb, inner)
        yield None
    l_split = each(_split_bf16, l_mats)
    t_split = each(_split_bf16, p)
    resid = each(lambda t0, ls, ts: eye - t0 - (_dot(ls[0], ts[0]) + _dot(ls[0], ts[1]) + _dot(ls[1], ts[0])),
                 p, l_split, t_split)
    yield None
    yield each(lambda t0, ts, r: t0 + _dot(ts[0], b(r)), p, t_split, resid)


def _delta_kernel(qf_ref, kf_ref, vf_ref, qr_ref, kr_ref, vr_ref, qc_ref, kc_ref, vc_ref,
                  baf_ref, bar_ref, bac_ref, coef_ref, of_ref, ob_ref,
                  u_s, w_s, kt_s, qd_s, in_s, el_s, l_s, rhs_s, state):
    step = pl.program_id(1)
    is_ctx = step == 0
    front_slot = lax.rem(step, 3)
    scan_slot = lax.rem(step + 1, 3)
    lr_write = step % 2
    lr_read = 1 - lr_write
    uw_write = lr_read
    uw_read = lr_write
    heads = [slice(hh * A_HEAD_DIM, (hh + 1) * A_HEAD_DIM) for hh in range(A_HEADS)]
    chunks = [slice(c * CHUNK, (c + 1) * CHUNK) for c in range(DELTA_BLOCK)]

    @pl.when(step == 0)
    def _():
        for scr in (kt_s, qd_s, in_s, el_s):
            scr[1] = jnp.zeros(scr.shape[1:], scr.dtype)
            scr[2] = jnp.zeros(scr.shape[1:], scr.dtype)
        u_s[0] = jnp.zeros(u_s.shape[1:], u_s.dtype)
        w_s[0] = jnp.zeros(w_s.shape[1:], w_s.dtype)
        l_s[1] = jnp.zeros(l_s.shape[1:], l_s.dtype)
        rhs_s[1] = jnp.zeros(rhs_s.shape[1:], rhs_s.dtype)
        state[...] = jnp.zeros(state.shape, F32)

    chains = [(d, hh) for d in range(2) for hh in range(A_HEADS)]
    s_mats = [state[d * A_HEADS + hh] for d, hh in chains]
    out_refs = (of_ref, ob_ref)

    def scan_substep(t, s_mats):
        where = []
        for d, hh in chains:
            c = t if d == 0 else DELTA_BLOCK - 1 - t
            where.append((d, c, chunks[c], heads[hh]))
        ws = [_dot(jnp.concatenate([w_s[uw_read, d, r, ln], qd_s[scan_slot, d, r, ln]], axis=0), s.astype(BF16))
              for (d, c, r, ln), s in zip(where, s_mats)]
        v_new = [(u_s[uw_read, d, r, ln] - x[:CHUNK]).astype(BF16) for (d, c, r, ln), x in zip(where, ws)]
        for (d, c, r, ln), x, vn in zip(where, ws, v_new):
            out_refs[d][r, ln] = x[CHUNK:] + _dot(in_s[scan_slot, d, r, ln][:, :CHUNK], vn)
        return [s * el_s[scan_slot, d, c * SUBLANES:c * SUBLANES + 1, ln] + _dot_tn(kt_s[scan_slot, d, r, ln], vn)
                for (d, c, r, ln), s, vn in zip(where, s_mats, v_new)]

    def per_head(main_ref, ctx_ref):
        x = jnp.where(is_ctx, ctx_ref[...], main_ref[...])
        return [x[:, ln] for ln in heads]

    q_h = (per_head(qf_ref, qc_ref), per_head(qr_ref, qc_ref))
    k_h = (per_head(kf_ref, kc_ref), per_head(kr_ref, kc_ref))
    v_h = (per_head(vf_ref, vc_ref), per_head(vr_ref, vc_ref))

    lane = lax.broadcasted_iota(jnp.int32, (1, LANES), 1)
    is_beta = jnp.bitwise_and(lane, 3) < 2
    neg_a = -jnp.exp(coef_ref[0:1, :])
    dtb = coef_ref[1:2, :]
    bg = []
    for raw in (jnp.where(is_ctx, bac_ref[...], baf_ref[...]), jnp.where(is_ctx, bac_ref[...], bar_ref[...])):
        xg = raw + dtb
        softplus = jnp.maximum(xg, 0.0) + jnp.log(1.0 + jnp.exp(-jnp.abs(xg)))
        bg.append(jnp.where(is_beta, jax.nn.sigmoid(raw), neg_a * softplus))

    i_idx = lax.broadcasted_iota(jnp.int32, (CHUNK, CHUNK), 0)
    j_idx = lax.broadcasted_iota(jnp.int32, (CHUNK, CHUNK), 1)
    i_w = lax.broadcasted_iota(jnp.int32, (CHUNK, LANES), 0)
    j_w = lax.broadcasted_iota(jnp.int32, (CHUNK, LANES), 1)
    in_chunk = j_w < CHUNK
    ones_rows = jnp.ones((2 * SUBLANES, CHUNK), BF16)

    def front_steps(d):
        incl = (i_idx >= j_idx) if d == 0 else (i_idx <= j_idx)
        strict = (i_idx > j_idx) if d == 0 else (i_idx < j_idx)
        incl_w = jnp.logical_and(in_chunk, (i_w >= j_w) if d == 0 else (i_w <= j_w))
        seen_w = jnp.logical_and(in_chunk, (i_w <= j_w) if d == 0 else (i_w >= j_w))
        seen_cat = jnp.concatenate([seen_w] * DELTA_BLOCK, axis=1)
        incl_b = incl.astype(BF16)
        for hh, ln in enumerate(heads):
            g_lane = 4 * hh + 2 + d
            g_cat = jnp.concatenate([jnp.broadcast_to(bg[d][r, g_lane:g_lane + 1], (CHUNK, LANES)) for r in chunks],
                                    axis=1)
            g_hi = g_cat.astype(BF16)
            g_rest = g_cat - g_hi.astype(F32)
            g_mid = g_rest.astype(BF16)
            pieces = (g_hi, g_mid, (g_rest - g_mid.astype(F32)).astype(BF16))
            gc_all = sum(_dot(incl_b, p) for p in pieces)
            gr_all = sum(_dot(ones_rows, jnp.where(seen_cat, p, jnp.zeros_like(p))) for p in pieces)
            for c, r in enumerate(chunks):
                q, k, v = q_h[d][hh][r], k_h[d][hh][r], v_h[d][hh][r]
                kbf = k.astype(BF16)
                prod = _dot_nt(jnp.concatenate([kbf, q.astype(BF16)], axis=0),
                               jnp.concatenate([kbf, jnp.zeros_like(kbf)], axis=0))
                beta = bg[d][r, 4 * hh + d:4 * hh + d + 1]
                gc_w = gc_all[:, c * LANES:(c + 1) * LANES]
                gr_w = jnp.broadcast_to(gr_all[0:1, c * LANES:(c + 1) * LANES], (CHUNK, LANES))
                decay = jnp.where(incl_w, jnp.exp(jnp.minimum(gc_w - gr_w, 0.0)), 0.0)
                g_last = gc_w[CHUNK - 1:CHUNK, :] if d == 0 else gc_w[0:1, :]
                egc = jnp.exp(gc_w)
                n = (d * A_HEADS + hh) * DELTA_BLOCK + c
                l_s[lr_write, n] = jnp.where(strict, (prod[:CHUNK] * beta * decay)[:, :CHUNK], 0.0)
                rhs_s[lr_write, n] = jnp.concatenate([v * beta, k * beta * egc], axis=1).astype(BF16)
                kt_s[front_slot, d, r, ln] = (k * jnp.exp(g_last - gc_w)).astype(BF16)
                qd_s[front_slot, d, r, ln] = (q * egc).astype(BF16)
                in_s[front_slot, d, r, ln] = (prod[CHUNK:] * decay).astype(BF16)
                el_s[front_slot, d, c * SUBLANES:(c + 1) * SUBLANES, ln] = jnp.broadcast_to(jnp.exp(g_last),
                                                                                            (SUBLANES, LANES))
                yield None

    def back_steps():
        n_inst = 2 * A_HEADS * DELTA_BLOCK
        inverse = None
        for inverse in _unit_tri_inverse_steps([l_s[lr_read, n] for n in range(n_inst)], i_idx, j_idx):
            yield None
        t_split = [_split_bf16(t) for t in inverse]
        rhs = [rhs_s[lr_read, n] for n in range(n_inst)]
        sols = [_dot(t_hi, rr) + _dot(t_lo, rr) for (t_hi, t_lo), rr in zip(t_split, rhs)]
        for n, sol in enumerate(sols):
            d, hh, c = n // (A_HEADS * DELTA_BLOCK), (n // DELTA_BLOCK) % A_HEADS, n % DELTA_BLOCK
            u_s[uw_write, d, chunks[c], heads[hh]] = sol[:, :A_HEAD_DIM]
            w_s[uw_write, d, chunks[c], heads[hh]] = sol[:, A_HEAD_DIM:].astype(BF16)
        yield None

    def scan_steps(s_mats):
        for t in range(DELTA_BLOCK):
            s_mats = scan_substep(t, s_mats)
            yield s_mats
            yield s_mats
            yield s_mats

    _, _, _, s_mats = _drain(front_steps(0), front_steps(1), back_steps(), scan_steps(s_mats))
    for (d, hh), s in zip(chains, s_mats):
        state[d * A_HEADS + hh] = s


def _delta_mixer(qkv, qkv_c, ba, ba_c, coef, bsz, n_lat, n_ctx):
    assert n_ctx == DELTA_ROWS and n_lat % DELTA_ROWS == 0
    wd = A_WIDTH
    lat_blocks = n_lat // DELTA_ROWS
    n_steps = lat_blocks + 3

    def fwd_blk(b, s):
        return b * lat_blocks + jnp.clip(s - 1, 0, lat_blocks - 1)

    def bwd_blk(b, s):
        return b * lat_blocks + jnp.clip(lat_blocks - s, 0, lat_blocks - 1)

    def lat(blk_fn, kd):
        return pl.BlockSpec((DELTA_ROWS, wd), lambda b, s: (blk_fn(b, s), kd))

    def ctxb(kd):
        return pl.BlockSpec((n_ctx, wd), lambda b, s: (b, kd))

    in_specs = ([lat(fwd_blk, kd) for kd in range(3)] + [lat(bwd_blk, kd) for kd in range(3)]
                + [ctxb(kd) for kd in range(3)]
                + [pl.BlockSpec((DELTA_ROWS, LANES), lambda b, s: (fwd_blk(b, s), 0)),
                   pl.BlockSpec((DELTA_ROWS, LANES), lambda b, s: (bwd_blk(b, s), 0)),
                   pl.BlockSpec((n_ctx, LANES), lambda b, s: (b, 0)),
                   pl.BlockSpec((SUBLANES, LANES), lambda b, s: (0, 0))])
    out_f = pl.BlockSpec((DELTA_ROWS, wd), lambda b, s: (b * lat_blocks + jnp.maximum(s - 3, 0), 0))
    out_b = pl.BlockSpec((DELTA_ROWS, wd), lambda b, s: (b * lat_blocks + lat_blocks - jnp.maximum(s - 2, 1), 0))
    shape = jax.ShapeDtypeStruct((bsz * n_lat, wd), F32)
    slots = lambda n, rows, dt: pltpu.VMEM((n, 2, rows, wd), dt)
    n_inst = 2 * A_HEADS * DELTA_BLOCK
    return pl.pallas_call(
        _delta_kernel,
        grid=(bsz, n_steps),
        in_specs=in_specs,
        out_specs=[out_f, out_b],
        out_shape=[shape, shape],
        scratch_shapes=[slots(2, DELTA_ROWS, F32), slots(2, DELTA_ROWS, BF16), slots(3, DELTA_ROWS, BF16),
                        slots(3, DELTA_ROWS, BF16), slots(3, DELTA_ROWS, BF16),
                        slots(3, DELTA_BLOCK * SUBLANES, F32),
                        pltpu.VMEM((2, n_inst, CHUNK, CHUNK), F32),
                        pltpu.VMEM((2, n_inst, CHUNK, 2 * A_HEAD_DIM), BF16),
                        pltpu.VMEM((2 * A_HEADS, A_HEAD_DIM, A_HEAD_DIM), F32)],
        compiler_params=_cparams(("arbitrary", "arbitrary")),
        name="delta",
    )(*([qkv] * 6), *([qkv_c] * 3), ba, ba, ba_c, coef)


def _attn_kernel(q_ref, g_ref, cos_ref, slo_ref, shi_ref, k_ref, vt_ref, o_ref):
    q = _head_rms(q_ref[...], g_ref[...], B_HEAD_DIM)
    q = _rope(q, cos_ref[...], slo_ref[...], shi_ref[...]) * (B_HEAD_DIM ** -0.5 * LOG2_E)
    tq = q.shape[0]
    qt = q.T.astype(BF16)
    n_keys = k_ref.shape[2]
    q_s = [qt[g * B_HEAD_DIM:(g + 1) * B_HEAD_DIM, c:c + ATTN_Q_COLS]
           for g in range(B_GROUP) for c in range(0, tq, ATTN_Q_COLS)]
    m = [jnp.full((1, ATTN_Q_COLS), -1e30, F32) for _ in q_s]
    acc = [jnp.zeros((ATTN_VT_ROWS, ATTN_Q_COLS), F32) for _ in q_s]

    def scores(kb):
        k_blk = k_ref[0, 0, kb:kb + ATTN_KEY_BLOCK, :]
        return [_dot(k_blk, qs) for qs in q_s]

    def value_update(acc, pending):
        alpha, p, kb = pending
        vt = vt_ref[0, 0, :, kb:kb + ATTN_KEY_BLOCK]
        return [a * al + _dot(vt, pp) for a, al, pp in zip(acc, alpha, p)]

    st_next = scores(0)
    pending = None
    for kb in range(0, n_keys, ATTN_KEY_BLOCK):
        st = st_next
        if kb + ATTN_KEY_BLOCK < n_keys:
            st_next = scores(kb + ATTN_KEY_BLOCK)
        m_new = [jnp.maximum(mo, jnp.max(s, axis=0, keepdims=True)) for mo, s in zip(m, st)]
        alpha = [jnp.exp2(mo - mn) for mo, mn in zip(m, m_new)]
        p = [jnp.exp2((s - mn).astype(BF16)) for s, mn in zip(st, m_new)]
        if pending is not None:
            acc = value_update(acc, pending)
        pending = (alpha, p, kb)
        m = m_new
    acc = value_update(acc, pending)
    outs = [a[:B_HEAD_DIM] / a[B_HEAD_DIM:B_HEAD_DIM + 1] for a in acc]
    n_col = tq // ATTN_Q_COLS
    o_ref[...] = jnp.concatenate([jnp.concatenate(outs[g * n_col:(g + 1) * n_col], axis=1)
                                  for g in range(B_GROUP)], axis=0).T.astype(o_ref.dtype)


def _attention(qb, gain, cos, slo, shi, k_hm, vt_hm, n_lat, tq):
    n, _ = qb.shape
    bsz, _, s_len, _ = k_hm.shape
    assert s_len % ATTN_KEY_BLOCK == 0 and tq % ATTN_Q_COLS == 0
    gw = B_GROUP * B_HEAD_DIM
    nq = n_lat // tq
    kv = pl.BlockSpec((1, 1, s_len, B_HEAD_DIM), lambda b, kh, i: (b, kh, 0, 0))
    vts = pl.BlockSpec((1, 1, ATTN_VT_ROWS, s_len), lambda b, kh, i: (b, kh, 0, 0))
    tab = pl.BlockSpec((tq, gw), lambda b, kh, i: (i, 0))
    return pl.pallas_call(
        _attn_kernel,
        grid=(bsz, B_KV_HEADS, nq),
        in_specs=[pl.BlockSpec((tq, gw), lambda b, kh, i: (b * nq + i, kh)),
                  pl.BlockSpec((1, gw), lambda b, kh, i: (0, 0)), tab, tab, tab, kv, vts],
        out_specs=pl.BlockSpec((tq, gw), lambda b, kh, i: (b * nq + i, kh)),
        out_shape=jax.ShapeDtypeStruct((n, B_WIDTH), BF16),
        compiler_params=_cparams(("arbitrary", "arbitrary", "arbitrary")),
        name="attn",
    )(qb, gain, cos, slo, shi, k_hm, vt_hm)


def _outproj_kernel(of_ref, ob_ref, z_ref, on_ref, yb_ref, x_ref, mod_ref, g_ref, wa_ref, wb_ref, o_ref):
    o = of_ref[...] + ob_ref[...]
    z = z_ref[...]
    parts = []
    for hh in range(A_HEADS):
        sl = slice(hh * A_HEAD_DIM, (hh + 1) * A_HEAD_DIM)
        parts.append((_rms(o[:, sl], on_ref[...]) * _silu(z[:, sl])).astype(BF16))
    ya = jnp.concatenate(parts, axis=1)
    y = _dot(ya, wa_ref[...]) + _dot(yb_ref[...], wb_ref[...])
    o_ref[...] = _gated_residual(x_ref[...], y, mod_ref, g_ref, 2)


def _outproj(o_f, o_b, z, out_norm, yb, x2, mod, gain, wa, wb, rows_per_mod, tm):
    n, d = x2.shape
    bpm = rows_per_mod // tm
    return pl.pallas_call(
        _outproj_kernel,
        grid=(n // tm,),
        in_specs=[pl.BlockSpec((tm, o_f.shape[1]), lambda i: (i, 0)),
                  pl.BlockSpec((tm, o_b.shape[1]), lambda i: (i, 0)),
                  pl.BlockSpec((tm, z.shape[1]), lambda i: (i, 0)),
                  pl.BlockSpec((1, A_HEAD_DIM), lambda i: (0, 0)),
                  pl.BlockSpec((tm, yb.shape[1]), lambda i: (i, 0)),
                  pl.BlockSpec((tm, d), lambda i: (i, 0)),
                  pl.BlockSpec((1, 6, d), lambda i: (i // bpm, 0, 0)),
                  pl.BlockSpec((1, d), lambda i: (0, 0)),
                  _resident(wa.shape),
                  _resident(wb.shape)],
        out_specs=pl.BlockSpec((tm, d), lambda i: (i, 0)),
        out_shape=jax.ShapeDtypeStruct((n, d), F32),
        compiler_params=_cparams(("arbitrary",)),
        name="outproj",
    )(o_f, o_b, z, out_norm, yb, x2, mod, gain, wa, wb)


def _ffn_kernel(x_ref, mod_ref, gpre_ref, gpost_ref, wi_ref, wo_ref, o_ref, *, hidden, hc):
    x = x_ref[...]
    a = _modulated(x, mod_ref, gpre_ref, 3).astype(BF16)
    acc = jnp.zeros(x.shape, F32)
    for c in range(hidden // hc):
        gate = _dot(a, wi_ref[:, c * hc:(c + 1) * hc])
        up = _dot(a, wi_ref[:, hidden + c * hc:hidden + (c + 1) * hc])
        acc = acc + _dot((_silu(gate) * up).astype(BF16), wo_ref[c * hc:(c + 1) * hc, :])
    o_ref[...] = _gated_residual(x, acc, mod_ref, gpost_ref, 5)


def _ffn(x2, mod, gpre, gpost, wi, wo, rows_per_mod, tm):
    n, d = x2.shape
    hidden = wo.shape[0]
    bpm = rows_per_mod // tm
    return pl.pallas_call(
        functools.partial(_ffn_kernel, hidden=hidden, hc=2 * LANES),
        grid=(n // tm,),
        in_specs=[pl.BlockSpec((tm, d), lambda i: (i, 0)),
                  pl.BlockSpec((1, 6, d), lambda i: (i // bpm, 0, 0)),
                  pl.BlockSpec((1, d), lambda i: (0, 0)),
                  pl.BlockSpec((1, d), lambda i: (0, 0)),
                  _resident(wi.shape),
                  _resident(wo.shape)],
        out_specs=pl.BlockSpec((tm, d), lambda i: (i, 0)),
        out_shape=jax.ShapeDtypeStruct((n, d), F32),
        compiler_params=_cparams(("arbitrary",)),
        name="ffn",
    )(x2, mod, gpre, gpost, wi, wo)


def _confin_kernel(x_ref, mod_ref, g_ref, w_ref, b_ref, o_ref, *, width):
    a = _modulated(x_ref[...], mod_ref, g_ref, 0).astype(BF16)
    val = _dot(a, w_ref[:, :width]) + b_ref[:, :width]
    gate = _dot(a, w_ref[:, width:]) + b_ref[:, width:]
    o_ref[...] = val * jax.nn.sigmoid(gate)


def _confin(x2, mod, gain, w, b, rows_per_mod, tm):
    n, d = x2.shape
    width = w.shape[1] // 2
    bpm = rows_per_mod // tm
    return pl.pallas_call(
        functools.partial(_confin_kernel, width=width),
        grid=(n // tm,),
        in_specs=[pl.BlockSpec((tm, d), lambda i: (i, 0)),
                  pl.BlockSpec((1, 6, d), lambda i: (i // bpm, 0, 0)),
                  pl.BlockSpec((1, d), lambda i: (0, 0)),
                  _resident(w.shape),
                  pl.BlockSpec((1, 2 * width), lambda i: (0, 0))],
        out_specs=pl.BlockSpec((tm, width), lambda i: (i, 0)),
        out_shape=jax.ShapeDtypeStruct((n, width), F32),
        compiler_params=_cparams(("arbitrary",)),
        name="confin",
    )(x2, mod, gain, w, b)


def _confout_kernel(u_ref, up_ref, un_ref, x_ref, mod_ref, g_ref, dww_ref, dwb_ref, lng_ref, lnb_ref,
                    w_ref, b_ref, o_ref, ext, conv, shifted, *, tm, tiles_per_seq):
    pos = pl.program_id(0) % tiles_per_seq
    width = u_ref.shape[1]
    halo = CONF_HALO
    ext[pl.ds(halo, tm), :] = u_ref[...]
    ext[0:halo, :] = jnp.where(pos == 0, 0.0, up_ref[...])
    ext[pl.ds(halo + tm, halo), :] = jnp.where(pos == tiles_per_seq - 1, 0.0, un_ref[...])

    rb = 64
    pad = CONF_KERNEL // 2
    copy_rows = tm + 2 * halo - SUBLANES

    def col_body(c, carry):
        cs = pl.ds(pl.multiple_of(c * LANES, LANES), LANES)
        w = dww_ref[:, cs]
        for s in range(SUBLANES):
            shifted[s, :, :] = ext[pl.ds(s, copy_rows), cs]
        for r in range(tm // rb):
            acc = jnp.zeros((rb, LANES), F32)
            for tap in range(CONF_KERNEL):
                off = halo - pad + tap
                acc = acc + shifted[off % SUBLANES, pl.ds(r * rb + off - off % SUBLANES, rb), :] * w[tap:tap + 1, :]
            conv[pl.ds(r * rb, rb), cs] = acc
        return carry

    lax.fori_loop(0, width // LANES, col_body, 0)

    y = conv[...] + dwb_ref[...]
    mu = jnp.mean(y, axis=-1, keepdims=True)
    yc = y - mu
    var = jnp.mean(yc * yc, axis=-1, keepdims=True)
    y = _silu(yc * lax.rsqrt(var + EPS) * lng_ref[...] + lnb_ref[...])
    out = _dot(y.astype(BF16), w_ref[...]) + b_ref[...]
    o_ref[...] = _gated_residual(x_ref[...], out, mod_ref, g_ref, 2)


def _confout(u, x2, mod, gain, dww, dwb, lng, lnb, w, b, rows_per_mod, tm):
    n, d = x2.shape
    width = u.shape[1]
    tiles_per_seq = rows_per_mod // tm
    hb = tm // CONF_HALO
    n_halo_blocks = n // CONF_HALO
    vec = lambda wd: pl.BlockSpec((1, wd), lambda i: (0, 0))
    return pl.pallas_call(
        functools.partial(_confout_kernel, tm=tm, tiles_per_seq=tiles_per_seq),
        grid=(n // tm,),
        in_specs=[pl.BlockSpec((tm, width), lambda i: (i, 0)),
                  pl.BlockSpec((CONF_HALO, width), lambda i: (jnp.maximum(i * hb - 1, 0), 0)),
                  pl.BlockSpec((CONF_HALO, width), lambda i: (jnp.minimum((i + 1) * hb, n_halo_blocks - 1), 0)),
                  pl.BlockSpec((tm, d), lambda i: (i, 0)),
                  pl.BlockSpec((1, 6, d), lambda i: (i // tiles_per_seq, 0, 0)),
                  vec(d),
                  pl.BlockSpec(dww.shape, lambda i: (0, 0)),
                  vec(width), vec(width), vec(width),
                  _resident(w.shape),
                  vec(d)],
        out_specs=pl.BlockSpec((tm, d), lambda i: (i, 0)),
        out_shape=jax.ShapeDtypeStruct((n, d), F32),
        scratch_shapes=[pltpu.VMEM((tm + 2 * CONF_HALO, width), F32), pltpu.VMEM((tm, width), F32),
                        pltpu.VMEM((SUBLANES, tm + 2 * CONF_HALO - SUBLANES, LANES), F32)],
        compiler_params=_cparams(("arbitrary",)),
        name="confout",
    )(u, u, u, x2, mod, gain, dww, dwb, lng, lnb, w, b)


def _rope_tables(n_tokens, reps):
    rows = n_tokens // GRID_W
    row = jnp.broadcast_to(jnp.arange(rows, dtype=F32)[:, None], (rows, GRID_W)).reshape(n_tokens)
    col = jnp.broadcast_to(jnp.arange(GRID_W, dtype=F32)[None, :], (rows, GRID_W)).reshape(n_tokens)
    inv_freq = ROPE_THETA ** (-jnp.arange(ROPE_AXIS_PAIRS, dtype=F32) / ROPE_AXIS_PAIRS)
    ang_r = row[:, None] * inv_freq
    ang_c = col[:, None] * inv_freq
    ang = jnp.concatenate([ang_r, ang_r, ang_c, ang_c], axis=-1)
    cos, sin = jnp.cos(ang), jnp.sin(ang)
    first_half = (jnp.arange(B_HEAD_DIM) % (2 * ROPE_AXIS_PAIRS)) < ROPE_AXIS_PAIRS
    sin_lo = jnp.where(first_half, -sin, 0.0)
    sin_hi = jnp.where(first_half, 0.0, sin)
    tile = lambda t: jnp.tile(t, (1, reps))
    return tile(cos), tile(sin_lo), tile(sin_hi)


def _hybrid_in_weight(w_in):
    off_z = 3 * A_WIDTH
    off_ba = off_z + A_WIDTH
    off_q = off_ba + 4 * A_HEADS
    off_k = off_q + B_WIDTH
    off_v = off_k + B_KV_WIDTH
    d = w_in.shape[0]
    ba = w_in[:, off_ba:off_q].reshape(d, 2, 2, A_HEADS)
    ba = ba.transpose(0, 3, 1, 2).reshape(d, 4 * A_HEADS)
    ba = jnp.pad(ba, ((0, 0), (0, LANES - 4 * A_HEADS)))
    w = jnp.concatenate([w_in[:, :off_z], w_in[:, off_z:off_ba], w_in[:, off_q:off_k], w_in[:, off_k:off_v],
                         w_in[:, off_v:], ba], axis=1)
    return w.astype(BF16)


def _row(v):
    return v.reshape(1, -1)


def kernel(x, c, ctx, c_ctx, w_mod, b_mod, g_mix_pre, g_mix_post, g_ffn_pre, g_ffn_post, w_ffn_in, w_ffn_out,
           hyb_w_in, hyb_conv_w, hyb_a_log, hyb_dt_bias, hyb_out_norm, hyb_q_norm, hyb_k_norm, hyb_w_out,
           conf_w_in, conf_b_in, conf_dw_w, conf_dw_b, conf_ln_g, conf_ln_b, conf_w_out, conf_b_out):
    bsz, n_lat, d = x.shape
    n_ctx = ctx.shape[1]
    depth = w_mod.shape[0]
    n = bsz * n_lat
    tm = ROW_TILE

    cond = jnp.zeros((SUBLANES, d), F32).at[:bsz].set(c).at[bsz].set(c_ctx)
    mods = _ada_terms(cond, w_mod, b_mod).reshape(depth, SUBLANES, 6, d)

    h = x.reshape(n, d)
    hc = ctx.reshape(bsz * n_ctx, d)
    for layer in range(depth):
        idx = layer // 2
        mod = mods[layer, :bsz]
        mod_ctx = mods[layer, bsz:bsz + 1]
        if layer % 2 == 0:
            w_in = _hybrid_in_weight(hyb_w_in[idx])
            gpre = _row(g_mix_pre[layer])
            k_gain = _row(jnp.tile(hyb_k_norm[idx], B_KV_HEADS))
            cos_k, slo_k, shi_k = _rope_tables(n_lat, B_KV_HEADS)
            no_pos = (jnp.ones((n_ctx, B_KV_WIDTH), F32), jnp.zeros((n_ctx, B_KV_WIDTH), F32),
                      jnp.zeros((n_ctx, B_KV_WIDTH), F32))
            qkv, z, qb, ba, k_lat, vt_lat = _hyb_inproj(h, mod, gpre, w_in, hyb_conv_w[idx], k_gain,
                                                        cos_k, slo_k, shi_k, bsz, n_lat, tm)
            qkv_c, _, _, ba_c, k_ctx, vt_ctx = _hyb_inproj(hc, mod_ctx, gpre, w_in, hyb_conv_w[idx], k_gain,
                                                           *no_pos, bsz, n_ctx, n_ctx)

            coef = jnp.stack([hyb_a_log[idx], hyb_dt_bias[idx]])
            coef = jnp.pad(coef.transpose(0, 2, 1), ((0, 0), (0, 0), (2, 0))).reshape(2, 4 * A_HEADS)
            coef = jnp.pad(coef, ((0, SUBLANES - 2), (0, LANES - 4 * A_HEADS)))
            o_f, o_b = _delta_mixer(qkv, qkv_c, ba, ba_c, coef, bsz, n_lat, n_ctx)

            k_hm = jnp.concatenate([k_lat, k_ctx], axis=2)
            vt_hm = jnp.concatenate([vt_lat, vt_ctx], axis=3)
            cos_q, slo_q, shi_q = _rope_tables(n_lat, B_GROUP)
            yb = _attention(qb, _row(jnp.tile(hyb_q_norm[idx], B_GROUP)), cos_q, slo_q, shi_q,
                            k_hm, vt_hm, n_lat, ATTN_Q_TILE)

            w_out = hyb_w_out[idx].astype(BF16)
            h = _outproj(o_f, o_b, z, _row(hyb_out_norm[idx]), yb, h, mod, _row(g_mix_post[layer]),
                         w_out[:A_WIDTH], w_out[A_WIDTH:], n_lat, tm)
        else:
            u = _confin(h, mod, _row(g_mix_pre[layer]), conf_w_in[idx].astype(BF16), _row(conf_b_in[idx]),
                        n_lat, tm)
            h = _confout(u, h, mod, _row(g_mix_post[layer]), conf_dw_w[idx], _row(conf_dw_b[idx]),
                         _row(conf_ln_g[idx]), _row(conf_ln_b[idx]), conf_w_out[idx].astype(BF16),
                         _row(conf_b_out[idx]), n_lat, CONF_TILE)
        h = _ffn(h, mod, _row(g_ffn_pre[layer]), _row(g_ffn_post[layer]), w_ffn_in[layer].astype(BF16),
                 w_ffn_out[layer].astype(BF16), n_lat, tm)
        assert not any(j % 2 == 0 for j in range(layer + 1, depth)), "context advance not implemented"
    return h.reshape(bsz, n_lat, d)
```

```python
import functools
import math

import jax
import jax.numpy as jnp
from jax import lax
from jax.experimental import pallas as pl
from jax.experimental.pallas import tpu as pltpu

F32 = jnp.float32
BF16 = jnp.bfloat16

EPS = 1e-6
GRID_W = 64
ROPE_THETA = 10000.0
A_HEADS = 4
A_HEAD_DIM = 128
A_WIDTH = A_HEADS * A_HEAD_DIM
SHORT_CONV_W = 5
CHUNK = 64
B_Q_HEADS = 8
B_KV_HEADS = 2
B_HEAD_DIM = 64
B_GROUP = B_Q_HEADS // B_KV_HEADS
B_WIDTH = B_Q_HEADS * B_HEAD_DIM
B_KV_WIDTH = B_KV_HEADS * B_HEAD_DIM
ROPE_AXIS_PAIRS = B_HEAD_DIM // 4
CONF_KERNEL = 31
LOG2_E = math.log2(math.e)

LANES = 128
SUBLANES = 8
VMEM_LIMIT = 56 * 1024 * 1024
ROW_TILE = 1024
CONF_TILE = 512
CONF_HALO = 16
PROJ_HALO = 16
DELTA_BLOCK = 4
DELTA_ROWS = DELTA_BLOCK * CHUNK
ATTN_Q_TILE = 512
ATTN_KEY_BLOCK = 128
ATTN_Q_COLS = 256
ATTN_VT_ROWS = B_HEAD_DIM + 16


def _cparams(sem):
    return pltpu.CompilerParams(dimension_semantics=sem, vmem_limit_bytes=VMEM_LIMIT)


def _resident(shape):
    return pl.BlockSpec(shape, lambda i: (0, 0), pipeline_mode=pl.Buffered(1))


def _silu(x):
    return x * jax.nn.sigmoid(x)


def _dot(a, b):
    return jnp.dot(a, b, preferred_element_type=F32)


def _dot_nt(a, b):
    return lax.dot_general(a, b, (((1,), (1,)), ((), ())), preferred_element_type=F32)


def _dot_tn(a, b):
    return lax.dot_general(a, b, (((0,), (0,)), ((), ())), preferred_element_type=F32)


def _rms(x, gain):
    return x * lax.rsqrt(jnp.mean(x * x, axis=-1, keepdims=True) + EPS) * gain


def _split_bf16(a):
    hi = a.astype(BF16)
    return hi, (a - hi.astype(F32)).astype(BF16)


def _drain(*gens):
    last = [None] * len(gens)
    live = list(range(len(gens)))
    while live:
        for g in list(live):
            try:
                last[g] = next(gens[g])
            except StopIteration:
                live.remove(g)
    return last


def _ada_kernel(c_ref, w_ref, b_ref, o_ref):
    s_hi, s_lo = _split_bf16(_silu(c_ref[...]))
    w_hi, w_lo = _split_bf16(w_ref[0])
    o_ref[0] = _dot(s_hi, w_hi) + _dot(s_hi, w_lo) + _dot(s_lo, w_hi) + b_ref[0]


def _ada_terms(cond, w_mod, b_mod):
    depth, d, n6 = w_mod.shape
    tn = n6 // 4
    return pl.pallas_call(
        _ada_kernel,
        grid=(depth, n6 // tn),
        in_specs=[pl.BlockSpec((SUBLANES, d), lambda l, j: (0, 0)),
                  pl.BlockSpec((1, d, tn), lambda l, j: (l, 0, j)),
                  pl.BlockSpec((1, 1, tn), lambda l, j: (l, 0, j))],
        out_specs=pl.BlockSpec((1, SUBLANES, tn), lambda l, j: (l, 0, j)),
        out_shape=jax.ShapeDtypeStruct((depth, SUBLANES, n6), F32),
        compiler_params=_cparams(("arbitrary", "arbitrary")),
        name="ada",
    )(cond, w_mod, b_mod.reshape(depth, 1, n6))


def _modulated(x, mod_ref, gain_ref, shift_row):
    y = _rms(x, gain_ref[...])
    return y * (1.0 + mod_ref[0, shift_row + 1:shift_row + 2, :]) + mod_ref[0, shift_row:shift_row + 1, :]


def _gated_residual(x, y, mod_ref, gain_ref, gate_row):
    return x + mod_ref[0, gate_row:gate_row + 1, :] * _rms(y, gain_ref[...])


def _head_rms(x, gain_row, head_dim):
    outs = []
    for s in range(x.shape[1] // LANES):
        xs = x[:, s * LANES:(s + 1) * LANES]
        lane = lax.broadcasted_iota(jnp.int32, xs.shape, 1)
        sq = xs * xs
        scale = jnp.zeros_like(xs)
        for part in range(LANES // head_dim):
            m = jnp.logical_and(lane >= part * head_dim, lane < (part + 1) * head_dim)
            ms = jnp.sum(jnp.where(m, sq, 0.0), axis=-1, keepdims=True) * (1.0 / head_dim)
            scale = jnp.where(m, lax.rsqrt(ms + EPS), scale)
        outs.append(xs * scale)
    y = outs[0] if len(outs) == 1 else jnp.concatenate(outs, axis=1)
    return y * gain_row


def _rope(x, cos, sin_lo, sin_hi):
    width = x.shape[1]
    fwd = pltpu.roll(x, width - ROPE_AXIS_PAIRS, axis=1)
    back = pltpu.roll(x, ROPE_AXIS_PAIRS, axis=1)
    return x * cos + fwd * sin_lo + back * sin_hi


def _hyb_inproj_kernel(x_ref, xp_ref, xn_ref, mod_ref, g_ref, w_ref, cw_ref, kg_ref, cos_ref, slo_ref, shi_ref,
                       qkv_ref, z_ref, qb_ref, ba_ref, ko_ref, vo_ref, *, tiles_per_seq):
    pos = pl.program_id(0) % tiles_per_seq
    tm = x_ref.shape[0]
    halo = PROJ_HALO
    qkv_w = 3 * A_WIDTH
    a_prev = jnp.where(pos > 0, _modulated(xp_ref[...], mod_ref, g_ref, 0), 0.0).astype(BF16)
    a_next = jnp.where(pos < tiles_per_seq - 1, _modulated(xn_ref[...], mod_ref, g_ref, 0), 0.0).astype(BF16)
    a = _modulated(x_ref[...], mod_ref, g_ref, 0).astype(BF16)
    a_ext = jnp.concatenate([a_prev, a, a_next], axis=0)

    off_z = qkv_w
    off_q = off_z + A_WIDTH
    off_k = off_q + B_WIDTH
    off_v = off_k + B_KV_WIDTH
    off_ba = off_v + B_KV_WIDTH

    def gate_proj():
        z_ref[...] = _dot(a, w_ref[:, off_z:off_q])

    def query_proj():
        qb_ref[...] = _dot(a, w_ref[:, off_q:off_k])

    def key_value_proj():
        kb = _dot(a, w_ref[:, off_k:off_v])
        vb = _dot(a, w_ref[:, off_v:off_ba])
        k = _rope(_head_rms(kb, kg_ref[...], B_HEAD_DIM), cos_ref[...], slo_ref[...], shi_ref[...]).astype(BF16)
        vt = vb.T.astype(BF16)
        for hh in range(B_KV_HEADS):
            ko_ref[0, hh] = k[:, hh * B_HEAD_DIM:(hh + 1) * B_HEAD_DIM]
            vo_ref[0, hh, :B_HEAD_DIM, :] = vt[hh * B_HEAD_DIM:(hh + 1) * B_HEAD_DIM, :]
            vo_ref[0, hh, B_HEAD_DIM:, :] = jnp.ones((ATTN_VT_ROWS - B_HEAD_DIM, tm), BF16)

    def logit_proj():
        ba_ref[...] = _dot(a, w_ref[:, off_ba:off_ba + LANES])

    others = [gate_proj, query_proj, key_value_proj, logit_proj]
    ext_rows = tm + 2 * halo
    pair = 2 * A_HEAD_DIM
    n_pairs = qkv_w // pair
    ext_next = _dot(a_ext, w_ref[:, :pair])
    for cb in range(n_pairs):
        cols = slice(cb * pair, (cb + 1) * pair)
        ext = ext_next
        if cb + 1 < n_pairs:
            ext_next = _dot(a_ext, w_ref[:, (cb + 1) * pair:(cb + 2) * pair])
        if cb < len(others):
            others[cb]()
        w = cw_ref[:, cols]
        acc = jnp.zeros((tm, pair), F32)
        for tap in range(SHORT_CONV_W):
            shift = (SHORT_CONV_W // 2 - tap) % ext_rows
            rolled = pltpu.roll(ext, shift, axis=0) if shift else ext
            acc = acc + rolled[halo:halo + tm, :] * w[tap:tap + 1, :]
        y = _silu(acc)
        for part in range(2):
            head = 2 * cb + part
            yh = y[:, part * A_HEAD_DIM:(part + 1) * A_HEAD_DIM]
            if head < 2 * A_HEADS:
                yh = yh * lax.rsqrt(jnp.sum(yh * yh, axis=-1, keepdims=True) + EPS)
            if head < A_HEADS:
                yh = yh * (A_HEAD_DIM ** -0.5)
            qkv_ref[:, head * A_HEAD_DIM:(head + 1) * A_HEAD_DIM] = yh


def _hyb_inproj(x2, mod, gain, w, conv_w, k_gain, cos, slo, shi, bsz, rows_per_seq, tm):
    n, d = x2.shape
    tiles_per_seq = rows_per_seq // tm
    hb = tm // PROJ_HALO
    n_halo = n // PROJ_HALO
    f32_out = lambda wd: (pl.BlockSpec((tm, wd), lambda i: (i, 0)), jax.ShapeDtypeStruct((n, wd), F32))
    outs = [f32_out(3 * A_WIDTH), f32_out(A_WIDTH), f32_out(B_WIDTH), f32_out(LANES),
            (pl.BlockSpec((1, B_KV_HEADS, tm, B_HEAD_DIM), lambda i: (i // tiles_per_seq, 0, i % tiles_per_seq, 0)),
             jax.ShapeDtypeStruct((bsz, B_KV_HEADS, rows_per_seq, B_HEAD_DIM), BF16)),
            (pl.BlockSpec((1, B_KV_HEADS, ATTN_VT_ROWS, tm), lambda i: (i // tiles_per_seq, 0, 0, i % tiles_per_seq)),
             jax.ShapeDtypeStruct((bsz, B_KV_HEADS, ATTN_VT_ROWS, rows_per_seq), BF16))]
    tab = pl.BlockSpec((tm, B_KV_WIDTH), lambda i: (i % tiles_per_seq, 0))
    return pl.pallas_call(
        functools.partial(_hyb_inproj_kernel, tiles_per_seq=tiles_per_seq),
        grid=(n // tm,),
        in_specs=[pl.BlockSpec((tm, d), lambda i: (i, 0)),
                  pl.BlockSpec((PROJ_HALO, d), lambda i: (jnp.maximum(i * hb - 1, 0), 0)),
                  pl.BlockSpec((PROJ_HALO, d), lambda i: (jnp.minimum((i + 1) * hb, n_halo - 1), 0)),
                  pl.BlockSpec((1, 6, d), lambda i: (i // tiles_per_seq if mod.shape[0] > 1 else 0, 0, 0)),
                  pl.BlockSpec((1, d), lambda i: (0, 0)),
                  _resident(w.shape),
                  _resident(conv_w.shape),
                  pl.BlockSpec((1, B_KV_WIDTH), lambda i: (0, 0)), tab, tab, tab],
        out_specs=[o[0] for o in outs],
        out_shape=[o[1] for o in outs],
        compiler_params=_cparams(("arbitrary",)),
        name="inproj",
    )(x2, x2, x2, mod, gain, w, conv_w, k_gain, cos, slo, shi)


def _unit_tri_inverse_steps(l_mats, i, j):
    eye = (i == j).astype(F32)
    same16 = jnp.right_shift(i, 4) == jnp.right_shift(j, 4)
    same32 = jnp.right_shift(i, 5) == jnp.right_shift(j, 5)
    off32 = jnp.logical_and(same32, jnp.logical_not(same16))
    b = lambda a: a.astype(BF16)
    each = lambda f, *ls: [f(*xs) for xs in zip(*ls)]
    d1 = each(lambda l: b(jnp.where(same16, l, 0.0)), l_mats)
    p = each(lambda d: eye - d.astype(F32), d1)
    dk = d1
    for _ in range(3):
        dk = each(lambda d: b(_dot(d, d)), dk)
        yield None
        p = each(lambda pp, d: pp + _dot(b(pp), d), p, dk)
        yield None
    for sel in (off32, jnp.logical_not(same32)):
        cm = each(lambda l: b(jnp.where(sel, l, 0.0)), l_mats)
        pb = each(b, p)
        inner = each(lambda c, q: b(_dot(c, q)), cm, pb)
        yield None
        p = each(lambda pp, q, m: pp - _dot(q, m), p, pb, inner)
        yield None
    l_split = each(_split_bf16, l_mats)
    t_split = each(_split_bf16, p)
    resid = each(lambda t0, ls, ts: eye - t0 - (_dot(ls[0], ts[0]) + _dot(ls[0], ts[1]) + _dot(ls[1], ts[0])),
                 p, l_split, t_split)
    yield None
    yield each(lambda t0, ts, r: t0 + _dot(ts[0], b(r)), p, t_split, resid)


def _delta_kernel(qf_ref, kf_ref, vf_ref, qr_ref, kr_ref, vr_ref, qc_ref, kc_ref, vc_ref,
                  baf_ref, bar_ref, bac_ref, coef_ref, of_ref, ob_ref,
                  u_s, w_s, kt_s, qd_s, in_s, el_s, state):
    step = pl.program_id(1)
    is_ctx = step == 0
    front_slot = step % 2
    scan_slot = 1 - front_slot
    uw_write, uw_read = front_slot, scan_slot
    heads = [slice(hh * A_HEAD_DIM, (hh + 1) * A_HEAD_DIM) for hh in range(A_HEADS)]
    chunks = [slice(c * CHUNK, (c + 1) * CHUNK) for c in range(DELTA_BLOCK)]

    @pl.when(step == 0)
    def _():
        for scr in (u_s, w_s, kt_s, qd_s, in_s, el_s):
            scr[1] = jnp.zeros(scr.shape[1:], scr.dtype)
        state[...] = jnp.zeros(state.shape, F32)

    chains = [(d, hh) for d in range(2) for hh in range(A_HEADS)]
    s_mats = [state[d * A_HEADS + hh] for d, hh in chains]
    out_refs = (of_ref, ob_ref)

    def scan_substep(t, s_mats):
        where = []
        for d, hh in chains:
            c = t if d == 0 else DELTA_BLOCK - 1 - t
            where.append((d, c, chunks[c], heads[hh]))
        ws = [_dot(jnp.concatenate([w_s[uw_read, d, r, ln], qd_s[scan_slot, d, r, ln]], axis=0), s.astype(BF16))
              for (d, c, r, ln), s in zip(where, s_mats)]
        v_new = [(u_s[uw_read, d, r, ln] - x[:CHUNK]).astype(BF16) for (d, c, r, ln), x in zip(where, ws)]
        for (d, c, r, ln), x, vn in zip(where, ws, v_new):
            out_refs[d][r, ln] = x[CHUNK:] + _dot(in_s[scan_slot, d, r, ln][:, :CHUNK], vn)
        return [s * el_s[scan_slot, d, c * SUBLANES:c * SUBLANES + 1, ln] + _dot_tn(kt_s[scan_slot, d, r, ln], vn)
                for (d, c, r, ln), s, vn in zip(where, s_mats, v_new)]

    def per_head(main_ref, ctx_ref):
        x = jnp.where(is_ctx, ctx_ref[...], main_ref[...])
        return [x[:, ln] for ln in heads]

    q_h = (per_head(qf_ref, qc_ref), per_head(qr_ref, qc_ref))
    k_h = (per_head(kf_ref, kc_ref), per_head(kr_ref, kc_ref))
    v_h = (per_head(vf_ref, vc_ref), per_head(vr_ref, vc_ref))

    lane = lax.broadcasted_iota(jnp.int32, (1, LANES), 1)
    is_beta = jnp.bitwise_and(lane, 3) < 2
    neg_a = -jnp.exp(coef_ref[0:1, :])
    dtb = coef_ref[1:2, :]
    bg = []
    for raw in (jnp.where(is_ctx, bac_ref[...], baf_ref[...]), jnp.where(is_ctx, bac_ref[...], bar_ref[...])):
        xg = raw + dtb
        softplus = jnp.maximum(xg, 0.0) + jnp.log(1.0 + jnp.exp(-jnp.abs(xg)))
        bg.append(jnp.where(is_beta, jax.nn.sigmoid(raw), neg_a * softplus))

    i_idx = lax.broadcasted_iota(jnp.int32, (CHUNK, CHUNK), 0)
    j_idx = lax.broadcasted_iota(jnp.int32, (CHUNK, CHUNK), 1)
    i_w = lax.broadcasted_iota(jnp.int32, (CHUNK, LANES), 0)
    j_w = lax.broadcasted_iota(jnp.int32, (CHUNK, LANES), 1)
    in_chunk = j_w < CHUNK
    ones_rows = jnp.ones((2 * SUBLANES, CHUNK), BF16)

    def front_steps(d):
        incl = (i_idx >= j_idx) if d == 0 else (i_idx <= j_idx)
        strict = (i_idx > j_idx) if d == 0 else (i_idx < j_idx)
        incl_w = jnp.logical_and(in_chunk, (i_w >= j_w) if d == 0 else (i_w <= j_w))
        seen_w = jnp.logical_and(in_chunk, (i_w <= j_w) if d == 0 else (i_w >= j_w))
        seen_cat = jnp.concatenate([seen_w] * DELTA_BLOCK, axis=1)
        incl_b = incl.astype(BF16)
        for hh, ln in enumerate(heads):
            g_lane = 4 * hh + 2 + d
            g_cat = jnp.concatenate([jnp.broadcast_to(bg[d][r, g_lane:g_lane + 1], (CHUNK, LANES)) for r in chunks],
                                    axis=1)
            g_hi = g_cat.astype(BF16)
            g_rest = g_cat - g_hi.astype(F32)
            g_mid = g_rest.astype(BF16)
            pieces = (g_hi, g_mid, (g_rest - g_mid.astype(F32)).astype(BF16))
            gc_all = sum(_dot(incl_b, p) for p in pieces)
            gr_all = sum(_dot(ones_rows, jnp.where(seen_cat, p, jnp.zeros_like(p))) for p in pieces)
            for c, r in enumerate(chunks):
                q, k, v = q_h[d][hh][r], k_h[d][hh][r], v_h[d][hh][r]
                kbf = k.astype(BF16)
                prod = _dot_nt(jnp.concatenate([kbf, q.astype(BF16)], axis=0),
                               jnp.concatenate([kbf, jnp.zeros_like(kbf)], axis=0))
                beta = bg[d][r, 4 * hh + d:4 * hh + d + 1]
                gc_w = gc_all[:, c * LANES:(c + 1) * LANES]
                gr_w = jnp.broadcast_to(gr_all[0:1, c * LANES:(c + 1) * LANES], (CHUNK, LANES))
                decay = jnp.where(incl_w, jnp.exp(jnp.minimum(gc_w - gr_w, 0.0)), 0.0)
                g_last = gc_w[CHUNK - 1:CHUNK, :] if d == 0 else gc_w[0:1, :]
                egc = jnp.exp(gc_w)
                l_mats.append(jnp.where(strict, (prod[:CHUNK] * beta * decay)[:, :CHUNK], 0.0))
                rhs_all.append(jnp.concatenate([v * beta, k * beta * egc], axis=1).astype(BF16))
                dests.append((d, r, ln))
                kt_s[front_slot, d, r, ln] = (k * jnp.exp(g_last - gc_w)).astype(BF16)
                qd_s[front_slot, d, r, ln] = (q * egc).astype(BF16)
                in_s[front_slot, d, r, ln] = (prod[CHUNK:] * decay).astype(BF16)
                el_s[front_slot, d, c * SUBLANES:(c + 1) * SUBLANES, ln] = jnp.broadcast_to(jnp.exp(g_last),
                                                                                            (SUBLANES, LANES))
                yield None

    def scan_steps(s_mats):
        for t in range(DELTA_BLOCK):
            s_mats = scan_substep(t, s_mats)
            yield s_mats
            yield s_mats
            yield s_mats

    l_mats, rhs_all, dests = [], [], []
    _, _, s_mats = _drain(front_steps(0), front_steps(1), scan_steps(s_mats))
    inverse = _drain(_unit_tri_inverse_steps(l_mats, i_idx, j_idx))[0]
    t_split = [_split_bf16(t) for t in inverse]
    sols = [_dot(t_hi, rr) + _dot(t_lo, rr) for (t_hi, t_lo), rr in zip(t_split, rhs_all)]
    for (d, r, ln), sol in zip(dests, sols):
        u_s[uw_write, d, r, ln] = sol[:, :A_HEAD_DIM]
        w_s[uw_write, d, r, ln] = sol[:, A_HEAD_DIM:].astype(BF16)
    for (d, hh), s in zip(chains, s_mats):
        state[d * A_HEADS + hh] = s


def _delta_mixer(qkv, qkv_c, ba, ba_c, coef, bsz, n_lat, n_ctx):
    assert n_ctx == DELTA_ROWS and n_lat % DELTA_ROWS == 0
    wd = A_WIDTH
    lat_blocks = n_lat // DELTA_ROWS
    n_steps = lat_blocks + 2

    def fwd_blk(b, s):
        return b * lat_blocks + jnp.clip(s - 1, 0, lat_blocks - 1)

    def bwd_blk(b, s):
        return b * lat_blocks + jnp.clip(lat_blocks - s, 0, lat_blocks - 1)

    def lat(blk_fn, kd):
        return pl.BlockSpec((DELTA_ROWS, wd), lambda b, s: (blk_fn(b, s), kd))

    def ctxb(kd):
        return pl.BlockSpec((n_ctx, wd), lambda b, s: (b, kd))

    in_specs = ([lat(fwd_blk, kd) for kd in range(3)] + [lat(bwd_blk, kd) for kd in range(3)]
                + [ctxb(kd) for kd in range(3)]
                + [pl.BlockSpec((DELTA_ROWS, LANES), lambda b, s: (fwd_blk(b, s), 0)),
                   pl.BlockSpec((DELTA_ROWS, LANES), lambda b, s: (bwd_blk(b, s), 0)),
                   pl.BlockSpec((n_ctx, LANES), lambda b, s: (b, 0)),
                   pl.BlockSpec((SUBLANES, LANES), lambda b, s: (0, 0))])
    out_f = pl.BlockSpec((DELTA_ROWS, wd), lambda b, s: (b * lat_blocks + jnp.maximum(s - 2, 0), 0))
    out_b = pl.BlockSpec((DELTA_ROWS, wd), lambda b, s: (b * lat_blocks + lat_blocks - jnp.maximum(s - 1, 1), 0))
    shape = jax.ShapeDtypeStruct((bsz * n_lat, wd), F32)
    slots = lambda rows, dt: pltpu.VMEM((2, 2, rows, wd), dt)
    return pl.pallas_call(
        _delta_kernel,
        grid=(bsz, n_steps),
        in_specs=in_specs,
        out_specs=[out_f, out_b],
        out_shape=[shape, shape],
        scratch_shapes=[slots(DELTA_ROWS, F32), slots(DELTA_ROWS, BF16), slots(DELTA_ROWS, BF16),
                        slots(DELTA_ROWS, BF16), slots(DELTA_ROWS, BF16), slots(DELTA_BLOCK * SUBLANES, F32),
                        pltpu.VMEM((2 * A_HEADS, A_HEAD_DIM, A_HEAD_DIM), F32)],
        compiler_params=_cparams(("arbitrary", "arbitrary")),
        name="delta",
    )(*([qkv] * 6), *([qkv_c] * 3), ba, ba, ba_c, coef)


def _attn_kernel(q_ref, g_ref, cos_ref, slo_ref, shi_ref, k_ref, vt_ref, o_ref):
    q = _head_rms(q_ref[...], g_ref[...], B_HEAD_DIM)
    q = _rope(q, cos_ref[...], slo_ref[...], shi_ref[...]) * (B_HEAD_DIM ** -0.5 * LOG2_E)
    tq = q.shape[0]
    qt = q.T.astype(BF16)
    n_keys = k_ref.shape[2]
    q_s = [qt[g * B_HEAD_DIM:(g + 1) * B_HEAD_DIM, c:c + ATTN_Q_COLS]
           for g in range(B_GROUP) for c in range(0, tq, ATTN_Q_COLS)]
    m = [jnp.full((1, ATTN_Q_COLS), -1e30, F32) for _ in q_s]
    acc = [jnp.zeros((ATTN_VT_ROWS, ATTN_Q_COLS), F32) for _ in q_s]

    def scores(kb):
        k_blk = k_ref[0, 0, kb:kb + ATTN_KEY_BLOCK, :]
        return [_dot(k_blk, qs) for qs in q_s]

    def value_update(acc, pending):
        alpha, p, kb = pending
        vt = vt_ref[0, 0, :, kb:kb + ATTN_KEY_BLOCK]
        return [a * al + _dot(vt, pp) for a, al, pp in zip(acc, alpha, p)]

    st_next = scores(0)
    pending = None
    for kb in range(0, n_keys, ATTN_KEY_BLOCK):
        st = st_next
        if kb + ATTN_KEY_BLOCK < n_keys:
            st_next = scores(kb + ATTN_KEY_BLOCK)
        m_new = [jnp.maximum(mo, jnp.max(s, axis=0, keepdims=True)) for mo, s in zip(m, st)]
        alpha = [jnp.exp2(mo - mn) for mo, mn in zip(m, m_new)]
        p = [jnp.exp2((s - mn).astype(BF16)) for s, mn in zip(st, m_new)]
        if pending is not None:
            acc = value_update(acc, pending)
        pending = (alpha, p, kb)
        m = m_new
    acc = value_update(acc, pending)
    outs = [a[:B_HEAD_DIM] / a[B_HEAD_DIM:B_HEAD_DIM + 1] for a in acc]
    n_col = tq // ATTN_Q_COLS
    o_ref[...] = jnp.concatenate([jnp.concatenate(outs[g * n_col:(g + 1) * n_col], axis=1)
                                  for g in range(B_GROUP)], axis=0).T.astype(o_ref.dtype)


def _attention(qb, gain, cos, slo, shi, k_hm, vt_hm, n_lat, tq):
    n, _ = qb.shape
    bsz, _, s_len, _ = k_hm.shape
    assert s_len % ATTN_KEY_BLOCK == 0 and tq % ATTN_Q_COLS == 0
    gw = B_GROUP * B_HEAD_DIM
    nq = n_lat // tq
    kv = pl.BlockSpec((1, 1, s_len, B_HEAD_DIM), lambda b, kh, i: (b, kh, 0, 0))
    vts = pl.BlockSpec((1, 1, ATTN_VT_ROWS, s_len), lambda b, kh, i: (b, kh, 0, 0))
    tab = pl.BlockSpec((tq, gw), lambda b, kh, i: (i, 0))
    return pl.pallas_call(
        _attn_kernel,
        grid=(bsz, B_KV_HEADS, nq),
        in_specs=[pl.BlockSpec((tq, gw), lambda b, kh, i: (b * nq + i, kh)),
                  pl.BlockSpec((1, gw), lambda b, kh, i: (0, 0)), tab, tab, tab, kv, vts],
        out_specs=pl.BlockSpec((tq, gw), lambda b, kh, i: (b * nq + i, kh)),
        out_shape=jax.ShapeDtypeStruct((n, B_WIDTH), BF16),
        compiler_params=_cparams(("arbitrary", "arbitrary", "arbitrary")),
        name="attn",
    )(qb, gain, cos, slo, shi, k_hm, vt_hm)


def _outproj_kernel(of_ref, ob_ref, z_ref, on_ref, yb_ref, x_ref, mod_ref, g_ref, wa_ref, wb_ref, o_ref):
    o = of_ref[...] + ob_ref[...]
    z = z_ref[...]
    parts = []
    for hh in range(A_HEADS):
        sl = slice(hh * A_HEAD_DIM, (hh + 1) * A_HEAD_DIM)
        parts.append((_rms(o[:, sl], on_ref[...]) * _silu(z[:, sl])).astype(BF16))
    ya = jnp.concatenate(parts, axis=1)
    y = _dot(ya, wa_ref[...]) + _dot(yb_ref[...], wb_ref[...])
    o_ref[...] = _gated_residual(x_ref[...], y, mod_ref, g_ref, 2)


def _outproj(o_f, o_b, z, out_norm, yb, x2, mod, gain, wa, wb, rows_per_mod, tm):
    n, d = x2.shape
    bpm = rows_per_mod // tm
    return pl.pallas_call(
        _outproj_kernel,
        grid=(n // tm,),
        in_specs=[pl.BlockSpec((tm, o_f.shape[1]), lambda i: (i, 0)),
                  pl.BlockSpec((tm, o_b.shape[1]), lambda i: (i, 0)),
                  pl.BlockSpec((tm, z.shape[1]), lambda i: (i, 0)),
                  pl.BlockSpec((1, A_HEAD_DIM), lambda i: (0, 0)),
                  pl.BlockSpec((tm, yb.shape[1]), lambda i: (i, 0)),
                  pl.BlockSpec((tm, d), lambda i: (i, 0)),
                  pl.BlockSpec((1, 6, d), lambda i: (i // bpm, 0, 0)),
                  pl.BlockSpec((1, d), lambda i: (0, 0)),
                  _resident(wa.shape),
                  _resident(wb.shape)],
        out_specs=pl.BlockSpec((tm, d), lambda i: (i, 0)),
        out_shape=jax.ShapeDtypeStruct((n, d), F32),
        compiler_params=_cparams(("arbitrary",)),
        name="outproj",
    )(o_f, o_b, z, out_norm, yb, x2, mod, gain, wa, wb)


def _ffn_kernel(x_ref, mod_ref, gpre_ref, gpost_ref, wi_ref, wo_ref, o_ref, *, hidden, hc):
    x = x_ref[...]
    a = _modulated(x, mod_ref, gpre_ref, 3).astype(BF16)
    acc = jnp.zeros(x.shape, F32)
    for c in range(hidden // hc):
        gate = _dot(a, wi_ref[:, c * hc:(c + 1) * hc])
        up = _dot(a, wi_ref[:, hidden + c * hc:hidden + (c + 1) * hc])
        acc = acc + _dot((_silu(gate) * up).astype(BF16), wo_ref[c * hc:(c + 1) * hc, :])
    o_ref[...] = _gated_residual(x, acc, mod_ref, gpost_ref, 5)


def _ffn(x2, mod, gpre, gpost, wi, wo, rows_per_mod, tm):
    n, d = x2.shape
    hidden = wo.shape[0]
    bpm = rows_per_mod // tm
    return pl.pallas_call(
        functools.partial(_ffn_kernel, hidden=hidden, hc=2 * LANES),
        grid=(n // tm,),
        in_specs=[pl.BlockSpec((tm, d), lambda i: (i, 0)),
                  pl.BlockSpec((1, 6, d), lambda i: (i // bpm, 0, 0)),
                  pl.BlockSpec((1, d), lambda i: (0, 0)),
                  pl.BlockSpec((1, d), lambda i: (0, 0)),
                  _resident(wi.shape),
                  _resident(wo.shape)],
        out_specs=pl.BlockSpec((tm, d), lambda i: (i, 0)),
        out_shape=jax.ShapeDtypeStruct((n, d), F32),
        compiler_params=_cparams(("arbitrary",)),
        name="ffn",
    )(x2, mod, gpre, gpost, wi, wo)


def _confin_kernel(x_ref, mod_ref, g_ref, w_ref, b_ref, o_ref, *, width):
    a = _modulated(x_ref[...], mod_ref, g_ref, 0).astype(BF16)
    val = _dot(a, w_ref[:, :width]) + b_ref[:, :width]
    gate = _dot(a, w_ref[:, width:]) + b_ref[:, width:]
    o_ref[...] = val * jax.nn.sigmoid(gate)


def _confin(x2, mod, gain, w, b, rows_per_mod, tm):
    n, d = x2.shape
    width = w.shape[1] // 2
    bpm = rows_per_mod // tm
    return pl.pallas_call(
        functools.partial(_confin_kernel, width=width),
        grid=(n // tm,),
        in_specs=[pl.BlockSpec((tm, d), lambda i: (i, 0)),
                  pl.BlockSpec((1, 6, d), lambda i: (i // bpm, 0, 0)),
                  pl.BlockSpec((1, d), lambda i: (0, 0)),
                  _resident(w.shape),
                  pl.BlockSpec((1, 2 * width), lambda i: (0, 0))],
        out_specs=pl.BlockSpec((tm, width), lambda i: (i, 0)),
        out_shape=jax.ShapeDtypeStruct((n, width), F32),
        compiler_params=_cparams(("arbitrary",)),
        name="confin",
    )(x2, mod, gain, w, b)


def _confout_kernel(u_ref, up_ref, un_ref, x_ref, mod_ref, g_ref, dww_ref, dwb_ref, lng_ref, lnb_ref,
                    w_ref, b_ref, o_ref, ext, conv, shifted, *, tm, tiles_per_seq):
    pos = pl.program_id(0) % tiles_per_seq
    width = u_ref.shape[1]
    halo = CONF_HALO
    ext[pl.ds(halo, tm), :] = u_ref[...]
    ext[0:halo, :] = jnp.where(pos == 0, 0.0, up_ref[...])
    ext[pl.ds(halo + tm, halo), :] = jnp.where(pos == tiles_per_seq - 1, 0.0, un_ref[...])

    rb = 64
    pad = CONF_KERNEL // 2
    copy_rows = tm + 2 * halo - SUBLANES

    def col_body(c, carry):
        cs = pl.ds(pl.multiple_of(c * LANES, LANES), LANES)
        w = dww_ref[:, cs]
        for s in range(SUBLANES):
            shifted[s, :, :] = ext[pl.ds(s, copy_rows), cs]
        for r in range(tm // rb):
            acc = jnp.zeros((rb, LANES), F32)
            for tap in range(CONF_KERNEL):
                off = halo - pad + tap
                acc = acc + shifted[off % SUBLANES, pl.ds(r * rb + off - off % SUBLANES, rb), :] * w[tap:tap + 1, :]
            conv[pl.ds(r * rb, rb), cs] = acc
        return carry

    lax.fori_loop(0, width // LANES, col_body, 0)

    y = conv[...] + dwb_ref[...]
    mu = jnp.mean(y, axis=-1, keepdims=True)
    yc = y - mu
    var = jnp.mean(yc * yc, axis=-1, keepdims=True)
    y = _silu(yc * lax.rsqrt(var + EPS) * lng_ref[...] + lnb_ref[...])
    out = _dot(y.astype(BF16), w_ref[...]) + b_ref[...]
    o_ref[...] = _gated_residual(x_ref[...], out, mod_ref, g_ref, 2)


def _confout(u, x2, mod, gain, dww, dwb, lng, lnb, w, b, rows_per_mod, tm):
    n, d = x2.shape
    width = u.shape[1]
    tiles_per_seq = rows_per_mod // tm
    hb = tm // CONF_HALO
    n_halo_blocks = n // CONF_HALO
    vec = lambda wd: pl.BlockSpec((1, wd), lambda i: (0, 0))
    return pl.pallas_call(
        functools.partial(_confout_kernel, tm=tm, tiles_per_seq=tiles_per_seq),
        grid=(n // tm,),
        in_specs=[pl.BlockSpec((tm, width), lambda i: (i, 0)),
                  pl.BlockSpec((CONF_HALO, width), lambda i: (jnp.maximum(i * hb - 1, 0), 0)),
                  pl.BlockSpec((CONF_HALO, width), lambda i: (jnp.minimum((i + 1) * hb, n_halo_blocks - 1), 0)),
                  pl.BlockSpec((tm, d), lambda i: (i, 0)),
                  pl.BlockSpec((1, 6, d), lambda i: (i // tiles_per_seq, 0, 0)),
                  vec(d),
                  pl.BlockSpec(dww.shape, lambda i: (0, 0)),
                  vec(width), vec(width), vec(width),
                  _resident(w.shape),
                  vec(d)],
        out_specs=pl.BlockSpec((tm, d), lambda i: (i, 0)),
        out_shape=jax.ShapeDtypeStruct((n, d), F32),
        scratch_shapes=[pltpu.VMEM((tm + 2 * CONF_HALO, width), F32), pltpu.VMEM((tm, width), F32),
                        pltpu.VMEM((SUBLANES, tm + 2 * CONF_HALO - SUBLANES, LANES), F32)],
        compiler_params=_cparams(("arbitrary",)),
        name="confout",
    )(u, u, u, x2, mod, gain, dww, dwb, lng, lnb, w, b)


def _rope_tables(n_tokens, reps):
    rows = n_tokens // GRID_W
    row = jnp.broadcast_to(jnp.arange(rows, dtype=F32)[:, None], (rows, GRID_W)).reshape(n_tokens)
    col = jnp.broadcast_to(jnp.arange(GRID_W, dtype=F32)[None, :], (rows, GRID_W)).reshape(n_tokens)
    inv_freq = ROPE_THETA ** (-jnp.arange(ROPE_AXIS_PAIRS, dtype=F32) / ROPE_AXIS_PAIRS)
    ang_r = row[:, None] * inv_freq
    ang_c = col[:, None] * inv_freq
    ang = jnp.concatenate([ang_r, ang_r, ang_c, ang_c], axis=-1)
    cos, sin = jnp.cos(ang), jnp.sin(ang)
    first_half = (jnp.arange(B_HEAD_DIM) % (2 * ROPE_AXIS_PAIRS)) < ROPE_AXIS_PAIRS
    sin_lo = jnp.where(first_half, -sin, 0.0)
    sin_hi = jnp.where(first_half, 0.0, sin)
    tile = lambda t: jnp.tile(t, (1, reps))
    return tile(cos), tile(sin_lo), tile(sin_hi)


def _hybrid_in_weight(w_in):
    off_z = 3 * A_WIDTH
    off_ba = off_z + A_WIDTH
    off_q = off_ba + 4 * A_HEADS
    off_k = off_q + B_WIDTH
    off_v = off_k + B_KV_WIDTH
    d = w_in.shape[0]
    ba = w_in[:, off_ba:off_q].reshape(d, 2, 2, A_HEADS)
    ba = ba.transpose(0, 3, 1, 2).reshape(d, 4 * A_HEADS)
    ba = jnp.pad(ba, ((0, 0), (0, LANES - 4 * A_HEADS)))
    w = jnp.concatenate([w_in[:, :off_z], w_in[:, off_z:off_ba], w_in[:, off_q:off_k], w_in[:, off_k:off_v],
                         w_in[:, off_v:], ba], axis=1)
    return w.astype(BF16)


def _row(v):
    return v.reshape(1, -1)


def kernel(x, c, ctx, c_ctx, w_mod, b_mod, g_mix_pre, g_mix_post, g_ffn_pre, g_ffn_post, w_ffn_in, w_ffn_out,
           hyb_w_in, hyb_conv_w, hyb_a_log, hyb_dt_bias, hyb_out_norm, hyb_q_norm, hyb_k_norm, hyb_w_out,
           conf_w_in, conf_b_in, conf_dw_w, conf_dw_b, conf_ln_g, conf_ln_b, conf_w_out, conf_b_out):
    bsz, n_lat, d = x.shape
    n_ctx = ctx.shape[1]
    depth = w_mod.shape[0]
    n = bsz * n_lat
    tm = ROW_TILE

    cond = jnp.zeros((SUBLANES, d), F32).at[:bsz].set(c).at[bsz].set(c_ctx)
    mods = _ada_terms(cond, w_mod, b_mod).reshape(depth, SUBLANES, 6, d)

    h = x.reshape(n, d)
    hc = ctx.reshape(bsz * n_ctx, d)
    for layer in range(depth):
        idx = layer // 2
        mod = mods[layer, :bsz]
        mod_ctx = mods[layer, bsz:bsz + 1]
        if layer % 2 == 0:
            w_in = _hybrid_in_weight(hyb_w_in[idx])
            gpre = _row(g_mix_pre[layer])
            k_gain = _row(jnp.tile(hyb_k_norm[idx], B_KV_HEADS))
            cos_k, slo_k, shi_k = _rope_tables(n_lat, B_KV_HEADS)
            no_pos = (jnp.ones((n_ctx, B_KV_WIDTH), F32), jnp.zeros((n_ctx, B_KV_WIDTH), F32),
                      jnp.zeros((n_ctx, B_KV_WIDTH), F32))
            qkv, z, qb, ba, k_lat, vt_lat = _hyb_inproj(h, mod, gpre, w_in, hyb_conv_w[idx], k_gain,
                                                        cos_k, slo_k, shi_k, bsz, n_lat, tm)
            qkv_c, _, _, ba_c, k_ctx, vt_ctx = _hyb_inproj(hc, mod_ctx, gpre, w_in, hyb_conv_w[idx], k_gain,
                                                           *no_pos, bsz, n_ctx, n_ctx)

            coef = jnp.stack([hyb_a_log[idx], hyb_dt_bias[idx]])
            coef = jnp.pad(coef.transpose(0, 2, 1), ((0, 0), (0, 0), (2, 0))).reshape(2, 4 * A_HEADS)
            coef = jnp.pad(coef, ((0, SUBLANES - 2), (0, LANES - 4 * A_HEADS)))
            o_f, o_b = _delta_mixer(qkv, qkv_c, ba, ba_c, coef, bsz, n_lat, n_ctx)

            k_hm = jnp.concatenate([k_lat, k_ctx], axis=2)
            vt_hm = jnp.concatenate([vt_lat, vt_ctx], axis=3)
            cos_q, slo_q, shi_q = _rope_tables(n_lat, B_GROUP)
            yb = _attention(qb, _row(jnp.tile(hyb_q_norm[idx], B_GROUP)), cos_q, slo_q, shi_q,
                            k_hm, vt_hm, n_lat, ATTN_Q_TILE)

            w_out = hyb_w_out[idx].astype(BF16)
            h = _outproj(o_f, o_b, z, _row(hyb_out_norm[idx]), yb, h, mod, _row(g_mix_post[layer]),
                         w_out[:A_WIDTH], w_out[A_WIDTH:], n_lat, tm)
        else:
            u = _confin(h, mod, _row(g_mix_pre[layer]), conf_w_in[idx].astype(BF16), _row(conf_b_in[idx]),
                        n_lat, tm)
            h = _confout(u, h, mod, _row(g_mix_post[layer]), conf_dw_w[idx], _row(conf_dw_b[idx]),
                         _row(conf_ln_g[idx]), _row(conf_ln_b[idx]), conf_w_out[idx].astype(BF16),
                         _row(conf_b_out[idx]), n_lat, CONF_TILE)
        h = _ffn(h, mod, _row(g_ffn_pre[layer]), _row(g_ffn_post[layer]), w_ffn_in[layer].astype(BF16),
                 w_ffn_out[layer].astype(BF16), n_lat, tm)
        assert not any(j % 2 == 0 for j in range(layer + 1, depth)), "context advance not implemented"
    return h.reshape(bsz, n_lat, d)
```

```python
import functools
import math

import jax
import jax.numpy as jnp
from jax import lax
from jax.experimental import pallas as pl
from jax.experimental.pallas import tpu as pltpu

F32 = jnp.float32
BF16 = jnp.bfloat16

EPS = 1e-6
GRID_W = 64
ROPE_THETA = 10000.0
A_HEADS = 4
A_HEAD_DIM = 128
A_WIDTH = A_HEADS * A_HEAD_DIM
SHORT_CONV_W = 5
CHUNK = 64
B_Q_HEADS = 8
B_KV_HEADS = 2
B_HEAD_DIM = 64
B_GROUP = B_Q_HEADS // B_KV_HEADS
B_WIDTH = B_Q_HEADS * B_HEAD_DIM
B_KV_WIDTH = B_KV_HEADS * B_HEAD_DIM
ROPE_AXIS_PAIRS = B_HEAD_DIM // 4
CONF_KERNEL = 31
LOG2_E = math.log2(math.e)

LANES = 128
SUBLANES = 8
VMEM_LIMIT = 56 * 1024 * 1024
ROW_TILE = 1024
CONF_TILE = 512
CONF_HALO = 16
PROJ_HALO = 16
CONV_PASS_ROWS = 256
DELTA_BLOCK = 4
DELTA_ROWS = DELTA_BLOCK * CHUNK
ATTN_Q_TILE = 512
ATTN_KEY_BLOCK = 128
ATTN_Q_COLS = 256
ATTN_STREAMS = 4
ATTN_VT_ROWS = B_HEAD_DIM + 16


def _cparams(sem):
    return pltpu.CompilerParams(dimension_semantics=sem, vmem_limit_bytes=VMEM_LIMIT)


def _resident(shape):
    return pl.BlockSpec(shape, lambda i: (0, 0), pipeline_mode=pl.Buffered(1))


def _silu(x):
    return x * jax.nn.sigmoid(x)


def _dot(a, b):
    return jnp.dot(a, b, preferred_element_type=F32)


def _dot_nt(a, b):
    return lax.dot_general(a, b, (((1,), (1,)), ((), ())), preferred_element_type=F32)


def _dot_tn(a, b):
    return lax.dot_general(a, b, (((0,), (0,)), ((), ())), preferred_element_type=F32)


def _rms(x, gain):
    return x * lax.rsqrt(jnp.mean(x * x, axis=-1, keepdims=True) + EPS) * gain


def _split_bf16(a):
    hi = a.astype(BF16)
    return hi, (a - hi.astype(F32)).astype(BF16)


def _drain(*gens):
    last = [None] * len(gens)
    live = list(range(len(gens)))
    while live:
        for g in list(live):
            try:
                last[g] = next(gens[g])
            except StopIteration:
                live.remove(g)
    return last


def _ada_kernel(c_ref, w_ref, b_ref, o_ref):
    s_hi, s_lo = _split_bf16(_silu(c_ref[...]))
    w_hi, w_lo = _split_bf16(w_ref[0])
    o_ref[0] = _dot(s_hi, w_hi) + _dot(s_hi, w_lo) + _dot(s_lo, w_hi) + b_ref[0]


def _ada_terms(cond, w_mod, b_mod):
    depth, d, n6 = w_mod.shape
    tn = n6 // 4
    return pl.pallas_call(
        _ada_kernel,
        grid=(depth, n6 // tn),
        in_specs=[pl.BlockSpec((SUBLANES, d), lambda l, j: (0, 0)),
                  pl.BlockSpec((1, d, tn), lambda l, j: (l, 0, j)),
                  pl.BlockSpec((1, 1, tn), lambda l, j: (l, 0, j))],
        out_specs=pl.BlockSpec((1, SUBLANES, tn), lambda l, j: (l, 0, j)),
        out_shape=jax.ShapeDtypeStruct((depth, SUBLANES, n6), F32),
        compiler_params=_cparams(("arbitrary", "arbitrary")),
        name="ada",
    )(cond, w_mod, b_mod.reshape(depth, 1, n6))


def _modulated(x, mod_ref, gain_ref, shift_row):
    y = _rms(x, gain_ref[...])
    return y * (1.0 + mod_ref[0, shift_row + 1:shift_row + 2, :]) + mod_ref[0, shift_row:shift_row + 1, :]


def _gated_residual(x, y, mod_ref, gain_ref, gate_row):
    return x + mod_ref[0, gate_row:gate_row + 1, :] * _rms(y, gain_ref[...])


def _head_rms(x, gain_row, head_dim):
    outs = []
    for s in range(x.shape[1] // LANES):
        xs = x[:, s * LANES:(s + 1) * LANES]
        lane = lax.broadcasted_iota(jnp.int32, xs.shape, 1)
        sq = xs * xs
        scale = jnp.zeros_like(xs)
        for part in range(LANES // head_dim):
            m = jnp.logical_and(lane >= part * head_dim, lane < (part + 1) * head_dim)
            ms = jnp.sum(jnp.where(m, sq, 0.0), axis=-1, keepdims=True) * (1.0 / head_dim)
            scale = jnp.where(m, lax.rsqrt(ms + EPS), scale)
        outs.append(xs * scale)
    y = outs[0] if len(outs) == 1 else jnp.concatenate(outs, axis=1)
    return y * gain_row


def _rope(x, cos, sin_lo, sin_hi):
    width = x.shape[1]
    fwd = pltpu.roll(x, width - ROPE_AXIS_PAIRS, axis=1)
    back = pltpu.roll(x, ROPE_AXIS_PAIRS, axis=1)
    return x * cos + fwd * sin_lo + back * sin_hi


def _hyb_inproj_kernel(x_ref, xp_ref, xn_ref, mod_ref, g_ref, w_ref, cw_ref, kg_ref, cos_ref, slo_ref, shi_ref,
                       qkv_ref, z_ref, qb_ref, ba_ref, ko_ref, vo_ref, *, tiles_per_seq):
    pos = pl.program_id(0) % tiles_per_seq
    tm = x_ref.shape[0]
    halo = PROJ_HALO
    qkv_w = 3 * A_WIDTH
    a_prev = jnp.where(pos > 0, _modulated(xp_ref[...], mod_ref, g_ref, 0), 0.0).astype(BF16)
    a_next = jnp.where(pos < tiles_per_seq - 1, _modulated(xn_ref[...], mod_ref, g_ref, 0), 0.0).astype(BF16)
    a = _modulated(x_ref[...], mod_ref, g_ref, 0).astype(BF16)
    a_ext = jnp.concatenate([a_prev, a, a_next], axis=0)

    off_z = qkv_w
    off_q = off_z + A_WIDTH
    off_k = off_q + B_WIDTH
    off_v = off_k + B_KV_WIDTH
    off_ba = off_v + B_KV_WIDTH

    def gate_proj():
        z_ref[...] = _dot(a, w_ref[:, off_z:off_q])

    def query_proj():
        qb_ref[...] = _dot(a, w_ref[:, off_q:off_k])

    def key_value_proj():
        kb = _dot(a, w_ref[:, off_k:off_v])
        vb = _dot(a, w_ref[:, off_v:off_ba])
        k = _rope(_head_rms(kb, kg_ref[...], B_HEAD_DIM), cos_ref[...], slo_ref[...], shi_ref[...]).astype(BF16)
        vt = vb.T.astype(BF16)
        for hh in range(B_KV_HEADS):
            ko_ref[0, hh] = k[:, hh * B_HEAD_DIM:(hh + 1) * B_HEAD_DIM]
            vo_ref[0, hh, :B_HEAD_DIM, :] = vt[hh * B_HEAD_DIM:(hh + 1) * B_HEAD_DIM, :]
            vo_ref[0, hh, B_HEAD_DIM:, :] = jnp.ones((ATTN_VT_ROWS - B_HEAD_DIM, tm), BF16)

    def logit_proj():
        ba_ref[...] = _dot(a, w_ref[:, off_ba:off_ba + LANES])

    others = [gate_proj, query_proj, key_value_proj, logit_proj]
    rows_per_pass = min(tm, CONV_PASS_ROWS)
    pair = 2 * A_HEAD_DIM
    n_pairs = qkv_w // pair
    ext_next = _dot(a_ext, w_ref[:, :pair])
    for cb in range(n_pairs):
        cols = slice(cb * pair, (cb + 1) * pair)
        ext = ext_next
        if cb + 1 < n_pairs:
            ext_next = _dot(a_ext, w_ref[:, (cb + 1) * pair:(cb + 2) * pair])
        if cb < len(others):
            others[cb]()
        w = cw_ref[:, cols]
        for r0 in range(0, tm, rows_per_pass):
            sub = ext[r0:r0 + rows_per_pass + 2 * halo, :]
            acc = jnp.zeros((rows_per_pass, pair), F32)
            for tap in range(SHORT_CONV_W):
                shift = (SHORT_CONV_W // 2 - tap) % (rows_per_pass + 2 * halo)
                rolled = pltpu.roll(sub, shift, axis=0) if shift else sub
                acc = acc + rolled[halo:halo + rows_per_pass, :] * w[tap:tap + 1, :]
            y = _silu(acc)
            for part in range(2):
                head = 2 * cb + part
                yh = y[:, part * A_HEAD_DIM:(part + 1) * A_HEAD_DIM]
                if head < 2 * A_HEADS:
                    yh = yh * lax.rsqrt(jnp.sum(yh * yh, axis=-1, keepdims=True) + EPS)
                if head < A_HEADS:
                    yh = yh * (A_HEAD_DIM ** -0.5)
                qkv_ref[r0:r0 + rows_per_pass, head * A_HEAD_DIM:(head + 1) * A_HEAD_DIM] = yh


def _hyb_inproj(x2, mod, gain, w, conv_w, k_gain, cos, slo, shi, bsz, rows_per_seq, tm):
    n, d = x2.shape
    tiles_per_seq = rows_per_seq // tm
    hb = tm // PROJ_HALO
    n_halo = n // PROJ_HALO
    f32_out = lambda wd: (pl.BlockSpec((tm, wd), lambda i: (i, 0)), jax.ShapeDtypeStruct((n, wd), F32))
    outs = [f32_out(3 * A_WIDTH), f32_out(A_WIDTH), f32_out(B_WIDTH), f32_out(LANES),
            (pl.BlockSpec((1, B_KV_HEADS, tm, B_HEAD_DIM), lambda i: (i // tiles_per_seq, 0, i % tiles_per_seq, 0)),
             jax.ShapeDtypeStruct((bsz, B_KV_HEADS, rows_per_seq, B_HEAD_DIM), BF16)),
            (pl.BlockSpec((1, B_KV_HEADS, ATTN_VT_ROWS, tm), lambda i: (i // tiles_per_seq, 0, 0, i % tiles_per_seq)),
             jax.ShapeDtypeStruct((bsz, B_KV_HEADS, ATTN_VT_ROWS, rows_per_seq), BF16))]
    tab = pl.BlockSpec((tm, B_KV_WIDTH), lambda i: (i % tiles_per_seq, 0))
    return pl.pallas_call(
        functools.partial(_hyb_inproj_kernel, tiles_per_seq=tiles_per_seq),
        grid=(n // tm,),
        in_specs=[pl.BlockSpec((tm, d), lambda i: (i, 0)),
                  pl.BlockSpec((PROJ_HALO, d), lambda i: (jnp.maximum(i * hb - 1, 0), 0)),
                  pl.BlockSpec((PROJ_HALO, d), lambda i: (jnp.minimum((i + 1) * hb, n_halo - 1), 0)),
                  pl.BlockSpec((1, 6, d), lambda i: (i // tiles_per_seq if mod.shape[0] > 1 else 0, 0, 0)),
                  pl.BlockSpec((1, d), lambda i: (0, 0)),
                  _resident(w.shape),
                  _resident(conv_w.shape),
                  pl.BlockSpec((1, B_KV_WIDTH), lambda i: (0, 0)), tab, tab, tab],
        out_specs=[o[0] for o in outs],
        out_shape=[o[1] for o in outs],
        compiler_params=_cparams(("arbitrary",)),
        name="inproj",
    )(x2, x2, x2, mod, gain, w, conv_w, k_gain, cos, slo, shi)


def _unit_tri_inverse_steps(l_mats, i, j):
    eye = (i == j).astype(F32)
    same16 = jnp.right_shift(i, 4) == jnp.right_shift(j, 4)
    same32 = jnp.right_shift(i, 5) == jnp.right_shift(j, 5)
    off32 = jnp.logical_and(same32, jnp.logical_not(same16))
    b = lambda a: a.astype(BF16)
    each = lambda f, *ls: [f(*xs) for xs in zip(*ls)]
    d1 = each(lambda l: b(jnp.where(same16, l, 0.0)), l_mats)
    p = each(lambda d: eye - d.astype(F32), d1)
    dk = d1
    for _ in range(3):
        dk = each(lambda d: b(_dot(d, d)), dk)
        yield None
        p = each(lambda pp, d: pp + _dot(b(pp), d), p, dk)
        yield None
    for sel in (off32, jnp.logical_not(same32)):
        cm = each(lambda l: b(jnp.where(sel, l, 0.0)), l_mats)
        pb = each(b, p)
        inner = each(lambda c, q: b(_dot(c, q)), cm, pb)
        yield None
        p = each(lambda pp, q, m: pp - _dot(q, m), p, pb, inner)
        yield None
    l_split = each(_split_bf16, l_mats)
    t_split = each(_split_bf16, p)
    resid = each(lambda t0, ls, ts: eye - t0 - (_dot(ls[0], ts[0]) + _dot(ls[0], ts[1]) + _dot(ls[1], ts[0])),
                 p, l_split, t_split)
    yield None
    yield each(lambda t0, ts, r: t0 + _dot(ts[0], b(r)), p, t_split, resid)


def _delta_kernel(qf_ref, kf_ref, vf_ref, qr_ref, kr_ref, vr_ref, qc_ref, kc_ref, vc_ref,
                  baf_ref, bar_ref, bac_ref, coef_ref, of_ref, ob_ref,
                  u_s, w_s, kt_s, qd_s, in_s, el_s, state):
    step = pl.program_id(1)
    is_ctx = step == 0
    front_slot = step % 2
    scan_slot = 1 - front_slot
    uw_write, uw_read = front_slot, scan_slot
    heads = [slice(hh * A_HEAD_DIM, (hh + 1) * A_HEAD_DIM) for hh in range(A_HEADS)]
    chunks = [slice(c * CHUNK, (c + 1) * CHUNK) for c in range(DELTA_BLOCK)]

    @pl.when(step == 0)
    def _():
        for scr in (u_s, w_s, kt_s, qd_s, in_s, el_s):
            scr[1] = jnp.zeros(scr.shape[1:], scr.dtype)
        state[...] = jnp.zeros(state.shape, F32)

    chains = [(d, hh) for d in range(2) for hh in range(A_HEADS)]
    s_mats = [state[d * A_HEADS + hh] for d, hh in chains]
    out_refs = (of_ref, ob_ref)

    def scan_substep(t, s_mats):
        where = []
        for d, hh in chains:
            c = t if d == 0 else DELTA_BLOCK - 1 - t
            where.append((d, c, chunks[c], heads[hh]))
        ws = [_dot(jnp.concatenate([w_s[uw_read, d, r, ln], qd_s[scan_slot, d, r, ln]], axis=0), s.astype(BF16))
              for (d, c, r, ln), s in zip(where, s_mats)]
        v_new = [(u_s[uw_read, d, r, ln] - x[:CHUNK]).astype(BF16) for (d, c, r, ln), x in zip(where, ws)]
        for (d, c, r, ln), x, vn in zip(where, ws, v_new):
            out_refs[d][r, ln] = x[CHUNK:] + _dot(in_s[scan_slot, d, r, ln][:, :CHUNK], vn)
        return [s * el_s[scan_slot, d, c * SUBLANES:c * SUBLANES + 1, ln] + _dot_tn(kt_s[scan_slot, d, r, ln], vn)
                for (d, c, r, ln), s, vn in zip(where, s_mats, v_new)]

    def per_head(main_ref, ctx_ref):
        x = jnp.where(is_ctx, ctx_ref[...], main_ref[...])
        return [x[:, ln] for ln in heads]

    q_h = (per_head(qf_ref, qc_ref), per_head(qr_ref, qc_ref))
    k_h = (per_head(kf_ref, kc_ref), per_head(kr_ref, kc_ref))
    v_h = (per_head(vf_ref, vc_ref), per_head(vr_ref, vc_ref))

    lane = lax.broadcasted_iota(jnp.int32, (1, LANES), 1)
    is_beta = jnp.bitwise_and(lane, 3) < 2
    neg_a = -jnp.exp(coef_ref[0:1, :])
    dtb = coef_ref[1:2, :]
    bg = []
    for raw in (jnp.where(is_ctx, bac_ref[...], baf_ref[...]), jnp.where(is_ctx, bac_ref[...], bar_ref[...])):
        xg = raw + dtb
        softplus = jnp.maximum(xg, 0.0) + jnp.log(1.0 + jnp.exp(-jnp.abs(xg)))
        bg.append(jnp.where(is_beta, jax.nn.sigmoid(raw), neg_a * softplus))

    i_idx = lax.broadcasted_iota(jnp.int32, (CHUNK, CHUNK), 0)
    j_idx = lax.broadcasted_iota(jnp.int32, (CHUNK, CHUNK), 1)
    i_w = lax.broadcasted_iota(jnp.int32, (CHUNK, LANES), 0)
    j_w = lax.broadcasted_iota(jnp.int32, (CHUNK, LANES), 1)
    in_chunk = j_w < CHUNK
    ones_rows = jnp.ones((2 * SUBLANES, CHUNK), BF16)

    def front_steps(d):
        incl = (i_idx >= j_idx) if d == 0 else (i_idx <= j_idx)
        strict = (i_idx > j_idx) if d == 0 else (i_idx < j_idx)
        incl_w = jnp.logical_and(in_chunk, (i_w >= j_w) if d == 0 else (i_w <= j_w))
        seen_w = jnp.logical_and(in_chunk, (i_w <= j_w) if d == 0 else (i_w >= j_w))
        seen_cat = jnp.concatenate([seen_w] * DELTA_BLOCK, axis=1)
        incl_b = incl.astype(BF16)
        for hh, ln in enumerate(heads):
            g_lane = 4 * hh + 2 + d
            g_cat = jnp.concatenate([jnp.broadcast_to(bg[d][r, g_lane:g_lane + 1], (CHUNK, LANES)) for r in chunks],
                                    axis=1)
            g_hi = g_cat.astype(BF16)
            g_rest = g_cat - g_hi.astype(F32)
            g_mid = g_rest.astype(BF16)
            pieces = (g_hi, g_mid, (g_rest - g_mid.astype(F32)).astype(BF16))
            gc_all = sum(_dot(incl_b, p) for p in pieces)
            gr_all = sum(_dot(ones_rows, jnp.where(seen_cat, p, jnp.zeros_like(p))) for p in pieces)
            for c, r in enumerate(chunks):
                q, k, v = q_h[d][hh][r], k_h[d][hh][r], v_h[d][hh][r]
                kbf = k.astype(BF16)
                prod = _dot_nt(jnp.concatenate([kbf, q.astype(BF16)], axis=0),
                               jnp.concatenate([kbf, jnp.zeros_like(kbf)], axis=0))
                beta = bg[d][r, 4 * hh + d:4 * hh + d + 1]
                gc_w = gc_all[:, c * LANES:(c + 1) * LANES]
                gr_w = jnp.broadcast_to(gr_all[0:1, c * LANES:(c + 1) * LANES], (CHUNK, LANES))
                decay = jnp.where(incl_w, jnp.exp(jnp.minimum(gc_w - gr_w, 0.0)), 0.0)
                g_last = gc_w[CHUNK - 1:CHUNK, :] if d == 0 else gc_w[0:1, :]
                egc = jnp.exp(gc_w)
                l_mats.append(jnp.where(strict, (prod[:CHUNK] * beta * decay)[:, :CHUNK], 0.0))
                rhs_all.append(jnp.concatenate([v * beta, k * beta * egc], axis=1).astype(BF16))
                dests.append((d, r, ln))
                kt_s[front_slot, d, r, ln] = (k * jnp.exp(g_last - gc_w)).astype(BF16)
                qd_s[front_slot, d, r, ln] = (q * egc).astype(BF16)
                in_s[front_slot, d, r, ln] = (prod[CHUNK:] * decay).astype(BF16)
                el_s[front_slot, d, c * SUBLANES:(c + 1) * SUBLANES, ln] = jnp.broadcast_to(jnp.exp(g_last),
                                                                                            (SUBLANES, LANES))
                yield None

    def scan_steps(s_mats):
        for t in range(DELTA_BLOCK):
            s_mats = scan_substep(t, s_mats)
            yield s_mats
            yield s_mats
            yield s_mats

    l_mats, rhs_all, dests = [], [], []
    _, _, s_mats = _drain(front_steps(0), front_steps(1), scan_steps(s_mats))
    inverse = _drain(_unit_tri_inverse_steps(l_mats, i_idx, j_idx))[0]
    t_split = [_split_bf16(t) for t in inverse]
    sols = [_dot(t_hi, rr) + _dot(t_lo, rr) for (t_hi, t_lo), rr in zip(t_split, rhs_all)]
    for (d, r, ln), sol in zip(dests, sols):
        u_s[uw_write, d, r, ln] = sol[:, :A_HEAD_DIM]
        w_s[uw_write, d, r, ln] = sol[:, A_HEAD_DIM:].astype(BF16)
    for (d, hh), s in zip(chains, s_mats):
        state[d * A_HEADS + hh] = s


def _delta_mixer(qkv, qkv_c, ba, ba_c, coef, bsz, n_lat, n_ctx):
    assert n_ctx == DELTA_ROWS and n_lat % DELTA_ROWS == 0
    wd = A_WIDTH
    lat_blocks = n_lat // DELTA_ROWS
    n_steps = lat_blocks + 2

    def fwd_blk(b, s):
        return b * lat_blocks + jnp.clip(s - 1, 0, lat_blocks - 1)

    def bwd_blk(b, s):
        return b * lat_blocks + jnp.clip(lat_blocks - s, 0, lat_blocks - 1)

    def lat(blk_fn, kd):
        return pl.BlockSpec((DELTA_ROWS, wd), lambda b, s: (blk_fn(b, s), kd))

    def ctxb(kd):
        return pl.BlockSpec((n_ctx, wd), lambda b, s: (b, kd))

    in_specs = ([lat(fwd_blk, kd) for kd in range(3)] + [lat(bwd_blk, kd) for kd in range(3)]
                + [ctxb(kd) for kd in range(3)]
                + [pl.BlockSpec((DELTA_ROWS, LANES), lambda b, s: (fwd_blk(b, s), 0)),
                   pl.BlockSpec((DELTA_ROWS, LANES), lambda b, s: (bwd_blk(b, s), 0)),
                   pl.BlockSpec((n_ctx, LANES), lambda b, s: (b, 0)),
                   pl.BlockSpec((SUBLANES, LANES), lambda b, s: (0, 0))])
    out_f = pl.BlockSpec((DELTA_ROWS, wd), lambda b, s: (b * lat_blocks + jnp.maximum(s - 2, 0), 0))
    out_b = pl.BlockSpec((DELTA_ROWS, wd), lambda b, s: (b * lat_blocks + lat_blocks - jnp.maximum(s - 1, 1), 0))
    shape = jax.ShapeDtypeStruct((bsz * n_lat, wd), F32)
    slots = lambda rows, dt: pltpu.VMEM((2, 2, rows, wd), dt)
    return pl.pallas_call(
        _delta_kernel,
        grid=(bsz, n_steps),
        in_specs=in_specs,
        out_specs=[out_f, out_b],
        out_shape=[shape, shape],
        scratch_shapes=[slots(DELTA_ROWS, F32), slots(DELTA_ROWS, BF16), slots(DELTA_ROWS, BF16),
                        slots(DELTA_ROWS, BF16), slots(DELTA_ROWS, BF16), slots(DELTA_BLOCK * SUBLANES, F32),
                        pltpu.VMEM((2 * A_HEADS, A_HEAD_DIM, A_HEAD_DIM), F32)],
        compiler_params=_cparams(("arbitrary", "arbitrary")),
        name="delta",
    )(*([qkv] * 6), *([qkv_c] * 3), ba, ba, ba_c, coef)


def _attn_kernel(q_ref, g_ref, cos_ref, slo_ref, shi_ref, k_ref, vt_ref, o_ref):
    q = _head_rms(q_ref[...], g_ref[...], B_HEAD_DIM)
    q = _rope(q, cos_ref[...], slo_ref[...], shi_ref[...]) * (B_HEAD_DIM ** -0.5 * LOG2_E)
    tq = q.shape[0]
    qt = q.T.astype(BF16)
    n_keys = k_ref.shape[2]
    all_streams = [qt[g * B_HEAD_DIM:(g + 1) * B_HEAD_DIM, c:c + ATTN_Q_COLS]
                   for g in range(B_GROUP) for c in range(0, tq, ATTN_Q_COLS)]

    def softmax_streams(q_s):
        m = [jnp.full((1, ATTN_Q_COLS), -1e30, F32) for _ in q_s]
        acc = [jnp.zeros((ATTN_VT_ROWS, ATTN_Q_COLS), F32) for _ in q_s]

        def scores(kb):
            k_blk = k_ref[0, 0, kb:kb + ATTN_KEY_BLOCK, :]
            return [_dot(k_blk, qs) for qs in q_s]

        def value_update(acc, pending):
            alpha, p, kb = pending
            vt = vt_ref[0, 0, :, kb:kb + ATTN_KEY_BLOCK]
            return [a * al + _dot(vt, pp) for a, al, pp in zip(acc, alpha, p)]

        st_next = scores(0)
        pending = None
        for kb in range(0, n_keys, ATTN_KEY_BLOCK):
            st = st_next
            if kb + ATTN_KEY_BLOCK < n_keys:
                st_next = scores(kb + ATTN_KEY_BLOCK)
            m_new = [jnp.maximum(mo, jnp.max(s, axis=0, keepdims=True)) for mo, s in zip(m, st)]
            alpha = [jnp.exp2(mo - mn) for mo, mn in zip(m, m_new)]
            p = [jnp.exp2((s - mn).astype(BF16)) for s, mn in zip(st, m_new)]
            if pending is not None:
                acc = value_update(acc, pending)
            pending = (alpha, p, kb)
            m = m_new
        acc = value_update(acc, pending)
        return [a[:B_HEAD_DIM] / a[B_HEAD_DIM:B_HEAD_DIM + 1] for a in acc]

    outs = []
    for first in range(0, len(all_streams), ATTN_STREAMS):
        outs += softmax_streams(all_streams[first:first + ATTN_STREAMS])
    n_col = tq // ATTN_Q_COLS
    o_ref[...] = jnp.concatenate([jnp.concatenate(outs[g * n_col:(g + 1) * n_col], axis=1)
                                  for g in range(B_GROUP)], axis=0).T.astype(o_ref.dtype)


def _attention(qb, gain, cos, slo, shi, k_hm, vt_hm, n_lat, tq):
    n, _ = qb.shape
    bsz, _, s_len, _ = k_hm.shape
    assert s_len % ATTN_KEY_BLOCK == 0 and tq % ATTN_Q_COLS == 0
    gw = B_GROUP * B_HEAD_DIM
    nq = n_lat // tq
    kv = pl.BlockSpec((1, 1, s_len, B_HEAD_DIM), lambda b, kh, i: (b, kh, 0, 0))
    vts = pl.BlockSpec((1, 1, ATTN_VT_ROWS, s_len), lambda b, kh, i: (b, kh, 0, 0))
    tab = pl.BlockSpec((tq, gw), lambda b, kh, i: (i, 0))
    return pl.pallas_call(
        _attn_kernel,
        grid=(bsz, B_KV_HEADS, nq),
        in_specs=[pl.BlockSpec((tq, gw), lambda b, kh, i: (b * nq + i, kh)),
                  pl.BlockSpec((1, gw), lambda b, kh, i: (0, 0)), tab, tab, tab, kv, vts],
        out_specs=pl.BlockSpec((tq, gw), lambda b, kh, i: (b * nq + i, kh)),
        out_shape=jax.ShapeDtypeStruct((n, B_WIDTH), BF16),
        compiler_params=_cparams(("arbitrary", "arbitrary", "arbitrary")),
        name="attn",
    )(qb, gain, cos, slo, shi, k_hm, vt_hm)


def _outproj_kernel(of_ref, ob_ref, z_ref, on_ref, yb_ref, x_ref, mod_ref, g_ref, wa_ref, wb_ref, o_ref):
    o = of_ref[...] + ob_ref[...]
    z = z_ref[...]
    parts = []
    for hh in range(A_HEADS):
        sl = slice(hh * A_HEAD_DIM, (hh + 1) * A_HEAD_DIM)
        parts.append((_rms(o[:, sl], on_ref[...]) * _silu(z[:, sl])).astype(BF16))
    ya = jnp.concatenate(parts, axis=1)
    y = _dot(ya, wa_ref[...]) + _dot(yb_ref[...], wb_ref[...])
    o_ref[...] = _gated_residual(x_ref[...], y, mod_ref, g_ref, 2)


def _outproj(o_f, o_b, z, out_norm, yb, x2, mod, gain, wa, wb, rows_per_mod, tm):
    n, d = x2.shape
    bpm = rows_per_mod // tm
    return pl.pallas_call(
        _outproj_kernel,
        grid=(n // tm,),
        in_specs=[pl.BlockSpec((tm, o_f.shape[1]), lambda i: (i, 0)),
                  pl.BlockSpec((tm, o_b.shape[1]), lambda i: (i, 0)),
                  pl.BlockSpec((tm, z.shape[1]), lambda i: (i, 0)),
                  pl.BlockSpec((1, A_HEAD_DIM), lambda i: (0, 0)),
                  pl.BlockSpec((tm, yb.shape[1]), lambda i: (i, 0)),
                  pl.BlockSpec((tm, d), lambda i: (i, 0)),
                  pl.BlockSpec((1, 6, d), lambda i: (i // bpm, 0, 0)),
                  pl.BlockSpec((1, d), lambda i: (0, 0)),
                  _resident(wa.shape),
                  _resident(wb.shape)],
        out_specs=pl.BlockSpec((tm, d), lambda i: (i, 0)),
        out_shape=jax.ShapeDtypeStruct((n, d), F32),
        compiler_params=_cparams(("arbitrary",)),
        name="outproj",
    )(o_f, o_b, z, out_norm, yb, x2, mod, gain, wa, wb)


def _ffn_kernel(x_ref, mod_ref, gpre_ref, gpost_ref, wi_ref, wo_ref, o_ref, *, hidden, hc):
    x = x_ref[...]
    a = _modulated(x, mod_ref, gpre_ref, 3).astype(BF16)
    acc = jnp.zeros(x.shape, F32)
    for c in range(hidden // hc):
        gate = _dot(a, wi_ref[:, c * hc:(c + 1) * hc])
        up = _dot(a, wi_ref[:, hidden + c * hc:hidden + (c + 1) * hc])
        acc = acc + _dot((_silu(gate) * up).astype(BF16), wo_ref[c * hc:(c + 1) * hc, :])
    o_ref[...] = _gated_residual(x, acc, mod_ref, gpost_ref, 5)


def _ffn(x2, mod, gpre, gpost, wi, wo, rows_per_mod, tm):
    n, d = x2.shape
    hidden = wo.shape[0]
    bpm = rows_per_mod // tm
    return pl.pallas_call(
        functools.partial(_ffn_kernel, hidden=hidden, hc=2 * LANES),
        grid=(n // tm,),
        in_specs=[pl.BlockSpec((tm, d), lambda i: (i, 0)),
                  pl.BlockSpec((1, 6, d), lambda i: (i // bpm, 0, 0)),
                  pl.BlockSpec((1, d), lambda i: (0, 0)),
                  pl.BlockSpec((1, d), lambda i: (0, 0)),
                  _resident(wi.shape),
                  _resident(wo.shape)],
        out_specs=pl.BlockSpec((tm, d), lambda i: (i, 0)),
        out_shape=jax.ShapeDtypeStruct((n, d), F32),
        compiler_params=_cparams(("arbitrary",)),
        name="ffn",
    )(x2, mod, gpre, gpost, wi, wo)


def _confin_kernel(x_ref, mod_ref, g_ref, w_ref, b_ref, o_ref, *, width):
    a = _modulated(x_ref[...], mod_ref, g_ref, 0).astype(BF16)
    val = _dot(a, w_ref[:, :width]) + b_ref[:, :width]
    gate = _dot(a, w_ref[:, width:]) + b_ref[:, width:]
    o_ref[...] = val * jax.nn.sigmoid(gate)


def _confin(x2, mod, gain, w, b, rows_per_mod, tm):
    n, d = x2.shape
    width = w.shape[1] // 2
    bpm = rows_per_mod // tm
    return pl.pallas_call(
        functools.partial(_confin_kernel, width=width),
        grid=(n // tm,),
        in_specs=[pl.BlockSpec((tm, d), lambda i: (i, 0)),
                  pl.BlockSpec((1, 6, d), lambda i: (i // bpm, 0, 0)),
                  pl.BlockSpec((1, d), lambda i: (0, 0)),
                  _resident(w.shape),
                  pl.BlockSpec((1, 2 * width), lambda i: (0, 0))],
        out_specs=pl.BlockSpec((tm, width), lambda i: (i, 0)),
        out_shape=jax.ShapeDtypeStruct((n, width), F32),
        compiler_params=_cparams(("arbitrary",)),
        name="confin",
    )(x2, mod, gain, w, b)


def _confout_kernel(u_ref, up_ref, un_ref, x_ref, mod_ref, g_ref, dww_ref, dwb_ref, lng_ref, lnb_ref,
                    w_ref, b_ref, o_ref, ext, conv, shifted, *, tm, tiles_per_seq):
    pos = pl.program_id(0) % tiles_per_seq
    width = u_ref.shape[1]
    halo = CONF_HALO
    ext[pl.ds(halo, tm), :] = u_ref[...]
    ext[0:halo, :] = jnp.where(pos == 0, 0.0, up_ref[...])
    ext[pl.ds(halo + tm, halo), :] = jnp.where(pos == tiles_per_seq - 1, 0.0, un_ref[...])

    rb = 64
    pad = CONF_KERNEL // 2
    copy_rows = tm + 2 * halo - SUBLANES

    def col_body(c, carry):
        cs = pl.ds(pl.multiple_of(c * LANES, LANES), LANES)
        w = dww_ref[:, cs]
        for s in range(SUBLANES):
            shifted[s, :, :] = ext[pl.ds(s, copy_rows), cs]
        for r in range(tm // rb):
            acc = jnp.zeros((rb, LANES), F32)
            for tap in range(CONF_KERNEL):
                off = halo - pad + tap
                acc = acc + shifted[off % SUBLANES, pl.ds(r * rb + off - off % SUBLANES, rb), :] * w[tap:tap + 1, :]
            conv[pl.ds(r * rb, rb), cs] = acc
        return carry

    lax.fori_loop(0, width // LANES, col_body, 0)

    y = conv[...] + dwb_ref[...]
    mu = jnp.mean(y, axis=-1, keepdims=True)
    yc = y - mu
    var = jnp.mean(yc * yc, axis=-1, keepdims=True)
    y = _silu(yc * lax.rsqrt(var + EPS) * lng_ref[...] + lnb_ref[...])
    out = _dot(y.astype(BF16), w_ref[...]) + b_ref[...]
    o_ref[...] = _gated_residual(x_ref[...], out, mod_ref, g_ref, 2)


def _confout(u, x2, mod, gain, dww, dwb, lng, lnb, w, b, rows_per_mod, tm):
    n, d = x2.shape
    width = u.shape[1]
    tiles_per_seq = rows_per_mod // tm
    hb = tm // CONF_HALO
    n_halo_blocks = n // CONF_HALO
    vec = lambda wd: pl.BlockSpec((1, wd), lambda i: (0, 0))
    return pl.pallas_call(
        functools.partial(_confout_kernel, tm=tm, tiles_per_seq=tiles_per_seq),
        grid=(n // tm,),
        in_specs=[pl.BlockSpec((tm, width), lambda i: (i, 0)),
                  pl.BlockSpec((CONF_HALO, width), lambda i: (jnp.maximum(i * hb - 1, 0), 0)),
                  pl.BlockSpec((CONF_HALO, width), lambda i: (jnp.minimum((i + 1) * hb, n_halo_blocks - 1), 0)),
                  pl.BlockSpec((tm, d), lambda i: (i, 0)),
                  pl.BlockSpec((1, 6, d), lambda i: (i // tiles_per_seq, 0, 0)),
                  vec(d),
                  pl.BlockSpec(dww.shape, lambda i: (0, 0)),
                  vec(width), vec(width), vec(width),
                  _resident(w.shape),
                  vec(d)],
        out_specs=pl.BlockSpec((tm, d), lambda i: (i, 0)),
        out_shape=jax.ShapeDtypeStruct((n, d), F32),
        scratch_shapes=[pltpu.VMEM((tm + 2 * CONF_HALO, width), F32), pltpu.VMEM((tm, width), F32),
                        pltpu.VMEM((SUBLANES, tm + 2 * CONF_HALO - SUBLANES, LANES), F32)],
        compiler_params=_cparams(("arbitrary",)),
        name="confout",
    )(u, u, u, x2, mod, gain, dww, dwb, lng, lnb, w, b)


def _rope_tables(n_tokens, reps):
    rows = n_tokens // GRID_W
    row = jnp.broadcast_to(jnp.arange(rows, dtype=F32)[:, None], (rows, GRID_W)).reshape(n_tokens)
    col = jnp.broadcast_to(jnp.arange(GRID_W, dtype=F32)[None, :], (rows, GRID_W)).reshape(n_tokens)
    inv_freq = ROPE_THETA ** (-jnp.arange(ROPE_AXIS_PAIRS, dtype=F32) / ROPE_AXIS_PAIRS)
    ang_r = row[:, None] * inv_freq
    ang_c = col[:, None] * inv_freq
    ang = jnp.concatenate([ang_r, ang_r, ang_c, ang_c], axis=-1)
    cos, sin = jnp.cos(ang), jnp.sin(ang)
    first_half = (jnp.arange(B_HEAD_DIM) % (2 * ROPE_AXIS_PAIRS)) < ROPE_AXIS_PAIRS
    sin_lo = jnp.where(first_half, -sin, 0.0)
    sin_hi = jnp.where(first_half, 0.0, sin)
    tile = lambda t: jnp.tile(t, (1, reps))
    return tile(cos), tile(sin_lo), tile(sin_hi)


def _hybrid_in_weight(w_in):
    off_z = 3 * A_WIDTH
    off_ba = off_z + A_WIDTH
    off_q = off_ba + 4 * A_HEADS
    off_k = off_q + B_WIDTH
    off_v = off_k + B_KV_WIDTH
    d = w_in.shape[0]
    ba = w_in[:, off_ba:off_q].reshape(d, 2, 2, A_HEADS)
    ba = ba.transpose(0, 3, 1, 2).reshape(d, 4 * A_HEADS)
    ba = jnp.pad(ba, ((0, 0), (0, LANES - 4 * A_HEADS)))
    w = jnp.concatenate([w_in[:, :off_z], w_in[:, off_z:off_ba], w_in[:, off_q:off_k], w_in[:, off_k:off_v],
                         w_in[:, off_v:], ba], axis=1)
    return w.astype(BF16)


def _row(v):
    return v.reshape(1, -1)


def kernel(x, c, ctx, c_ctx, w_mod, b_mod, g_mix_pre, g_mix_post, g_ffn_pre, g_ffn_post, w_ffn_in, w_ffn_out,
           hyb_w_in, hyb_conv_w, hyb_a_log, hyb_dt_bias, hyb_out_norm, hyb_q_norm, hyb_k_norm, hyb_w_out,
           conf_w_in, conf_b_in, conf_dw_w, conf_dw_b, conf_ln_g, conf_ln_b, conf_w_out, conf_b_out):
    bsz, n_lat, d = x.shape
    n_ctx = ctx.shape[1]
    depth = w_mod.shape[0]
    n = bsz * n_lat
    tm = ROW_TILE

    cond = jnp.zeros((SUBLANES, d), F32).at[:bsz].set(c).at[bsz].set(c_ctx)
    mods = _ada_terms(cond, w_mod, b_mod).reshape(depth, SUBLANES, 6, d)

    h = x.reshape(n, d)
    hc = ctx.reshape(bsz * n_ctx, d)
    for layer in range(depth):
        idx = layer // 2
        mod = mods[layer, :bsz]
        mod_ctx = mods[layer, bsz:bsz + 1]
        if layer % 2 == 0:
            w_in = _hybrid_in_weight(hyb_w_in[idx])
            gpre = _row(g_mix_pre[layer])
            k_gain = _row(jnp.tile(hyb_k_norm[idx], B_KV_HEADS))
            cos_k, slo_k, shi_k = _rope_tables(n_lat, B_KV_HEADS)
            no_pos = (jnp.ones((n_ctx, B_KV_WIDTH), F32), jnp.zeros((n_ctx, B_KV_WIDTH), F32),
                      jnp.zeros((n_ctx, B_KV_WIDTH), F32))
            qkv, z, qb, ba, k_lat, vt_lat = _hyb_inproj(h, mod, gpre, w_in, hyb_conv_w[idx], k_gain,
                                                        cos_k, slo_k, shi_k, bsz, n_lat, tm)
            qkv_c, _, _, ba_c, k_ctx, vt_ctx = _hyb_inproj(hc, mod_ctx, gpre, w_in, hyb_conv_w[idx], k_gain,
                                                           *no_pos, bsz, n_ctx, n_ctx)

            coef = jnp.stack([hyb_a_log[idx], hyb_dt_bias[idx]])
            coef = jnp.pad(coef.transpose(0, 2, 1), ((0, 0), (0, 0), (2, 0))).reshape(2, 4 * A_HEADS)
            coef = jnp.pad(coef, ((0, SUBLANES - 2), (0, LANES - 4 * A_HEADS)))
            o_f, o_b = _delta_mixer(qkv, qkv_c, ba, ba_c, coef, bsz, n_lat, n_ctx)

            k_hm = jnp.concatenate([k_lat, k_ctx], axis=2)
            vt_hm = jnp.concatenate([vt_lat, vt_ctx], axis=3)
            cos_q, slo_q, shi_q = _rope_tables(n_lat, B_GROUP)
            yb = _attention(qb, _row(jnp.tile(hyb_q_norm[idx], B_GROUP)), cos_q, slo_q, shi_q,
                            k_hm, vt_hm, n_lat, ATTN_Q_TILE)

            w_out = hyb_w_out[idx].astype(BF16)
            h = _outproj(o_f, o_b, z, _row(hyb_out_norm[idx]), yb, h, mod, _row(g_mix_post[layer]),
                         w_out[:A_WIDTH], w_out[A_WIDTH:], n_lat, tm)
        else:
            u = _confin(h, mod, _row(g_mix_pre[layer]), conf_w_in[idx].astype(BF16), _row(conf_b_in[idx]),
                        n_lat, tm)
            h = _confout(u, h, mod, _row(g_mix_post[layer]), conf_dw_w[idx], _row(conf_dw_b[idx]),
                         _row(conf_ln_g[idx]), _row(conf_ln_b[idx]), conf_w_out[idx].astype(BF16),
                         _row(conf_b_out[idx]), n_lat, CONF_TILE)
        h = _ffn(h, mod, _row(g_ffn_pre[layer]), _row(g_ffn_post[layer]), w_ffn_in[layer].astype(BF16),
                 w_ffn_out[layer].astype(BF16), n_lat, tm)
        assert not any(j % 2 == 0 for j in range(layer + 1, depth)), "context advance not implemented"
    return h.reshape(bsz, n_lat, d)
```

```python
import functools
import math

import jax
import jax.numpy as jnp
from jax import lax
from jax.experimental import pallas as pl
from jax.experimental.pallas import tpu as pltpu

F32 = jnp.float32
BF16 = jnp.bfloat16

EPS = 1e-6
GRID_W = 64
ROPE_THETA = 10000.0
A_HEADS = 4
A_HEAD_DIM = 128
A_WIDTH = A_HEADS * A_HEAD_DIM
SHORT_CONV_W = 5
CHUNK = 64
B_Q_HEADS = 8
B_KV_HEADS = 2
B_HEAD_DIM = 64
B_GROUP = B_Q_HEADS // B_KV_HEADS
B_WIDTH = B_Q_HEADS * B_HEAD_DIM
B_KV_WIDTH = B_KV_HEADS * B_HEAD_DIM
ROPE_AXIS_PAIRS = B_HEAD_DIM // 4
CONF_KERNEL = 31
LOG2_E = math.log2(math.e)

LANES = 128
SUBLANES = 8
VMEM_LIMIT = 56 * 1024 * 1024
ROW_TILE = 1024
CONF_TILE = 1024
CONF_HALO = 16
PROJ_HALO = 16
DELTA_BLOCK = 4
DELTA_ROWS = DELTA_BLOCK * CHUNK
ATTN_Q_TILE = 512
ATTN_KEY_BLOCK = 128
ATTN_Q_COLS = 256
ATTN_VT_ROWS = B_HEAD_DIM + 16


def _cparams(sem):
    return pltpu.CompilerParams(dimension_semantics=sem, vmem_limit_bytes=VMEM_LIMIT)


def _resident(shape):
    return pl.BlockSpec(shape, lambda i: (0, 0), pipeline_mode=pl.Buffered(1))


def _silu(x):
    return x * jax.nn.sigmoid(x)


def _dot(a, b):
    return jnp.dot(a, b, preferred_element_type=F32)


def _dot_nt(a, b):
    return lax.dot_general(a, b, (((1,), (1,)), ((), ())), preferred_element_type=F32)


def _dot_tn(a, b):
    return lax.dot_general(a, b, (((0,), (0,)), ((), ())), preferred_element_type=F32)


def _rms(x, gain):
    return x * lax.rsqrt(jnp.mean(x * x, axis=-1, keepdims=True) + EPS) * gain


def _split_bf16(a):
    hi = a.astype(BF16)
    return hi, (a - hi.astype(F32)).astype(BF16)


def _drain(*gens):
    last = [None] * len(gens)
    live = list(range(len(gens)))
    while live:
        for g in list(live):
            try:
                last[g] = next(gens[g])
            except StopIteration:
                live.remove(g)
    return last


def _ada_kernel(c_ref, w_ref, b_ref, o_ref):
    s_hi, s_lo = _split_bf16(_silu(c_ref[...]))
    w_hi, w_lo = _split_bf16(w_ref[0])
    o_ref[0] = _dot(s_hi, w_hi) + _dot(s_hi, w_lo) + _dot(s_lo, w_hi) + b_ref[0]


def _ada_terms(cond, w_mod, b_mod):
    depth, d, n6 = w_mod.shape
    tn = n6 // 4
    return pl.pallas_call(
        _ada_kernel,
        grid=(depth, n6 // tn),
        in_specs=[pl.BlockSpec((SUBLANES, d), lambda l, j: (0, 0)),
                  pl.BlockSpec((1, d, tn), lambda l, j: (l, 0, j)),
                  pl.BlockSpec((1, 1, tn), lambda l, j: (l, 0, j))],
        out_specs=pl.BlockSpec((1, SUBLANES, tn), lambda l, j: (l, 0, j)),
        out_shape=jax.ShapeDtypeStruct((depth, SUBLANES, n6), F32),
        compiler_params=_cparams(("arbitrary", "arbitrary")),
        name="ada",
    )(cond, w_mod, b_mod.reshape(depth, 1, n6))


def _modulated(x, mod_ref, gain_ref, shift_row):
    y = _rms(x, gain_ref[...])
    return y * (1.0 + mod_ref[0, shift_row + 1:shift_row + 2, :]) + mod_ref[0, shift_row:shift_row + 1, :]


def _gated_residual(x, y, mod_ref, gain_ref, gate_row):
    return x + mod_ref[0, gate_row:gate_row + 1, :] * _rms(y, gain_ref[...])


def _head_rms(x, gain_row, head_dim):
    outs = []
    for s in range(x.shape[1] // LANES):
        xs = x[:, s * LANES:(s + 1) * LANES]
        lane = lax.broadcasted_iota(jnp.int32, xs.shape, 1)
        sq = xs * xs
        scale = jnp.zeros_like(xs)
        for part in range(LANES // head_dim):
            m = jnp.logical_and(lane >= part * head_dim, lane < (part + 1) * head_dim)
            ms = jnp.sum(jnp.where(m, sq, 0.0), axis=-1, keepdims=True) * (1.0 / head_dim)
            scale = jnp.where(m, lax.rsqrt(ms + EPS), scale)
        outs.append(xs * scale)
    y = outs[0] if len(outs) == 1 else jnp.concatenate(outs, axis=1)
    return y * gain_row


def _rope(x, cos, sin_lo, sin_hi):
    width = x.shape[1]
    fwd = pltpu.roll(x, width - ROPE_AXIS_PAIRS, axis=1)
    back = pltpu.roll(x, ROPE_AXIS_PAIRS, axis=1)
    return x * cos + fwd * sin_lo + back * sin_hi


def _hyb_inproj_kernel(x_ref, xp_ref, xn_ref, mod_ref, g_ref, w_ref, cw_ref, kg_ref, cos_ref, slo_ref, shi_ref,
                       qkv_ref, z_ref, qb_ref, ba_ref, ko_ref, vo_ref, *, tiles_per_seq):
    pos = pl.program_id(0) % tiles_per_seq
    tm = x_ref.shape[0]
    halo = PROJ_HALO
    qkv_w = 3 * A_WIDTH
    a_prev = jnp.where(pos > 0, _modulated(xp_ref[...], mod_ref, g_ref, 0), 0.0).astype(BF16)
    a_next = jnp.where(pos < tiles_per_seq - 1, _modulated(xn_ref[...], mod_ref, g_ref, 0), 0.0).astype(BF16)
    a = _modulated(x_ref[...], mod_ref, g_ref, 0).astype(BF16)
    a_ext = jnp.concatenate([a_prev, a, a_next], axis=0)

    off_z = qkv_w
    off_q = off_z + A_WIDTH
    off_k = off_q + B_WIDTH
    off_v = off_k + B_KV_WIDTH
    off_ba = off_v + B_KV_WIDTH

    def gate_proj():
        z_ref[...] = _dot(a, w_ref[:, off_z:off_q])

    def query_proj():
        qb_ref[...] = _dot(a, w_ref[:, off_q:off_k])

    def key_value_proj():
        kb = _dot(a, w_ref[:, off_k:off_v])
        vb = _dot(a, w_ref[:, off_v:off_ba])
        k = _rope(_head_rms(kb, kg_ref[...], B_HEAD_DIM), cos_ref[...], slo_ref[...], shi_ref[...]).astype(BF16)
        vt = vb.T.astype(BF16)
        for hh in range(B_KV_HEADS):
            ko_ref[0, hh] = k[:, hh * B_HEAD_DIM:(hh + 1) * B_HEAD_DIM]
            vo_ref[0, hh, :B_HEAD_DIM, :] = vt[hh * B_HEAD_DIM:(hh + 1) * B_HEAD_DIM, :]
            vo_ref[0, hh, B_HEAD_DIM:, :] = jnp.ones((ATTN_VT_ROWS - B_HEAD_DIM, tm), BF16)

    def logit_proj():
        ba_ref[...] = _dot(a, w_ref[:, off_ba:off_ba + LANES])

    others = [gate_proj, query_proj, key_value_proj, logit_proj]
    ext_rows = tm + 2 * halo
    pair = 2 * A_HEAD_DIM
    n_pairs = qkv_w // pair
    ext_next = _dot(a_ext, w_ref[:, :pair])
    for cb in range(n_pairs):
        cols = slice(cb * pair, (cb + 1) * pair)
        ext = ext_next
        if cb + 1 < n_pairs:
            ext_next = _dot(a_ext, w_ref[:, (cb + 1) * pair:(cb + 2) * pair])
        if cb < len(others):
            others[cb]()
        w = cw_ref[:, cols]
        acc = jnp.zeros((tm, pair), F32)
        for tap in range(SHORT_CONV_W):
            shift = (SHORT_CONV_W // 2 - tap) % ext_rows
            rolled = pltpu.roll(ext, shift, axis=0) if shift else ext
            acc = acc + rolled[halo:halo + tm, :] * w[tap:tap + 1, :]
        y = _silu(acc)
        for part in range(2):
            head = 2 * cb + part
            yh = y[:, part * A_HEAD_DIM:(part + 1) * A_HEAD_DIM]
            if head < 2 * A_HEADS:
                yh = yh * lax.rsqrt(jnp.sum(yh * yh, axis=-1, keepdims=True) + EPS)
            if head < A_HEADS:
                yh = yh * (A_HEAD_DIM ** -0.5)
            qkv_ref[:, head * A_HEAD_DIM:(head + 1) * A_HEAD_DIM] = yh


def _hyb_inproj(x2, mod, gain, w, conv_w, k_gain, cos, slo, shi, bsz, rows_per_seq, tm):
    n, d = x2.shape
    tiles_per_seq = rows_per_seq // tm
    hb = tm // PROJ_HALO
    n_halo = n // PROJ_HALO
    f32_out = lambda wd: (pl.BlockSpec((tm, wd), lambda i: (i, 0)), jax.ShapeDtypeStruct((n, wd), F32))
    outs = [f32_out(3 * A_WIDTH), f32_out(A_WIDTH), f32_out(B_WIDTH), f32_out(LANES),
            (pl.BlockSpec((1, B_KV_HEADS, tm, B_HEAD_DIM), lambda i: (i // tiles_per_seq, 0, i % tiles_per_seq, 0)),
             jax.ShapeDtypeStruct((bsz, B_KV_HEADS, rows_per_seq, B_HEAD_DIM), BF16)),
            (pl.BlockSpec((1, B_KV_HEADS, ATTN_VT_ROWS, tm), lambda i: (i // tiles_per_seq, 0, 0, i % tiles_per_seq)),
             jax.ShapeDtypeStruct((bsz, B_KV_HEADS, ATTN_VT_ROWS, rows_per_seq), BF16))]
    tab = pl.BlockSpec((tm, B_KV_WIDTH), lambda i: (i % tiles_per_seq, 0))
    return pl.pallas_call(
        functools.partial(_hyb_inproj_kernel, tiles_per_seq=tiles_per_seq),
        grid=(n // tm,),
        in_specs=[pl.BlockSpec((tm, d), lambda i: (i, 0)),
                  pl.BlockSpec((PROJ_HALO, d), lambda i: (jnp.maximum(i * hb - 1, 0), 0)),
                  pl.BlockSpec((PROJ_HALO, d), lambda i: (jnp.minimum((i + 1) * hb, n_halo - 1), 0)),
                  pl.BlockSpec((1, 6, d), lambda i: (i // tiles_per_seq if mod.shape[0] > 1 else 0, 0, 0)),
                  pl.BlockSpec((1, d), lambda i: (0, 0)),
                  _resident(w.shape),
                  _resident(conv_w.shape),
                  pl.BlockSpec((1, B_KV_WIDTH), lambda i: (0, 0)), tab, tab, tab],
        out_specs=[o[0] for o in outs],
        out_shape=[o[1] for o in outs],
        compiler_params=_cparams(("arbitrary",)),
        name="inproj",
    )(x2, x2, x2, mod, gain, w, conv_w, k_gain, cos, slo, shi)


def _unit_tri_inverse_steps(l_mats, i, j):
    eye = (i == j).astype(F32)
    same16 = jnp.right_shift(i, 4) == jnp.right_shift(j, 4)
    same32 = jnp.right_shift(i, 5) == jnp.right_shift(j, 5)
    off32 = jnp.logical_and(same32, jnp.logical_not(same16))
    b = lambda a: a.astype(BF16)
    each = lambda f, *ls: [f(*xs) for xs in zip(*ls)]
    d1 = each(lambda l: b(jnp.where(same16, l, 0.0)), l_mats)
    p = each(lambda d: eye - d.astype(F32), d1)
    dk = d1
    for _ in range(3):
        dk = each(lambda d: b(_dot(d, d)), dk)
        yield None
        p = each(lambda pp, d: pp + _dot(b(pp), d), p, dk)
        yield None
    for sel in (off32, jnp.logical_not(same32)):
        cm = each(lambda l: b(jnp.where(sel, l, 0.0)), l_mats)
        pb = each(b, p)
        inner = each(lambda c, q: b(_dot(c, q)), cm, pb)
        yield None
        p = each(lambda pp, q, m: pp - _dot(q, m), p, pb, inner)
        yield None
    l_split = each(_split_bf16, l_mats)
    t_split = each(_split_bf16, p)
    resid = each(lambda t0, ls, ts: eye - t0 - (_dot(ls[0], ts[0]) + _dot(ls[0], ts[1]) + _dot(ls[1], ts[0])),
                 p, l_split, t_split)
    yield None
    yield each(lambda t0, ts, r: t0 + _dot(ts[0], b(r)), p, t_split, resid)


def _delta_kernel(qf_ref, kf_ref, vf_ref, qr_ref, kr_ref, vr_ref, qc_ref, kc_ref, vc_ref,
                  baf_ref, bar_ref, bac_ref, coef_ref, of_ref, ob_ref,
                  u_s, w_s, kt_s, qd_s, in_s, el_s, state):
    step = pl.program_id(1)
    is_ctx = step == 0
    front_slot = step % 2
    scan_slot = 1 - front_slot
    uw_write, uw_read = front_slot, scan_slot
    heads = [slice(hh * A_HEAD_DIM, (hh + 1) * A_HEAD_DIM) for hh in range(A_HEADS)]
    chunks = [slice(c * CHUNK, (c + 1) * CHUNK) for c in range(DELTA_BLOCK)]

    @pl.when(step == 0)
    def _():
        for scr in (u_s, w_s, kt_s, qd_s, in_s, el_s):
            scr[1] = jnp.zeros(scr.shape[1:], scr.dtype)
        state[...] = jnp.zeros(state.shape, F32)

    chains = [(d, hh) for d in range(2) for hh in range(A_HEADS)]
    s_mats = [state[d * A_HEADS + hh] for d, hh in chains]
    out_refs = (of_ref, ob_ref)

    def scan_substep(t, s_mats):
        where = []
        for d, hh in chains:
            c = t if d == 0 else DELTA_BLOCK - 1 - t
            where.append((d, c, chunks[c], heads[hh]))
        ws = [_dot(jnp.concatenate([w_s[uw_read, d, r, ln], qd_s[scan_slot, d, r, ln]], axis=0), s.astype(BF16))
              for (d, c, r, ln), s in zip(where, s_mats)]
        v_new = [(u_s[uw_read, d, r, ln] - x[:CHUNK]).astype(BF16) for (d, c, r, ln), x in zip(where, ws)]
        for (d, c, r, ln), x, vn in zip(where, ws, v_new):
            out_refs[d][r, ln] = x[CHUNK:] + _dot(in_s[scan_slot, d, r, ln][:, :CHUNK], vn)
        return [s * el_s[scan_slot, d, c * SUBLANES:c * SUBLANES + 1, ln] + _dot_tn(kt_s[scan_slot, d, r, ln], vn)
                for (d, c, r, ln), s, vn in zip(where, s_mats, v_new)]

    def per_head(main_ref, ctx_ref):
        x = jnp.where(is_ctx, ctx_ref[...], main_ref[...])
        return [x[:, ln] for ln in heads]

    q_h = (per_head(qf_ref, qc_ref), per_head(qr_ref, qc_ref))
    k_h = (per_head(kf_ref, kc_ref), per_head(kr_ref, kc_ref))
    v_h = (per_head(vf_ref, vc_ref), per_head(vr_ref, vc_ref))

    lane = lax.broadcasted_iota(jnp.int32, (1, LANES), 1)
    is_beta = jnp.bitwise_and(lane, 3) < 2
    neg_a = -jnp.exp(coef_ref[0:1, :])
    dtb = coef_ref[1:2, :]
    bg = []
    for raw in (jnp.where(is_ctx, bac_ref[...], baf_ref[...]), jnp.where(is_ctx, bac_ref[...], bar_ref[...])):
        xg = raw + dtb
        softplus = jnp.maximum(xg, 0.0) + jnp.log(1.0 + jnp.exp(-jnp.abs(xg)))
        bg.append(jnp.where(is_beta, jax.nn.sigmoid(raw), neg_a * softplus))

    i_idx = lax.broadcasted_iota(jnp.int32, (CHUNK, CHUNK), 0)
    j_idx = lax.broadcasted_iota(jnp.int32, (CHUNK, CHUNK), 1)
    i_w = lax.broadcasted_iota(jnp.int32, (CHUNK, LANES), 0)
    j_w = lax.broadcasted_iota(jnp.int32, (CHUNK, LANES), 1)
    in_chunk = j_w < CHUNK
    ones_rows = jnp.ones((2 * SUBLANES, CHUNK), BF16)

    def front_steps(d):
        incl = (i_idx >= j_idx) if d == 0 else (i_idx <= j_idx)
        strict = (i_idx > j_idx) if d == 0 else (i_idx < j_idx)
        incl_w = jnp.logical_and(in_chunk, (i_w >= j_w) if d == 0 else (i_w <= j_w))
        seen_w = jnp.logical_and(in_chunk, (i_w <= j_w) if d == 0 else (i_w >= j_w))
        seen_cat = jnp.concatenate([seen_w] * DELTA_BLOCK, axis=1)
        incl_b = incl.astype(BF16)
        for hh, ln in enumerate(heads):
            g_lane = 4 * hh + 2 + d
            g_cat = jnp.concatenate([jnp.broadcast_to(bg[d][r, g_lane:g_lane + 1], (CHUNK, LANES)) for r in chunks],
                                    axis=1)
            g_hi = g_cat.astype(BF16)
            g_rest = g_cat - g_hi.astype(F32)
            g_mid = g_rest.astype(BF16)
            pieces = (g_hi, g_mid, (g_rest - g_mid.astype(F32)).astype(BF16))
            gc_all = sum(_dot(incl_b, p) for p in pieces)
            gr_all = sum(_dot(ones_rows, jnp.where(seen_cat, p, jnp.zeros_like(p))) for p in pieces)
            for c, r in enumerate(chunks):
                q, k, v = q_h[d][hh][r], k_h[d][hh][r], v_h[d][hh][r]
                kbf = k.astype(BF16)
                prod = _dot_nt(jnp.concatenate([kbf, q.astype(BF16)], axis=0),
                               jnp.concatenate([kbf, jnp.zeros_like(kbf)], axis=0))
                beta = bg[d][r, 4 * hh + d:4 * hh + d + 1]
                gc_w = gc_all[:, c * LANES:(c + 1) * LANES]
                gr_w = jnp.broadcast_to(gr_all[0:1, c * LANES:(c + 1) * LANES], (CHUNK, LANES))
                decay = jnp.where(incl_w, jnp.exp(jnp.minimum(gc_w - gr_w, 0.0)), 0.0)
                g_last = gc_w[CHUNK - 1:CHUNK, :] if d == 0 else gc_w[0:1, :]
                egc = jnp.exp(gc_w)
                l_mats.append(jnp.where(strict, (prod[:CHUNK] * beta * decay)[:, :CHUNK], 0.0))
                rhs_all.append(jnp.concatenate([v * beta, k * beta * egc], axis=1).astype(BF16))
                dests.append((d, r, ln))
                kt_s[front_slot, d, r, ln] = (k * jnp.exp(g_last - gc_w)).astype(BF16)
                qd_s[front_slot, d, r, ln] = (q * egc).astype(BF16)
                in_s[front_slot, d, r, ln] = (prod[CHUNK:] * decay).astype(BF16)
                el_s[front_slot, d, c * SUBLANES:(c + 1) * SUBLANES, ln] = jnp.broadcast_to(jnp.exp(g_last),
                                                                                            (SUBLANES, LANES))
                yield None

    def scan_steps(s_mats):
        for t in range(DELTA_BLOCK):
            s_mats = scan_substep(t, s_mats)
            yield s_mats
            yield s_mats
            yield s_mats

    l_mats, rhs_all, dests = [], [], []
    _, _, s_mats = _drain(front_steps(0), front_steps(1), scan_steps(s_mats))
    inverse = _drain(_unit_tri_inverse_steps(l_mats, i_idx, j_idx))[0]
    t_split = [_split_bf16(t) for t in inverse]
    sols = [_dot(t_hi, rr) + _dot(t_lo, rr) for (t_hi, t_lo), rr in zip(t_split, rhs_all)]
    for (d, r, ln), sol in zip(dests, sols):
        u_s[uw_write, d, r, ln] = sol[:, :A_HEAD_DIM]
        w_s[uw_write, d, r, ln] = sol[:, A_HEAD_DIM:].astype(BF16)
    for (d, hh), s in zip(chains, s_mats):
        state[d * A_HEADS + hh] = s


def _delta_mixer(qkv, qkv_c, ba, ba_c, coef, bsz, n_lat, n_ctx):
    assert n_ctx == DELTA_ROWS and n_lat % DELTA_ROWS == 0
    wd = A_WIDTH
    lat_blocks = n_lat // DELTA_ROWS
    n_steps = lat_blocks + 2

    def fwd_blk(b, s):
        return b * lat_blocks + jnp.clip(s - 1, 0, lat_blocks - 1)

    def bwd_blk(b, s):
        return b * lat_blocks + jnp.clip(lat_blocks - s, 0, lat_blocks - 1)

    def lat(blk_fn, kd):
        return pl.BlockSpec((DELTA_ROWS, wd), lambda b, s: (blk_fn(b, s), kd))

    def ctxb(kd):
        return pl.BlockSpec((n_ctx, wd), lambda b, s: (b, kd))

    in_specs = ([lat(fwd_blk, kd) for kd in range(3)] + [lat(bwd_blk, kd) for kd in range(3)]
                + [ctxb(kd) for kd in range(3)]
                + [pl.BlockSpec((DELTA_ROWS, LANES), lambda b, s: (fwd_blk(b, s), 0)),
                   pl.BlockSpec((DELTA_ROWS, LANES), lambda b, s: (bwd_blk(b, s), 0)),
                   pl.BlockSpec((n_ctx, LANES), lambda b, s: (b, 0)),
                   pl.BlockSpec((SUBLANES, LANES), lambda b, s: (0, 0))])
    out_f = pl.BlockSpec((DELTA_ROWS, wd), lambda b, s: (b * lat_blocks + jnp.maximum(s - 2, 0), 0))
    out_b = pl.BlockSpec((DELTA_ROWS, wd), lambda b, s: (b * lat_blocks + lat_blocks - jnp.maximum(s - 1, 1), 0))
    shape = jax.ShapeDtypeStruct((bsz * n_lat, wd), F32)
    slots = lambda rows, dt: pltpu.VMEM((2, 2, rows, wd), dt)
    return pl.pallas_call(
        _delta_kernel,
        grid=(bsz, n_steps),
        in_specs=in_specs,
        out_specs=[out_f, out_b],
        out_shape=[shape, shape],
        scratch_shapes=[slots(DELTA_ROWS, F32), slots(DELTA_ROWS, BF16), slots(DELTA_ROWS, BF16),
                        slots(DELTA_ROWS, BF16), slots(DELTA_ROWS, BF16), slots(DELTA_BLOCK * SUBLANES, F32),
                        pltpu.VMEM((2 * A_HEADS, A_HEAD_DIM, A_HEAD_DIM), F32)],
        compiler_params=_cparams(("arbitrary", "arbitrary")),
        name="delta",
    )(*([qkv] * 6), *([qkv_c] * 3), ba, ba, ba_c, coef)


def _attn_kernel(q_ref, g_ref, cos_ref, slo_ref, shi_ref, k_ref, vt_ref, o_ref):
    q = _head_rms(q_ref[...], g_ref[...], B_HEAD_DIM)
    q = _rope(q, cos_ref[...], slo_ref[...], shi_ref[...]) * (B_HEAD_DIM ** -0.5 * LOG2_E)
    tq = q.shape[0]
    qt = q.T.astype(BF16)
    n_keys = k_ref.shape[2]
    q_s = [qt[g * B_HEAD_DIM:(g + 1) * B_HEAD_DIM, c:c + ATTN_Q_COLS]
           for g in range(B_GROUP) for c in range(0, tq, ATTN_Q_COLS)]
    m = [jnp.full((1, ATTN_Q_COLS), -1e30, F32) for _ in q_s]
    acc = [jnp.zeros((ATTN_VT_ROWS, ATTN_Q_COLS), F32) for _ in q_s]

    def scores(kb):
        k_blk = k_ref[0, 0, kb:kb + ATTN_KEY_BLOCK, :]
        return [_dot(k_blk, qs) for qs in q_s]

    def value_update(acc, pending):
        alpha, p, kb = pending
        vt = vt_ref[0, 0, :, kb:kb + ATTN_KEY_BLOCK]
        return [a * al + _dot(vt, pp) for a, al, pp in zip(acc, alpha, p)]

    st_next = scores(0)
    pending = None
    for kb in range(0, n_keys, ATTN_KEY_BLOCK):
        st = st_next
        if kb + ATTN_KEY_BLOCK < n_keys:
            st_next = scores(kb + ATTN_KEY_BLOCK)
        m_new = [jnp.maximum(mo, jnp.max(s, axis=0, keepdims=True)) for mo, s in zip(m, st)]
        alpha = [jnp.exp2(mo - mn) for mo, mn in zip(m, m_new)]
        p = [jnp.exp2((s - mn).astype(BF16)) for s, mn in zip(st, m_new)]
        if pending is not None:
            acc = value_update(acc, pending)
        pending = (alpha, p, kb)
        m = m_new
    acc = value_update(acc, pending)
    outs = [a[:B_HEAD_DIM] / a[B_HEAD_DIM:B_HEAD_DIM + 1] for a in acc]
    n_col = tq // ATTN_Q_COLS
    o_ref[...] = jnp.concatenate([jnp.concatenate(outs[g * n_col:(g + 1) * n_col], axis=1)
                                  for g in range(B_GROUP)], axis=0).T.astype(o_ref.dtype)


def _attention(qb, gain, cos, slo, shi, k_hm, vt_hm, n_lat, tq):
    n, _ = qb.shape
    bsz, _, s_len, _ = k_hm.shape
    assert s_len % ATTN_KEY_BLOCK == 0 and tq % ATTN_Q_COLS == 0
    gw = B_GROUP * B_HEAD_DIM
    nq = n_lat // tq
    kv = pl.BlockSpec((1, 1, s_len, B_HEAD_DIM), lambda b, kh, i: (b, kh, 0, 0))
    vts = pl.BlockSpec((1, 1, ATTN_VT_ROWS, s_len), lambda b, kh, i: (b, kh, 0, 0))
    tab = pl.BlockSpec((tq, gw), lambda b, kh, i: (i, 0))
    return pl.pallas_call(
        _attn_kernel,
        grid=(bsz, B_KV_HEADS, nq),
        in_specs=[pl.BlockSpec((tq, gw), lambda b, kh, i: (b * nq + i, kh)),
                  pl.BlockSpec((1, gw), lambda b, kh, i: (0, 0)), tab, tab, tab, kv, vts],
        out_specs=pl.BlockSpec((tq, gw), lambda b, kh, i: (b * nq + i, kh)),
        out_shape=jax.ShapeDtypeStruct((n, B_WIDTH), BF16),
        compiler_params=_cparams(("arbitrary", "arbitrary", "arbitrary")),
        name="attn",
    )(qb, gain, cos, slo, shi, k_hm, vt_hm)


def _outproj_kernel(of_ref, ob_ref, z_ref, on_ref, yb_ref, x_ref, mod_ref, g_ref, wa_ref, wb_ref, o_ref):
    o = of_ref[...] + ob_ref[...]
    z = z_ref[...]
    parts = []
    for hh in range(A_HEADS):
        sl = slice(hh * A_HEAD_DIM, (hh + 1) * A_HEAD_DIM)
        parts.append((_rms(o[:, sl], on_ref[...]) * _silu(z[:, sl])).astype(BF16))
    ya = jnp.concatenate(parts, axis=1)
    y = _dot(ya, wa_ref[...]) + _dot(yb_ref[...], wb_ref[...])
    o_ref[...] = _gated_residual(x_ref[...], y, mod_ref, g_ref, 2)


def _outproj(o_f, o_b, z, out_norm, yb, x2, mod, gain, wa, wb, rows_per_mod, tm):
    n, d = x2.shape
    bpm = rows_per_mod // tm
    return pl.pallas_call(
        _outproj_kernel,
        grid=(n // tm,),
        in_specs=[pl.BlockSpec((tm, o_f.shape[1]), lambda i: (i, 0)),
                  pl.BlockSpec((tm, o_b.shape[1]), lambda i: (i, 0)),
                  pl.BlockSpec((tm, z.shape[1]), lambda i: (i, 0)),
                  pl.BlockSpec((1, A_HEAD_DIM), lambda i: (0, 0)),
                  pl.BlockSpec((tm, yb.shape[1]), lambda i: (i, 0)),
                  pl.BlockSpec((tm, d), lambda i: (i, 0)),
                  pl.BlockSpec((1, 6, d), lambda i: (i // bpm, 0, 0)),
                  pl.BlockSpec((1, d), lambda i: (0, 0)),
                  _resident(wa.shape),
                  _resident(wb.shape)],
        out_specs=pl.BlockSpec((tm, d), lambda i: (i, 0)),
        out_shape=jax.ShapeDtypeStruct((n, d), F32),
        compiler_params=_cparams(("arbitrary",)),
        name="outproj",
    )(o_f, o_b, z, out_norm, yb, x2, mod, gain, wa, wb)


def _ffn_kernel(x_ref, mod_ref, gpre_ref, gpost_ref, wi_ref, wo_ref, o_ref, *, hidden, hc):
    x = x_ref[...]
    a = _modulated(x, mod_ref, gpre_ref, 3).astype(BF16)
    acc = jnp.zeros(x.shape, F32)
    for c in range(hidden // hc):
        gate = _dot(a, wi_ref[:, c * hc:(c + 1) * hc])
        up = _dot(a, wi_ref[:, hidden + c * hc:hidden + (c + 1) * hc])
        acc = acc + _dot((_silu(gate) * up).astype(BF16), wo_ref[c * hc:(c + 1) * hc, :])
    o_ref[...] = _gated_residual(x, acc, mod_ref, gpost_ref, 5)


def _ffn(x2, mod, gpre, gpost, wi, wo, rows_per_mod, tm):
    n, d = x2.shape
    hidden = wo.shape[0]
    bpm = rows_per_mod // tm
    return pl.pallas_call(
        functools.partial(_ffn_kernel, hidden=hidden, hc=2 * LANES),
        grid=(n // tm,),
        in_specs=[pl.BlockSpec((tm, d), lambda i: (i, 0)),
                  pl.BlockSpec((1, 6, d), lambda i: (i // bpm, 0, 0)),
                  pl.BlockSpec((1, d), lambda i: (0, 0)),
                  pl.BlockSpec((1, d), lambda i: (0, 0)),
                  _resident(wi.shape),
                  _resident(wo.shape)],
        out_specs=pl.BlockSpec((tm, d), lambda i: (i, 0)),
        out_shape=jax.ShapeDtypeStruct((n, d), F32),
        compiler_params=_cparams(("arbitrary",)),
        name="ffn",
    )(x2, mod, gpre, gpost, wi, wo)


def _confin_kernel(x_ref, mod_ref, g_ref, w_ref, b_ref, o_ref, *, width):
    a = _modulated(x_ref[...], mod_ref, g_ref, 0).astype(BF16)
    val = _dot(a, w_ref[:, :width]) + b_ref[:, :width]
    gate = _dot(a, w_ref[:, width:]) + b_ref[:, width:]
    o_ref[...] = val * jax.nn.sigmoid(gate)


def _confin(x2, mod, gain, w, b, rows_per_mod, tm):
    n, d = x2.shape
    width = w.shape[1] // 2
    bpm = rows_per_mod // tm
    return pl.pallas_call(
        functools.partial(_confin_kernel, width=width),
        grid=(n // tm,),
        in_specs=[pl.BlockSpec((tm, d), lambda i: (i, 0)),
                  pl.BlockSpec((1, 6, d), lambda i: (i // bpm, 0, 0)),
                  pl.BlockSpec((1, d), lambda i: (0, 0)),
                  _resident(w.shape),
                  pl.BlockSpec((1, 2 * width), lambda i: (0, 0))],
        out_specs=pl.BlockSpec((tm, width), lambda i: (i, 0)),
        out_shape=jax.ShapeDtypeStruct((n, width), F32),
        compiler_params=_cparams(("arbitrary",)),
        name="confin",
    )(x2, mod, gain, w, b)


def _confout_kernel(u_ref, up_ref, un_ref, x_ref, mod_ref, g_ref, dww_ref, dwb_ref, lng_ref, lnb_ref,
                    w_ref, b_ref, o_ref, ext, conv, shifted, *, tm, tiles_per_seq):
    pos = pl.program_id(0) % tiles_per_seq
    width = u_ref.shape[1]
    halo = CONF_HALO
    ext[pl.ds(halo, tm), :] = u_ref[...]
    ext[0:halo, :] = jnp.where(pos == 0, 0.0, up_ref[...])
    ext[pl.ds(halo + tm, halo), :] = jnp.where(pos == tiles_per_seq - 1, 0.0, un_ref[...])

    rb = 64
    pad = CONF_KERNEL // 2
    copy_rows = tm + 2 * halo - SUBLANES

    def col_body(c, carry):
        cs = pl.ds(pl.multiple_of(c * LANES, LANES), LANES)
        w = dww_ref[:, cs]
        for s in range(SUBLANES):
            shifted[s, :, :] = ext[pl.ds(s, copy_rows), cs]
        for r in range(tm // rb):
            acc = jnp.zeros((rb, LANES), F32)
            for tap in range(CONF_KERNEL):
                off = halo - pad + tap
                acc = acc + shifted[off % SUBLANES, pl.ds(r * rb + off - off % SUBLANES, rb), :] * w[tap:tap + 1, :]
            conv[pl.ds(r * rb, rb), cs] = acc
        return carry

    lax.fori_loop(0, width // LANES, col_body, 0)

    y = conv[...] + dwb_ref[...]
    mu = jnp.mean(y, axis=-1, keepdims=True)
    yc = y - mu
    var = jnp.mean(yc * yc, axis=-1, keepdims=True)
    y = _silu(yc * lax.rsqrt(var + EPS) * lng_ref[...] + lnb_ref[...])
    out = _dot(y.astype(BF16), w_ref[...]) + b_ref[...]
    o_ref[...] = _gated_residual(x_ref[...], out, mod_ref, g_ref, 2)


def _confout(u, x2, mod, gain, dww, dwb, lng, lnb, w, b, rows_per_mod, tm):
    n, d = x2.shape
    width = u.shape[1]
    tiles_per_seq = rows_per_mod // tm
    hb = tm // CONF_HALO
    n_halo_blocks = n // CONF_HALO
    vec = lambda wd: pl.BlockSpec((1, wd), lambda i: (0, 0))
    return pl.pallas_call(
        functools.partial(_confout_kernel, tm=tm, tiles_per_seq=tiles_per_seq),
        grid=(n // tm,),
        in_specs=[pl.BlockSpec((tm, width), lambda i: (i, 0)),
                  pl.BlockSpec((CONF_HALO, width), lambda i: (jnp.maximum(i * hb - 1, 0), 0)),
                  pl.BlockSpec((CONF_HALO, width), lambda i: (jnp.minimum((i + 1) * hb, n_halo_blocks - 1), 0)),
                  pl.BlockSpec((tm, d), lambda i: (i, 0)),
                  pl.BlockSpec((1, 6, d), lambda i: (i // tiles_per_seq, 0, 0)),
                  vec(d),
                  pl.BlockSpec(dww.shape, lambda i: (0, 0)),
                  vec(width), vec(width), vec(width),
                  _resident(w.shape),
                  vec(d)],
        out_specs=pl.BlockSpec((tm, d), lambda i: (i, 0)),
        out_shape=jax.ShapeDtypeStruct((n, d), F32),
        scratch_shapes=[pltpu.VMEM((tm + 2 * CONF_HALO, width), F32), pltpu.VMEM((tm, width), F32),
                        pltpu.VMEM((SUBLANES, tm + 2 * CONF_HALO - SUBLANES, LANES), F32)],
        compiler_params=_cparams(("arbitrary",)),
        name="confout",
    )(u, u, u, x2, mod, gain, dww, dwb, lng, lnb, w, b)


def _rope_tables(n_tokens, reps):
    rows = n_tokens // GRID_W
    row = jnp.broadcast_to(jnp.arange(rows, dtype=F32)[:, None], (rows, GRID_W)).reshape(n_tokens)
    col = jnp.broadcast_to(jnp.arange(GRID_W, dtype=F32)[None, :], (rows, GRID_W)).reshape(n_tokens)
    inv_freq = ROPE_THETA ** (-jnp.arange(ROPE_AXIS_PAIRS, dtype=F32) / ROPE_AXIS_PAIRS)
    ang_r = row[:, None] * inv_freq
    ang_c = col[:, None] * inv_freq
    ang = jnp.concatenate([ang_r, ang_r, ang_c, ang_c], axis=-1)
    cos, sin = jnp.cos(ang), jnp.sin(ang)
    first_half = (jnp.arange(B_HEAD_DIM) % (2 * ROPE_AXIS_PAIRS)) < ROPE_AXIS_PAIRS
    sin_lo = jnp.where(first_half, -sin, 0.0)
    sin_hi = jnp.where(first_half, 0.0, sin)
    tile = lambda t: jnp.tile(t, (1, reps))
    return tile(cos), tile(sin_lo), tile(sin_hi)


def _hybrid_in_weight(w_in):
    off_z = 3 * A_WIDTH
    off_ba = off_z + A_WIDTH
    off_q = off_ba + 4 * A_HEADS
    off_k = off_q + B_WIDTH
    off_v = off_k + B_KV_WIDTH
    d = w_in.shape[0]
    ba = w_in[:, off_ba:off_q].reshape(d, 2, 2, A_HEADS)
    ba = ba.transpose(0, 3, 1, 2).reshape(d, 4 * A_HEADS)
    ba = jnp.pad(ba, ((0, 0), (0, LANES - 4 * A_HEADS)))
    w = jnp.concatenate([w_in[:, :off_z], w_in[:, off_z:off_ba], w_in[:, off_q:off_k], w_in[:, off_k:off_v],
                         w_in[:, off_v:], ba], axis=1)
    return w.astype(BF16)


def _row(v):
    return v.reshape(1, -1)


def kernel(x, c, ctx, c_ctx, w_mod, b_mod, g_mix_pre, g_mix_post, g_ffn_pre, g_ffn_post, w_ffn_in, w_ffn_out,
           hyb_w_in, hyb_conv_w, hyb_a_log, hyb_dt_bias, hyb_out_norm, hyb_q_norm, hyb_k_norm, hyb_w_out,
           conf_w_in, conf_b_in, conf_dw_w, conf_dw_b, conf_ln_g, conf_ln_b, conf_w_out, conf_b_out):
    bsz, n_lat, d = x.shape
    n_ctx = ctx.shape[1]
    depth = w_mod.shape[0]
    n = bsz * n_lat
    tm = ROW_TILE

    cond = jnp.zeros((SUBLANES, d), F32).at[:bsz].set(c).at[bsz].set(c_ctx)
    mods = _ada_terms(cond, w_mod, b_mod).reshape(depth, SUBLANES, 6, d)

    h = x.reshape(n, d)
    hc = ctx.reshape(bsz * n_ctx, d)
    for layer in range(depth):
        idx = layer // 2
        mod = mods[layer, :bsz]
        mod_ctx = mods[layer, bsz:bsz + 1]
        if layer % 2 == 0:
            w_in = _hybrid_in_weight(hyb_w_in[idx])
            gpre = _row(g_mix_pre[layer])
            k_gain = _row(jnp.tile(hyb_k_norm[idx], B_KV_HEADS))
            cos_k, slo_k, shi_k = _rope_tables(n_lat, B_KV_HEADS)
            no_pos = (jnp.ones((n_ctx, B_KV_WIDTH), F32), jnp.zeros((n_ctx, B_KV_WIDTH), F32),
                      jnp.zeros((n_ctx, B_KV_WIDTH), F32))
            qkv, z, qb, ba, k_lat, vt_lat = _hyb_inproj(h, mod, gpre, w_in, hyb_conv_w[idx], k_gain,
                                                        cos_k, slo_k, shi_k, bsz, n_lat, tm)
            qkv_c, _, _, ba_c, k_ctx, vt_ctx = _hyb_inproj(hc, mod_ctx, gpre, w_in, hyb_conv_w[idx], k_gain,
                                                           *no_pos, bsz, n_ctx, n_ctx)

            coef = jnp.stack([hyb_a_log[idx], hyb_dt_bias[idx]])
            coef = jnp.pad(coef.transpose(0, 2, 1), ((0, 0), (0, 0), (2, 0))).reshape(2, 4 * A_HEADS)
            coef = jnp.pad(coef, ((0, SUBLANES - 2), (0, LANES - 4 * A_HEADS)))
            o_f, o_b = _delta_mixer(qkv, qkv_c, ba, ba_c, coef, bsz, n_lat, n_ctx)

            k_hm = jnp.concatenate([k_lat, k_ctx], axis=2)
            vt_hm = jnp.concatenate([vt_lat, vt_ctx], axis=3)
            cos_q, slo_q, shi_q = _rope_tables(n_lat, B_GROUP)
            yb = _attention(qb, _row(jnp.tile(hyb_q_norm[idx], B_GROUP)), cos_q, slo_q, shi_q,
                            k_hm, vt_hm, n_lat, ATTN_Q_TILE)

            w_out = hyb_w_out[idx].astype(BF16)
            h = _outproj(o_f, o_b, z, _row(hyb_out_norm[idx]), yb, h, mod, _row(g_mix_post[layer]),
                         w_out[:A_WIDTH], w_out[A_WIDTH:], n_lat, tm)
        else:
            u = _confin(h, mod, _row(g_mix_pre[layer]), conf_w_in[idx].astype(BF16), _row(conf_b_in[idx]),
                        n_lat, tm)
            h = _confout(u, h, mod, _row(g_mix_post[layer]), conf_dw_w[idx], _row(conf_dw_b[idx]),
                         _row(conf_ln_g[idx]), _row(conf_ln_b[idx]), conf_w_out[idx].astype(BF16),
                         _row(conf_b_out[idx]), n_lat, CONF_TILE)
        h = _ffn(h, mod, _row(g_ffn_pre[layer]), _row(g_ffn_post[layer]), w_ffn_in[layer].astype(BF16),
                 w_ffn_out[layer].astype(BF16), n_lat, tm)
        assert not any(j % 2 == 0 for j in range(layer + 1, depth)), "context advance not implemented"
    return h.reshape(bsz, n_lat, d)
```

```python
import functools
import math

import jax
import jax.numpy as jnp
from jax import lax
from jax.experimental import pallas as pl
from jax.experimental.pallas import tpu as pltpu

F32 = jnp.float32
BF16 = jnp.bfloat16

EPS = 1e-6
GRID_W = 64
ROPE_THETA = 10000.0
A_HEADS = 4
A_HEAD_DIM = 128
A_WIDTH = A_HEADS * A_HEAD_DIM
SHORT_CONV_W = 5
CHUNK = 64
B_Q_HEADS = 8
B_KV_HEADS = 2
B_HEAD_DIM = 64
B_GROUP = B_Q_HEADS // B_KV_HEADS
B_WIDTH = B_Q_HEADS * B_HEAD_DIM
B_KV_WIDTH = B_KV_HEADS * B_HEAD_DIM
ROPE_AXIS_PAIRS = B_HEAD_DIM // 4
CONF_KERNEL = 31
LOG2_E = math.log2(math.e)

LANES = 128
SUBLANES = 8
VMEM_LIMIT = 56 * 1024 * 1024
ROW_TILE = 1024
CONF_TILE = 1024
CONF_HALO = 16
PROJ_HALO = 16
DELTA_BLOCK = 4
DELTA_ROWS = DELTA_BLOCK * CHUNK
ATTN_Q_TILE = 512
ATTN_KEY_BLOCK = 128
ATTN_Q_COLS = 256
ATTN_VT_ROWS = B_HEAD_DIM + 16


def _cparams(sem):
    return pltpu.CompilerParams(dimension_semantics=sem, vmem_limit_bytes=VMEM_LIMIT)


def _resident(shape):
    return pl.BlockSpec(shape, lambda i: (0, 0), pipeline_mode=pl.Buffered(1))


def _silu(x):
    return x * jax.nn.sigmoid(x)


def _dot(a, b):
    return jnp.dot(a, b, preferred_element_type=F32)


def _dot_nt(a, b):
    return lax.dot_general(a, b, (((1,), (1,)), ((), ())), preferred_element_type=F32)


def _dot_tn(a, b):
    return lax.dot_general(a, b, (((0,), (0,)), ((), ())), preferred_element_type=F32)


def _rms(x, gain):
    return x * lax.rsqrt(jnp.mean(x * x, axis=-1, keepdims=True) + EPS) * gain


def _split_bf16(a):
    hi = a.astype(BF16)
    return hi, (a - hi.astype(F32)).astype(BF16)


def _drain(*gens):
    last = [None] * len(gens)
    live = list(range(len(gens)))
    while live:
        for g in list(live):
            try:
                last[g] = next(gens[g])
            except StopIteration:
                live.remove(g)
    return last


def _ada_kernel(c_ref, w_ref, b_ref, o_ref):
    s_hi, s_lo = _split_bf16(_silu(c_ref[...]))
    w_hi, w_lo = _split_bf16(w_ref[0])
    o_ref[0] = _dot(s_hi, w_hi) + _dot(s_hi, w_lo) + _dot(s_lo, w_hi) + b_ref[0]


def _ada_terms(cond, w_mod, b_mod):
    depth, d, n6 = w_mod.shape
    tn = n6 // 4
    return pl.pallas_call(
        _ada_kernel,
        grid=(depth, n6 // tn),
        in_specs=[pl.BlockSpec((SUBLANES, d), lambda l, j: (0, 0)),
                  pl.BlockSpec((1, d, tn), lambda l, j: (l, 0, j)),
                  pl.BlockSpec((1, 1, tn), lambda l, j: (l, 0, j))],
        out_specs=pl.BlockSpec((1, SUBLANES, tn), lambda l, j: (l, 0, j)),
        out_shape=jax.ShapeDtypeStruct((depth, SUBLANES, n6), F32),
        compiler_params=_cparams(("arbitrary", "arbitrary")),
        name="ada",
    )(cond, w_mod, b_mod.reshape(depth, 1, n6))


def _modulated(x, mod_ref, gain_ref, shift_row):
    y = _rms(x, gain_ref[...])
    return y * (1.0 + mod_ref[0, shift_row + 1:shift_row + 2, :]) + mod_ref[0, shift_row:shift_row + 1, :]


def _gated_residual(x, y, mod_ref, gain_ref, gate_row):
    return x + mod_ref[0, gate_row:gate_row + 1, :] * _rms(y, gain_ref[...])


def _head_rms(x, gain_row, head_dim):
    outs = []
    for s in range(x.shape[1] // LANES):
        xs = x[:, s * LANES:(s + 1) * LANES]
        lane = lax.broadcasted_iota(jnp.int32, xs.shape, 1)
        sq = xs * xs
        scale = jnp.zeros_like(xs)
        for part in range(LANES // head_dim):
            m = jnp.logical_and(lane >= part * head_dim, lane < (part + 1) * head_dim)
            ms = jnp.sum(jnp.where(m, sq, 0.0), axis=-1, keepdims=True) * (1.0 / head_dim)
            scale = jnp.where(m, lax.rsqrt(ms + EPS), scale)
        outs.append(xs * scale)
    y = outs[0] if len(outs) == 1 else jnp.concatenate(outs, axis=1)
    return y * gain_row


def _rope(x, cos, sin_lo, sin_hi):
    width = x.shape[1]
    fwd = pltpu.roll(x, width - ROPE_AXIS_PAIRS, axis=1)
    back = pltpu.roll(x, ROPE_AXIS_PAIRS, axis=1)
    return x * cos + fwd * sin_lo + back * sin_hi


def _hyb_inproj_kernel(x_ref, xp_ref, xn_ref, mod_ref, g_ref, w_ref, cw_ref, kg_ref, cos_ref, slo_ref, shi_ref,
                       qkv_ref, z_ref, qb_ref, ba_ref, ko_ref, vo_ref, *, tiles_per_seq):
    pos = pl.program_id(0) % tiles_per_seq
    tm = x_ref.shape[0]
    halo = PROJ_HALO
    qkv_w = 3 * A_WIDTH
    a_prev = jnp.where(pos > 0, _modulated(xp_ref[...], mod_ref, g_ref, 0), 0.0).astype(BF16)
    a_next = jnp.where(pos < tiles_per_seq - 1, _modulated(xn_ref[...], mod_ref, g_ref, 0), 0.0).astype(BF16)
    a = _modulated(x_ref[...], mod_ref, g_ref, 0).astype(BF16)
    a_ext = jnp.concatenate([a_prev, a, a_next], axis=0)

    off_z = qkv_w
    off_q = off_z + A_WIDTH
    off_k = off_q + B_WIDTH
    off_v = off_k + B_KV_WIDTH
    off_ba = off_v + B_KV_WIDTH

    def gate_proj():
        z_ref[...] = _dot(a, w_ref[:, off_z:off_q])

    def query_proj():
        qb_ref[...] = _dot(a, w_ref[:, off_q:off_k])

    def key_value_proj():
        kb = _dot(a, w_ref[:, off_k:off_v])
        vb = _dot(a, w_ref[:, off_v:off_ba])
        k = _rope(_head_rms(kb, kg_ref[...], B_HEAD_DIM), cos_ref[...], slo_ref[...], shi_ref[...]).astype(BF16)
        vt = vb.T.astype(BF16)
        for hh in range(B_KV_HEADS):
            ko_ref[0, hh] = k[:, hh * B_HEAD_DIM:(hh + 1) * B_HEAD_DIM]
            vo_ref[0, hh, :B_HEAD_DIM, :] = vt[hh * B_HEAD_DIM:(hh + 1) * B_HEAD_DIM, :]
            vo_ref[0, hh, B_HEAD_DIM:, :] = jnp.ones((ATTN_VT_ROWS - B_HEAD_DIM, tm), BF16)

    def logit_proj():
        ba_ref[...] = _dot(a, w_ref[:, off_ba:off_ba + LANES])

    others = [gate_proj, query_proj, key_value_proj, logit_proj]
    ext_rows = tm + 2 * halo
    pair = 2 * A_HEAD_DIM
    n_pairs = qkv_w // pair
    ext_next = _dot(a_ext, w_ref[:, :pair])
    for cb in range(n_pairs):
        cols = slice(cb * pair, (cb + 1) * pair)
        ext = ext_next
        if cb + 1 < n_pairs:
            ext_next = _dot(a_ext, w_ref[:, (cb + 1) * pair:(cb + 2) * pair])
        if cb < len(others):
            others[cb]()
        w = cw_ref[:, cols]
        acc = jnp.zeros((tm, pair), F32)
        for tap in range(SHORT_CONV_W):
            shift = (SHORT_CONV_W // 2 - tap) % ext_rows
            rolled = pltpu.roll(ext, shift, axis=0) if shift else ext
            acc = acc + rolled[halo:halo + tm, :] * w[tap:tap + 1, :]
        y = _silu(acc)
        for part in range(2):
            head = 2 * cb + part
            yh = y[:, part * A_HEAD_DIM:(part + 1) * A_HEAD_DIM]
            if head < 2 * A_HEADS:
                yh = yh * lax.rsqrt(jnp.sum(yh * yh, axis=-1, keepdims=True) + EPS)
            if head < A_HEADS:
                yh = yh * (A_HEAD_DIM ** -0.5)
            qkv_ref[:, head * A_HEAD_DIM:(head + 1) * A_HEAD_DIM] = yh


def _hyb_inproj(x2, mod, gain, w, conv_w, k_gain, cos, slo, shi, bsz, rows_per_seq, tm):
    n, d = x2.shape
    tiles_per_seq = rows_per_seq // tm
    hb = tm // PROJ_HALO
    n_halo = n // PROJ_HALO
    f32_out = lambda wd: (pl.BlockSpec((tm, wd), lambda i: (i, 0)), jax.ShapeDtypeStruct((n, wd), F32))
    outs = [f32_out(3 * A_WIDTH), f32_out(A_WIDTH), f32_out(B_WIDTH), f32_out(LANES),
            (pl.BlockSpec((1, B_KV_HEADS, tm, B_HEAD_DIM), lambda i: (i // tiles_per_seq, 0, i % tiles_per_seq, 0)),
             jax.ShapeDtypeStruct((bsz, B_KV_HEADS, rows_per_seq, B_HEAD_DIM), BF16)),
            (pl.BlockSpec((1, B_KV_HEADS, ATTN_VT_ROWS, tm), lambda i: (i // tiles_per_seq, 0, 0, i % tiles_per_seq)),
             jax.ShapeDtypeStruct((bsz, B_KV_HEADS, ATTN_VT_ROWS, rows_per_seq), BF16))]
    tab = pl.BlockSpec((tm, B_KV_WIDTH), lambda i: (i % tiles_per_seq, 0))
    return pl.pallas_call(
        functools.partial(_hyb_inproj_kernel, tiles_per_seq=tiles_per_seq),
        grid=(n // tm,),
        in_specs=[pl.BlockSpec((tm, d), lambda i: (i, 0)),
                  pl.BlockSpec((PROJ_HALO, d), lambda i: (jnp.maximum(i * hb - 1, 0), 0)),
                  pl.BlockSpec((PROJ_HALO, d), lambda i: (jnp.minimum((i + 1) * hb, n_halo - 1), 0)),
                  pl.BlockSpec((1, 6, d), lambda i: (i // tiles_per_seq if mod.shape[0] > 1 else 0, 0, 0)),
                  pl.BlockSpec((1, d), lambda i: (0, 0)),
                  _resident(w.shape),
                  _resident(conv_w.shape),
                  pl.BlockSpec((1, B_KV_WIDTH), lambda i: (0, 0)), tab, tab, tab],
        out_specs=[o[0] for o in outs],
        out_shape=[o[1] for o in outs],
        compiler_params=_cparams(("arbitrary",)),
        name="inproj",
    )(x2, x2, x2, mod, gain, w, conv_w, k_gain, cos, slo, shi)


def _unit_tri_inverse_steps(l_mats, i, j):
    eye = (i == j).astype(F32)
    same16 = jnp.right_shift(i, 4) == jnp.right_shift(j, 4)
    same32 = jnp.right_shift(i, 5) == jnp.right_shift(j, 5)
    off32 = jnp.logical_and(same32, jnp.logical_not(same16))
    b = lambda a: a.astype(BF16)
    each = lambda f, *ls: [f(*xs) for xs in zip(*ls)]
    d1 = each(lambda l: b(jnp.where(same16, l, 0.0)), l_mats)
    p = each(lambda d: eye - d.astype(F32), d1)
    dk = d1
    for _ in range(3):
        dk = each(lambda d: b(_dot(d, d)), dk)
        yield None
        p = each(lambda pp, d: pp + _dot(b(pp), d), p, dk)
        yield None
    for sel in (off32, jnp.logical_not(same32)):
        cm = each(lambda l: b(jnp.where(sel, l, 0.0)), l_mats)
        pb = each(b, p)
        inner = each(lambda c, q: b(_dot(c, q)), cm, pb)
        yield None
        p = each(lambda pp, q, m: pp - _dot(q, m), p, pb, inner)
        yield None
    l_split = each(_split_bf16, l_mats)
    t_split = each(_split_bf16, p)
    resid = each(lambda t0, ls, ts: eye - t0 - (_dot(ls[0], ts[0]) + _dot(ls[0], ts[1]) + _dot(ls[1], ts[0])),
                 p, l_split, t_split)
    yield None
    yield each(lambda t0, ts, r: t0 + _dot(ts[0], b(r)), p, t_split, resid)


def _delta_kernel(qf_ref, kf_ref, vf_ref, qr_ref, kr_ref, vr_ref, qc_ref, kc_ref, vc_ref,
                  baf_ref, bar_ref, bac_ref, coef_ref, of_ref, ob_ref,
                  u_s, w_s, kt_s, qd_s, in_s, el_s, state):
    step = pl.program_id(1)
    is_ctx = step == 0
    front_slot = step % 2
    scan_slot = 1 - front_slot
    uw_write, uw_read = front_slot, scan_slot
    heads = [slice(hh * A_HEAD_DIM, (hh + 1) * A_HEAD_DIM) for hh in range(A_HEADS)]
    chunks = [slice(c * CHUNK, (c + 1) * CHUNK) for c in range(DELTA_BLOCK)]

    @pl.when(step == 0)
    def _():
        for scr in (u_s, w_s, kt_s, qd_s, in_s, el_s):
            scr[1] = jnp.zeros(scr.shape[1:], scr.dtype)
        state[...] = jnp.zeros(state.shape, F32)

    chains = [(d, hh) for d in range(2) for hh in range(A_HEADS)]
    s_mats = [state[d * A_HEADS + hh] for d, hh in chains]
    out_refs = (of_ref, ob_ref)

    def scan_substep(t, s_mats):
        where = []
        for d, hh in chains:
            c = t if d == 0 else DELTA_BLOCK - 1 - t
            where.append((d, c, chunks[c], heads[hh]))
        ws = [_dot(jnp.concatenate([w_s[uw_read, d, r, ln], qd_s[scan_slot, d, r, ln]], axis=0), s.astype(BF16))
              for (d, c, r, ln), s in zip(where, s_mats)]
        v_new = [(u_s[uw_read, d, r, ln] - x[:CHUNK]).astype(BF16) for (d, c, r, ln), x in zip(where, ws)]
        for (d, c, r, ln), x, vn in zip(where, ws, v_new):
            out_refs[d][r, ln] = x[CHUNK:] + _dot(in_s[scan_slot, d, r, ln][:, :CHUNK], vn)
        return [s * el_s[scan_slot, d, c * SUBLANES:c * SUBLANES + 1, ln] + _dot_tn(kt_s[scan_slot, d, r, ln], vn)
                for (d, c, r, ln), s, vn in zip(where, s_mats, v_new)]

    def per_head(main_ref, ctx_ref):
        x = jnp.where(is_ctx, ctx_ref[...], main_ref[...])
        return [x[:, ln] for ln in heads]

    q_h = (per_head(qf_ref, qc_ref), per_head(qr_ref, qc_ref))
    k_h = (per_head(kf_ref, kc_ref), per_head(kr_ref, kc_ref))
    v_h = (per_head(vf_ref, vc_ref), per_head(vr_ref, vc_ref))

    lane = lax.broadcasted_iota(jnp.int32, (1, LANES), 1)
    is_beta = jnp.bitwise_and(lane, 3) < 2
    neg_a = -jnp.exp(coef_ref[0:1, :])
    dtb = coef_ref[1:2, :]
    bg = []
    for raw in (jnp.where(is_ctx, bac_ref[...], baf_ref[...]), jnp.where(is_ctx, bac_ref[...], bar_ref[...])):
        xg = raw + dtb
        softplus = jnp.maximum(xg, 0.0) + jnp.log(1.0 + jnp.exp(-jnp.abs(xg)))
        bg.append(jnp.where(is_beta, jax.nn.sigmoid(raw), neg_a * softplus))

    i_idx = lax.broadcasted_iota(jnp.int32, (CHUNK, CHUNK), 0)
    j_idx = lax.broadcasted_iota(jnp.int32, (CHUNK, CHUNK), 1)
    i_w = lax.broadcasted_iota(jnp.int32, (CHUNK, LANES), 0)
    j_w = lax.broadcasted_iota(jnp.int32, (CHUNK, LANES), 1)
    in_chunk = j_w < CHUNK
    ones_rows = jnp.ones((2 * SUBLANES, CHUNK), BF16)

    def front_steps(d):
        incl = (i_idx >= j_idx) if d == 0 else (i_idx <= j_idx)
        strict = (i_idx > j_idx) if d == 0 else (i_idx < j_idx)
        incl_w = jnp.logical_and(in_chunk, (i_w >= j_w) if d == 0 else (i_w <= j_w))
        seen_w = jnp.logical_and(in_chunk, (i_w <= j_w) if d == 0 else (i_w >= j_w))
        seen_cat = jnp.concatenate([seen_w] * DELTA_BLOCK, axis=1)
        incl_b = incl.astype(BF16)
        for hh, ln in enumerate(heads):
            g_lane = 4 * hh + 2 + d
            g_cat = jnp.concatenate([jnp.broadcast_to(bg[d][r, g_lane:g_lane + 1], (CHUNK, LANES)) for r in chunks],
                                    axis=1)
            g_hi = g_cat.astype(BF16)
            g_rest = g_cat - g_hi.astype(F32)
            g_mid = g_rest.astype(BF16)
            pieces = (g_hi, g_mid, (g_rest - g_mid.astype(F32)).astype(BF16))
            gc_all = sum(_dot(incl_b, p) for p in pieces)
            gr_all = sum(_dot(ones_rows, jnp.where(seen_cat, p, jnp.zeros_like(p))) for p in pieces)
            for c, r in enumerate(chunks):
                q, k, v = q_h[d][hh][r], k_h[d][hh][r], v_h[d][hh][r]
                kbf = k.astype(BF16)
                prod = _dot_nt(jnp.concatenate([kbf, q.astype(BF16)], axis=0),
                               jnp.concatenate([kbf, jnp.zeros_like(kbf)], axis=0))
                beta = bg[d][r, 4 * hh + d:4 * hh + d + 1]
                gc_w = gc_all[:, c * LANES:(c + 1) * LANES]
                gr_w = jnp.broadcast_to(gr_all[0:1, c * LANES:(c + 1) * LANES], (CHUNK, LANES))
                decay = jnp.where(incl_w, jnp.exp(jnp.minimum(gc_w - gr_w, 0.0)), 0.0)
                g_last = gc_w[CHUNK - 1:CHUNK, :] if d == 0 else gc_w[0:1, :]
                egc = jnp.exp(gc_w)
                l_mats.append(jnp.where(strict, (prod[:CHUNK] * beta * decay)[:, :CHUNK], 0.0))
                rhs_all.append(jnp.concatenate([v * beta, k * beta * egc], axis=1).astype(BF16))
                dests.append((d, r, ln))
                kt_s[front_slot, d, r, ln] = (k * jnp.exp(g_last - gc_w)).astype(BF16)
                qd_s[front_slot, d, r, ln] = (q * egc).astype(BF16)
                in_s[front_slot, d, r, ln] = (prod[CHUNK:] * decay).astype(BF16)
                el_s[front_slot, d, c * SUBLANES:(c + 1) * SUBLANES, ln] = jnp.broadcast_to(jnp.exp(g_last),
                                                                                            (SUBLANES, LANES))
                yield None

    def scan_steps(s_mats):
        for t in range(DELTA_BLOCK):
            s_mats = scan_substep(t, s_mats)
            for _ in range(4):
                yield s_mats

    l_mats, rhs_all, dests = [], [], []
    _, _, s_mats = _drain(front_steps(0), front_steps(1), scan_steps(s_mats))
    inverse = _drain(_unit_tri_inverse_steps(l_mats, i_idx, j_idx))[0]
    t_split = [_split_bf16(t) for t in inverse]
    sols = [_dot(t_hi, rr) + _dot(t_lo, rr) for (t_hi, t_lo), rr in zip(t_split, rhs_all)]
    for (d, r, ln), sol in zip(dests, sols):
        u_s[uw_write, d, r, ln] = sol[:, :A_HEAD_DIM]
        w_s[uw_write, d, r, ln] = sol[:, A_HEAD_DIM:].astype(BF16)
    for (d, hh), s in zip(chains, s_mats):
        state[d * A_HEADS + hh] = s


def _delta_mixer(qkv, qkv_c, ba, ba_c, coef, bsz, n_lat, n_ctx):
    assert n_ctx == DELTA_ROWS and n_lat % DELTA_ROWS == 0
    wd = A_WIDTH
    lat_blocks = n_lat // DELTA_ROWS
    n_steps = lat_blocks + 2

    def fwd_blk(b, s):
        return b * lat_blocks + jnp.clip(s - 1, 0, lat_blocks - 1)

    def bwd_blk(b, s):
        return b * lat_blocks + jnp.clip(lat_blocks - s, 0, lat_blocks - 1)

    def lat(blk_fn, kd):
        return pl.BlockSpec((DELTA_ROWS, wd), lambda b, s: (blk_fn(b, s), kd))

    def ctxb(kd):
        return pl.BlockSpec((n_ctx, wd), lambda b, s: (b, kd))

    in_specs = ([lat(fwd_blk, kd) for kd in range(3)] + [lat(bwd_blk, kd) for kd in range(3)]
                + [ctxb(kd) for kd in range(3)]
                + [pl.BlockSpec((DELTA_ROWS, LANES), lambda b, s: (fwd_blk(b, s), 0)),
                   pl.BlockSpec((DELTA_ROWS, LANES), lambda b, s: (bwd_blk(b, s), 0)),
                   pl.BlockSpec((n_ctx, LANES), lambda b, s: (b, 0)),
                   pl.BlockSpec((SUBLANES, LANES), lambda b, s: (0, 0))])
    out_f = pl.BlockSpec((DELTA_ROWS, wd), lambda b, s: (b * lat_blocks + jnp.maximum(s - 2, 0), 0))
    out_b = pl.BlockSpec((DELTA_ROWS, wd), lambda b, s: (b * lat_blocks + lat_blocks - jnp.maximum(s - 1, 1), 0))
    shape = jax.ShapeDtypeStruct((bsz * n_lat, wd), F32)
    slots = lambda rows, dt: pltpu.VMEM((2, 2, rows, wd), dt)
    return pl.pallas_call(
        _delta_kernel,
        grid=(bsz, n_steps),
        in_specs=in_specs,
        out_specs=[out_f, out_b],
        out_shape=[shape, shape],
        scratch_shapes=[slots(DELTA_ROWS, F32), slots(DELTA_ROWS, BF16), slots(DELTA_ROWS, BF16),
                        slots(DELTA_ROWS, BF16), slots(DELTA_ROWS, BF16), slots(DELTA_BLOCK * SUBLANES, F32),
                        pltpu.VMEM((2 * A_HEADS, A_HEAD_DIM, A_HEAD_DIM), F32)],
        compiler_params=_cparams(("arbitrary", "arbitrary")),
        name="delta",
    )(*([qkv] * 6), *([qkv_c] * 3), ba, ba, ba_c, coef)


def _attn_kernel(q_ref, g_ref, cos_ref, slo_ref, shi_ref, k_ref, vt_ref, o_ref):
    q = _head_rms(q_ref[...], g_ref[...], B_HEAD_DIM)
    q = _rope(q, cos_ref[...], slo_ref[...], shi_ref[...]) * (B_HEAD_DIM ** -0.5 * LOG2_E)
    tq = q.shape[0]
    qt = q.T.astype(BF16)
    n_keys = k_ref.shape[2]
    q_s = [qt[g * B_HEAD_DIM:(g + 1) * B_HEAD_DIM, c:c + ATTN_Q_COLS]
           for g in range(B_GROUP) for c in range(0, tq, ATTN_Q_COLS)]
    m = [jnp.full((1, ATTN_Q_COLS), -1e30, F32) for _ in q_s]
    acc = [jnp.zeros((ATTN_VT_ROWS, ATTN_Q_COLS), F32) for _ in q_s]

    def scores(kb):
        k_blk = k_ref[0, 0, kb:kb + ATTN_KEY_BLOCK, :]
        return [_dot(k_blk, qs) for qs in q_s]

    def value_update(acc, pending):
        alpha, p, kb = pending
        vt = vt_ref[0, 0, :, kb:kb + ATTN_KEY_BLOCK]
        return [a * al + _dot(vt, pp) for a, al, pp in zip(acc, alpha, p)]

    st_next = scores(0)
    pending = None
    for kb in range(0, n_keys, ATTN_KEY_BLOCK):
        st = st_next
        if kb + ATTN_KEY_BLOCK < n_keys:
            st_next = scores(kb + ATTN_KEY_BLOCK)
        m_new = [jnp.maximum(mo, jnp.max(s, axis=0, keepdims=True)) for mo, s in zip(m, st)]
        alpha = [jnp.exp2(mo - mn) for mo, mn in zip(m, m_new)]
        p = [jnp.exp2((s - mn).astype(BF16)) for s, mn in zip(st, m_new)]
        if pending is not None:
            acc = value_update(acc, pending)
        pending = (alpha, p, kb)
        m = m_new
    acc = value_update(acc, pending)
    outs = [a[:B_HEAD_DIM] / a[B_HEAD_DIM:B_HEAD_DIM + 1] for a in acc]
    n_col = tq // ATTN_Q_COLS
    o_ref[...] = jnp.concatenate([jnp.concatenate(outs[g * n_col:(g + 1) * n_col], axis=1)
                                  for g in range(B_GROUP)], axis=0).T.astype(o_ref.dtype)


def _attention(qb, gain, cos, slo, shi, k_hm, vt_hm, n_lat, tq):
    n, _ = qb.shape
    bsz, _, s_len, _ = k_hm.shape
    assert s_len % ATTN_KEY_BLOCK == 0 and tq % ATTN_Q_COLS == 0
    gw = B_GROUP * B_HEAD_DIM
    nq = n_lat // tq
    kv = pl.BlockSpec((1, 1, s_len, B_HEAD_DIM), lambda b, kh, i: (b, kh, 0, 0))
    vts = pl.BlockSpec((1, 1, ATTN_VT_ROWS, s_len), lambda b, kh, i: (b, kh, 0, 0))
    tab = pl.BlockSpec((tq, gw), lambda b, kh, i: (i, 0))
    return pl.pallas_call(
        _attn_kernel,
        grid=(bsz, B_KV_HEADS, nq),
        in_specs=[pl.BlockSpec((tq, gw), lambda b, kh, i: (b * nq + i, kh)),
                  pl.BlockSpec((1, gw), lambda b, kh, i: (0, 0)), tab, tab, tab, kv, vts],
        out_specs=pl.BlockSpec((tq, gw), lambda b, kh, i: (b * nq + i, kh)),
        out_shape=jax.ShapeDtypeStruct((n, B_WIDTH), BF16),
        compiler_params=_cparams(("arbitrary", "arbitrary", "arbitrary")),
        name="attn",
    )(qb, gain, cos, slo, shi, k_hm, vt_hm)


def _outproj_kernel(of_ref, ob_ref, z_ref, on_ref, yb_ref, x_ref, mod_ref, g_ref, wa_ref, wb_ref, o_ref):
    o = of_ref[...] + ob_ref[...]
    z = z_ref[...]
    parts = []
    for hh in range(A_HEADS):
        sl = slice(hh * A_HEAD_DIM, (hh + 1) * A_HEAD_DIM)
        parts.append((_rms(o[:, sl], on_ref[...]) * _silu(z[:, sl])).astype(BF16))
    ya = jnp.concatenate(parts, axis=1)
    y = _dot(ya, wa_ref[...]) + _dot(yb_ref[...], wb_ref[...])
    o_ref[...] = _gated_residual(x_ref[...], y, mod_ref, g_ref, 2)


def _outproj(o_f, o_b, z, out_norm, yb, x2, mod, gain, wa, wb, rows_per_mod, tm):
    n, d = x2.shape
    bpm = rows_per_mod // tm
    return pl.pallas_call(
        _outproj_kernel,
        grid=(n // tm,),
        in_specs=[pl.BlockSpec((tm, o_f.shape[1]), lambda i: (i, 0)),
                  pl.BlockSpec((tm, o_b.shape[1]), lambda i: (i, 0)),
                  pl.BlockSpec((tm, z.shape[1]), lambda i: (i, 0)),
                  pl.BlockSpec((1, A_HEAD_DIM), lambda i: (0, 0)),
                  pl.BlockSpec((tm, yb.shape[1]), lambda i: (i, 0)),
                  pl.BlockSpec((tm, d), lambda i: (i, 0)),
                  pl.BlockSpec((1, 6, d), lambda i: (i // bpm, 0, 0)),
                  pl.BlockSpec((1, d), lambda i: (0, 0)),
                  _resident(wa.shape),
                  _resident(wb.shape)],
        out_specs=pl.BlockSpec((tm, d), lambda i: (i, 0)),
        out_shape=jax.ShapeDtypeStruct((n, d), F32),
        compiler_params=_cparams(("arbitrary",)),
        name="outproj",
    )(o_f, o_b, z, out_norm, yb, x2, mod, gain, wa, wb)


def _ffn_kernel(x_ref, mod_ref, gpre_ref, gpost_ref, wi_ref, wo_ref, o_ref, *, hidden, hc):
    x = x_ref[...]
    a = _modulated(x, mod_ref, gpre_ref, 3).astype(BF16)
    acc = jnp.zeros(x.shape, F32)
    for c in range(hidden // hc):
        gate = _dot(a, wi_ref[:, c * hc:(c + 1) * hc])
        up = _dot(a, wi_ref[:, hidden + c * hc:hidden + (c + 1) * hc])
        acc = acc + _dot((_silu(gate) * up).astype(BF16), wo_ref[c * hc:(c + 1) * hc, :])
    o_ref[...] = _gated_residual(x, acc, mod_ref, gpost_ref, 5)


def _ffn(x2, mod, gpre, gpost, wi, wo, rows_per_mod, tm):
    n, d = x2.shape
    hidden = wo.shape[0]
    bpm = rows_per_mod // tm
    return pl.pallas_call(
        functools.partial(_ffn_kernel, hidden=hidden, hc=2 * LANES),
        grid=(n // tm,),
        in_specs=[pl.BlockSpec((tm, d), lambda i: (i, 0)),
                  pl.BlockSpec((1, 6, d), lambda i: (i // bpm, 0, 0)),
                  pl.BlockSpec((1, d), lambda i: (0, 0)),
                  pl.BlockSpec((1, d), lambda i: (0, 0)),
                  _resident(wi.shape),
                  _resident(wo.shape)],
        out_specs=pl.BlockSpec((tm, d), lambda i: (i, 0)),
        out_shape=jax.ShapeDtypeStruct((n, d), F32),
        compiler_params=_cparams(("arbitrary",)),
        name="ffn",
    )(x2, mod, gpre, gpost, wi, wo)


def _confin_kernel(x_ref, mod_ref, g_ref, w_ref, b_ref, o_ref, *, width):
    a = _modulated(x_ref[...], mod_ref, g_ref, 0).astype(BF16)
    val = _dot(a, w_ref[:, :width]) + b_ref[:, :width]
    gate = _dot(a, w_ref[:, width:]) + b_ref[:, width:]
    o_ref[...] = val * jax.nn.sigmoid(gate)


def _confin(x2, mod, gain, w, b, rows_per_mod, tm):
    n, d = x2.shape
    width = w.shape[1] // 2
    bpm = rows_per_mod // tm
    return pl.pallas_call(
        functools.partial(_confin_kernel, width=width),
        grid=(n // tm,),
        in_specs=[pl.BlockSpec((tm, d), lambda i: (i, 0)),
                  pl.BlockSpec((1, 6, d), lambda i: (i // bpm, 0, 0)),
                  pl.BlockSpec((1, d), lambda i: (0, 0)),
                  _resident(w.shape),
                  pl.BlockSpec((1, 2 * width), lambda i: (0, 0))],
        out_specs=pl.BlockSpec((tm, width), lambda i: (i, 0)),
        out_shape=jax.ShapeDtypeStruct((n, width), F32),
        compiler_params=_cparams(("arbitrary",)),
        name="confin",
    )(x2, mod, gain, w, b)


def _confout_kernel(u_ref, up_ref, un_ref, x_ref, mod_ref, g_ref, dww_ref, dwb_ref, lng_ref, lnb_ref,
                    w_ref, b_ref, o_ref, ext, conv, shifted, *, tm, tiles_per_seq):
    pos = pl.program_id(0) % tiles_per_seq
    width = u_ref.shape[1]
    halo = CONF_HALO
    ext[pl.ds(halo, tm), :] = u_ref[...]
    ext[0:halo, :] = jnp.where(pos == 0, 0.0, up_ref[...])
    ext[pl.ds(halo + tm, halo), :] = jnp.where(pos == tiles_per_seq - 1, 0.0, un_ref[...])

    rb = 64
    pad = CONF_KERNEL // 2
    copy_rows = tm + 2 * halo - SUBLANES

    def col_body(c, carry):
        cs = pl.ds(pl.multiple_of(c * LANES, LANES), LANES)
        w = dww_ref[:, cs]
        for s in range(SUBLANES):
            shifted[s, :, :] = ext[pl.ds(s, copy_rows), cs]
        for r in range(tm // rb):
            acc = jnp.zeros((rb, LANES), F32)
            for tap in range(CONF_KERNEL):
                off = halo - pad + tap
                acc = acc + shifted[off % SUBLANES, pl.ds(r * rb + off - off % SUBLANES, rb), :] * w[tap:tap + 1, :]
            conv[pl.ds(r * rb, rb), cs] = acc
        return carry

    lax.fori_loop(0, width // LANES, col_body, 0)

    y = conv[...] + dwb_ref[...]
    mu = jnp.mean(y, axis=-1, keepdims=True)
    yc = y - mu
    var = jnp.mean(yc * yc, axis=-1, keepdims=True)
    y = _silu(yc * lax.rsqrt(var + EPS) * lng_ref[...] + lnb_ref[...])
    out = _dot(y.astype(BF16), w_ref[...]) + b_ref[...]
    o_ref[...] = _gated_residual(x_ref[...], out, mod_ref, g_ref, 2)


def _confout(u, x2, mod, gain, dww, dwb, lng, lnb, w, b, rows_per_mod, tm):
    n, d = x2.shape
    width = u.shape[1]
    tiles_per_seq = rows_per_mod // tm
    hb = tm // CONF_HALO
    n_halo_blocks = n // CONF_HALO
    vec = lambda wd: pl.BlockSpec((1, wd), lambda i: (0, 0))
    return pl.pallas_call(
        functools.partial(_confout_kernel, tm=tm, tiles_per_seq=tiles_per_seq),
        grid=(n // tm,),
        in_specs=[pl.BlockSpec((tm, width), lambda i: (i, 0)),
                  pl.BlockSpec((CONF_HALO, width), lambda i: (jnp.maximum(i * hb - 1, 0), 0)),
                  pl.BlockSpec((CONF_HALO, width), lambda i: (jnp.minimum((i + 1) * hb, n_halo_blocks - 1), 0)),
                  pl.BlockSpec((tm, d), lambda i: (i, 0)),
                  pl.BlockSpec((1, 6, d), lambda i: (i // tiles_per_seq, 0, 0)),
                  vec(d),
                  pl.BlockSpec(dww.shape, lambda i: (0, 0)),
                  vec(width), vec(width), vec(width),
                  _resident(w.shape),
                  vec(d)],
        out_specs=pl.BlockSpec((tm, d), lambda i: (i, 0)),
        out_shape=jax.ShapeDtypeStruct((n, d), F32),
        scratch_shapes=[pltpu.VMEM((tm + 2 * CONF_HALO, width), F32), pltpu.VMEM((tm, width), F32),
                        pltpu.VMEM((SUBLANES, tm + 2 * CONF_HALO - SUBLANES, LANES), F32)],
        compiler_params=_cparams(("arbitrary",)),
        name="confout",
    )(u, u, u, x2, mod, gain, dww, dwb, lng, lnb, w, b)


def _rope_tables(n_tokens, reps):
    rows = n_tokens // GRID_W
    row = jnp.broadcast_to(jnp.arange(rows, dtype=F32)[:, None], (rows, GRID_W)).reshape(n_tokens)
    col = jnp.broadcast_to(jnp.arange(GRID_W, dtype=F32)[None, :], (rows, GRID_W)).reshape(n_tokens)
    inv_freq = ROPE_THETA ** (-jnp.arange(ROPE_AXIS_PAIRS, dtype=F32) / ROPE_AXIS_PAIRS)
    ang_r = row[:, None] * inv_freq
    ang_c = col[:, None] * inv_freq
    ang = jnp.concatenate([ang_r, ang_r, ang_c, ang_c], axis=-1)
    cos, sin = jnp.cos(ang), jnp.sin(ang)
    first_half = (jnp.arange(B_HEAD_DIM) % (2 * ROPE_AXIS_PAIRS)) < ROPE_AXIS_PAIRS
    sin_lo = jnp.where(first_half, -sin, 0.0)
    sin_hi = jnp.where(first_half, 0.0, sin)
    tile = lambda t: jnp.tile(t, (1, reps))
    return tile(cos), tile(sin_lo), tile(sin_hi)


def _hybrid_in_weight(w_in):
    off_z = 3 * A_WIDTH
    off_ba = off_z + A_WIDTH
    off_q = off_ba + 4 * A_HEADS
    off_k = off_q + B_WIDTH
    off_v = off_k + B_KV_WIDTH
    d = w_in.shape[0]
    ba = w_in[:, off_ba:off_q].reshape(d, 2, 2, A_HEADS)
    ba = ba.transpose(0, 3, 1, 2).reshape(d, 4 * A_HEADS)
    ba = jnp.pad(ba, ((0, 0), (0, LANES - 4 * A_HEADS)))
    w = jnp.concatenate([w_in[:, :off_z], w_in[:, off_z:off_ba], w_in[:, off_q:off_k], w_in[:, off_k:off_v],
                         w_in[:, off_v:], ba], axis=1)
    return w.astype(BF16)


def _row(v):
    return v.reshape(1, -1)


def kernel(x, c, ctx, c_ctx, w_mod, b_mod, g_mix_pre, g_mix_post, g_ffn_pre, g_ffn_post, w_ffn_in, w_ffn_out,
           hyb_w_in, hyb_conv_w, hyb_a_log, hyb_dt_bias, hyb_out_norm, hyb_q_norm, hyb_k_norm, hyb_w_out,
           conf_w_in, conf_b_in, conf_dw_w, conf_dw_b, conf_ln_g, conf_ln_b, conf_w_out, conf_b_out):
    bsz, n_lat, d = x.shape
    n_ctx = ctx.shape[1]
    depth = w_mod.shape[0]
    n = bsz * n_lat
    tm = ROW_TILE

    cond = jnp.zeros((SUBLANES, d), F32).at[:bsz].set(c).at[bsz].set(c_ctx)
    mods = _ada_terms(cond, w_mod, b_mod).reshape(depth, SUBLANES, 6, d)

    h = x.reshape(n, d)
    hc = ctx.reshape(bsz * n_ctx, d)
    for layer in range(depth):
        idx = layer // 2
        mod = mods[layer, :bsz]
        mod_ctx = mods[layer, bsz:bsz + 1]
        if layer % 2 == 0:
            w_in = _hybrid_in_weight(hyb_w_in[idx])
            gpre = _row(g_mix_pre[layer])
            k_gain = _row(jnp.tile(hyb_k_norm[idx], B_KV_HEADS))
            cos_k, slo_k, shi_k = _rope_tables(n_lat, B_KV_HEADS)
            no_pos = (jnp.ones((n_ctx, B_KV_WIDTH), F32), jnp.zeros((n_ctx, B_KV_WIDTH), F32),
                      jnp.zeros((n_ctx, B_KV_WIDTH), F32))
            qkv, z, qb, ba, k_lat, vt_lat = _hyb_inproj(h, mod, gpre, w_in, hyb_conv_w[idx], k_gain,
                                                        cos_k, slo_k, shi_k, bsz, n_lat, tm)
            qkv_c, _, _, ba_c, k_ctx, vt_ctx = _hyb_inproj(hc, mod_ctx, gpre, w_in, hyb_conv_w[idx], k_gain,
                                                           *no_pos, bsz, n_ctx, n_ctx)

            coef = jnp.stack([hyb_a_log[idx], hyb_dt_bias[idx]])
            coef = jnp.pad(coef.transpose(0, 2, 1), ((0, 0), (0, 0), (2, 0))).reshape(2, 4 * A_HEADS)
            coef = jnp.pad(coef, ((0, SUBLANES - 2), (0, LANES - 4 * A_HEADS)))
            o_f, o_b = _delta_mixer(qkv, qkv_c, ba, ba_c, coef, bsz, n_lat, n_ctx)

            k_hm = jnp.concatenate([k_lat, k_ctx], axis=2)
            vt_hm = jnp.concatenate([vt_lat, vt_ctx], axis=3)
            cos_q, slo_q, shi_q = _rope_tables(n_lat, B_GROUP)
            yb = _attention(qb, _row(jnp.tile(hyb_q_norm[idx], B_GROUP)), cos_q, slo_q, shi_q,
                            k_hm, vt_hm, n_lat, ATTN_Q_TILE)

            w_out = hyb_w_out[idx].astype(BF16)
            h = _outproj(o_f, o_b, z, _row(hyb_out_norm[idx]), yb, h, mod, _row(g_mix_post[layer]),
                         w_out[:A_WIDTH], w_out[A_WIDTH:], n_lat, tm)
        else:
            u = _confin(h, mod, _row(g_mix_pre[layer]), conf_w_in[idx].astype(BF16), _row(conf_b_in[idx]),
                        n_lat, tm)
            h = _confout(u, h, mod, _row(g_mix_post[layer]), conf_dw_w[idx], _row(conf_dw_b[idx]),
                         _row(conf_ln_g[idx]), _row(conf_ln_b[idx]), conf_w_out[idx].astype(BF16),
                         _row(conf_b_out[idx]), n_lat, CONF_TILE)
        h = _ffn(h, mod, _row(g_ffn_pre[layer]), _row(g_ffn_post[layer]), w_ffn_in[layer].astype(BF16),
                 w_ffn_out[layer].astype(BF16), n_lat, tm)
        assert not any(j % 2 == 0 for j in range(layer + 1, depth)), "context advance not implemented"
    return h.reshape(bsz, n_lat, d)
```
